```python
import jax, jax.numpy as jnp
from jax import lax
import numpy as np

D_MODEL = 2048
BATCH = 8
SEQ = 4096
DEPTH = 2

CHUNK = 64
Q_BLOCK = 128
HEAD_DIM = 128
ROPE_THETA = 500000.0
PARTIAL_ROPE_DIM = HEAD_DIM // 4
NORM_EPS = 1e-6
D_MIX = D_MODEL
D_FF = 5632
N_MOD = 9

A_HEADS = D_MIX // (4 * HEAD_DIM)
IDX_HEADS = 16
IDX_DIM = 64
TOPK_MAX = 256
B_HEADS = D_MIX // (2 * HEAD_DIM)
MLA_Q_RANK = 448
MLA_KV_RANK = 128
MLA_NOPE = 128
MLA_ROPE = 64
MLA_V = HEAD_DIM
C_HEADS = D_MIX // (4 * HEAD_DIM)

SPLIT_SIZES = (
    A_HEADS * HEAD_DIM, A_HEADS * HEAD_DIM, A_HEADS * HEAD_DIM,
    IDX_HEADS * IDX_DIM, IDX_DIM, IDX_HEADS,
    MLA_Q_RANK, MLA_KV_RANK, MLA_ROPE,
    C_HEADS * HEAD_DIM, C_HEADS * HEAD_DIM, C_HEADS * HEAD_DIM,
)
N_IN = sum(SPLIT_SIZES)

kernel_name = "hybrid_dsa_mla_stickbreaking_macaron_block"


def rms_norm(x, g):
    xf = x.astype(jnp.float32)
    y = xf * lax.rsqrt(jnp.mean(xf * xf, axis=-1, keepdims=True) + NORM_EPS)
    return (y * g.astype(jnp.float32)).astype(x.dtype)


def rope_tables(seq, dim):
    inv = 1.0 / (ROPE_THETA ** (jnp.arange(0, dim, 2, dtype=jnp.float32) / dim))
    ang = jnp.arange(seq, dtype=jnp.float32)[:, None] * inv[None, :]
    return jnp.cos(ang), jnp.sin(ang)


def apply_rope(x, cos, sin):
    x1, x2 = jnp.split(x, 2, axis=-1)
    c = cos[None, :, None, :].astype(x.dtype)
    s = sin[None, :, None, :].astype(x.dtype)
    return jnp.concatenate([x1 * c - x2 * s, x1 * s + x2 * c], axis=-1)


def partial_rope(x, cos, sin):
    r = PARTIAL_ROPE_DIM
    return jnp.concatenate([apply_rope(x[..., :r], cos, sin), x[..., r:]], axis=-1)


def chunk_mask(start, seq):
    q_pos = start + jnp.arange(Q_BLOCK)
    k_pos = jnp.arange(seq)
    return (k_pos[None, :] // CHUNK) <= (q_pos[:, None] // CHUNK)


def sweep_query_blocks(fn, *qs):
    b, s = qs[0].shape[:2]
    nb = s // Q_BLOCK
    blk = tuple(jnp.moveaxis(q.reshape((b, nb, Q_BLOCK) + q.shape[2:]), 1, 0) for q in qs)
    starts = jnp.arange(nb, dtype=jnp.int32) * Q_BLOCK
    out = lax.map(lambda a: fn(*a[0], a[1]), (blk, starts))
    return jnp.moveaxis(out, 0, 1).reshape((b, s) + out.shape[3:])


def dsa_attention(q, k, v, q_idx, k_idx, w_idx, topk):
    seq = k.shape[1]
    scale = HEAD_DIM ** -0.5

    def block(qb, qib, wib, start):
        adm = chunk_mask(start, seq)
        logits = jnp.einsum('bqhd,bsd->bqhs', qib, k_idx) * (IDX_DIM ** -0.5)
        score = jnp.einsum('bqh,bqhs->bqs', wib * (IDX_HEADS ** -0.5), jax.nn.relu(logits))
        score = jnp.where(adm[None], score.astype(jnp.float32), -jnp.inf)
        top_val, top_idx = lax.top_k(score, topk)
        valid = jnp.isfinite(top_val)
        k_sel = jax.vmap(lambda kb, ib: kb[ib])(k, top_idx)
        v_sel = jax.vmap(lambda vb, ib: vb[ib])(v, top_idx)
        att = jnp.einsum('bqhd,bqkhd->bhqk', qb, k_sel).astype(jnp.float32) * scale
        att = jnp.where(valid[:, None], att, -jnp.inf)
        p = jax.nn.softmax(att, axis=-1).astype(v.dtype)
        return jnp.einsum('bhqk,bqkhd->bqhd', p, v_sel)

    return sweep_query_blocks(block, q, q_idx, w_idx)


def mla_attention(q_nope, q_rope, k_nope, k_rope, v):
    seq = k_nope.shape[1]
    scale = (MLA_NOPE + MLA_ROPE) ** -0.5

    def block(qn, qr, start):
        adm = chunk_mask(start, seq)
        sc = (jnp.einsum('bqhd,bshd->bhqs', qn, k_nope)
              + jnp.einsum('bqhr,bsr->bhqs', qr, k_rope)).astype(jnp.float32) * scale
        sc = jnp.where(adm[None, None], sc, -jnp.inf)
        p = jax.nn.softmax(sc, axis=-1).astype(v.dtype)
        return jnp.einsum('bhqs,bshd->bqhd', p, v)

    return sweep_query_blocks(block, q_nope, q_rope)


def stick_breaking_attention(q, k, v):
    seq = k.shape[1]
    scale = HEAD_DIM ** -0.5
    k_pos = jnp.arange(seq)

    def block(qb, start):
        q_pos = start + jnp.arange(Q_BLOCK)
        before = k_pos[None, :] < q_pos[:, None]
        z = jnp.einsum('bqhd,bshd->bhqs', qb, k).astype(jnp.float32) * scale
        log_stay = jnp.where(before, jax.nn.log_sigmoid(-z), 0.0)
        later = lax.cumsum(log_stay, axis=log_stay.ndim - 1, reverse=True) - log_stay
        w = jnp.where(before, jnp.exp(jax.nn.log_sigmoid(z) + later), 0.0)
        return jnp.einsum('bhqs,bshd->bqhd', w.astype(v.dtype), v)

    return sweep_query_blocks(block, q)


def swiglu(h, w_gate, w_up, w_down):
    return (jax.nn.silu(h @ w_gate) * (h @ w_up)) @ w_down


def modulate(h, shift, scale):
    return h * (1.0 + scale[:, None, :]) + shift[:, None, :]


def setup_inputs(seed: int = 0) -> dict:
    key = jax.random.key(seed)
    ks = jax.random.split(key, 32)
    L, D = DEPTH, D_MODEL

    def dense(k, shape, fan_in, gain=1.0):
        return jax.random.normal(k, shape, jnp.float32) * (gain * fan_in ** -0.5)

    def gain(k, shape):
        return 1.0 + 0.05 * jax.random.normal(k, shape, jnp.float32)

    return {
        "x": jax.random.normal(ks[0], (BATCH, SEQ, D), jnp.float32),
        "c": jax.random.normal(ks[1], (BATCH, D), jnp.float32),
        "w_ada": dense(ks[2], (L, D, N_MOD * D), D, 0.1),
        "b_ada": 0.01 * jax.random.normal(ks[3], (L, N_MOD * D), jnp.float32),
        "g_ffn1": gain(ks[4], (L, D)),
        "w1_gate": dense(ks[5], (L, D, D_FF), D),
        "w1_up": dense(ks[6], (L, D, D_FF), D),
        "w1_down": dense(ks[7], (L, D_FF, D), D_FF),
        "g_mix": gain(ks[8], (L, D)),
        "w_in": dense(ks[9], (L, D, N_IN), D),
        "g_qa": gain(ks[10], (L, HEAD_DIM)),
        "g_ka": gain(ks[11], (L, HEAD_DIM)),
        "g_cq": gain(ks[12], (L, MLA_Q_RANK)),
        "g_ckv": gain(ks[13], (L, MLA_KV_RANK)),
        "w_uq": dense(ks[14], (L, MLA_Q_RANK, B_HEADS * (MLA_NOPE + MLA_ROPE)), MLA_Q_RANK),
        "w_ukv": dense(ks[15], (L, MLA_KV_RANK, B_HEADS * (MLA_NOPE + MLA_V)), MLA_KV_RANK),
        "g_q_nope": gain(ks[16], (L, MLA_NOPE)),
        "g_k_nope": gain(ks[17], (L, MLA_NOPE)),
        "g_q_rope": gain(ks[18], (L, MLA_ROPE)),
        "g_k_rope": gain(ks[19], (L, MLA_ROPE)),
        "w_out": dense(ks[20], (L, D_MIX, D), D_MIX),
        "g_ffn2": gain(ks[21], (L, D)),
        "w2_gate": dense(ks[22], (L, D, D_FF), D),
        "w2_up": dense(ks[23], (L, D, D_FF), D),
        "w2_down": dense(ks[24], (L, D_FF, D), D_FF),
    }


def reference(x, c, w_ada, b_ada, g_ffn1, w1_gate, w1_up, w1_down, g_mix, w_in,
              g_qa, g_ka, g_cq, g_ckv, w_uq, w_ukv, g_q_nope, g_k_nope, g_q_rope, g_k_rope,
              w_out, g_ffn2, w2_gate, w2_up, w2_down):
    b, s, _ = x.shape
    topk = min(TOPK_MAX, s // 4)
    cos_p, sin_p = rope_tables(s, PARTIAL_ROPE_DIM)
    cos_m, sin_m = rope_tables(s, MLA_ROPE)
    split_at = [int(v) for v in np.cumsum(SPLIT_SIZES)[:-1]]
    c_act = jax.nn.silu(c)

    for l in range(DEPTH):
        mod = c_act @ w_ada[l] + b_ada[l]
        (sh1, sc1, ga1, shm, scm, gam, sh2, sc2, ga2) = jnp.split(mod, N_MOD, axis=-1)

        h = modulate(rms_norm(x, g_ffn1[l]), sh1, sc1)
        x = x + 0.5 * (1.0 + ga1[:, None, :]) * swiglu(h, w1_gate[l], w1_up[l], w1_down[l])

        h = modulate(rms_norm(x, g_mix[l]), shm, scm)
        (qa, ka, va, qi, ki, wi, cq, ckv, kr, qc, kc, vc) = jnp.split(h @ w_in[l], split_at, axis=-1)

        qa = partial_rope(rms_norm(qa.reshape(b, s, A_HEADS, HEAD_DIM), g_qa[l]), cos_p, sin_p)
        ka = partial_rope(rms_norm(ka.reshape(b, s, A_HEADS, HEAD_DIM), g_ka[l]), cos_p, sin_p)
        va = va.reshape(b, s, A_HEADS, HEAD_DIM)
        out_a = dsa_attention(qa, ka, va, qi.reshape(b, s, IDX_HEADS, IDX_DIM), ki, wi, topk)

        qb = (rms_norm(cq, g_cq[l]) @ w_uq[l]).reshape(b, s, B_HEADS, MLA_NOPE + MLA_ROPE)
        kvb = (rms_norm(ckv, g_ckv[l]) @ w_ukv[l]).reshape(b, s, B_HEADS, MLA_NOPE + MLA_V)
        q_nope = rms_norm(qb[..., :MLA_NOPE], g_q_nope[l])
        q_rope = apply_rope(rms_norm(qb[..., MLA_NOPE:], g_q_rope[l]), cos_m, sin_m)
        k_nope = rms_norm(kvb[..., :MLA_NOPE], g_k_nope[l])
        vb = kvb[..., MLA_NOPE:]
        k_rope = apply_rope(rms_norm(kr, g_k_rope[l])[:, :, None, :], cos_m, sin_m)[:, :, 0]
        out_b = mla_attention(q_nope, q_rope, k_nope, k_rope, vb)

        out_c = stick_breaking_attention(qc.reshape(b, s, C_HEADS, HEAD_DIM),
                                         kc.reshape(b, s, C_HEADS, HEAD_DIM),
                                         vc.reshape(b, s, C_HEADS, HEAD_DIM))

        mixed = jnp.concatenate([out_a.reshape(b, s, -1), out_b.reshape(b, s, -1),
                                 out_c.reshape(b, s, -1)], axis=-1) @ w_out[l]
        x = x + (1.0 + gam[:, None, :]) * mixed

        h = modulate(rms_norm(x, g_ffn2[l]), sh2, sc2)
        x = x + 0.5 * (1.0 + ga2[:, None, :]) * swiglu(h, w2_gate[l], w2_up[l], w2_down[l])

    return x
```

```python
import functools

import jax
import jax.numpy as jnp
from jax import lax
from jax.experimental import pallas as pl
from jax.experimental.pallas import tpu as pltpu

F32 = jnp.float32
BF16 = jnp.bfloat16
I32 = jnp.int32

HEAD_DIM = 128
CHUNK = 64
CHUNK_SHIFT = 6
ROPE_THETA = 500000.0
PARTIAL_ROPE_DIM = HEAD_DIM // 4
NORM_EPS = 1e-6
D_FF = 5632
N_MOD = 9
A_HEADS = 4
IDX_HEADS = 16
IDX_DIM = 64
TOPK_MAX = 256
B_HEADS = 8
MLA_Q_RANK = 448
MLA_KV_RANK = 128
MLA_NOPE = 128
MLA_ROPE = 64
C_HEADS = 4

LANES = 128
KEY_BLOCK = 256
Q_RANK_PAD = 512
VMEM_LIMIT = 56 * 1024 * 1024

OFF_QA, OFF_KA, OFF_VA, OFF_QI = 0, 512, 1024, 1536
OFF_QC, OFF_KC, OFF_VC = 2560, 3072, 3584
OFF_CQ, OFF_CKV, OFF_KR, OFF_KIW = 4096, 4608, 4736, 4864
N_PROJ = 4992

A_SCALE = HEAD_DIM ** -0.5
B_SCALE = (MLA_NOPE + MLA_ROPE) ** -0.5
C_SCALE = HEAD_DIM ** -0.5
IDX_SCALE = (IDX_DIM ** -0.5) * (IDX_HEADS ** -0.5)

NT_DIMS = (((1,), (1,)), ((), ()))
NEG_INF = float("-inf")
INT_MIN = -2 ** 31


def _params(*sem):
    return pltpu.CompilerParams(dimension_semantics=sem, vmem_limit_bytes=VMEM_LIMIT)


def _ada_kernel(c_ref, w_ref, b_ref, o_ref):
    c = c_ref[...]
    ca = (c * jax.nn.sigmoid(c)).astype(BF16)
    o_ref[0] = jnp.dot(ca, w_ref[0].astype(BF16), preferred_element_type=F32) + b_ref[0]


def _ada(c, w_ada, b_ada):
    L, D, N = w_ada.shape
    B = c.shape[0]
    tn = 1024
    return pl.pallas_call(
        _ada_kernel,
        grid=(L, N // tn),
        in_specs=[
            pl.BlockSpec((B, D), lambda l, j: (0, 0)),
            pl.BlockSpec((1, D, tn), lambda l, j: (l, 0, j)),
            pl.BlockSpec((1, 1, tn), lambda l, j: (l, 0, j)),
        ],
        out_specs=pl.BlockSpec((1, B, tn), lambda l, j: (l, 0, j)),
        out_shape=jax.ShapeDtypeStruct((L, B, N), F32),
        compiler_params=_params("arbitrary", "arbitrary"),
        name="ada_mod",
    )(c, w_ada, b_ada.reshape(L, 1, N))


def _norm_mod(x, g, shift, scale):
    y = x * lax.rsqrt(jnp.mean(x * x, axis=-1, keepdims=True) + NORM_EPS)
    return (y * g) * (1.0 + scale) + shift


def _ffn_kernel(x_ref, mod_ref, g_ref, wg_ref, wu_ref, wd_ref, o_ref, h_ref, acc_ref, *, row):
    j = pl.program_id(1)

    @pl.when(j == 0)
    def _():
        h = _norm_mod(x_ref[...], g_ref[...], mod_ref[0, row:row + 1, :], mod_ref[0, row + 1:row + 2, :])
        h_ref[...] = h.astype(BF16)

    h = h_ref[...]
    g = jnp.dot(h, wg_ref[...], preferred_element_type=F32)
    u = jnp.dot(h, wu_ref[...], preferred_element_type=F32)
    a = ((g * jax.nn.sigmoid(g)) * u).astype(BF16)
    d = jnp.dot(a, wd_ref[...], preferred_element_type=F32)

    @pl.when(j == 0)
    def _():
        acc_ref[...] = d

    @pl.when(j > 0)
    def _():
        acc_ref[...] += d

    @pl.when(j == pl.num_programs(1) - 1)
    def _():
        gate = mod_ref[0, row + 2:row + 3, :]
        o_ref[...] = x_ref[...] + (0.5 * (1.0 + gate)) * acc_ref[...]


def _ffn(x2, mod, g, wg, wu, wd, row, seq):
    T, D = x2.shape
    F = wg.shape[1]
    tm = min(512, seq)
    tf = 512
    per_seq = seq // tm
    return pl.pallas_call(
        functools.partial(_ffn_kernel, row=row),
        grid=(T // tm, F // tf),
        in_specs=[
            pl.BlockSpec((tm, D), lambda i, j: (i, 0)),
            pl.BlockSpec((1, N_MOD, D), lambda i, j: (i // per_seq, 0, 0)),
            pl.BlockSpec((1, D), lambda i, j: (0, 0)),
            pl.BlockSpec((D, tf), lambda i, j: (0, j)),
            pl.BlockSpec((D, tf), lambda i, j: (0, j)),
            pl.BlockSpec((tf, D), lambda i, j: (j, 0)),
        ],
        out_specs=pl.BlockSpec((tm, D), lambda i, j: (i, 0)),
        out_shape=jax.ShapeDtypeStruct((T, D), F32),
        scratch_shapes=[pltpu.VMEM((tm, D), BF16), pltpu.VMEM((tm, D), F32)],
        compiler_params=_params("arbitrary", "arbitrary"),
        name="ffn",
    )(x2, mod, g.reshape(1, D), wg, wu, wd)


def _proj_kernel(x_ref, mod_ref, g_ref, w_ref, o_ref, h_ref):
    @pl.when(pl.program_id(1) == 0)
    def _():
        h = _norm_mod(x_ref[...], g_ref[...], mod_ref[0, 3:4, :], mod_ref[0, 4:5, :])
        h_ref[...] = h.astype(BF16)

    o_ref[...] = jnp.dot(h_ref[...], w_ref[...], preferred_element_type=F32)


def _proj(x2, mod, g, w, seq):
    T, D = x2.shape
    N = w.shape[1]
    tm = min(512, seq)
    tn = N // 3
    per_seq = seq // tm
    return pl.pallas_call(
        _proj_kernel,
        grid=(T // tm, N // tn),
        in_specs=[
            pl.BlockSpec((tm, D), lambda i, j: (i, 0)),
            pl.BlockSpec((1, N_MOD, D), lambda i, j: (i // per_seq, 0, 0)),
            pl.BlockSpec((1, D), lambda i, j: (0, 0)),
            pl.BlockSpec((D, tn), lambda i, j: (0, j)),
        ],
        out_specs=pl.BlockSpec((tm, tn), lambda i, j: (i, j)),
        out_shape=jax.ShapeDtypeStruct((T, N), F32),
        scratch_shapes=[pltpu.VMEM((tm, D), BF16)],
        compiler_params=_params("arbitrary", "arbitrary"),
        name="in_proj",
    )(x2, mod, g.reshape(1, D), w)


def _rms_lanes(x, g, n):
    return x * lax.rsqrt(jnp.sum(x * x, axis=-1, keepdims=True) / n + NORM_EPS) * g


def _rope_lanes(x, cos, sin_lo, sin_hi, half):
    return (x * cos + pltpu.roll(x, LANES - half, 1) * sin_lo + pltpu.roll(x, half, 1) * sin_hi)


def _prep_kernel(p_ref, tab_ref, gqa_ref, gka_ref, gcq_ref, gckv_ref, gqn_ref, gkn_ref, gqr_ref, gkr_ref,
                 wuq_ref, wukv_ref,
                 qa_o, ka_o, vat_o, qi_o, ki_o, wt_o, qcat_o, kcat_o, vbt_o, qc_o, kc_o, vct_o):
    ca, sa_lo, sa_hi = tab_ref[0], tab_ref[1], tab_ref[2]
    cm, sm_lo, sm_hi = tab_ref[3], tab_ref[4], tab_ref[5]
    half_a = PARTIAL_ROPE_DIM // 2
    half_m = MLA_ROPE // 2

    for h in range(A_HEADS):
        lo = h * HEAD_DIM
        q = _rms_lanes(p_ref[:, OFF_QA + lo:OFF_QA + lo + HEAD_DIM], gqa_ref[...], HEAD_DIM)
        qa_o[0, :, lo:lo + HEAD_DIM] = (_rope_lanes(q, ca, sa_lo, sa_hi, half_a) * A_SCALE).astype(BF16)
        k = _rms_lanes(p_ref[:, OFF_KA + lo:OFF_KA + lo + HEAD_DIM], gka_ref[...], HEAD_DIM)
        ka_o[0, :, lo:lo + HEAD_DIM] = _rope_lanes(k, ca, sa_lo, sa_hi, half_a).astype(BF16)
    vat_o[0, 0] = p_ref[:, OFF_VA:OFF_VA + A_HEADS * HEAD_DIM].T.astype(BF16)
    qi_o[0] = p_ref[:, OFF_QI:OFF_QI + IDX_HEADS * IDX_DIM].astype(BF16)
    kiw = p_ref[:, OFF_KIW:OFF_KIW + LANES]
    ki_o[0] = kiw[:, :IDX_DIM].astype(BF16)
    wt_o[0] = kiw.T[IDX_DIM:IDX_DIM + IDX_HEADS, :] * IDX_SCALE

    width_c = C_HEADS * HEAD_DIM
    qc_o[0] = (p_ref[:, OFF_QC:OFF_QC + width_c] * C_SCALE).astype(BF16)
    kc_o[0] = p_ref[:, OFF_KC:OFF_KC + width_c].astype(BF16)
    vct_o[0, 0] = p_ref[:, OFF_VC:OFF_VC + width_c].T.astype(BF16)

    cq = _rms_lanes(p_ref[:, OFF_CQ:OFF_CQ + Q_RANK_PAD], gcq_ref[...], MLA_Q_RANK)
    qb = jnp.dot(cq.astype(BF16), wuq_ref[...], preferred_element_type=F32)
    ckv = _rms_lanes(p_ref[:, OFF_CKV:OFF_CKV + MLA_KV_RANK], gckv_ref[...], MLA_KV_RANK)
    kvb = jnp.dot(ckv.astype(BF16), wukv_ref[...], preferred_element_type=F32)
    kr = _rms_lanes(p_ref[:, OFF_KR:OFF_KR + LANES], gkr_ref[...], MLA_ROPE)
    kr = _rope_lanes(kr, cm, sm_lo, sm_hi, half_m).astype(BF16)
    for h in range(B_HEADS):
        lo = h * 2 * LANES
        qn = _rms_lanes(qb[:, lo:lo + MLA_NOPE], gqn_ref[...], MLA_NOPE)
        qcat_o[0, :, lo:lo + MLA_NOPE] = (qn * B_SCALE).astype(BF16)
        qr = _rms_lanes(qb[:, lo + MLA_NOPE:lo + 2 * LANES], gqr_ref[...], MLA_ROPE)
        qcat_o[0, :, lo + MLA_NOPE:lo + 2 * LANES] = (
            _rope_lanes(qr, cm, sm_lo, sm_hi, half_m) * B_SCALE).astype(BF16)
        kn = _rms_lanes(kvb[:, h * MLA_NOPE:(h + 1) * MLA_NOPE], gkn_ref[...], MLA_NOPE)
        kcat_o[0, :, lo:lo + MLA_NOPE] = kn.astype(BF16)
        kcat_o[0, :, lo + MLA_NOPE:lo + 2 * LANES] = kr
    vbt_o[0, 0] = kvb[:, B_HEADS * MLA_NOPE:].T.astype(BF16)


def _prep(proj, tabs, gains, wuq, wukv, batch, seq):
    tm = KEY_BLOCK
    nt = seq // tm
    p3 = proj.reshape(batch * seq, N_PROJ)
    wa, wb, wc = A_HEADS * HEAD_DIM, B_HEADS * HEAD_DIM, C_HEADS * HEAD_DIM
    wcat = B_HEADS * 2 * LANES

    def tok(width, dtype):
        return (jax.ShapeDtypeStruct((batch, seq, width), dtype),
                pl.BlockSpec((1, tm, width), lambda b, i: (b, i, 0)))

    def tok_t(width):
        return (jax.ShapeDtypeStruct((batch, nt, width, tm), BF16),
                pl.BlockSpec((1, 1, width, tm), lambda b, i: (b, i, 0, 0)))

    outs = [tok(wa, BF16), tok(wa, BF16), tok_t(wa), tok(IDX_HEADS * IDX_DIM, BF16), tok(IDX_DIM, BF16),
            (jax.ShapeDtypeStruct((batch, IDX_HEADS, seq), F32),
             pl.BlockSpec((1, IDX_HEADS, tm), lambda b, i: (b, 0, i))),
            tok(wcat, BF16), tok(wcat, BF16), tok_t(wb), tok(wc, BF16), tok(wc, BF16), tok_t(wc)]
    gain_specs = [pl.BlockSpec(g.shape, lambda b, i: (0, 0)) for g in gains]
    return pl.pallas_call(
        _prep_kernel,
        grid=(batch, nt),
        in_specs=[pl.BlockSpec((tm, N_PROJ), lambda b, i: (b * nt + i, 0)),
                  pl.BlockSpec((6, tm, LANES), lambda b, i: (0, i, 0))] + gain_specs + [
                  pl.BlockSpec(wuq.shape, lambda b, i: (0, 0)),
                  pl.BlockSpec(wukv.shape, lambda b, i: (0, 0))],
        out_specs=[o[1] for o in outs],
        out_shape=[o[0] for o in outs],
        compiler_params=_params("arbitrary", "arbitrary"),
        name="head_prep",
    )(p3, tabs, *gains, wuq, wukv)


def _mla_kernel(q_ref, k_ref, vt_ref, o_ref, m_ref, l_ref, acc_ref, *, t):
    qb = pl.program_id(2)
    q = q_ref[0]
    m_ref[...] = jnp.full(m_ref.shape, NEG_INF, F32)
    l_ref[...] = jnp.zeros(l_ref.shape, F32)
    acc_ref[...] = jnp.zeros(acc_ref.shape, F32)

    def step(kb, diagonal):
        k = k_ref[0, pl.ds(pl.multiple_of(kb * t, t), t), :]
        s = lax.dot_general(k, q, NT_DIMS, preferred_element_type=F32)
        if diagonal:
            kc = lax.broadcasted_iota(I32, (t, t), 0) >> CHUNK_SHIFT
            qc = lax.broadcasted_iota(I32, (t, t), 1) >> CHUNK_SHIFT
            s = jnp.where(kc <= qc, s, NEG_INF)
        m_prev = m_ref[...]
        m_new = jnp.maximum(m_prev, jnp.max(s, axis=0, keepdims=True))
        p = jnp.exp(s - m_new)
        alpha = jnp.exp(m_prev - m_new)
        l_ref[...] = alpha * l_ref[...] + jnp.sum(p, axis=0, keepdims=True)
        acc_ref[...] = alpha * acc_ref[...] + jnp.dot(vt_ref[0, kb], p.astype(BF16),
                                                     preferred_element_type=F32)
        m_ref[...] = m_new

    def body(kb, carry):
        step(kb, False)
        return carry

    lax.fori_loop(0, qb, body, 0)
    step(qb, True)
    o_ref[0] = (acc_ref[...] / l_ref[...]).T.astype(BF16)


def _mla(qcat, kcat, vbt, batch, seq):
    t = KEY_BLOCK
    nt = seq // t
    return pl.pallas_call(
        functools.partial(_mla_kernel, t=t),
        grid=(batch, B_HEADS, nt),
        in_specs=[
            pl.BlockSpec((1, t, 2 * LANES), lambda b, h, i: (b, i, h)),
            pl.BlockSpec((1, seq, 2 * LANES), lambda b, h, i: (b, 0, h)),
            pl.BlockSpec((1, nt, HEAD_DIM, t), lambda b, h, i: (b, 0, h, 0)),
        ],
        out_specs=pl.BlockSpec((1, t, HEAD_DIM), lambda b, h, i: (b, i, h)),
        out_shape=jax.ShapeDtypeStruct((batch, seq, B_HEADS * HEAD_DIM), BF16),
        scratch_shapes=[pltpu.VMEM((1, t), F32), pltpu.VMEM((1, t), F32), pltpu.VMEM((HEAD_DIM, t), F32)],
        compiler_params=_params("arbitrary", "arbitrary", "arbitrary"),
        name="mla_attn",
    )(qcat, kcat, vbt)


def _sb_kernel(q_ref, k_ref, vt_ref, o_ref, r_ref, acc_ref, *, t):
    qb = pl.program_id(2)
    q = q_ref[0]
    row = lax.broadcasted_iota(I32, (t, t), 0)
    col = lax.broadcasted_iota(I32, (t, t), 1)
    later_keys = jnp.where(col > row, 1.0, 0.0).astype(BF16)
    r_ref[...] = jnp.zeros(r_ref.shape, F32)
    acc_ref[...] = jnp.zeros(acc_ref.shape, F32)

    def step(kb, diagonal):
        k = k_ref[0, pl.ds(pl.multiple_of(kb * t, t), t), :]
        z = lax.dot_general(k, q, NT_DIMS, preferred_element_type=F32)
        soft = jnp.log1p(jnp.exp(-jnp.abs(z)))
        log_stay = -jnp.maximum(z, 0.0) - soft
        log_take = jnp.minimum(z, 0.0) - soft
        if diagonal:
            before = row < col
            log_stay = jnp.where(before, log_stay, 0.0)
        hi = log_stay.astype(BF16)
        lo = (log_stay - hi.astype(F32)).astype(BF16)
        later = (jnp.dot(later_keys, hi, preferred_element_type=F32)
                 + jnp.dot(later_keys, lo, preferred_element_type=F32))
        w = jnp.exp(log_take + later + r_ref[...])
        if diagonal:
            w = jnp.where(before, w, 0.0)
        acc_ref[...] += jnp.dot(vt_ref[0, kb], w.astype(BF16), preferred_element_type=F32)
        r_ref[...] += jnp.sum(log_stay, axis=0, keepdims=True)

    step(qb, True)

    def body(j, carry):
        step(qb - 1 - j, False)
        return carry

    lax.fori_loop(0, qb, body, 0)
    o_ref[0] = acc_ref[...].T.astype(BF16)


def _sb(qc, kc, vct, batch, seq):
    t = KEY_BLOCK
    nt = seq // t
    return pl.pallas_call(
        functools.partial(_sb_kernel, t=t),
        grid=(batch, C_HEADS, nt),
        in_specs=[
            pl.BlockSpec((1, t, HEAD_DIM), lambda b, h, i: (b, i, h)),
            pl.BlockSpec((1, seq, HEAD_DIM), lambda b, h, i: (b, 0, h)),
            pl.BlockSpec((1, nt, HEAD_DIM, t), lambda b, h, i: (b, 0, h, 0)),
        ],
        out_specs=pl.BlockSpec((1, t, HEAD_DIM), lambda b, h, i: (b, i, h)),
        out_shape=jax.ShapeDtypeStruct((batch, seq, C_HEADS * HEAD_DIM), BF16),
        scratch_shapes=[pltpu.VMEM((1, t), F32), pltpu.VMEM((HEAD_DIM, t), F32)],
        compiler_params=_params("arbitrary", "arbitrary", "arbitrary"),
        name="sb_attn",
    )(qc, kc, vct)


def _dsa_kernel(qi_ref, ki_ref, wt_ref, qa_ref, ka_ref, vt_ref, o_ref,
                keys_ref, thr_ref, lim_ref, m_ref, l_ref, acc_ref, *, t, topk, seq):
    qb = pl.program_id(1)
    nk = qb + 1
    row = lax.broadcasted_iota(I32, (t, t), 0)
    col = lax.broadcasted_iota(I32, (t, t), 1)
    wt = wt_ref[0]

    def score_block(kb, carry):
        ks = pl.multiple_of(kb * t, t)
        kix = ki_ref[0, pl.ds(ks, t), :]
        sc = jnp.zeros((t, t), F32)
        for h in range(IDX_HEADS):
            lg = lax.dot_general(kix, qi_ref[0, :, h * IDX_DIM:(h + 1) * IDX_DIM], NT_DIMS,
                                 preferred_element_type=F32)
            sc = sc + wt[h:h + 1, :] * jnp.maximum(lg, 0.0)
        bits = lax.bitcast_convert_type(sc, I32)
        key = bits ^ ((bits >> 31) & 0x7FFFFFFF)
        visible = ((ks + row) >> CHUNK_SHIFT) <= ((qb * t + col) >> CHUNK_SHIFT)
        keys_ref[kb] = jnp.where(visible, key, INT_MIN)
        return carry

    lax.fori_loop(0, nk, score_block, 0)

    def count(pred):
        def block(kb, acc):
            hit = jnp.where(pred(keys_ref[kb], kb * t + row), 1, 0).astype(I32)
            return acc + jnp.sum(hit.reshape(t // 8, 8, t), axis=0)
        acc = lax.fori_loop(0, nk, block, jnp.zeros((8, t), I32))
        return jnp.sum(acc, axis=0, keepdims=True)

    thr0 = jnp.where(count(lambda k, i: k >= 0) >= topk, 0, INT_MIN).astype(I32)

    def thr_bit(b, thr):
        cand = thr | lax.shift_left(jnp.int32(1), 30 - b)
        return jnp.where(count(lambda k, i: k >= cand) >= topk, cand, thr)

    thr = lax.fori_loop(0, 31, thr_bit, thr0)
    n_ge = count(lambda k, i: k >= thr)
    n_gt = count(lambda k, i: k > thr)
    tie = (thr != INT_MIN) & (n_ge > topk)
    need = topk - n_gt
    thr_ref[...] = thr
    lim_ref[...] = jnp.where(thr == INT_MIN, 0, seq).astype(I32)

    @pl.when(jnp.max(tie.astype(I32)) > 0)
    def _():
        nbits = seq.bit_length() - 1

        def lim_bit(b, lim):
            cand = lim | lax.shift_left(jnp.int32(1), nbits - 1 - b)
            below = count(lambda k, i: (k == thr) & (i < cand))
            return jnp.where(below < need, cand, lim)

        lim = lax.fori_loop(0, nbits, lim_bit, jnp.zeros((1, t), I32))
        lim_ref[...] = jnp.where(tie, lim + 1, lim_ref[...])

    m_ref[...] = jnp.full(m_ref.shape, NEG_INF, F32)
    l_ref[...] = jnp.zeros(l_ref.shape, F32)
    acc_ref[...] = jnp.zeros(acc_ref.shape, F32)

    def attend_block(kb, carry):
        ks = pl.multiple_of(kb * t, t)
        key = keys_ref[kb]
        thr_q = thr_ref[...]
        sel = (key > thr_q) | ((key == thr_q) & ((ks + row) < lim_ref[...]))
        bias = jnp.where(sel, 0.0, NEG_INF)
        for h in range(A_HEADS):
            lo = h * HEAD_DIM
            s = lax.dot_general(ka_ref[0, pl.ds(ks, t), lo:lo + HEAD_DIM], qa_ref[0, :, lo:lo + HEAD_DIM],
                                NT_DIMS, preferred_element_type=F32) + bias
            m_prev = m_ref[h]
            m_new = jnp.maximum(m_prev, jnp.max(s, axis=0, keepdims=True))
            m_safe = jnp.where(m_new == NEG_INF, 0.0, m_new)
            p = jnp.exp(s - m_safe)
            alpha = jnp.exp(m_prev - m_safe)
            l_ref[h] = alpha * l_ref[h] + jnp.sum(p, axis=0, keepdims=True)
            acc_ref[h] = alpha * acc_ref[h] + jnp.dot(vt_ref[0, kb, lo:lo + HEAD_DIM, :], p.astype(BF16),
                                                      preferred_element_type=F32)
            m_ref[h] = m_new
        return carry

    lax.fori_loop(0, nk, attend_block, 0)
    for h in range(A_HEADS):
        o_ref[0, :, h * HEAD_DIM:(h + 1) * HEAD_DIM] = (acc_ref[h] / l_ref[h]).T.astype(BF16)


def _dsa(qi, ki, wt, qa, ka, vat, batch, seq, topk):
    t = KEY_BLOCK
    nt = seq // t
    wa = A_HEADS * HEAD_DIM
    return pl.pallas_call(
        functools.partial(_dsa_kernel, t=t, topk=topk, seq=seq),
        grid=(batch, nt),
        in_specs=[
            pl.BlockSpec((1, t, IDX_HEADS * IDX_DIM), lambda b, i: (b, i, 0)),
            pl.BlockSpec((1, seq, IDX_DIM), lambda b, i: (b, 0, 0)),
            pl.BlockSpec((1, IDX_HEADS, t), lambda b, i: (b, 0, i)),
            pl.BlockSpec((1, t, wa), lambda b, i: (b, i, 0)),
            pl.BlockSpec((1, seq, wa), lambda b, i: (b, 0, 0)),
            pl.BlockSpec((1, nt, wa, t), lambda b, i: (b, 0, 0, 0)),
        ],
        out_specs=pl.BlockSpec((1, t, wa), lambda b, i: (b, i, 0)),
        out_shape=jax.ShapeDtypeStruct((batch, seq, wa), BF16),
        scratch_shapes=[
            pltpu.VMEM((nt, t, t), I32),
            pltpu.VMEM((1, t), I32), pltpu.VMEM((1, t), I32),
            pltpu.VMEM((A_HEADS, 1, t), F32), pltpu.VMEM((A_HEADS, 1, t), F32),
            pltpu.VMEM((A_HEADS, HEAD_DIM, t), F32),
        ],
        compiler_params=_params("arbitrary", "arbitrary"),
        name="dsa_attn",
    )(qi, ki, wt, qa, ka, vat)


def _out_kernel(x_ref, mod_ref, oa_ref, ob_ref, oc_ref, w_ref, o_ref):
    wa = A_HEADS * HEAD_DIM
    wb = B_HEADS * HEAD_DIM
    mixed = (jnp.dot(oa_ref[...], w_ref[0:wa, :], preferred_element_type=F32)
             + jnp.dot(ob_ref[...], w_ref[wa:wa + wb, :], preferred_element_type=F32)
             + jnp.dot(oc_ref[...], w_ref[wa + wb:, :], preferred_element_type=F32))
    o_ref[...] = x_ref[...] + (1.0 + mod_ref[0, 5:6, :]) * mixed


def _out_proj(x2, mod, oa, ob, oc, w, seq):
    T, D = x2.shape
    tm = min(512, seq)
    per_seq = seq // tm
    return pl.pallas_call(
        _out_kernel,
        grid=(T // tm,),
        in_specs=[
            pl.BlockSpec((tm, D), lambda i: (i, 0)),
            pl.BlockSpec((1, N_MOD, D), lambda i: (i // per_seq, 0, 0)),
            pl.BlockSpec((tm, oa.shape[1]), lambda i: (i, 0)),
            pl.BlockSpec((tm, ob.shape[1]), lambda i: (i, 0)),
            pl.BlockSpec((tm, oc.shape[1]), lambda i: (i, 0)),
            pl.BlockSpec(w.shape, lambda i: (0, 0)),
        ],
        out_specs=pl.BlockSpec((tm, D), lambda i: (i, 0)),
        out_shape=jax.ShapeDtypeStruct((T, D), F32),
        compiler_params=_params("arbitrary"),
        name="out_proj",
    )(x2, mod, oa, ob, oc, w)


def _rope_tables(seq):
    def tables(dim):
        inv = 1.0 / (ROPE_THETA ** (jnp.arange(0, dim, 2, dtype=F32) / dim))
        ang = jnp.arange(seq, dtype=F32)[:, None] * inv[None, :]
        return jnp.cos(ang), jnp.sin(ang)

    def lane_tables(cos, sin, fill):
        half = cos.shape[1]
        rest = LANES - 2 * half
        zeros_h = jnp.zeros((seq, half), F32)
        zeros_r = jnp.zeros((seq, rest), F32)
        return [jnp.concatenate([cos, cos, jnp.full((seq, rest), fill, F32)], axis=1),
                jnp.concatenate([-sin, zeros_h, zeros_r], axis=1),
                jnp.concatenate([zeros_h, sin, zeros_r], axis=1)]

    cos_p, sin_p = tables(PARTIAL_ROPE_DIM)
    cos_m, sin_m = tables(MLA_ROPE)
    return jnp.stack(lane_tables(cos_p, sin_p, 1.0) + lane_tables(cos_m, sin_m, 0.0))


def _pad_cols(a, width):
    return jnp.pad(a, ((0, 0), (0, width - a.shape[1])))


def _layer_weights(w_in, w_uq, w_ukv):
    w_in_p = jnp.concatenate([
        w_in[:, 0:2560],
        w_in[:, 3280:4816],
        _pad_cols(w_in[:, 2640:3088], Q_RANK_PAD),
        w_in[:, 3088:3216],
        _pad_cols(w_in[:, 3216:3280], LANES),
        _pad_cols(w_in[:, 2560:2640], LANES),
    ], axis=1).astype(BF16)
    wuq = w_uq.reshape(MLA_Q_RANK, B_HEADS, MLA_NOPE + MLA_ROPE)
    wuq = jnp.pad(wuq, ((0, Q_RANK_PAD - MLA_Q_RANK), (0, 0), (0, 2 * LANES - MLA_NOPE - MLA_ROPE)))
    wuq = wuq.reshape(Q_RANK_PAD, B_HEADS * 2 * LANES).astype(BF16)
    wukv = w_ukv.reshape(MLA_KV_RANK, B_HEADS, MLA_NOPE + HEAD_DIM)
    wukv = jnp.concatenate([wukv[:, :, :MLA_NOPE].reshape(MLA_KV_RANK, -1),
                            wukv[:, :, MLA_NOPE:].reshape(MLA_KV_RANK, -1)], axis=1).astype(BF16)
    return w_in_p, wuq, wukv


def kernel(x, c, w_ada, b_ada, g_ffn1, w1_gate, w1_up, w1_down, g_mix, w_in, g_qa, g_ka, g_cq, g_ckv, w_uq, w_ukv, g_q_nope, g_k_nope, g_q_rope, g_k_rope, w_out, g_ffn2, w2_gate, w2_up, w2_down):
    batch, seq, d_model = x.shape
    depth = w_ada.shape[0]
    topk = min(TOPK_MAX, seq // 4)
    tabs = _rope_tables(seq)
    mods = _ada(c, w_ada, b_ada).reshape(depth, batch, N_MOD, d_model)
    x2 = x.reshape(batch * seq, d_model)

    for l in range(depth):
        mod = mods[l]
        x2 = _ffn(x2, mod, g_ffn1[l], w1_gate[l].astype(BF16), w1_up[l].astype(BF16),
                  w1_down[l].astype(BF16), 0, seq)

        w_in_p, wuq, wukv = _layer_weights(w_in[l], w_uq[l], w_ukv[l])
        proj = _proj(x2, mod, g_mix[l], w_in_p, seq)
        gains = [g_qa[l][None, :], g_ka[l][None, :], _pad_cols(g_cq[l][None, :], Q_RANK_PAD),
                 g_ckv[l][None, :], g_q_nope[l][None, :], g_k_nope[l][None, :],
                 _pad_cols(g_q_rope[l][None, :], LANES), _pad_cols(g_k_rope[l][None, :], LANES)]
        (qa, ka, vat, qi, ki, wt, qcat, kcat, vbt, qc, kc, vct) = _prep(
            proj, tabs, gains, wuq, wukv, batch, seq)

        out_a = _dsa(qi, ki, wt, qa, ka, vat, batch, seq, topk)
        out_b = _mla(qcat, kcat, vbt, batch, seq)
        out_c = _sb(qc, kc, vct, batch, seq)
        x2 = _out_proj(x2, mod, out_a.reshape(batch * seq, -1), out_b.reshape(batch * seq, -1),
                       out_c.reshape(batch * seq, -1), w_out[l].astype(BF16), seq)

        x2 = _ffn(x2, mod, g_ffn2[l], w2_gate[l].astype(BF16), w2_up[l].astype(BF16),
                  w2_down[l].astype(BF16), 6, seq)

    return x2.reshape(batch, seq, d_model)
```

```python
import functools

import jax
import jax.numpy as jnp
from jax import lax
from jax.experimental import pallas as pl
from jax.experimental.pallas import tpu as pltpu

F32 = jnp.float32
BF16 = jnp.bfloat16
I32 = jnp.int32

HEAD_DIM = 128
CHUNK = 64
CHUNK_SHIFT = 6
ROPE_THETA = 500000.0
PARTIAL_ROPE_DIM = HEAD_DIM // 4
NORM_EPS = 1e-6
D_FF = 5632
N_MOD = 9
A_HEADS = 4
IDX_HEADS = 16
IDX_DIM = 64
TOPK_MAX = 256
B_HEADS = 8
MLA_Q_RANK = 448
MLA_KV_RANK = 128
MLA_NOPE = 128
MLA_ROPE = 64
C_HEADS = 4

LANES = 128
KEY_BLOCK = 256
Q_RANK_PAD = 512
VMEM_LIMIT = 56 * 1024 * 1024

OFF_QA, OFF_KA, OFF_VA, OFF_QI = 0, 512, 1024, 1536
OFF_QC, OFF_KC, OFF_VC = 2560, 3072, 3584
OFF_CQ, OFF_CKV, OFF_KR, OFF_KIW = 4096, 4608, 4736, 4864
N_PROJ = 4992

LOG2E = 1.4426950408889634
A_SCALE = HEAD_DIM ** -0.5 * LOG2E
B_SCALE = (MLA_NOPE + MLA_ROPE) ** -0.5 * LOG2E
C_SCALE = HEAD_DIM ** -0.5
IDX_SCALE = (IDX_DIM ** -0.5) * (IDX_HEADS ** -0.5)

NT_DIMS = (((1,), (1,)), ((), ()))
NEG_INF = float("-inf")
INT_MIN = -2 ** 31


def _params(*sem):
    return pltpu.CompilerParams(dimension_semantics=sem, vmem_limit_bytes=VMEM_LIMIT)


def _ada_kernel(c_ref, w_ref, b_ref, o_ref):
    c = c_ref[...]
    ca = (c * jax.nn.sigmoid(c)).astype(BF16)
    o_ref[0] = jnp.dot(ca, w_ref[0].astype(BF16), preferred_element_type=F32) + b_ref[0]


def _ada(c, w_ada, b_ada):
    L, D, N = w_ada.shape
    B = c.shape[0]
    tn = 1024
    return pl.pallas_call(
        _ada_kernel,
        grid=(L, N // tn),
        in_specs=[
            pl.BlockSpec((B, D), lambda l, j: (0, 0)),
            pl.BlockSpec((1, D, tn), lambda l, j: (l, 0, j)),
            pl.BlockSpec((1, 1, tn), lambda l, j: (l, 0, j)),
        ],
        out_specs=pl.BlockSpec((1, B, tn), lambda l, j: (l, 0, j)),
        out_shape=jax.ShapeDtypeStruct((L, B, N), F32),
        compiler_params=_params("arbitrary", "arbitrary"),
        name="ada_mod",
    )(c, w_ada, b_ada.reshape(L, 1, N))


def _norm_mod(x, g, shift, scale):
    y = x * lax.rsqrt(jnp.mean(x * x, axis=-1, keepdims=True) + NORM_EPS)
    return (y * g) * (1.0 + scale) + shift


def _ffn_kernel(x_ref, mod_ref, g_ref, wg_ref, wu_ref, wd_ref, o_ref, h_ref, acc_ref, *, row):
    j = pl.program_id(1)

    @pl.when(j == 0)
    def _():
        h = _norm_mod(x_ref[...], g_ref[...], mod_ref[0, row:row + 1, :], mod_ref[0, row + 1:row + 2, :])
        h_ref[...] = h.astype(BF16)
        acc_ref[...] = jnp.zeros(acc_ref.shape, F32)

    h = h_ref[...]
    g = jnp.dot(h, wg_ref[...], preferred_element_type=F32)
    u = jnp.dot(h, wu_ref[...], preferred_element_type=F32)
    a = ((g * jax.nn.sigmoid(g)) * u).astype(BF16)
    acc_ref[...] += jnp.dot(a, wd_ref[...], preferred_element_type=F32)

    @pl.when(j == pl.num_programs(1) - 1)
    def _():
        gate = mod_ref[0, row + 2:row + 3, :]
        o_ref[...] = x_ref[...] + (0.5 * (1.0 + gate)) * acc_ref[...]


def _ffn(x2, mod, g, wg, wu, wd, row, seq):
    T, D = x2.shape
    F = wg.shape[1]
    tm = min(512, seq)
    tf = 512
    per_seq = seq // tm
    return pl.pallas_call(
        functools.partial(_ffn_kernel, row=row),
        grid=(T // tm, F // tf),
        in_specs=[
            pl.BlockSpec((tm, D), lambda i, j: (i, 0)),
            pl.BlockSpec((1, N_MOD, D), lambda i, j: (i // per_seq, 0, 0)),
            pl.BlockSpec((1, D), lambda i, j: (0, 0)),
            pl.BlockSpec((D, tf), lambda i, j: (0, j)),
            pl.BlockSpec((D, tf), lambda i, j: (0, j)),
            pl.BlockSpec((tf, D), lambda i, j: (j, 0)),
        ],
        out_specs=pl.BlockSpec((tm, D), lambda i, j: (i, 0)),
        out_shape=jax.ShapeDtypeStruct((T, D), F32),
        scratch_shapes=[pltpu.VMEM((tm, D), BF16), pltpu.VMEM((tm, D), F32)],
        compiler_params=_params("arbitrary", "arbitrary"),
        name="ffn",
    )(x2, mod, g.reshape(1, D), wg, wu, wd)


def _proj_kernel(x_ref, mod_ref, g_ref, w_ref, o_ref, h_ref):
    @pl.when(pl.program_id(1) == 0)
    def _():
        h = _norm_mod(x_ref[...], g_ref[...], mod_ref[0, 3:4, :], mod_ref[0, 4:5, :])
        h_ref[...] = h.astype(BF16)

    o_ref[...] = jnp.dot(h_ref[...], w_ref[...], preferred_element_type=F32)


def _proj(x2, mod, g, w, seq):
    T, D = x2.shape
    N = w.shape[1]
    tm = min(512, seq)
    tn = N // 3
    per_seq = seq // tm
    return pl.pallas_call(
        _proj_kernel,
        grid=(T // tm, N // tn),
        in_specs=[
            pl.BlockSpec((tm, D), lambda i, j: (i, 0)),
            pl.BlockSpec((1, N_MOD, D), lambda i, j: (i // per_seq, 0, 0)),
            pl.BlockSpec((1, D), lambda i, j: (0, 0)),
            pl.BlockSpec((D, tn), lambda i, j: (0, j)),
        ],
        out_specs=pl.BlockSpec((tm, tn), lambda i, j: (i, j)),
        out_shape=jax.ShapeDtypeStruct((T, N), F32),
        scratch_shapes=[pltpu.VMEM((tm, D), BF16)],
        compiler_params=_params("arbitrary", "arbitrary"),
        name="in_proj",
    )(x2, mod, g.reshape(1, D), w)


def _rms_lanes(x, g, n):
    return x * lax.rsqrt(jnp.sum(x * x, axis=-1, keepdims=True) / n + NORM_EPS) * g


def _rope_lanes(x, cos, sin_lo, sin_hi, half):
    return (x * cos + pltpu.roll(x, LANES - half, 1) * sin_lo + pltpu.roll(x, half, 1) * sin_hi)


def _prep_kernel(p_ref, tab_ref, gqa_ref, gka_ref, gcq_ref, gckv_ref, gqn_ref, gkn_ref, gqr_ref, gkr_ref,
                 wuq_ref, wukv_ref,
                 qa_o, ka_o, vat_o, qi_o, ki_o, wt_o, qcat_o, kcat_o, vbt_o, qc_o, kc_o, vct_o):
    ca, sa_lo, sa_hi = tab_ref[0], tab_ref[1], tab_ref[2]
    cm, sm_lo, sm_hi = tab_ref[3], tab_ref[4], tab_ref[5]
    half_a = PARTIAL_ROPE_DIM // 2
    half_m = MLA_ROPE // 2

    for h in range(A_HEADS):
        lo = h * HEAD_DIM
        q = _rms_lanes(p_ref[:, OFF_QA + lo:OFF_QA + lo + HEAD_DIM], gqa_ref[...], HEAD_DIM)
        qa_o[0, :, lo:lo + HEAD_DIM] = (_rope_lanes(q, ca, sa_lo, sa_hi, half_a) * A_SCALE).astype(BF16)
        k = _rms_lanes(p_ref[:, OFF_KA + lo:OFF_KA + lo + HEAD_DIM], gka_ref[...], HEAD_DIM)
        ka_o[0, :, lo:lo + HEAD_DIM] = _rope_lanes(k, ca, sa_lo, sa_hi, half_a).astype(BF16)
    vat_o[0, 0] = p_ref[:, OFF_VA:OFF_VA + A_HEADS * HEAD_DIM].T.astype(BF16)
    qi_o[0] = p_ref[:, OFF_QI:OFF_QI + IDX_HEADS * IDX_DIM].astype(BF16)
    kiw = p_ref[:, OFF_KIW:OFF_KIW + LANES]
    ki_o[0] = kiw[:, :IDX_DIM].astype(BF16)
    wt_o[0] = kiw.T[IDX_DIM:IDX_DIM + IDX_HEADS, :] * IDX_SCALE

    width_c = C_HEADS * HEAD_DIM
    qc_o[0] = (p_ref[:, OFF_QC:OFF_QC + width_c] * C_SCALE).astype(BF16)
    kc_o[0] = p_ref[:, OFF_KC:OFF_KC + width_c].astype(BF16)
    vct_o[0, 0] = p_ref[:, OFF_VC:OFF_VC + width_c].T.astype(BF16)

    cq = _rms_lanes(p_ref[:, OFF_CQ:OFF_CQ + Q_RANK_PAD], gcq_ref[...], MLA_Q_RANK)
    qb = jnp.dot(cq.astype(BF16), wuq_ref[...], preferred_element_type=F32)
    ckv = _rms_lanes(p_ref[:, OFF_CKV:OFF_CKV + MLA_KV_RANK], gckv_ref[...], MLA_KV_RANK)
    kvb = jnp.dot(ckv.astype(BF16), wukv_ref[...], preferred_element_type=F32)
    kr = _rms_lanes(p_ref[:, OFF_KR:OFF_KR + LANES], gkr_ref[...], MLA_ROPE)
    kr = _rope_lanes(kr, cm, sm_lo, sm_hi, half_m).astype(BF16)
    for h in range(B_HEADS):
        lo = h * 2 * LANES
        qn = _rms_lanes(qb[:, lo:lo + MLA_NOPE], gqn_ref[...], MLA_NOPE)
        qcat_o[0, :, lo:lo + MLA_NOPE] = (qn * B_SCALE).astype(BF16)
        qr = _rms_lanes(qb[:, lo + MLA_NOPE:lo + 2 * LANES], gqr_ref[...], MLA_ROPE)
        qcat_o[0, :, lo + MLA_NOPE:lo + 2 * LANES] = (
            _rope_lanes(qr, cm, sm_lo, sm_hi, half_m) * B_SCALE).astype(BF16)
        kn = _rms_lanes(kvb[:, h * MLA_NOPE:(h + 1) * MLA_NOPE], gkn_ref[...], MLA_NOPE)
        kcat_o[0, :, lo:lo + MLA_NOPE] = kn.astype(BF16)
        kcat_o[0, :, lo + MLA_NOPE:lo + 2 * LANES] = kr
    vbt_o[0, 0] = kvb[:, B_HEADS * MLA_NOPE:].T.astype(BF16)


def _prep(proj, tabs, gains, wuq, wukv, batch, seq):
    tm = KEY_BLOCK
    nt = seq // tm
    p3 = proj.reshape(batch * seq, N_PROJ)
    wa, wb, wc = A_HEADS * HEAD_DIM, B_HEADS * HEAD_DIM, C_HEADS * HEAD_DIM
    wcat = B_HEADS * 2 * LANES

    def tok(width, dtype):
        return (jax.ShapeDtypeStruct((batch, seq, width), dtype),
                pl.BlockSpec((1, tm, width), lambda b, i: (b, i, 0)))

    def tok_t(width):
        return (jax.ShapeDtypeStruct((batch, nt, width, tm), BF16),
                pl.BlockSpec((1, 1, width, tm), lambda b, i: (b, i, 0, 0)))

    outs = [tok(wa, BF16), tok(wa, BF16), tok_t(wa), tok(IDX_HEADS * IDX_DIM, BF16), tok(IDX_DIM, BF16),
            (jax.ShapeDtypeStruct((batch, IDX_HEADS, seq), F32),
             pl.BlockSpec((1, IDX_HEADS, tm), lambda b, i: (b, 0, i))),
            tok(wcat, BF16), tok(wcat, BF16), tok_t(wb), tok(wc, BF16), tok(wc, BF16), tok_t(wc)]
    gain_specs = [pl.BlockSpec(g.shape, lambda b, i: (0, 0)) for g in gains]
    return pl.pallas_call(
        _prep_kernel,
        grid=(batch, nt),
        in_specs=[pl.BlockSpec((tm, N_PROJ), lambda b, i: (b * nt + i, 0)),
                  pl.BlockSpec((6, tm, LANES), lambda b, i: (0, i, 0))] + gain_specs + [
                  pl.BlockSpec(wuq.shape, lambda b, i: (0, 0)),
                  pl.BlockSpec(wukv.shape, lambda b, i: (0, 0))],
        out_specs=[o[1] for o in outs],
        out_shape=[o[0] for o in outs],
        compiler_params=_params("arbitrary", "arbitrary"),
        name="head_prep",
    )(p3, tabs, *gains, wuq, wukv)


MLA_HEADS_PER_STEP = 4


def _mla_kernel(q_ref, k_ref, vt_ref, o_ref, m_ref, l_ref, acc_ref, *, t):
    qb = pl.program_id(2)
    sub = t // KEY_BLOCK
    dq = 2 * LANES
    m_ref[...] = jnp.full(m_ref.shape, NEG_INF, F32)
    l_ref[...] = jnp.zeros(l_ref.shape, F32)
    acc_ref[...] = jnp.zeros(acc_ref.shape, F32)

    def step(kb, diagonal):
        ks = pl.multiple_of(kb * t, t)
        scores = [lax.dot_general(k_ref[0, pl.ds(ks, t), h * dq:(h + 1) * dq], q_ref[0, :, h * dq:(h + 1) * dq],
                                  NT_DIMS, preferred_element_type=F32)
                  for h in range(MLA_HEADS_PER_STEP)]
        for h in range(MLA_HEADS_PER_STEP):
            s = scores[h]
            if diagonal:
                kc = lax.broadcasted_iota(I32, (t, t), 0) >> CHUNK_SHIFT
                qc = lax.broadcasted_iota(I32, (t, t), 1) >> CHUNK_SHIFT
                s = jnp.where(kc <= qc, s, NEG_INF)
            m_prev = m_ref[h]
            m_new = jnp.maximum(m_prev, jnp.max(s, axis=0, keepdims=True))
            p = jnp.exp2(s - m_new)
            alpha = jnp.exp2(m_prev - m_new)
            l_ref[h] = alpha * l_ref[h] + jnp.sum(p, axis=0, keepdims=True)
            p = p.astype(BF16)
            pv = jnp.dot(vt_ref[0, kb * sub, h * HEAD_DIM:(h + 1) * HEAD_DIM, :], p[0:KEY_BLOCK],
                         preferred_element_type=F32)
            for c in range(1, sub):
                pv += jnp.dot(vt_ref[0, kb * sub + c, h * HEAD_DIM:(h + 1) * HEAD_DIM, :],
                              p[c * KEY_BLOCK:(c + 1) * KEY_BLOCK], preferred_element_type=F32)
            acc_ref[h] = alpha * acc_ref[h] + pv
            m_ref[h] = m_new

    def body(kb, carry):
        step(kb, False)
        return carry

    lax.fori_loop(0, qb, body, 0)
    step(qb, True)
    for h in range(MLA_HEADS_PER_STEP):
        o_ref[0, :, h * HEAD_DIM:(h + 1) * HEAD_DIM] = (acc_ref[h] / l_ref[h]).T.astype(BF16)


def _mla(qcat, kcat, vbt, batch, seq):
    t = min(512, seq)
    hp = MLA_HEADS_PER_STEP
    nkb = seq // KEY_BLOCK
    return pl.pallas_call(
        functools.partial(_mla_kernel, t=t),
        grid=(batch, B_HEADS // hp, seq // t),
        in_specs=[
            pl.BlockSpec((1, t, hp * 2 * LANES), lambda b, h, i: (b, i, h)),
            pl.BlockSpec((1, seq, hp * 2 * LANES), lambda b, h, i: (b, 0, h)),
            pl.BlockSpec((1, nkb, hp * HEAD_DIM, KEY_BLOCK), lambda b, h, i: (b, 0, h, 0)),
        ],
        out_specs=pl.BlockSpec((1, t, hp * HEAD_DIM), lambda b, h, i: (b, i, h)),
        out_shape=jax.ShapeDtypeStruct((batch, seq, B_HEADS * HEAD_DIM), BF16),
        scratch_shapes=[pltpu.VMEM((hp, 1, t), F32), pltpu.VMEM((hp, 1, t), F32),
                        pltpu.VMEM((hp, HEAD_DIM, t), F32)],
        compiler_params=_params("arbitrary", "arbitrary", "arbitrary"),
        name="mla_attn",
    )(qcat, kcat, vbt)


SB_HEADS_PER_STEP = 2


def _sb_kernel(q_ref, k_ref, vt_ref, o_ref, r_ref, acc_ref, *, tq):
    qb = pl.program_id(2)
    tk = KEY_BLOCK
    sub = tq // tk
    tri_r = lax.broadcasted_iota(I32, (tk, tk), 0)
    tri_c = lax.broadcasted_iota(I32, (tk, tk), 1)
    later_keys = jnp.where(tri_c > tri_r, 1.0, 0.0).astype(BF16)
    r_ref[...] = jnp.zeros(r_ref.shape, F32)
    acc_ref[...] = jnp.zeros(acc_ref.shape, F32)

    def step(kb, key_offset):
        ks = pl.multiple_of(kb * tk, tk)
        heads = range(SB_HEADS_PER_STEP)
        if key_offset is not None:
            before = (lax.broadcasted_iota(I32, (tk, tq), 0) + key_offset
                      < lax.broadcasted_iota(I32, (tk, tq), 1))
        zs = [lax.dot_general(k_ref[0, pl.ds(ks, tk), h * HEAD_DIM:(h + 1) * HEAD_DIM],
                              q_ref[0, :, h * HEAD_DIM:(h + 1) * HEAD_DIM],
                              NT_DIMS, preferred_element_type=F32) for h in heads]
        go = []
        for z in zs:
            neg_abs = lax.bitcast_convert_type(lax.bitcast_convert_type(z, I32) | INT_MIN, F32)
            g = jnp.maximum(z, 0.0) + jnp.log(1.0 + jnp.exp(neg_abs))
            if key_offset is not None:
                g = jnp.where(before, g, 0.0)
            go.append(g)
        later = []
        for g in go:
            hi = g.astype(BF16)
            lo = (g - hi.astype(F32)).astype(BF16)
            later.append(jnp.dot(later_keys, hi, preferred_element_type=F32)
                         + jnp.dot(later_keys, lo, preferred_element_type=F32))
        for h in heads:
            w = jnp.exp(zs[h] - (go[h] + later[h] + r_ref[h]))
            if key_offset is not None:
                w = jnp.where(before, w, 0.0)
            acc_ref[h] += jnp.dot(vt_ref[0, kb, h * HEAD_DIM:(h + 1) * HEAD_DIM, :], w.astype(BF16),
                                  preferred_element_type=F32)
            r_ref[h] += jnp.sum(go[h], axis=0, keepdims=True)

    for d in reversed(range(sub)):
        step(qb * sub + d, d * tk)

    def body(j, carry):
        step(qb * sub - 1 - j, None)
        return carry

    lax.fori_loop(0, qb * sub, body, 0)
    for h in range(SB_HEADS_PER_STEP):
        o_ref[0, :, h * HEAD_DIM:(h + 1) * HEAD_DIM] = acc_ref[h].T.astype(BF16)


def _sb(qc, kc, vct, batch, seq):
    tq = min(512, seq)
    hp = SB_HEADS_PER_STEP
    nkb = seq // KEY_BLOCK
    return pl.pallas_call(
        functools.partial(_sb_kernel, tq=tq),
        grid=(batch, C_HEADS // hp, seq // tq),
        in_specs=[
            pl.BlockSpec((1, tq, hp * HEAD_DIM), lambda b, h, i: (b, i, h)),
            pl.BlockSpec((1, seq, hp * HEAD_DIM), lambda b, h, i: (b, 0, h)),
            pl.BlockSpec((1, nkb, hp * HEAD_DIM, KEY_BLOCK), lambda b, h, i: (b, 0, h, 0)),
        ],
        out_specs=pl.BlockSpec((1, tq, hp * HEAD_DIM), lambda b, h, i: (b, i, h)),
        out_shape=jax.ShapeDtypeStruct((batch, seq, C_HEADS * HEAD_DIM), BF16),
        scratch_shapes=[pltpu.VMEM((hp, 1, tq), F32), pltpu.VMEM((hp, HEAD_DIM, tq), F32)],
        compiler_params=_params("arbitrary", "arbitrary", "arbitrary"),
        name="sb_attn",
    )(qc, kc, vct)


def _dsa_kernel(qi_ref, ki_ref, wt_ref, qa_ref, ka_ref, vt_ref, o_ref,
                keys_ref, bias_ref, thr_ref, lim_ref, m_ref, l_ref, acc_ref, *, t, topk, seq):
    qb = pl.program_id(1)
    nk = qb + 1
    row = lax.broadcasted_iota(I32, (t, t), 0)
    col = lax.broadcasted_iota(I32, (t, t), 1)
    wt = wt_ref[0]

    def score_block(kb, carry):
        ks = pl.multiple_of(kb * t, t)
        kix = ki_ref[0, pl.ds(ks, t), :]
        sc = jnp.zeros((t, t), F32)
        for h in range(IDX_HEADS):
            lg = lax.dot_general(kix, qi_ref[0, :, h * IDX_DIM:(h + 1) * IDX_DIM], NT_DIMS,
                                 preferred_element_type=F32)
            sc = sc + wt[h:h + 1, :] * jnp.maximum(lg, 0.0)
        bits = lax.bitcast_convert_type(sc, I32)
        key = bits ^ ((bits >> 31) & 0x7FFFFFFF)
        visible = ((ks + row) >> CHUNK_SHIFT) <= ((qb * t + col) >> CHUNK_SHIFT)
        keys_ref[kb] = jnp.where(visible, key, INT_MIN)
        return carry

    lax.fori_loop(0, nk, score_block, 0)

    def count(pred):
        def block(kb, acc):
            hit = jnp.where(pred(keys_ref[kb], kb * t + row), 1, 0).astype(I32)
            return acc + jnp.sum(hit.reshape(t // 8, 8, t), axis=0)
        acc = lax.fori_loop(0, nk, block, jnp.zeros((8, t), I32))
        return jnp.sum(acc, axis=0, keepdims=True)

    thr0 = jnp.where(count(lambda k, i: k >= 0) >= topk, 0, INT_MIN).astype(I32)

    def thr_bit(b, thr):
        cand = thr | lax.shift_left(jnp.int32(1), 30 - b)
        return jnp.where(count(lambda k, i: k >= cand) >= topk, cand, thr)

    thr = lax.fori_loop(0, 31, thr_bit, thr0)
    n_ge = count(lambda k, i: k >= thr)
    n_gt = count(lambda k, i: k > thr)
    tie = (thr != INT_MIN) & (n_ge > topk)
    need = topk - n_gt
    thr_ref[...] = thr
    lim_ref[...] = jnp.where(thr == INT_MIN, 0, seq).astype(I32)

    @pl.when(jnp.max(tie.astype(I32)) > 0)
    def _():
        nbits = seq.bit_length() - 1

        def lim_bit(b, lim):
            cand = lim | lax.shift_left(jnp.int32(1), nbits - 1 - b)
            below = count(lambda k, i: (k == thr) & (i < cand))
            return jnp.where(below < need, cand, lim)

        lim = lax.fori_loop(0, nbits, lim_bit, jnp.zeros((1, t), I32))
        lim_ref[...] = jnp.where(tie, lim + 1, lim_ref[...])

    def bias_block(kb, carry):
        key = keys_ref[kb]
        thr_q = thr_ref[...]
        at_thr = jnp.where((kb * t + row) < lim_ref[...], 0.0, NEG_INF)
        bias_ref[kb] = jnp.where(key > thr_q, 0.0, jnp.where(key == thr_q, at_thr, NEG_INF))
        return carry

    lax.fori_loop(0, nk, bias_block, 0)
    m_ref[...] = jnp.full(m_ref.shape, NEG_INF, F32)
    l_ref[...] = jnp.zeros(l_ref.shape, F32)
    acc_ref[...] = jnp.zeros(acc_ref.shape, F32)

    def attend_block(kb, carry):
        ks = pl.multiple_of(kb * t, t)
        bias = bias_ref[kb]
        scores = [lax.dot_general(ka_ref[0, pl.ds(ks, t), h * HEAD_DIM:(h + 1) * HEAD_DIM],
                                  qa_ref[0, :, h * HEAD_DIM:(h + 1) * HEAD_DIM],
                                  NT_DIMS, preferred_element_type=F32) + bias for h in range(A_HEADS)]
        for h in range(A_HEADS):
            lo = h * HEAD_DIM
            s = scores[h]
            m_prev = m_ref[h]
            m_new = jnp.maximum(m_prev, jnp.max(s, axis=0, keepdims=True))
            m_safe = jnp.where(m_new == NEG_INF, 0.0, m_new)
            p = jnp.exp2(s - m_safe)
            alpha = jnp.exp2(m_prev - m_safe)
            l_ref[h] = alpha * l_ref[h] + jnp.sum(p, axis=0, keepdims=True)
            acc_ref[h] = alpha * acc_ref[h] + jnp.dot(vt_ref[0, kb, lo:lo + HEAD_DIM, :], p.astype(BF16),
                                                      preferred_element_type=F32)
            m_ref[h] = m_new
        return carry

    lax.fori_loop(0, nk, attend_block, 0)
    for h in range(A_HEADS):
        o_ref[0, :, h * HEAD_DIM:(h + 1) * HEAD_DIM] = (acc_ref[h] / l_ref[h]).T.astype(BF16)


def _dsa(qi, ki, wt, qa, ka, vat, batch, seq, topk):
    t = KEY_BLOCK
    nt = seq // t
    wa = A_HEADS * HEAD_DIM
    return pl.pallas_call(
        functools.partial(_dsa_kernel, t=t, topk=topk, seq=seq),
        grid=(batch, nt),
        in_specs=[
            pl.BlockSpec((1, t, IDX_HEADS * IDX_DIM), lambda b, i: (b, i, 0)),
            pl.BlockSpec((1, seq, IDX_DIM), lambda b, i: (b, 0, 0)),
            pl.BlockSpec((1, IDX_HEADS, t), lambda b, i: (b, 0, i)),
            pl.BlockSpec((1, t, wa), lambda b, i: (b, i, 0)),
            pl.BlockSpec((1, seq, wa), lambda b, i: (b, 0, 0)),
            pl.BlockSpec((1, nt, wa, t), lambda b, i: (b, 0, 0, 0)),
        ],
        out_specs=pl.BlockSpec((1, t, wa), lambda b, i: (b, i, 0)),
        out_shape=jax.ShapeDtypeStruct((batch, seq, wa), BF16),
        scratch_shapes=[
            pltpu.VMEM((nt, t, t), I32), pltpu.VMEM((nt, t, t), F32),
            pltpu.VMEM((1, t), I32), pltpu.VMEM((1, t), I32),
            pltpu.VMEM((A_HEADS, 1, t), F32), pltpu.VMEM((A_HEADS, 1, t), F32),
            pltpu.VMEM((A_HEADS, HEAD_DIM, t), F32),
        ],
        compiler_params=_params("arbitrary", "arbitrary"),
        name="dsa_attn",
    )(qi, ki, wt, qa, ka, vat)


def _out_kernel(x_ref, mod_ref, oa_ref, ob_ref, oc_ref, w_ref, o_ref):
    wa = A_HEADS * HEAD_DIM
    wb = B_HEADS * HEAD_DIM
    mixed = (jnp.dot(oa_ref[...], w_ref[0:wa, :], preferred_element_type=F32)
             + jnp.dot(ob_ref[...], w_ref[wa:wa + wb, :], preferred_element_type=F32)
             + jnp.dot(oc_ref[...], w_ref[wa + wb:, :], preferred_element_type=F32))
    o_ref[...] = x_ref[...] + (1.0 + mod_ref[0, 5:6, :]) * mixed


def _out_proj(x2, mod, oa, ob, oc, w, seq):
    T, D = x2.shape
    tm = min(512, seq)
    per_seq = seq // tm
    return pl.pallas_call(
        _out_kernel,
        grid=(T // tm,),
        in_specs=[
            pl.BlockSpec((tm, D), lambda i: (i, 0)),
            pl.BlockSpec((1, N_MOD, D), lambda i: (i // per_seq, 0, 0)),
            pl.BlockSpec((tm, oa.shape[1]), lambda i: (i, 0)),
            pl.BlockSpec((tm, ob.shape[1]), lambda i: (i, 0)),
            pl.BlockSpec((tm, oc.shape[1]), lambda i: (i, 0)),
            pl.BlockSpec(w.shape, lambda i: (0, 0)),
        ],
        out_specs=pl.BlockSpec((tm, D), lambda i: (i, 0)),
        out_shape=jax.ShapeDtypeStruct((T, D), F32),
        compiler_params=_params("arbitrary"),
        name="out_proj",
    )(x2, mod, oa, ob, oc, w)


def _rope_tables(seq):
    def tables(dim):
        inv = 1.0 / (ROPE_THETA ** (jnp.arange(0, dim, 2, dtype=F32) / dim))
        ang = jnp.arange(seq, dtype=F32)[:, None] * inv[None, :]
        return jnp.cos(ang), jnp.sin(ang)

    def lane_tables(cos, sin, fill):
        half = cos.shape[1]
        rest = LANES - 2 * half
        zeros_h = jnp.zeros((seq, half), F32)
        zeros_r = jnp.zeros((seq, rest), F32)
        return [jnp.concatenate([cos, cos, jnp.full((seq, rest), fill, F32)], axis=1),
                jnp.concatenate([-sin, zeros_h, zeros_r], axis=1),
                jnp.concatenate([zeros_h, sin, zeros_r], axis=1)]

    cos_p, sin_p = tables(PARTIAL_ROPE_DIM)
    cos_m, sin_m = tables(MLA_ROPE)
    return jnp.stack(lane_tables(cos_p, sin_p, 1.0) + lane_tables(cos_m, sin_m, 0.0))


def _pad_cols(a, width):
    return jnp.pad(a, ((0, 0), (0, width - a.shape[1])))


def _layer_weights(w_in, w_uq, w_ukv):
    w_in_p = jnp.concatenate([
        w_in[:, 0:2560],
        w_in[:, 3280:4816],
        _pad_cols(w_in[:, 2640:3088], Q_RANK_PAD),
        w_in[:, 3088:3216],
        _pad_cols(w_in[:, 3216:3280], LANES),
        _pad_cols(w_in[:, 2560:2640], LANES),
    ], axis=1).astype(BF16)
    wuq = w_uq.reshape(MLA_Q_RANK, B_HEADS, MLA_NOPE + MLA_ROPE)
    wuq = jnp.pad(wuq, ((0, Q_RANK_PAD - MLA_Q_RANK), (0, 0), (0, 2 * LANES - MLA_NOPE - MLA_ROPE)))
    wuq = wuq.reshape(Q_RANK_PAD, B_HEADS * 2 * LANES).astype(BF16)
    wukv = w_ukv.reshape(MLA_KV_RANK, B_HEADS, MLA_NOPE + HEAD_DIM)
    wukv = jnp.concatenate([wukv[:, :, :MLA_NOPE].reshape(MLA_KV_RANK, -1),
                            wukv[:, :, MLA_NOPE:].reshape(MLA_KV_RANK, -1)], axis=1).astype(BF16)
    return w_in_p, wuq, wukv


def kernel(x, c, w_ada, b_ada, g_ffn1, w1_gate, w1_up, w1_down, g_mix, w_in, g_qa, g_ka, g_cq, g_ckv, w_uq, w_ukv, g_q_nope, g_k_nope, g_q_rope, g_k_rope, w_out, g_ffn2, w2_gate, w2_up, w2_down):
    batch, seq, d_model = x.shape
    depth = w_ada.shape[0]
    topk = min(TOPK_MAX, seq // 4)
    tabs = _rope_tables(seq)
    mods = _ada(c, w_ada, b_ada).reshape(depth, batch, N_MOD, d_model)
    x2 = x.reshape(batch * seq, d_model)

    for l in range(depth):
        mod = mods[l]
        x2 = _ffn(x2, mod, g_ffn1[l], w1_gate[l].astype(BF16), w1_up[l].astype(BF16),
                  w1_down[l].astype(BF16), 0, seq)

        w_in_p, wuq, wukv = _layer_weights(w_in[l], w_uq[l], w_ukv[l])
        proj = _proj(x2, mod, g_mix[l], w_in_p, seq)
        gains = [g_qa[l][None, :], g_ka[l][None, :], _pad_cols(g_cq[l][None, :], Q_RANK_PAD),
                 g_ckv[l][None, :], g_q_nope[l][None, :], g_k_nope[l][None, :],
                 _pad_cols(g_q_rope[l][None, :], LANES), _pad_cols(g_k_rope[l][None, :], LANES)]
        (qa, ka, vat, qi, ki, wt, qcat, kcat, vbt, qc, kc, vct) = _prep(
            proj, tabs, gains, wuq, wukv, batch, seq)

        out_a = _dsa(qi, ki, wt, qa, ka, vat, batch, seq, topk)
        out_b = _mla(qcat, kcat, vbt, batch, seq)
        out_c = _sb(qc, kc, vct, batch, seq)
        x2 = _out_proj(x2, mod, out_a.reshape(batch * seq, -1), out_b.reshape(batch * seq, -1),
                       out_c.reshape(batch * seq, -1), w_out[l].astype(BF16), seq)

        x2 = _ffn(x2, mod, g_ffn2[l], w2_gate[l].astype(BF16), w2_up[l].astype(BF16),
                  w2_down[l].astype(BF16), 6, seq)

    return x2.reshape(batch, seq, d_model)
```

```python
import functools

import jax
import jax.numpy as jnp
from jax import lax
from jax.experimental import pallas as pl
from jax.experimental.pallas import tpu as pltpu

F32 = jnp.float32
BF16 = jnp.bfloat16
I32 = jnp.int32
I16 = jnp.int16

HEAD_DIM = 128
CHUNK = 64
CHUNK_SHIFT = 6
ROPE_THETA = 500000.0
PARTIAL_ROPE_DIM = HEAD_DIM // 4
NORM_EPS = 1e-6
D_FF = 5632
N_MOD = 9
A_HEADS = 4
IDX_HEADS = 16
IDX_DIM = 64
TOPK_MAX = 256
B_HEADS = 8
MLA_Q_RANK = 448
MLA_KV_RANK = 128
MLA_NOPE = 128
MLA_ROPE = 64
C_HEADS = 4

LANES = 128
KEY_BLOCK = 256
Q_RANK_PAD = 512
VMEM_LIMIT = 56 * 1024 * 1024

OFF_QA, OFF_KA, OFF_VA, OFF_QI = 0, 512, 1024, 1536
OFF_QC, OFF_KC, OFF_VC = 2560, 3072, 3584
OFF_CQ, OFF_CKV, OFF_KR, OFF_KIW = 4096, 4608, 4736, 4864
N_PROJ = 4992

LOG2E = 1.4426950408889634
A_SCALE = HEAD_DIM ** -0.5 * LOG2E
B_SCALE = (MLA_NOPE + MLA_ROPE) ** -0.5 * LOG2E
C_SCALE = HEAD_DIM ** -0.5
IDX_SCALE = (IDX_DIM ** -0.5) * (IDX_HEADS ** -0.5)

NT_DIMS = (((1,), (1,)), ((), ()))
NEG_INF = float("-inf")
INT_MIN = -2 ** 31


def _params(*sem):
    return pltpu.CompilerParams(dimension_semantics=sem, vmem_limit_bytes=VMEM_LIMIT)


def _ada_kernel(c_ref, w_ref, b_ref, o_ref):
    c = c_ref[...]
    ca = (c * jax.nn.sigmoid(c)).astype(BF16)
    o_ref[0] = jnp.dot(ca, w_ref[0].astype(BF16), preferred_element_type=F32) + b_ref[0]


def _ada(c, w_ada, b_ada):
    L, D, N = w_ada.shape
    B = c.shape[0]
    tn = 1024
    return pl.pallas_call(
        _ada_kernel,
        grid=(L, N // tn),
        in_specs=[
            pl.BlockSpec((B, D), lambda l, j: (0, 0)),
            pl.BlockSpec((1, D, tn), lambda l, j: (l, 0, j)),
            pl.BlockSpec((1, 1, tn), lambda l, j: (l, 0, j)),
        ],
        out_specs=pl.BlockSpec((1, B, tn), lambda l, j: (l, 0, j)),
        out_shape=jax.ShapeDtypeStruct((L, B, N), F32),
        compiler_params=_params("arbitrary", "arbitrary"),
        name="ada_mod",
    )(c, w_ada, b_ada.reshape(L, 1, N))


def _norm_mod(x, g, shift, scale):
    y = x * lax.rsqrt(jnp.mean(x * x, axis=-1, keepdims=True) + NORM_EPS)
    return (y * g) * (1.0 + scale) + shift


def _ffn_kernel(x_ref, mod_ref, g_ref, wg_ref, wu_ref, wd_ref, o_ref, h_ref, acc_ref, *, row):
    j = pl.program_id(1)

    @pl.when(j == 0)
    def _():
        h = _norm_mod(x_ref[...], g_ref[...], mod_ref[0, row:row + 1, :], mod_ref[0, row + 1:row + 2, :])
        h_ref[...] = h.astype(BF16)
        acc_ref[...] = jnp.zeros(acc_ref.shape, F32)

    h = h_ref[...]
    g = jnp.dot(h, wg_ref[...], preferred_element_type=F32)
    u = jnp.dot(h, wu_ref[...], preferred_element_type=F32)
    a = ((g * jax.nn.sigmoid(g)) * u).astype(BF16)
    acc_ref[...] += jnp.dot(a, wd_ref[...], preferred_element_type=F32)

    @pl.when(j == pl.num_programs(1) - 1)
    def _():
        gate = mod_ref[0, row + 2:row + 3, :]
        o_ref[...] = x_ref[...] + (0.5 * (1.0 + gate)) * acc_ref[...]


def _ffn(x2, mod, g, wg, wu, wd, row, seq):
    T, D = x2.shape
    F = wg.shape[1]
    tm = min(512, seq)
    tf = 512
    per_seq = seq // tm
    return pl.pallas_call(
        functools.partial(_ffn_kernel, row=row),
        grid=(T // tm, F // tf),
        in_specs=[
            pl.BlockSpec((tm, D), lambda i, j: (i, 0)),
            pl.BlockSpec((1, N_MOD, D), lambda i, j: (i // per_seq, 0, 0)),
            pl.BlockSpec((1, D), lambda i, j: (0, 0)),
            pl.BlockSpec((D, tf), lambda i, j: (0, j)),
            pl.BlockSpec((D, tf), lambda i, j: (0, j)),
            pl.BlockSpec((tf, D), lambda i, j: (j, 0)),
        ],
        out_specs=pl.BlockSpec((tm, D), lambda i, j: (i, 0)),
        out_shape=jax.ShapeDtypeStruct((T, D), F32),
        scratch_shapes=[pltpu.VMEM((tm, D), BF16), pltpu.VMEM((tm, D), F32)],
        compiler_params=_params("arbitrary", "arbitrary"),
        name="ffn",
    )(x2, mod, g.reshape(1, D), wg, wu, wd)


def _rms_lanes(x, g, n):
    return x * lax.rsqrt(jnp.sum(x * x, axis=-1, keepdims=True) / n + NORM_EPS) * g


def _rope_lanes(x, cos, sin_lo, sin_hi, half):
    return (x * cos + pltpu.roll(x, LANES - half, 1) * sin_lo + pltpu.roll(x, half, 1) * sin_hi)


def _prep_kernel(x_ref, mod_ref, gmix_ref, win_ref, tab_ref,
                 gqa_ref, gka_ref, gcq_ref, gckv_ref, gqn_ref, gkn_ref, gqr_ref, gkr_ref,
                 wuq_ref, wukv_ref,
                 qa_o, ka_o, vat_o, qi_o, ki_o, wt_o, qcat_o, kcat_o, vbt_o, qc_o, kc_o, vct_o, h_ref, t_ref):
    ca, sa_lo, sa_hi = tab_ref[0], tab_ref[1], tab_ref[2]
    cm, sm_lo, sm_hi = tab_ref[3], tab_ref[4], tab_ref[5]
    half_a = PARTIAL_ROPE_DIM // 2
    half_m = MLA_ROPE // 2
    wa, wc = A_HEADS * HEAD_DIM, C_HEADS * HEAD_DIM

    h_ref[...] = _norm_mod(x_ref[...], gmix_ref[...], mod_ref[0, 3:4, :], mod_ref[0, 4:5, :]).astype(BF16)

    def transposed(v):
        t_ref[:, 0:v.shape[1]] = v
        return t_ref[:, 0:v.shape[1]].T

    def proj(lo, width):
        return jnp.dot(h_ref[...], win_ref[:, lo:lo + width], preferred_element_type=F32)

    qa = proj(OFF_QA, wa)
    for h in range(A_HEADS):
        lo = h * HEAD_DIM
        q = _rms_lanes(qa[:, lo:lo + HEAD_DIM], gqa_ref[...], HEAD_DIM)
        qa_o[0, :, lo:lo + HEAD_DIM] = (_rope_lanes(q, ca, sa_lo, sa_hi, half_a) * A_SCALE).astype(BF16)
    ka = proj(OFF_KA, wa)
    for h in range(A_HEADS):
        lo = h * HEAD_DIM
        k = _rms_lanes(ka[:, lo:lo + HEAD_DIM], gka_ref[...], HEAD_DIM)
        ka_o[0, :, lo:lo + HEAD_DIM] = _rope_lanes(k, ca, sa_lo, sa_hi, half_a).astype(BF16)
    vat_o[0, 0] = transposed(proj(OFF_VA, wa)).astype(BF16)
    qi_o[0] = proj(OFF_QI, IDX_HEADS * IDX_DIM).astype(BF16)

    qc_o[0] = (proj(OFF_QC, wc) * C_SCALE).astype(BF16)
    kc_o[0] = proj(OFF_KC, wc).astype(BF16)
    vct_o[0, 0] = transposed(proj(OFF_VC, wc)).astype(BF16)

    cq = _rms_lanes(proj(OFF_CQ, Q_RANK_PAD), gcq_ref[...], MLA_Q_RANK)
    qb = jnp.dot(cq.astype(BF16), wuq_ref[...], preferred_element_type=F32)
    small = proj(OFF_CKV, N_PROJ - OFF_CKV)
    ckv = _rms_lanes(small[:, 0:MLA_KV_RANK], gckv_ref[...], MLA_KV_RANK)
    kvb = jnp.dot(ckv.astype(BF16), wukv_ref[...], preferred_element_type=F32)
    kr = _rms_lanes(small[:, OFF_KR - OFF_CKV:OFF_KR - OFF_CKV + LANES], gkr_ref[...], MLA_ROPE)
    kr = _rope_lanes(kr, cm, sm_lo, sm_hi, half_m).astype(BF16)
    kiw = small[:, OFF_KIW - OFF_CKV:OFF_KIW - OFF_CKV + LANES]
    ki_o[0] = kiw[:, :IDX_DIM].astype(BF16)
    wt_o[0] = transposed(kiw)[IDX_DIM:IDX_DIM + IDX_HEADS, :] * IDX_SCALE
    for h in range(B_HEADS):
        lo = h * 2 * LANES
        qn = _rms_lanes(qb[:, lo:lo + MLA_NOPE], gqn_ref[...], MLA_NOPE)
        qcat_o[0, :, lo:lo + MLA_NOPE] = (qn * B_SCALE).astype(BF16)
        qr = _rms_lanes(qb[:, lo + MLA_NOPE:lo + 2 * LANES], gqr_ref[...], MLA_ROPE)
        qcat_o[0, :, lo + MLA_NOPE:lo + 2 * LANES] = (
            _rope_lanes(qr, cm, sm_lo, sm_hi, half_m) * B_SCALE).astype(BF16)
        kn = _rms_lanes(kvb[:, h * MLA_NOPE:(h + 1) * MLA_NOPE], gkn_ref[...], MLA_NOPE)
        kcat_o[0, :, lo:lo + MLA_NOPE] = kn.astype(BF16)
        kcat_o[0, :, lo + MLA_NOPE:lo + 2 * LANES] = kr
    vbt_o[0, 0] = transposed(kvb[:, B_HEADS * MLA_NOPE:]).astype(BF16)


def _resident(shape):
    return pl.BlockSpec(shape, lambda b, i: (0,) * len(shape), pipeline_mode=pl.Buffered(1))


def _prep(x2, mod, g_mix, w_in_p, tabs, gains, wuq, wukv, batch, seq):
    tm = KEY_BLOCK
    nt = seq // tm
    d_model = x2.shape[1]
    wa, wb, wc = A_HEADS * HEAD_DIM, B_HEADS * HEAD_DIM, C_HEADS * HEAD_DIM
    wcat = B_HEADS * 2 * LANES

    def tok(width, dtype):
        return (jax.ShapeDtypeStruct((batch, seq, width), dtype),
                pl.BlockSpec((1, tm, width), lambda b, i: (b, i, 0)))

    def tok_t(width):
        return (jax.ShapeDtypeStruct((batch, nt, width, tm), BF16),
                pl.BlockSpec((1, 1, width, tm), lambda b, i: (b, i, 0, 0)))

    outs = [tok(wa, BF16), tok(wa, BF16), tok_t(wa), tok(IDX_HEADS * IDX_DIM, BF16), tok(IDX_DIM, BF16),
            (jax.ShapeDtypeStruct((batch, IDX_HEADS, seq), F32),
             pl.BlockSpec((1, IDX_HEADS, tm), lambda b, i: (b, 0, i))),
            tok(wcat, BF16), tok(wcat, BF16), tok_t(wb), tok(wc, BF16), tok(wc, BF16), tok_t(wc)]
    return pl.pallas_call(
        _prep_kernel,
        grid=(batch, nt),
        in_specs=[pl.BlockSpec((tm, d_model), lambda b, i: (b * nt + i, 0)),
                  pl.BlockSpec((1, N_MOD, d_model), lambda b, i: (b, 0, 0)),
                  _resident((1, d_model)),
                  _resident(w_in_p.shape),
                  pl.BlockSpec((6, tm, LANES), lambda b, i: (0, i, 0))]
                 + [_resident(g.shape) for g in gains]
                 + [_resident(wuq.shape), _resident(wukv.shape)],
        out_specs=[o[1] for o in outs],
        out_shape=[o[0] for o in outs],
        scratch_shapes=[pltpu.VMEM((tm, d_model), BF16), pltpu.VMEM((tm, wb), F32)],
        compiler_params=_params("arbitrary", "arbitrary"),
        name="head_prep",
    )(x2, mod, g_mix.reshape(1, d_model), w_in_p, tabs, *gains, wuq, wukv)


MLA_HEADS_PER_STEP = 4


def _mla_kernel(q_ref, k_ref, vt_ref, o_ref, s0_ref, s1_ref, m_ref, l_ref, acc_ref, *, t):
    qb = pl.program_id(2)
    sub = t // KEY_BLOCK
    dq = 2 * LANES
    m_ref[...] = jnp.full(m_ref.shape, NEG_INF, F32)
    l_ref[...] = jnp.zeros(l_ref.shape, F32)
    acc_ref[...] = jnp.zeros(acc_ref.shape, F32)

    def produce(s_ref, kb):
        ks = pl.multiple_of(kb * t, t)
        for h in range(MLA_HEADS_PER_STEP):
            s_ref[h] = lax.dot_general(k_ref[0, pl.ds(ks, t), h * dq:(h + 1) * dq],
                                       q_ref[0, :, h * dq:(h + 1) * dq],
                                       NT_DIMS, preferred_element_type=F32)

    def consume(s_ref, kb, diagonal):
        for h in range(MLA_HEADS_PER_STEP):
            s = s_ref[h]
            if diagonal:
                kc = lax.broadcasted_iota(I32, (t, t), 0) >> CHUNK_SHIFT
                qc = lax.broadcasted_iota(I32, (t, t), 1) >> CHUNK_SHIFT
                s = jnp.where(kc <= qc, s, NEG_INF)
            m_prev = m_ref[h]
            m_new = jnp.maximum(m_prev, jnp.max(s, axis=0, keepdims=True))
            p = jnp.exp2(s - m_new)
            alpha = jnp.exp2(m_prev - m_new)
            l_ref[h] = alpha * l_ref[h] + jnp.sum(p, axis=0, keepdims=True)
            p = p.astype(BF16)
            pv = jnp.dot(vt_ref[0, kb * sub, h * HEAD_DIM:(h + 1) * HEAD_DIM, :], p[0:KEY_BLOCK],
                         preferred_element_type=F32)
            for c in range(1, sub):
                pv += jnp.dot(vt_ref[0, kb * sub + c, h * HEAD_DIM:(h + 1) * HEAD_DIM, :],
                              p[c * KEY_BLOCK:(c + 1) * KEY_BLOCK], preferred_element_type=F32)
            acc_ref[h] = alpha * acc_ref[h] + pv
            m_ref[h] = m_new

    produce(s0_ref, 0)

    def body(j, carry):
        produce(s1_ref, 2 * j + 1)
        consume(s0_ref, 2 * j, False)
        produce(s0_ref, 2 * j + 2)
        consume(s1_ref, 2 * j + 1, False)
        return carry

    lax.fori_loop(0, qb // 2, body, 0)

    @pl.when(qb % 2 == 0)
    def _():
        consume(s0_ref, qb, True)

    @pl.when(qb % 2 == 1)
    def _():
        produce(s1_ref, qb)
        consume(s0_ref, qb - 1, False)
        consume(s1_ref, qb, True)

    for h in range(MLA_HEADS_PER_STEP):
        o_ref[0, :, h * HEAD_DIM:(h + 1) * HEAD_DIM] = (acc_ref[h] / l_ref[h]).T.astype(BF16)


def _mla(qcat, kcat, vbt, batch, seq):
    t = min(512, seq)
    hp = MLA_HEADS_PER_STEP
    nkb = seq // KEY_BLOCK
    return pl.pallas_call(
        functools.partial(_mla_kernel, t=t),
        grid=(batch, B_HEADS // hp, seq // t),
        in_specs=[
            pl.BlockSpec((1, t, hp * 2 * LANES), lambda b, h, i: (b, i, h)),
            pl.BlockSpec((1, seq, hp * 2 * LANES), lambda b, h, i: (b, 0, h)),
            pl.BlockSpec((1, nkb, hp * HEAD_DIM, KEY_BLOCK), lambda b, h, i: (b, 0, h, 0)),
        ],
        out_specs=pl.BlockSpec((1, t, hp * HEAD_DIM), lambda b, h, i: (b, i, h)),
        out_shape=jax.ShapeDtypeStruct((batch, seq, B_HEADS * HEAD_DIM), BF16),
        scratch_shapes=[pltpu.VMEM((hp, t, t), F32), pltpu.VMEM((hp, t, t), F32),
                        pltpu.VMEM((hp, 1, t), F32), pltpu.VMEM((hp, 1, t), F32),
                        pltpu.VMEM((hp, HEAD_DIM, t), F32)],
        compiler_params=_params("arbitrary", "arbitrary", "arbitrary"),
        name="mla_attn",
    )(qcat, kcat, vbt)


SB_HEADS_PER_STEP = 4


def _sb_kernel(q_ref, k_ref, vt_ref, o_ref, r_ref, acc_ref, *, tq):
    qb = pl.program_id(2)
    tk = KEY_BLOCK
    sub = tq // tk
    tri_r = lax.broadcasted_iota(I32, (tk, tk), 0)
    tri_c = lax.broadcasted_iota(I32, (tk, tk), 1)
    later_keys = jnp.where(tri_c > tri_r, 1.0, 0.0).astype(BF16)
    r_ref[...] = jnp.zeros(r_ref.shape, F32)
    acc_ref[...] = jnp.zeros(acc_ref.shape, F32)

    def step(kb, key_offset):
        ks = pl.multiple_of(kb * tk, tk)
        heads = range(SB_HEADS_PER_STEP)
        if key_offset is not None:
            before = (lax.broadcasted_iota(I32, (tk, tq), 0) + key_offset
                      < lax.broadcasted_iota(I32, (tk, tq), 1))
        zs = [lax.dot_general(k_ref[0, pl.ds(ks, tk), h * HEAD_DIM:(h + 1) * HEAD_DIM],
                              q_ref[0, :, h * HEAD_DIM:(h + 1) * HEAD_DIM],
                              NT_DIMS, preferred_element_type=F32) for h in heads]
        go = []
        for z in zs:
            neg_abs = lax.bitcast_convert_type(lax.bitcast_convert_type(z, I32) | INT_MIN, F32)
            g = jnp.maximum(z, 0.0) + jnp.log(1.0 + jnp.exp(neg_abs))
            if key_offset is not None:
                g = jnp.where(before, g, 0.0)
            go.append(g)
        later = []
        for g in go:
            hi = g.astype(BF16)
            lo = (g - hi.astype(F32)).astype(BF16)
            later.append(jnp.dot(later_keys, hi, preferred_element_type=F32)
                         + jnp.dot(later_keys, lo, preferred_element_type=F32))
        for h in heads:
            w = jnp.exp(zs[h] - (go[h] + later[h] + r_ref[h]))
            if key_offset is not None:
                w = jnp.where(before, w, 0.0)
            acc_ref[h] += jnp.dot(vt_ref[0, kb, h * HEAD_DIM:(h + 1) * HEAD_DIM, :], w.astype(BF16),
                                  preferred_element_type=F32)
            r_ref[h] += jnp.sum(go[h], axis=0, keepdims=True)

    for d in reversed(range(sub)):
        step(qb * sub + d, d * tk)

    def body(j, carry):
        step(qb * sub - 1 - j, None)
        return carry

    lax.fori_loop(0, qb * sub, body, 0)
    for h in range(SB_HEADS_PER_STEP):
        o_ref[0, :, h * HEAD_DIM:(h + 1) * HEAD_DIM] = acc_ref[h].T.astype(BF16)


def _sb(qc, kc, vct, batch, seq):
    tq = min(512, seq)
    hp = SB_HEADS_PER_STEP
    nkb = seq // KEY_BLOCK
    return pl.pallas_call(
        functools.partial(_sb_kernel, tq=tq),
        grid=(batch, C_HEADS // hp, seq // tq),
        in_specs=[
            pl.BlockSpec((1, tq, hp * HEAD_DIM), lambda b, h, i: (b, i, h)),
            pl.BlockSpec((1, seq, hp * HEAD_DIM), lambda b, h, i: (b, 0, h)),
            pl.BlockSpec((1, nkb, hp * HEAD_DIM, KEY_BLOCK), lambda b, h, i: (b, 0, h, 0)),
        ],
        out_specs=pl.BlockSpec((1, tq, hp * HEAD_DIM), lambda b, h, i: (b, i, h)),
        out_shape=jax.ShapeDtypeStruct((batch, seq, C_HEADS * HEAD_DIM), BF16),
        scratch_shapes=[pltpu.VMEM((hp, 1, tq), F32), pltpu.VMEM((hp, HEAD_DIM, tq), F32)],
        compiler_params=_params("arbitrary", "arbitrary", "arbitrary"),
        name="sb_attn",
    )(qc, kc, vct)


def _dsa_kernel(qi_ref, ki_ref, wt_ref, qa_ref, ka_ref, vt_ref, o_ref,
                keys_ref, half_ref, bias_ref, thr_ref, lim_ref, m_ref, l_ref, acc_ref, *, t, topk, seq):
    qb = pl.program_id(1)
    nk = qb + 1
    row = lax.broadcasted_iota(I32, (t, t), 0)
    col = lax.broadcasted_iota(I32, (t, t), 1)
    wt = wt_ref[0]

    def score_block(kb, carry):
        ks = pl.multiple_of(kb * t, t)
        kix = ki_ref[0, pl.ds(ks, t), :]
        sc = jnp.zeros((t, t), F32)
        for h in range(IDX_HEADS):
            lg = lax.dot_general(kix, qi_ref[0, :, h * IDX_DIM:(h + 1) * IDX_DIM], NT_DIMS,
                                 preferred_element_type=F32)
            sc = sc + wt[h:h + 1, :] * jnp.maximum(lg, 0.0)
        bits = lax.bitcast_convert_type(sc, I32)
        key = bits ^ ((bits >> 31) & 0x7FFFFFFF)
        visible = ((ks + row) >> CHUNK_SHIFT) <= ((qb * t + col) >> CHUNK_SHIFT)
        key = jnp.where(visible, key, INT_MIN)
        keys_ref[kb] = key
        half_ref[kb] = (key >> 16).astype(I16)
        return carry

    lax.fori_loop(0, nk, score_block, 0)

    i16_min, i16_max = -2 ** 15, 2 ** 15 - 1

    def search16():
        def count16(cand):
            cand16 = cand.astype(I16)

            def block(kb, acc):
                hit = jnp.where(half_ref[kb] >= cand16, jnp.ones((), I16), jnp.zeros((), I16))
                for i in range(t // 16):
                    acc = acc + hit[i * 16:(i + 1) * 16]
                return acc
            acc = lax.fori_loop(0, nk, block, jnp.zeros((16, t), I16))
            return jnp.sum(acc.astype(I32), axis=0, keepdims=True)

        v0 = jnp.where(count16(jnp.zeros((1, t), I32)) >= topk, 0, i16_min).astype(I32)

        def bit(b, v):
            cand = v | lax.shift_left(jnp.int32(1), 14 - b)
            return jnp.where(count16(cand) >= topk, cand, v)

        return lax.fori_loop(0, 15, bit, v0)

    def count(pred):
        def block(kb, acc):
            hit = jnp.where(pred(keys_ref[kb], kb * t + row), 1, 0).astype(I32)
            return acc + jnp.sum(hit.reshape(t // 8, 8, t), axis=0)
        acc = lax.fori_loop(0, nk, block, jnp.zeros((8, t), I32))
        return jnp.sum(acc, axis=0, keepdims=True)

    thr_hi = search16()

    def low_half_block(kb, carry):
        key = keys_ref[kb]
        hi = key >> 16
        low = (key & 0xFFFF) + i16_min
        half_ref[kb] = jnp.where(hi == thr_hi, low, jnp.where(hi > thr_hi, i16_max, i16_min)).astype(I16)
        return carry

    lax.fori_loop(0, nk, low_half_block, 0)
    thr_lo = search16()
    thr = lax.shift_left(thr_hi, 16) + (thr_lo - i16_min)
    n_ge = count(lambda k, i: k >= thr)
    n_gt = count(lambda k, i: k > thr)
    tie = (thr != INT_MIN) & (n_ge > topk)
    need = topk - n_gt
    thr_ref[...] = thr
    lim_ref[...] = jnp.where(thr == INT_MIN, 0, seq).astype(I32)

    @pl.when(jnp.max(tie.astype(I32)) > 0)
    def _():
        nbits = seq.bit_length() - 1

        def lim_bit(b, lim):
            cand = lim | lax.shift_left(jnp.int32(1), nbits - 1 - b)
            below = count(lambda k, i: (k == thr) & (i < cand))
            return jnp.where(below < need, cand, lim)

        lim = lax.fori_loop(0, nbits, lim_bit, jnp.zeros((1, t), I32))
        lim_ref[...] = jnp.where(tie, lim + 1, lim_ref[...])

    def bias_block(kb, carry):
        key = keys_ref[kb]
        thr_q = thr_ref[...]
        at_thr = jnp.where((kb * t + row) < lim_ref[...], 0.0, NEG_INF)
        bias_ref[kb] = jnp.where(key > thr_q, 0.0, jnp.where(key == thr_q, at_thr, NEG_INF))
        return carry

    lax.fori_loop(0, nk, bias_block, 0)
    m_ref[...] = jnp.full(m_ref.shape, NEG_INF, F32)
    l_ref[...] = jnp.zeros(l_ref.shape, F32)
    acc_ref[...] = jnp.zeros(acc_ref.shape, F32)

    def attend_block(kb, carry):
        ks = pl.multiple_of(kb * t, t)
        bias = bias_ref[kb]
        scores = [lax.dot_general(ka_ref[0, pl.ds(ks, t), h * HEAD_DIM:(h + 1) * HEAD_DIM],
                                  qa_ref[0, :, h * HEAD_DIM:(h + 1) * HEAD_DIM],
                                  NT_DIMS, preferred_element_type=F32) + bias for h in range(A_HEADS)]
        for h in range(A_HEADS):
            lo = h * HEAD_DIM
            s = scores[h]
            m_prev = m_ref[h]
            m_new = jnp.maximum(m_prev, jnp.max(s, axis=0, keepdims=True))
            m_safe = jnp.where(m_new == NEG_INF, 0.0, m_new)
            p = jnp.exp2(s - m_safe)
            alpha = jnp.exp2(m_prev - m_safe)
            l_ref[h] = alpha * l_ref[h] + jnp.sum(p, axis=0, keepdims=True)
            acc_ref[h] = alpha * acc_ref[h] + jnp.dot(vt_ref[0, kb, lo:lo + HEAD_DIM, :], p.astype(BF16),
                                                      preferred_element_type=F32)
            m_ref[h] = m_new
        return carry

    lax.fori_loop(0, nk, attend_block, 0)
    for h in range(A_HEADS):
        o_ref[0, :, h * HEAD_DIM:(h + 1) * HEAD_DIM] = (acc_ref[h] / l_ref[h]).T.astype(BF16)


def _dsa(qi, ki, wt, qa, ka, vat, batch, seq, topk):
    t = KEY_BLOCK
    nt = seq // t
    wa = A_HEADS * HEAD_DIM
    return pl.pallas_call(
        functools.partial(_dsa_kernel, t=t, topk=topk, seq=seq),
        grid=(batch, nt),
        in_specs=[
            pl.BlockSpec((1, t, IDX_HEADS * IDX_DIM), lambda b, i: (b, i, 0)),
            pl.BlockSpec((1, seq, IDX_DIM), lambda b, i: (b, 0, 0)),
            pl.BlockSpec((1, IDX_HEADS, t), lambda b, i: (b, 0, i)),
            pl.BlockSpec((1, t, wa), lambda b, i: (b, i, 0)),
            pl.BlockSpec((1, seq, wa), lambda b, i: (b, 0, 0)),
            pl.BlockSpec((1, nt, wa, t), lambda b, i: (b, 0, 0, 0)),
        ],
        out_specs=pl.BlockSpec((1, t, wa), lambda b, i: (b, i, 0)),
        out_shape=jax.ShapeDtypeStruct((batch, seq, wa), BF16),
        scratch_shapes=[
            pltpu.VMEM((nt, t, t), I32), pltpu.VMEM((nt, t, t), I16), pltpu.VMEM((nt, t, t), F32),
            pltpu.VMEM((1, t), I32), pltpu.VMEM((1, t), I32),
            pltpu.VMEM((A_HEADS, 1, t), F32), pltpu.VMEM((A_HEADS, 1, t), F32),
            pltpu.VMEM((A_HEADS, HEAD_DIM, t), F32),
        ],
        compiler_params=_params("arbitrary", "arbitrary"),
        name="dsa_attn",
    )(qi, ki, wt, qa, ka, vat)


def _out_kernel(x_ref, mod_ref, oa_ref, ob_ref, oc_ref, w_ref, o_ref):
    wa = A_HEADS * HEAD_DIM
    wb = B_HEADS * HEAD_DIM
    mixed = (jnp.dot(oa_ref[...], w_ref[0:wa, :], preferred_element_type=F32)
             + jnp.dot(ob_ref[...], w_ref[wa:wa + wb, :], preferred_element_type=F32)
             + jnp.dot(oc_ref[...], w_ref[wa + wb:, :], preferred_element_type=F32))
    o_ref[...] = x_ref[...] + (1.0 + mod_ref[0, 5:6, :]) * mixed


def _out_proj(x2, mod, oa, ob, oc, w, seq):
    T, D = x2.shape
    tm = min(512, seq)
    per_seq = seq // tm
    return pl.pallas_call(
        _out_kernel,
        grid=(T // tm,),
        in_specs=[
            pl.BlockSpec((tm, D), lambda i: (i, 0)),
            pl.BlockSpec((1, N_MOD, D), lambda i: (i // per_seq, 0, 0)),
            pl.BlockSpec((tm, oa.shape[1]), lambda i: (i, 0)),
            pl.BlockSpec((tm, ob.shape[1]), lambda i: (i, 0)),
            pl.BlockSpec((tm, oc.shape[1]), lambda i: (i, 0)),
            pl.BlockSpec(w.shape, lambda i: (0, 0)),
        ],
        out_specs=pl.BlockSpec((tm, D), lambda i: (i, 0)),
        out_shape=jax.ShapeDtypeStruct((T, D), F32),
        compiler_params=_params("arbitrary"),
        name="out_proj",
    )(x2, mod, oa, ob, oc, w)


def _rope_tables(seq):
    def tables(dim):
        inv = 1.0 / (ROPE_THETA ** (jnp.arange(0, dim, 2, dtype=F32) / dim))
        ang = jnp.arange(seq, dtype=F32)[:, None] * inv[None, :]
        return jnp.cos(ang), jnp.sin(ang)

    def lane_tables(cos, sin, fill):
        half = cos.shape[1]
        rest = LANES - 2 * half
        zeros_h = jnp.zeros((seq, half), F32)
        zeros_r = jnp.zeros((seq, rest), F32)
        return [jnp.concatenate([cos, cos, jnp.full((seq, rest), fill, F32)], axis=1),
                jnp.concatenate([-sin, zeros_h, zeros_r], axis=1),
                jnp.concatenate([zeros_h, sin, zeros_r], axis=1)]

    cos_p, sin_p = tables(PARTIAL_ROPE_DIM)
    cos_m, sin_m = tables(MLA_ROPE)
    return jnp.stack(lane_tables(cos_p, sin_p, 1.0) + lane_tables(cos_m, sin_m, 0.0))


def _pad_cols(a, width):
    return jnp.pad(a, ((0, 0), (0, width - a.shape[1])))


def _layer_weights(w_in, w_uq, w_ukv):
    w_in_p = jnp.concatenate([
        w_in[:, 0:2560],
        w_in[:, 3280:4816],
        _pad_cols(w_in[:, 2640:3088], Q_RANK_PAD),
        w_in[:, 3088:3216],
        _pad_cols(w_in[:, 3216:3280], LANES),
        _pad_cols(w_in[:, 2560:2640], LANES),
    ], axis=1).astype(BF16)
    wuq = w_uq.reshape(MLA_Q_RANK, B_HEADS, MLA_NOPE + MLA_ROPE)
    wuq = jnp.pad(wuq, ((0, Q_RANK_PAD - MLA_Q_RANK), (0, 0), (0, 2 * LANES - MLA_NOPE - MLA_ROPE)))
    wuq = wuq.reshape(Q_RANK_PAD, B_HEADS * 2 * LANES).astype(BF16)
    wukv = w_ukv.reshape(MLA_KV_RANK, B_HEADS, MLA_NOPE + HEAD_DIM)
    wukv = jnp.concatenate([wukv[:, :, :MLA_NOPE].reshape(MLA_KV_RANK, -1),
                            wukv[:, :, MLA_NOPE:].reshape(MLA_KV_RANK, -1)], axis=1).astype(BF16)
    return w_in_p, wuq, wukv


def kernel(x, c, w_ada, b_ada, g_ffn1, w1_gate, w1_up, w1_down, g_mix, w_in, g_qa, g_ka, g_cq, g_ckv, w_uq, w_ukv, g_q_nope, g_k_nope, g_q_rope, g_k_rope, w_out, g_ffn2, w2_gate, w2_up, w2_down):
    batch, seq, d_model = x.shape
    depth = w_ada.shape[0]
    topk = min(TOPK_MAX, seq // 4)
    tabs = _rope_tables(seq)
    mods = _ada(c, w_ada, b_ada).reshape(depth, batch, N_MOD, d_model)
    x2 = x.reshape(batch * seq, d_model)

    for l in range(depth):
        mod = mods[l]
        x2 = _ffn(x2, mod, g_ffn1[l], w1_gate[l].astype(BF16), w1_up[l].astype(BF16),
                  w1_down[l].astype(BF16), 0, seq)

        w_in_p, wuq, wukv = _layer_weights(w_in[l], w_uq[l], w_ukv[l])
        gains = [g_qa[l][None, :], g_ka[l][None, :], _pad_cols(g_cq[l][None, :], Q_RANK_PAD),
                 g_ckv[l][None, :], g_q_nope[l][None, :], g_k_nope[l][None, :],
                 _pad_cols(g_q_rope[l][None, :], LANES), _pad_cols(g_k_rope[l][None, :], LANES)]
        (qa, ka, vat, qi, ki, wt, qcat, kcat, vbt, qc, kc, vct) = _prep(
            x2, mod, g_mix[l], w_in_p, tabs, gains, wuq, wukv, batch, seq)

        out_a = _dsa(qi, ki, wt, qa, ka, vat, batch, seq, topk)
        out_b = _mla(qcat, kcat, vbt, batch, seq)
        out_c = _sb(qc, kc, vct, batch, seq)
        x2 = _out_proj(x2, mod, out_a.reshape(batch * seq, -1), out_b.reshape(batch * seq, -1),
                       out_c.reshape(batch * seq, -1), w_out[l].astype(BF16), seq)

        x2 = _ffn(x2, mod, g_ffn2[l], w2_gate[l].astype(BF16), w2_up[l].astype(BF16),
                  w2_down[l].astype(BF16), 6, seq)

    return x2.reshape(batch, seq, d_model)
```

```python
import functools

import jax
import jax.numpy as jnp
from jax import lax
from jax.experimental import pallas as pl
from jax.experimental.pallas import tpu as pltpu

F32 = jnp.float32
BF16 = jnp.bfloat16
I32 = jnp.int32
I16 = jnp.int16

HEAD_DIM = 128
CHUNK = 64
CHUNK_SHIFT = 6
ROPE_THETA = 500000.0
PARTIAL_ROPE_DIM = HEAD_DIM // 4
NORM_EPS = 1e-6
D_FF = 5632
N_MOD = 9
A_HEADS = 4
IDX_HEADS = 16
IDX_DIM = 64
TOPK_MAX = 256
B_HEADS = 8
MLA_Q_RANK = 448
MLA_KV_RANK = 128
MLA_NOPE = 128
MLA_ROPE = 64
C_HEADS = 4

LANES = 128
KEY_BLOCK = 256
Q_RANK_PAD = 512
VMEM_LIMIT = 56 * 1024 * 1024
FFN_VMEM_LIMIT = 61 * 1024 * 1024

OFF_QA, OFF_KA, OFF_VA, OFF_QI = 0, 512, 1024, 1536
OFF_QC, OFF_KC, OFF_VC = 2560, 3072, 3584
OFF_CQ, OFF_CKV, OFF_KR, OFF_KIW = 4096, 4608, 4736, 4864
N_PROJ = 4992

LOG2E = 1.4426950408889634
A_SCALE = HEAD_DIM ** -0.5 * LOG2E
B_SCALE = (MLA_NOPE + MLA_ROPE) ** -0.5 * LOG2E
C_SCALE = HEAD_DIM ** -0.5
IDX_SCALE = (IDX_DIM ** -0.5) * (IDX_HEADS ** -0.5)

NT_DIMS = (((1,), (1,)), ((), ()))
NEG_INF = float("-inf")
INT_MIN = -2 ** 31


def _params(*sem, vmem_limit=VMEM_LIMIT):
    return pltpu.CompilerParams(dimension_semantics=sem, vmem_limit_bytes=vmem_limit)


def _ada_kernel(c_ref, w_ref, b_ref, o_ref):
    c = c_ref[...]
    ca = (c * jax.nn.sigmoid(c)).astype(BF16)
    o_ref[0] = jnp.dot(ca, w_ref[0].astype(BF16), preferred_element_type=F32) + b_ref[0]


def _ada(c, w_ada, b_ada):
    L, D, N = w_ada.shape
    B = c.shape[0]
    tn = 1024
    return pl.pallas_call(
        _ada_kernel,
        grid=(L, N // tn),
        in_specs=[
            pl.BlockSpec((B, D), lambda l, j: (0, 0)),
            pl.BlockSpec((1, D, tn), lambda l, j: (l, 0, j)),
            pl.BlockSpec((1, 1, tn), lambda l, j: (l, 0, j)),
        ],
        out_specs=pl.BlockSpec((1, B, tn), lambda l, j: (l, 0, j)),
        out_shape=jax.ShapeDtypeStruct((L, B, N), F32),
        compiler_params=_params("arbitrary", "arbitrary"),
        name="ada_mod",
    )(c, w_ada, b_ada.reshape(L, 1, N))


NORM_ROWS = 16


def _norm_mod_into(h_ref, x_ref, g, shift, scale):
    gain = g * (1.0 + scale)

    def rows(r, carry):
        sl = pl.ds(pl.multiple_of(r * NORM_ROWS, NORM_ROWS), NORM_ROWS)
        x = x_ref[sl, :]
        y = x * lax.rsqrt(jnp.mean(x * x, axis=-1, keepdims=True) + NORM_EPS)
        h_ref[sl, :] = (y * gain + shift).astype(BF16)
        return carry

    lax.fori_loop(0, x_ref.shape[0] // NORM_ROWS, rows, 0, unroll=8)


def _ffn_kernel(x_ref, mod_ref, g_ref, wg_ref, wu_ref, wd_ref, o_ref, h_ref, *, row):
    j = pl.program_id(1)

    @pl.when(j == 0)
    def _():
        _norm_mod_into(h_ref, x_ref, g_ref[...], mod_ref[0, row:row + 1, :], mod_ref[0, row + 1:row + 2, :])
        o_ref[...] = jnp.zeros(o_ref.shape, F32)

    h = h_ref[...]
    g = jnp.dot(h, wg_ref[...], preferred_element_type=F32)
    u = jnp.dot(h, wu_ref[...], preferred_element_type=F32)
    a = ((g * jax.nn.sigmoid(g)) * u).astype(BF16)
    o_ref[...] += jnp.dot(a, wd_ref[...], preferred_element_type=F32)

    @pl.when(j == pl.num_programs(1) - 1)
    def _():
        gate = mod_ref[0, row + 2:row + 3, :]
        o_ref[...] = x_ref[...] + (0.5 * (1.0 + gate)) * o_ref[...]


def _ffn(x2, mod, g, wg, wu, wd, row, seq):
    T, D = x2.shape
    F = wg.shape[1]
    tm = min(1024, seq)
    tf = 512
    per_seq = seq // tm
    return pl.pallas_call(
        functools.partial(_ffn_kernel, row=row),
        grid=(T // tm, F // tf),
        in_specs=[
            pl.BlockSpec((tm, D), lambda i, j: (i, 0)),
            pl.BlockSpec((1, N_MOD, D), lambda i, j: (i // per_seq, 0, 0)),
            pl.BlockSpec((1, D), lambda i, j: (0, 0)),
            pl.BlockSpec((D, tf), lambda i, j: (0, j)),
            pl.BlockSpec((D, tf), lambda i, j: (0, j)),
            pl.BlockSpec((tf, D), lambda i, j: (j, 0)),
        ],
        out_specs=pl.BlockSpec((tm, D), lambda i, j: (i, 0)),
        out_shape=jax.ShapeDtypeStruct((T, D), F32),
        scratch_shapes=[pltpu.VMEM((tm, D), BF16)],
        compiler_params=_params("arbitrary", "arbitrary", vmem_limit=FFN_VMEM_LIMIT),
        name="ffn",
    )(x2, mod, g.reshape(1, D), wg, wu, wd)


def _rms_lanes(x, g, n):
    return x * lax.rsqrt(jnp.sum(x * x, axis=-1, keepdims=True) / n + NORM_EPS) * g


def _rope_lanes(x, cos, sin_lo, sin_hi, half):
    return (x * cos + pltpu.roll(x, LANES - half, 1) * sin_lo + pltpu.roll(x, half, 1) * sin_hi)


def _prep_kernel(x_ref, mod_ref, gmix_ref, win_ref, tab_ref,
                 gqa_ref, gka_ref, gcq_ref, gckv_ref, gqn_ref, gkn_ref, gqr_ref, gkr_ref,
                 wuq_ref, wukv_ref,
                 qa_o, ka_o, vat_o, qi_o, ki_o, wt_o, qcat_o, kcat_o, vbt_o, qc_o, kc_o, vct_o, h_ref, t_ref):
    ca, sa_lo, sa_hi = tab_ref[0], tab_ref[1], tab_ref[2]
    cm, sm_lo, sm_hi = tab_ref[3], tab_ref[4], tab_ref[5]
    half_a = PARTIAL_ROPE_DIM // 2
    half_m = MLA_ROPE // 2
    wa, wc = A_HEADS * HEAD_DIM, C_HEADS * HEAD_DIM

    _norm_mod_into(h_ref, x_ref, gmix_ref[...], mod_ref[0, 3:4, :], mod_ref[0, 4:5, :])

    def transposed(v):
        t_ref[:, 0:v.shape[1]] = v
        return t_ref[:, 0:v.shape[1]].T

    def proj(lo, width):
        return jnp.dot(h_ref[...], win_ref[:, lo:lo + width], preferred_element_type=F32)

    qa = proj(OFF_QA, wa)
    for h in range(A_HEADS):
        lo = h * HEAD_DIM
        q = _rms_lanes(qa[:, lo:lo + HEAD_DIM], gqa_ref[...], HEAD_DIM)
        t_ref[:, lo:lo + HEAD_DIM] = _rope_lanes(q, ca, sa_lo, sa_hi, half_a) * A_SCALE
    qa_o[0, 0] = t_ref[:, 0:wa].T.astype(BF16)
    ka = proj(OFF_KA, wa)
    for h in range(A_HEADS):
        lo = h * HEAD_DIM
        k = _rms_lanes(ka[:, lo:lo + HEAD_DIM], gka_ref[...], HEAD_DIM)
        ka_o[0, :, lo:lo + HEAD_DIM] = _rope_lanes(k, ca, sa_lo, sa_hi, half_a).astype(BF16)
    vat_o[0, 0] = transposed(proj(OFF_VA, wa)).astype(BF16)
    qi_o[0, 0] = transposed(proj(OFF_QI, IDX_HEADS * IDX_DIM)).astype(BF16)

    qc_o[0, 0] = transposed(proj(OFF_QC, wc) * C_SCALE).astype(BF16)
    kc_o[0] = proj(OFF_KC, wc).astype(BF16)
    vct_o[0, 0] = transposed(proj(OFF_VC, wc)).astype(BF16)

    cq = _rms_lanes(proj(OFF_CQ, Q_RANK_PAD), gcq_ref[...], MLA_Q_RANK)
    qb = jnp.dot(cq.astype(BF16), wuq_ref[...], preferred_element_type=F32)
    small = proj(OFF_CKV, N_PROJ - OFF_CKV)
    ckv = _rms_lanes(small[:, 0:MLA_KV_RANK], gckv_ref[...], MLA_KV_RANK)
    kvb = jnp.dot(ckv.astype(BF16), wukv_ref[...], preferred_element_type=F32)
    kr = _rms_lanes(small[:, OFF_KR - OFF_CKV:OFF_KR - OFF_CKV + LANES], gkr_ref[...], MLA_ROPE)
    kr = _rope_lanes(kr, cm, sm_lo, sm_hi, half_m).astype(BF16)
    kiw = small[:, OFF_KIW - OFF_CKV:OFF_KIW - OFF_CKV + LANES]
    ki_o[0] = kiw[:, :IDX_DIM].astype(BF16)
    wt_o[0] = transposed(kiw)[IDX_DIM:IDX_DIM + IDX_HEADS, :] * IDX_SCALE
    for h in range(B_HEADS):
        lo = h * 2 * LANES
        kn = _rms_lanes(kvb[:, h * MLA_NOPE:(h + 1) * MLA_NOPE], gkn_ref[...], MLA_NOPE)
        kcat_o[0, :, lo:lo + MLA_NOPE] = kn.astype(BF16)
        kcat_o[0, :, lo + MLA_NOPE:lo + 2 * LANES] = kr
    vbt_o[0, 0] = transposed(kvb[:, B_HEADS * MLA_NOPE:]).astype(BF16)
    for h in range(B_HEADS):
        lo = h * 2 * LANES
        qn = _rms_lanes(qb[:, lo:lo + MLA_NOPE], gqn_ref[...], MLA_NOPE)
        t_ref[:, lo:lo + MLA_NOPE] = qn * B_SCALE
        qr = _rms_lanes(qb[:, lo + MLA_NOPE:lo + 2 * LANES], gqr_ref[...], MLA_ROPE)
        t_ref[:, lo + MLA_NOPE:lo + 2 * LANES] = _rope_lanes(qr, cm, sm_lo, sm_hi, half_m) * B_SCALE
    qcat_o[0, 0] = t_ref[...].T.astype(BF16)


def _resident(shape):
    return pl.BlockSpec(shape, lambda b, i: (0,) * len(shape), pipeline_mode=pl.Buffered(1))


def _prep(x2, mod, g_mix, w_in_p, tabs, gains, wuq, wukv, batch, seq):
    tm = KEY_BLOCK
    nt = seq // tm
    d_model = x2.shape[1]
    wa, wb, wc = A_HEADS * HEAD_DIM, B_HEADS * HEAD_DIM, C_HEADS * HEAD_DIM
    wcat = B_HEADS * 2 * LANES

    def tok(width, dtype):
        return (jax.ShapeDtypeStruct((batch, seq, width), dtype),
                pl.BlockSpec((1, tm, width), lambda b, i: (b, i, 0)))

    def tok_t(width):
        return (jax.ShapeDtypeStruct((batch, nt, width, tm), BF16),
                pl.BlockSpec((1, 1, width, tm), lambda b, i: (b, i, 0, 0)))

    outs = [tok_t(wa), tok(wa, BF16), tok_t(wa), tok_t(IDX_HEADS * IDX_DIM), tok(IDX_DIM, BF16),
            (jax.ShapeDtypeStruct((batch, IDX_HEADS, seq), F32),
             pl.BlockSpec((1, IDX_HEADS, tm), lambda b, i: (b, 0, i))),
            tok_t(wcat), tok(wcat, BF16), tok_t(wb), tok_t(wc), tok(wc, BF16), tok_t(wc)]
    return pl.pallas_call(
        _prep_kernel,
        grid=(batch, nt),
        in_specs=[pl.BlockSpec((tm, d_model), lambda b, i: (b * nt + i, 0)),
                  pl.BlockSpec((1, N_MOD, d_model), lambda b, i: (b, 0, 0)),
                  _resident((1, d_model)),
                  _resident(w_in_p.shape),
                  pl.BlockSpec((6, tm, LANES), lambda b, i: (0, i, 0))]
                 + [_resident(g.shape) for g in gains]
                 + [_resident(wuq.shape), _resident(wukv.shape)],
        out_specs=[o[1] for o in outs],
        out_shape=[o[0] for o in outs],
        scratch_shapes=[pltpu.VMEM((tm, d_model), BF16), pltpu.VMEM((tm, wcat), F32)],
        compiler_params=_params("arbitrary", "arbitrary"),
        name="head_prep",
    )(x2, mod, g_mix.reshape(1, d_model), w_in_p, tabs, *gains, wuq, wukv)


MLA_HEADS_PER_STEP = 4


def _mla_kernel(qt_ref, k_ref, vt_ref, o_ref, s0_ref, s1_ref, m_ref, l_ref, acc_ref, *, t):
    qb = pl.program_id(2)
    sub = t // KEY_BLOCK
    dq = 2 * LANES
    m_ref[...] = jnp.full(m_ref.shape, NEG_INF, F32)
    l_ref[...] = jnp.zeros(l_ref.shape, F32)
    acc_ref[...] = jnp.zeros(acc_ref.shape, F32)

    def produce(s_ref, kb):
        ks = pl.multiple_of(kb * t, t)
        for h in range(MLA_HEADS_PER_STEP):
            k = k_ref[0, pl.ds(ks, t), h * dq:(h + 1) * dq]
            for c in range(sub):
                s_ref[h, :, c * KEY_BLOCK:(c + 1) * KEY_BLOCK] = jnp.dot(
                    k, qt_ref[0, c, h * dq:(h + 1) * dq, :], preferred_element_type=F32)

    def consume(s_ref, kb, diagonal):
        for h in range(MLA_HEADS_PER_STEP):
            s = s_ref[h]
            if diagonal:
                kc = lax.broadcasted_iota(I32, (t, t), 0) >> CHUNK_SHIFT
                qc = lax.broadcasted_iota(I32, (t, t), 1) >> CHUNK_SHIFT
                s = jnp.where(kc <= qc, s, NEG_INF)
            m_prev = m_ref[h]
            m_new = jnp.maximum(m_prev, jnp.max(s, axis=0, keepdims=True))
            p = jnp.exp2(s - m_new)
            alpha = jnp.exp2(m_prev - m_new)
            l_ref[h] = alpha * l_ref[h] + jnp.sum(p, axis=0, keepdims=True)
            p = p.astype(BF16)
            pv = jnp.dot(vt_ref[0, kb * sub, h * HEAD_DIM:(h + 1) * HEAD_DIM, :], p[0:KEY_BLOCK],
                         preferred_element_type=F32)
            for c in range(1, sub):
                pv += jnp.dot(vt_ref[0, kb * sub + c, h * HEAD_DIM:(h + 1) * HEAD_DIM, :],
                              p[c * KEY_BLOCK:(c + 1) * KEY_BLOCK], preferred_element_type=F32)
            acc_ref[h] = alpha * acc_ref[h] + pv
            m_ref[h] = m_new

    produce(s0_ref, 0)

    def body(j, carry):
        produce(s1_ref, 2 * j + 1)
        consume(s0_ref, 2 * j, False)
        produce(s0_ref, 2 * j + 2)
        consume(s1_ref, 2 * j + 1, False)
        return carry

    lax.fori_loop(0, qb // 2, body, 0)

    @pl.when(qb % 2 == 0)
    def _():
        consume(s0_ref, qb, True)

    @pl.when(qb % 2 == 1)
    def _():
        produce(s1_ref, qb)
        consume(s0_ref, qb - 1, False)
        consume(s1_ref, qb, True)

    for h in range(MLA_HEADS_PER_STEP):
        o_ref[0, :, h * HEAD_DIM:(h + 1) * HEAD_DIM] = (acc_ref[h] / l_ref[h]).T.astype(BF16)


def _mla(qcat, kcat, vbt, batch, seq):
    t = min(512, seq)
    hp = MLA_HEADS_PER_STEP
    nkb = seq // KEY_BLOCK
    return pl.pallas_call(
        functools.partial(_mla_kernel, t=t),
        grid=(batch, B_HEADS // hp, seq // t),
        in_specs=[
            pl.BlockSpec((1, t // KEY_BLOCK, hp * 2 * LANES, KEY_BLOCK), lambda b, h, i: (b, i, h, 0)),
            pl.BlockSpec((1, seq, hp * 2 * LANES), lambda b, h, i: (b, 0, h)),
            pl.BlockSpec((1, nkb, hp * HEAD_DIM, KEY_BLOCK), lambda b, h, i: (b, 0, h, 0)),
        ],
        out_specs=pl.BlockSpec((1, t, hp * HEAD_DIM), lambda b, h, i: (b, i, h)),
        out_shape=jax.ShapeDtypeStruct((batch, seq, B_HEADS * HEAD_DIM), BF16),
        scratch_shapes=[pltpu.VMEM((hp, t, t), F32), pltpu.VMEM((hp, t, t), F32),
                        pltpu.VMEM((hp, 1, t), F32), pltpu.VMEM((hp, 1, t), F32),
                        pltpu.VMEM((hp, HEAD_DIM, t), F32)],
        compiler_params=_params("arbitrary", "arbitrary", "arbitrary"),
        name="mla_attn",
    )(qcat, kcat, vbt)


SB_HEADS_PER_STEP = 4


def _sb_kernel(qt_ref, k_ref, vt_ref, o_ref, r_ref, acc_ref, *, tq):
    qb = pl.program_id(2)
    tk = KEY_BLOCK
    sub = tq // tk
    tri_r = lax.broadcasted_iota(I32, (tk, tk), 0)
    tri_c = lax.broadcasted_iota(I32, (tk, tk), 1)
    later_keys = jnp.where(tri_c > tri_r, 1.0, 0.0).astype(BF16)
    r_ref[...] = jnp.zeros(r_ref.shape, F32)
    acc_ref[...] = jnp.zeros(acc_ref.shape, F32)

    def step(kb, key_offset):
        ks = pl.multiple_of(kb * tk, tk)
        heads = range(SB_HEADS_PER_STEP)
        if key_offset is not None:
            before = (lax.broadcasted_iota(I32, (tk, tq), 0) + key_offset
                      < lax.broadcasted_iota(I32, (tk, tq), 1))
        zs = [jnp.concatenate(
            [jnp.dot(k_ref[0, pl.ds(ks, tk), h * HEAD_DIM:(h + 1) * HEAD_DIM],
                     qt_ref[0, c, h * HEAD_DIM:(h + 1) * HEAD_DIM, :], preferred_element_type=F32)
             for c in range(sub)], axis=1) for h in heads]
        go = []
        for z in zs:
            neg_abs = lax.bitcast_convert_type(lax.bitcast_convert_type(z, I32) | INT_MIN, F32)
            g = jnp.maximum(z, 0.0) + jnp.log(1.0 + jnp.exp(neg_abs))
            if key_offset is not None:
                g = jnp.where(before, g, 0.0)
            go.append(g)
        later = []
        for g in go:
            hi = g.astype(BF16)
            lo = (g - hi.astype(F32)).astype(BF16)
            later.append(jnp.dot(later_keys, hi, preferred_element_type=F32)
                         + jnp.dot(later_keys, lo, preferred_element_type=F32))
        for h in heads:
            w = jnp.exp(zs[h] - (go[h] + later[h] + r_ref[h]))
            if key_offset is not None:
                w = jnp.where(before, w, 0.0)
            acc_ref[h] += jnp.dot(vt_ref[0, kb, h * HEAD_DIM:(h + 1) * HEAD_DIM, :], w.astype(BF16),
                                  preferred_element_type=F32)
            r_ref[h] += jnp.sum(go[h], axis=0, keepdims=True)

    for d in reversed(range(sub)):
        step(qb * sub + d, d * tk)

    def body(j, carry):
        step(qb * sub - 1 - j, None)
        return carry

    lax.fori_loop(0, qb * sub, body, 0)
    for h in range(SB_HEADS_PER_STEP):
        o_ref[0, :, h * HEAD_DIM:(h + 1) * HEAD_DIM] = acc_ref[h].T.astype(BF16)


def _sb(qc, kc, vct, batch, seq):
    tq = min(512, seq)
    hp = SB_HEADS_PER_STEP
    nkb = seq // KEY_BLOCK
    return pl.pallas_call(
        functools.partial(_sb_kernel, tq=tq),
        grid=(batch, C_HEADS // hp, seq // tq),
        in_specs=[
            pl.BlockSpec((1, tq // KEY_BLOCK, hp * HEAD_DIM, KEY_BLOCK), lambda b, h, i: (b, i, h, 0)),
            pl.BlockSpec((1, seq, hp * HEAD_DIM), lambda b, h, i: (b, 0, h)),
            pl.BlockSpec((1, nkb, hp * HEAD_DIM, KEY_BLOCK), lambda b, h, i: (b, 0, h, 0)),
        ],
        out_specs=pl.BlockSpec((1, tq, hp * HEAD_DIM), lambda b, h, i: (b, i, h)),
        out_shape=jax.ShapeDtypeStruct((batch, seq, C_HEADS * HEAD_DIM), BF16),
        scratch_shapes=[pltpu.VMEM((hp, 1, tq), F32), pltpu.VMEM((hp, HEAD_DIM, tq), F32)],
        compiler_params=_params("arbitrary", "arbitrary", "arbitrary"),
        name="sb_attn",
    )(qc, kc, vct)


def _dsa_kernel(qit_ref, ki_ref, wt_ref, qat_ref, ka_ref, vt_ref, o_ref,
                keys_ref, half_ref, bias_ref, thr_ref, lim_ref, m_ref, l_ref, acc_ref, *, t, topk, seq):
    qb = pl.program_id(1)
    nk = qb + 1
    row = lax.broadcasted_iota(I32, (t, t), 0)
    col = lax.broadcasted_iota(I32, (t, t), 1)
    wt = wt_ref[0]

    def score_block(kb, carry):
        ks = pl.multiple_of(kb * t, t)
        kix = ki_ref[0, pl.ds(ks, t), :]
        sc = jnp.zeros((t, t), F32)
        for h in range(IDX_HEADS):
            lg = jnp.dot(kix, qit_ref[0, 0, h * IDX_DIM:(h + 1) * IDX_DIM, :],
                         preferred_element_type=F32)
            sc = sc + wt[h:h + 1, :] * jnp.maximum(lg, 0.0)
        bits = lax.bitcast_convert_type(sc, I32)
        key = bits ^ ((bits >> 31) & 0x7FFFFFFF)
        visible = ((ks + row) >> CHUNK_SHIFT) <= ((qb * t + col) >> CHUNK_SHIFT)
        key = jnp.where(visible, key, INT_MIN)
        keys_ref[kb] = key
        half_ref[kb] = (key >> 16).astype(I16)
        return carry

    lax.fori_loop(0, nk, score_block, 0)

    i16_min, i16_max = -2 ** 15, 2 ** 15 - 1

    def search16():
        def count16(cand):
            cand16 = cand.astype(I16)

            def block(kb, acc):
                hit = jnp.where(half_ref[kb] >= cand16, jnp.ones((), I16), jnp.zeros((), I16))
                for i in range(t // 16):
                    acc = acc + hit[i * 16:(i + 1) * 16]
                return acc
            acc = lax.fori_loop(0, nk, block, jnp.zeros((16, t), I16))
            return jnp.sum(acc.astype(I32), axis=0, keepdims=True)

        v0 = jnp.where(count16(jnp.zeros((1, t), I32)) >= topk, 0, i16_min).astype(I32)

        def bit(b, v):
            cand = v | lax.shift_left(jnp.int32(1), 14 - b)
            return jnp.where(count16(cand) >= topk, cand, v)

        return lax.fori_loop(0, 15, bit, v0)

    def count(pred):
        def block(kb, acc):
            hit = jnp.where(pred(keys_ref[kb], kb * t + row), 1, 0).astype(I32)
            return acc + jnp.sum(hit.reshape(t // 8, 8, t), axis=0)
        acc = lax.fori_loop(0, nk, block, jnp.zeros((8, t), I32))
        return jnp.sum(acc, axis=0, keepdims=True)

    thr_hi = search16()

    def low_half_block(kb, carry):
        key = keys_ref[kb]
        hi = key >> 16
        low = (key & 0xFFFF) + i16_min
        half_ref[kb] = jnp.where(hi == thr_hi, low, jnp.where(hi > thr_hi, i16_max, i16_min)).astype(I16)
        return carry

    lax.fori_loop(0, nk, low_half_block, 0)
    thr_lo = search16()
    thr = lax.shift_left(thr_hi, 16) + (thr_lo - i16_min)
    n_ge = count(lambda k, i: k >= thr)
    n_gt = count(lambda k, i: k > thr)
    tie = (thr != INT_MIN) & (n_ge > topk)
    need = topk - n_gt
    thr_ref[...] = thr
    lim_ref[...] = jnp.where(thr == INT_MIN, 0, seq).astype(I32)

    @pl.when(jnp.max(tie.astype(I32)) > 0)
    def _():
        nbits = seq.bit_length() - 1

        def lim_bit(b, lim):
            cand = lim | lax.shift_left(jnp.int32(1), nbits - 1 - b)
            below = count(lambda k, i: (k == thr) & (i < cand))
            return jnp.where(below < need, cand, lim)

        lim = lax.fori_loop(0, nbits, lim_bit, jnp.zeros((1, t), I32))
        lim_ref[...] = jnp.where(tie, lim + 1, lim_ref[...])

    def bias_block(kb, carry):
        key = keys_ref[kb]
        thr_q = thr_ref[...]
        at_thr = jnp.where((kb * t + row) < lim_ref[...], 0.0, NEG_INF)
        bias_ref[kb] = jnp.where(key > thr_q, 0.0, jnp.where(key == thr_q, at_thr, NEG_INF))
        return carry

    lax.fori_loop(0, nk, bias_block, 0)
    m_ref[...] = jnp.full(m_ref.shape, NEG_INF, F32)
    l_ref[...] = jnp.zeros(l_ref.shape, F32)
    acc_ref[...] = jnp.zeros(acc_ref.shape, F32)

    def attend_block(kb, carry):
        ks = pl.multiple_of(kb * t, t)
        bias = bias_ref[kb]
        scores = [jnp.dot(ka_ref[0, pl.ds(ks, t), h * HEAD_DIM:(h + 1) * HEAD_DIM],
                          qat_ref[0, 0, h * HEAD_DIM:(h + 1) * HEAD_DIM, :],
                          preferred_element_type=F32) + bias for h in range(A_HEADS)]
        for h in range(A_HEADS):
            lo = h * HEAD_DIM
            s = scores[h]
            m_prev = m_ref[h]
            m_new = jnp.maximum(m_prev, jnp.max(s, axis=0, keepdims=True))
            m_safe = jnp.where(m_new == NEG_INF, 0.0, m_new)
            p = jnp.exp2(s - m_safe)
            alpha = jnp.exp2(m_prev - m_safe)
            l_ref[h] = alpha * l_ref[h] + jnp.sum(p, axis=0, keepdims=True)
            acc_ref[h] = alpha * acc_ref[h] + jnp.dot(vt_ref[0, kb, lo:lo + HEAD_DIM, :], p.astype(BF16),
                                                      preferred_element_type=F32)
            m_ref[h] = m_new
        return carry

    lax.fori_loop(0, nk, attend_block, 0)
    for h in range(A_HEADS):
        o_ref[0, :, h * HEAD_DIM:(h + 1) * HEAD_DIM] = (acc_ref[h] / l_ref[h]).T.astype(BF16)


def _dsa(qi, ki, wt, qa, ka, vat, batch, seq, topk):
    t = KEY_BLOCK
    nt = seq // t
    wa = A_HEADS * HEAD_DIM
    return pl.pallas_call(
        functools.partial(_dsa_kernel, t=t, topk=topk, seq=seq),
        grid=(batch, nt),
        in_specs=[
            pl.BlockSpec((1, 1, IDX_HEADS * IDX_DIM, t), lambda b, i: (b, i, 0, 0)),
            pl.BlockSpec((1, seq, IDX_DIM), lambda b, i: (b, 0, 0)),
            pl.BlockSpec((1, IDX_HEADS, t), lambda b, i: (b, 0, i)),
            pl.BlockSpec((1, 1, wa, t), lambda b, i: (b, i, 0, 0)),
            pl.BlockSpec((1, seq, wa), lambda b, i: (b, 0, 0)),
            pl.BlockSpec((1, nt, wa, t), lambda b, i: (b, 0, 0, 0)),
        ],
        out_specs=pl.BlockSpec((1, t, wa), lambda b, i: (b, i, 0)),
        out_shape=jax.ShapeDtypeStruct((batch, seq, wa), BF16),
        scratch_shapes=[
            pltpu.VMEM((nt, t, t), I32), pltpu.VMEM((nt, t, t), I16), pltpu.VMEM((nt, t, t), F32),
            pltpu.VMEM((1, t), I32), pltpu.VMEM((1, t), I32),
            pltpu.VMEM((A_HEADS, 1, t), F32), pltpu.VMEM((A_HEADS, 1, t), F32),
            pltpu.VMEM((A_HEADS, HEAD_DIM, t), F32),
        ],
        compiler_params=_params("arbitrary", "arbitrary"),
        name="dsa_attn",
    )(qi, ki, wt, qa, ka, vat)


def _out_kernel(x_ref, mod_ref, oa_ref, ob_ref, oc_ref, w_ref, o_ref):
    wa = A_HEADS * HEAD_DIM
    wb = B_HEADS * HEAD_DIM
    mixed = (jnp.dot(oa_ref[...], w_ref[0:wa, :], preferred_element_type=F32)
             + jnp.dot(ob_ref[...], w_ref[wa:wa + wb, :], preferred_element_type=F32)
             + jnp.dot(oc_ref[...], w_ref[wa + wb:, :], preferred_element_type=F32))
    o_ref[...] = x_ref[...] + (1.0 + mod_ref[0, 5:6, :]) * mixed


def _out_proj(x2, mod, oa, ob, oc, w, seq):
    T, D = x2.shape
    tm = min(512, seq)
    per_seq = seq // tm
    return pl.pallas_call(
        _out_kernel,
        grid=(T // tm,),
        in_specs=[
            pl.BlockSpec((tm, D), lambda i: (i, 0)),
            pl.BlockSpec((1, N_MOD, D), lambda i: (i // per_seq, 0, 0)),
            pl.BlockSpec((tm, oa.shape[1]), lambda i: (i, 0)),
            pl.BlockSpec((tm, ob.shape[1]), lambda i: (i, 0)),
            pl.BlockSpec((tm, oc.shape[1]), lambda i: (i, 0)),
            pl.BlockSpec(w.shape, lambda i: (0, 0)),
        ],
        out_specs=pl.BlockSpec((tm, D), lambda i: (i, 0)),
        out_shape=jax.ShapeDtypeStruct((T, D), F32),
        compiler_params=_params("arbitrary"),
        name="out_proj",
    )(x2, mod, oa, ob, oc, w)


def _rope_tables(seq):
    def tables(dim):
        inv = 1.0 / (ROPE_THETA ** (jnp.arange(0, dim, 2, dtype=F32) / dim))
        ang = jnp.arange(seq, dtype=F32)[:, None] * inv[None, :]
        return jnp.cos(ang), jnp.sin(ang)

    def lane_tables(cos, sin, fill):
        half = cos.shape[1]
        rest = LANES - 2 * half
        zeros_h = jnp.zeros((seq, half), F32)
        zeros_r = jnp.zeros((seq, rest), F32)
        return [jnp.concatenate([cos, cos, jnp.full((seq, rest), fill, F32)], axis=1),
                jnp.concatenate([-sin, zeros_h, zeros_r], axis=1),
                jnp.concatenate([zeros_h, sin, zeros_r], axis=1)]

    cos_p, sin_p = tables(PARTIAL_ROPE_DIM)
    cos_m, sin_m = tables(MLA_ROPE)
    return jnp.stack(lane_tables(cos_p, sin_p, 1.0) + lane_tables(cos_m, sin_m, 0.0))


def _pad_cols(a, width):
    return jnp.pad(a, ((0, 0), (0, width - a.shape[1])))


def _layer_weights(w_in, w_uq, w_ukv):
    w_in_p = jnp.concatenate([
        w_in[:, 0:2560],
        w_in[:, 3280:4816],
        _pad_cols(w_in[:, 2640:3088], Q_RANK_PAD),
        w_in[:, 3088:3216],
        _pad_cols(w_in[:, 3216:3280], LANES),
        _pad_cols(w_in[:, 2560:2640], LANES),
    ], axis=1).astype(BF16)
    wuq = w_uq.reshape(MLA_Q_RANK, B_HEADS, MLA_NOPE + MLA_ROPE)
    wuq = jnp.pad(wuq, ((0, Q_RANK_PAD - MLA_Q_RANK), (0, 0), (0, 2 * LANES - MLA_NOPE - MLA_ROPE)))
    wuq = wuq.reshape(Q_RANK_PAD, B_HEADS * 2 * LANES).astype(BF16)
    wukv = w_ukv.reshape(MLA_KV_RANK, B_HEADS, MLA_NOPE + HEAD_DIM)
    wukv = jnp.concatenate([wukv[:, :, :MLA_NOPE].reshape(MLA_KV_RANK, -1),
                            wukv[:, :, MLA_NOPE:].reshape(MLA_KV_RANK, -1)], axis=1).astype(BF16)
    return w_in_p, wuq, wukv


def kernel(x, c, w_ada, b_ada, g_ffn1, w1_gate, w1_up, w1_down, g_mix, w_in, g_qa, g_ka, g_cq, g_ckv, w_uq, w_ukv, g_q_nope, g_k_nope, g_q_rope, g_k_rope, w_out, g_ffn2, w2_gate, w2_up, w2_down):
    batch, seq, d_model = x.shape
    depth = w_ada.shape[0]
    topk = min(TOPK_MAX, seq // 4)
    tabs = _rope_tables(seq)
    mods = _ada(c, w_ada, b_ada).reshape(depth, batch, N_MOD, d_model)
    x2 = x.reshape(batch * seq, d_model)

    for l in range(depth):
        mod = mods[l]
        x2 = _ffn(x2, mod, g_ffn1[l], w1_gate[l].astype(BF16), w1_up[l].astype(BF16),
                  w1_down[l].astype(BF16), 0, seq)

        w_in_p, wuq, wukv = _layer_weights(w_in[l], w_uq[l], w_ukv[l])
        gains = [g_qa[l][None, :], g_ka[l][None, :], _pad_cols(g_cq[l][None, :], Q_RANK_PAD),
                 g_ckv[l][None, :], g_q_nope[l][None, :], g_k_nope[l][None, :],
                 _pad_cols(g_q_rope[l][None, :], LANES), _pad_cols(g_k_rope[l][None, :], LANES)]
        (qa, ka, vat, qi, ki, wt, qcat, kcat, vbt, qc, kc, vct) = _prep(
            x2, mod, g_mix[l], w_in_p, tabs, gains, wuq, wukv, batch, seq)

        out_a = _dsa(qi, ki, wt, qa, ka, vat, batch, seq, topk)
        out_b = _mla(qcat, kcat, vbt, batch, seq)
        out_c = _sb(qc, kc, vct, batch, seq)
        x2 = _out_proj(x2, mod, out_a.reshape(batch * seq, -1), out_b.reshape(batch * seq, -1),
                       out_c.reshape(batch * seq, -1), w_out[l].astype(BF16), seq)

        x2 = _ffn(x2, mod, g_ffn2[l], w2_gate[l].astype(BF16), w2_up[l].astype(BF16),
                  w2_down[l].astype(BF16), 6, seq)

    return x2.reshape(batch, seq, d_model)
```

```python
import functools

import jax
import jax.numpy as jnp
from jax import lax
from jax.experimental import pallas as pl
from jax.experimental.pallas import tpu as pltpu

F32 = jnp.float32
BF16 = jnp.bfloat16
I32 = jnp.int32
I16 = jnp.int16

HEAD_DIM = 128
CHUNK = 64
CHUNK_SHIFT = 6
ROPE_THETA = 500000.0
PARTIAL_ROPE_DIM = HEAD_DIM // 4
NORM_EPS = 1e-6
D_FF = 5632
N_MOD = 9
A_HEADS = 4
IDX_HEADS = 16
IDX_DIM = 64
TOPK_MAX = 256
B_HEADS = 8
MLA_Q_RANK = 448
MLA_KV_RANK = 128
MLA_NOPE = 128
MLA_ROPE = 64
C_HEADS = 4

LANES = 128
KEY_BLOCK = 256
Q_RANK_PAD = 512
VMEM_LIMIT = 56 * 1024 * 1024
FFN_VMEM_LIMIT = 61 * 1024 * 1024

OFF_QA, OFF_KA, OFF_VA, OFF_QI = 0, 512, 1024, 1536
OFF_QC, OFF_KC, OFF_VC = 2560, 3072, 3584
OFF_CQ, OFF_CKV, OFF_KR, OFF_KIW = 4096, 4608, 4736, 4864
N_PROJ = 4992

LOG2E = 1.4426950408889634
A_SCALE = HEAD_DIM ** -0.5 * LOG2E
B_SCALE = (MLA_NOPE + MLA_ROPE) ** -0.5 * LOG2E
C_SCALE = HEAD_DIM ** -0.5
IDX_SCALE = (IDX_DIM ** -0.5) * (IDX_HEADS ** -0.5)

NT_DIMS = (((1,), (1,)), ((), ()))
NEG_INF = float("-inf")
INT_MIN = -2 ** 31


def _params(*sem, vmem_limit=VMEM_LIMIT):
    return pltpu.CompilerParams(dimension_semantics=sem, vmem_limit_bytes=vmem_limit)


def _ada_kernel(c_ref, w_ref, b_ref, o_ref):
    c = c_ref[...]
    ca = (c * jax.nn.sigmoid(c)).astype(BF16)
    o_ref[0] = jnp.dot(ca, w_ref[0].astype(BF16), preferred_element_type=F32) + b_ref[0]


def _ada(c, w_ada, b_ada):
    L, D, N = w_ada.shape
    B = c.shape[0]
    tn = 1024
    return pl.pallas_call(
        _ada_kernel,
        grid=(L, N // tn),
        in_specs=[
            pl.BlockSpec((B, D), lambda l, j: (0, 0)),
            pl.BlockSpec((1, D, tn), lambda l, j: (l, 0, j)),
            pl.BlockSpec((1, 1, tn), lambda l, j: (l, 0, j)),
        ],
        out_specs=pl.BlockSpec((1, B, tn), lambda l, j: (l, 0, j)),
        out_shape=jax.ShapeDtypeStruct((L, B, N), F32),
        compiler_params=_params("arbitrary", "arbitrary"),
        name="ada_mod",
    )(c, w_ada, b_ada.reshape(L, 1, N))


NORM_ROWS = 16


def _norm_mod_into(h_ref, x_ref, g, shift, scale):
    gain = g * (1.0 + scale)

    def rows(r, carry):
        sl = pl.ds(pl.multiple_of(r * NORM_ROWS, NORM_ROWS), NORM_ROWS)
        x = x_ref[sl, :]
        y = x * lax.rsqrt(jnp.mean(x * x, axis=-1, keepdims=True) + NORM_EPS)
        h_ref[sl, :] = (y * gain + shift).astype(BF16)
        return carry

    lax.fori_loop(0, x_ref.shape[0] // NORM_ROWS, rows, 0, unroll=8)


def _ffn_kernel(x_ref, mod_ref, g_ref, wg_ref, wu_ref, wd_ref, o_ref, h_ref, *, row):
    j = pl.program_id(1)

    @pl.when(j == 0)
    def _():
        _norm_mod_into(h_ref, x_ref, g_ref[...], mod_ref[0, row:row + 1, :], mod_ref[0, row + 1:row + 2, :])
        o_ref[...] = jnp.zeros(o_ref.shape, F32)

    h = h_ref[...]
    g = jnp.dot(h, wg_ref[...], preferred_element_type=F32)
    u = jnp.dot(h, wu_ref[...], preferred_element_type=F32)
    a = ((g * jax.nn.sigmoid(g)) * u).astype(BF16)
    o_ref[...] += jnp.dot(a, wd_ref[...], preferred_element_type=F32)

    @pl.when(j == pl.num_programs(1) - 1)
    def _():
        gate = mod_ref[0, row + 2:row + 3, :]
        o_ref[...] = x_ref[...] + (0.5 * (1.0 + gate)) * o_ref[...]


def _ffn(x2, mod, g, wg, wu, wd, row, seq):
    T, D = x2.shape
    F = wg.shape[1]
    tm = min(1024, seq)
    tf = 512
    per_seq = seq // tm
    return pl.pallas_call(
        functools.partial(_ffn_kernel, row=row),
        grid=(T // tm, F // tf),
        in_specs=[
            pl.BlockSpec((tm, D), lambda i, j: (i, 0)),
            pl.BlockSpec((1, N_MOD, D), lambda i, j: (i // per_seq, 0, 0)),
            pl.BlockSpec((1, D), lambda i, j: (0, 0)),
            pl.BlockSpec((D, tf), lambda i, j: (0, j)),
            pl.BlockSpec((D, tf), lambda i, j: (0, j)),
            pl.BlockSpec((tf, D), lambda i, j: (j, 0)),
        ],
        out_specs=pl.BlockSpec((tm, D), lambda i, j: (i, 0)),
        out_shape=jax.ShapeDtypeStruct((T, D), F32),
        scratch_shapes=[pltpu.VMEM((tm, D), BF16)],
        compiler_params=_params("arbitrary", "arbitrary", vmem_limit=FFN_VMEM_LIMIT),
        name="ffn",
    )(x2, mod, g.reshape(1, D), wg, wu, wd)


def _rms_lanes(x, g, n):
    return x * lax.rsqrt(jnp.sum(x * x, axis=-1, keepdims=True) / n + NORM_EPS) * g


def _rope_lanes(x, cos, sin_lo, sin_hi, half):
    return (x * cos + pltpu.roll(x, LANES - half, 1) * sin_lo + pltpu.roll(x, half, 1) * sin_hi)


def _prep_kernel(x_ref, mod_ref, gmix_ref, win_ref, tab_ref,
                 gqa_ref, gka_ref, gcq_ref, gckv_ref, gqn_ref, gkn_ref, gqr_ref, gkr_ref,
                 wuq_ref, wukv_ref,
                 qa_o, ka_o, vat_o, qi_o, ki_o, wt_o, qcat_o, kcat_o, vbt_o, qc_o, kc_o, vct_o, h_ref, t_ref):
    ca, sa_lo, sa_hi = tab_ref[0], tab_ref[1], tab_ref[2]
    cm, sm_lo, sm_hi = tab_ref[3], tab_ref[4], tab_ref[5]
    half_a = PARTIAL_ROPE_DIM // 2
    half_m = MLA_ROPE // 2
    wa, wc = A_HEADS * HEAD_DIM, C_HEADS * HEAD_DIM

    _norm_mod_into(h_ref, x_ref, gmix_ref[...], mod_ref[0, 3:4, :], mod_ref[0, 4:5, :])

    def transposed(v):
        t_ref[:, 0:v.shape[1]] = v
        return t_ref[:, 0:v.shape[1]].T

    def proj(lo, width):
        return jnp.dot(h_ref[...], win_ref[:, lo:lo + width], preferred_element_type=F32)

    qa = proj(OFF_QA, wa)
    for h in range(A_HEADS):
        lo = h * HEAD_DIM
        q = _rms_lanes(qa[:, lo:lo + HEAD_DIM], gqa_ref[...], HEAD_DIM)
        t_ref[:, lo:lo + HEAD_DIM] = _rope_lanes(q, ca, sa_lo, sa_hi, half_a) * A_SCALE
    qa_o[0, 0] = t_ref[:, 0:wa].T.astype(BF16)
    ka = proj(OFF_KA, wa)
    for h in range(A_HEADS):
        lo = h * HEAD_DIM
        k = _rms_lanes(ka[:, lo:lo + HEAD_DIM], gka_ref[...], HEAD_DIM)
        ka_o[0, :, lo:lo + HEAD_DIM] = _rope_lanes(k, ca, sa_lo, sa_hi, half_a).astype(BF16)
    vat_o[0, 0] = transposed(proj(OFF_VA, wa)).astype(BF16)
    qi_o[0, 0] = transposed(proj(OFF_QI, IDX_HEADS * IDX_DIM)).astype(BF16)

    qc_o[0, 0] = transposed(proj(OFF_QC, wc) * C_SCALE).astype(BF16)
    kc_o[0] = proj(OFF_KC, wc).astype(BF16)
    vct_o[0, 0] = transposed(proj(OFF_VC, wc)).astype(BF16)

    cq = _rms_lanes(proj(OFF_CQ, Q_RANK_PAD), gcq_ref[...], MLA_Q_RANK)
    qb = jnp.dot(cq.astype(BF16), wuq_ref[...], preferred_element_type=F32)
    small = proj(OFF_CKV, N_PROJ - OFF_CKV)
    ckv = _rms_lanes(small[:, 0:MLA_KV_RANK], gckv_ref[...], MLA_KV_RANK)
    kvb = jnp.dot(ckv.astype(BF16), wukv_ref[...], preferred_element_type=F32)
    kr = _rms_lanes(small[:, OFF_KR - OFF_CKV:OFF_KR - OFF_CKV + LANES], gkr_ref[...], MLA_ROPE)
    kr = _rope_lanes(kr, cm, sm_lo, sm_hi, half_m).astype(BF16)
    kiw = small[:, OFF_KIW - OFF_CKV:OFF_KIW - OFF_CKV + LANES]
    ki_o[0] = kiw[:, :IDX_DIM].astype(BF16)
    wt_o[0] = transposed(kiw)[IDX_DIM:IDX_DIM + IDX_HEADS, :] * IDX_SCALE
    for h in range(B_HEADS):
        lo = h * 2 * LANES
        kn = _rms_lanes(kvb[:, h * MLA_NOPE:(h + 1) * MLA_NOPE], gkn_ref[...], MLA_NOPE)
        kcat_o[0, :, lo:lo + MLA_NOPE] = kn.astype(BF16)
        kcat_o[0, :, lo + MLA_NOPE:lo + 2 * LANES] = kr
    vbt_o[0, 0] = transposed(kvb[:, B_HEADS * MLA_NOPE:]).astype(BF16)
    for h in range(B_HEADS):
        lo = h * 2 * LANES
        qn = _rms_lanes(qb[:, lo:lo + MLA_NOPE], gqn_ref[...], MLA_NOPE)
        t_ref[:, lo:lo + MLA_NOPE] = qn * B_SCALE
        qr = _rms_lanes(qb[:, lo + MLA_NOPE:lo + 2 * LANES], gqr_ref[...], MLA_ROPE)
        t_ref[:, lo + MLA_NOPE:lo + 2 * LANES] = _rope_lanes(qr, cm, sm_lo, sm_hi, half_m) * B_SCALE
    qcat_o[0, 0] = t_ref[...].T.astype(BF16)


def _resident(shape):
    return pl.BlockSpec(shape, lambda b, i: (0,) * len(shape), pipeline_mode=pl.Buffered(1))


def _prep(x2, mod, g_mix, w_in_p, tabs, gains, wuq, wukv, batch, seq):
    tm = KEY_BLOCK
    nt = seq // tm
    d_model = x2.shape[1]
    wa, wb, wc = A_HEADS * HEAD_DIM, B_HEADS * HEAD_DIM, C_HEADS * HEAD_DIM
    wcat = B_HEADS * 2 * LANES

    def tok(width, dtype):
        return (jax.ShapeDtypeStruct((batch, seq, width), dtype),
                pl.BlockSpec((1, tm, width), lambda b, i: (b, i, 0)))

    def tok_t(width):
        return (jax.ShapeDtypeStruct((batch, nt, width, tm), BF16),
                pl.BlockSpec((1, 1, width, tm), lambda b, i: (b, i, 0, 0)))

    outs = [tok_t(wa), tok(wa, BF16), tok_t(wa), tok_t(IDX_HEADS * IDX_DIM), tok(IDX_DIM, BF16),
            (jax.ShapeDtypeStruct((batch, IDX_HEADS, seq), F32),
             pl.BlockSpec((1, IDX_HEADS, tm), lambda b, i: (b, 0, i))),
            tok_t(wcat), tok(wcat, BF16), tok_t(wb), tok_t(wc), tok(wc, BF16), tok_t(wc)]
    return pl.pallas_call(
        _prep_kernel,
        grid=(batch, nt),
        in_specs=[pl.BlockSpec((tm, d_model), lambda b, i: (b * nt + i, 0)),
                  pl.BlockSpec((1, N_MOD, d_model), lambda b, i: (b, 0, 0)),
                  _resident((1, d_model)),
                  _resident(w_in_p.shape),
                  pl.BlockSpec((6, tm, LANES), lambda b, i: (0, i, 0))]
                 + [_resident(g.shape) for g in gains]
                 + [_resident(wuq.shape), _resident(wukv.shape)],
        out_specs=[o[1] for o in outs],
        out_shape=[o[0] for o in outs],
        scratch_shapes=[pltpu.VMEM((tm, d_model), BF16), pltpu.VMEM((tm, wcat), F32)],
        compiler_params=_params("arbitrary", "arbitrary"),
        name="head_prep",
    )(x2, mod, g_mix.reshape(1, d_model), w_in_p, tabs, *gains, wuq, wukv)


MLA_HEADS_PER_STEP = 4


def _mla_kernel(qt_ref, k_ref, vt_ref, o_ref, s0_ref, s1_ref, m_ref, l_ref, acc_ref, *, t):
    qb = pl.program_id(2)
    sub = t // KEY_BLOCK
    dq = 2 * LANES
    m_ref[...] = jnp.full(m_ref.shape, NEG_INF, F32)
    l_ref[...] = jnp.zeros(l_ref.shape, F32)
    acc_ref[...] = jnp.zeros(acc_ref.shape, F32)

    def produce(s_ref, kb):
        ks = pl.multiple_of(kb * t, t)
        for h in range(MLA_HEADS_PER_STEP):
            k = k_ref[0, pl.ds(ks, t), h * dq:(h + 1) * dq]
            for c in range(sub):
                s_ref[h, :, c * KEY_BLOCK:(c + 1) * KEY_BLOCK] = jnp.dot(
                    k, qt_ref[0, c, h * dq:(h + 1) * dq, :], preferred_element_type=F32)

    def consume(s_ref, kb, diagonal):
        for h in range(MLA_HEADS_PER_STEP):
            s = s_ref[h]
            if diagonal:
                kc = lax.broadcasted_iota(I32, (t, t), 0) >> CHUNK_SHIFT
                qc = lax.broadcasted_iota(I32, (t, t), 1) >> CHUNK_SHIFT
                s = jnp.where(kc <= qc, s, NEG_INF)
            m_prev = m_ref[h]
            m_new = jnp.maximum(m_prev, jnp.max(s, axis=0, keepdims=True))
            p = jnp.exp2(s - m_new)
            alpha = jnp.exp2(m_prev - m_new)
            l_ref[h] = alpha * l_ref[h] + jnp.sum(p, axis=0, keepdims=True)
            p = p.astype(BF16)
            pv = jnp.dot(vt_ref[0, kb * sub, h * HEAD_DIM:(h + 1) * HEAD_DIM, :], p[0:KEY_BLOCK],
                         preferred_element_type=F32)
            for c in range(1, sub):
                pv += jnp.dot(vt_ref[0, kb * sub + c, h * HEAD_DIM:(h + 1) * HEAD_DIM, :],
                              p[c * KEY_BLOCK:(c + 1) * KEY_BLOCK], preferred_element_type=F32)
            acc_ref[h] = alpha * acc_ref[h] + pv
            m_ref[h] = m_new

    produce(s0_ref, 0)

    def body(j, carry):
        produce(s1_ref, 2 * j + 1)
        consume(s0_ref, 2 * j, False)
        produce(s0_ref, 2 * j + 2)
        consume(s1_ref, 2 * j + 1, False)
        return carry

    lax.fori_loop(0, qb // 2, body, 0)

    @pl.when(qb % 2 == 0)
    def _():
        consume(s0_ref, qb, True)

    @pl.when(qb % 2 == 1)
    def _():
        produce(s1_ref, qb)
        consume(s0_ref, qb - 1, False)
        consume(s1_ref, qb, True)

    for h in range(MLA_HEADS_PER_STEP):
        o_ref[0, :, h * HEAD_DIM:(h + 1) * HEAD_DIM] = (acc_ref[h] / l_ref[h]).T.astype(BF16)


def _mla(qcat, kcat, vbt, batch, seq):
    t = min(512, seq)
    hp = MLA_HEADS_PER_STEP
    nkb = seq // KEY_BLOCK
    return pl.pallas_call(
        functools.partial(_mla_kernel, t=t),
        grid=(batch, B_HEADS // hp, seq // t),
        in_specs=[
            pl.BlockSpec((1, t // KEY_BLOCK, hp * 2 * LANES, KEY_BLOCK), lambda b, h, i: (b, i, h, 0)),
            pl.BlockSpec((1, seq, hp * 2 * LANES), lambda b, h, i: (b, 0, h)),
            pl.BlockSpec((1, nkb, hp * HEAD_DIM, KEY_BLOCK), lambda b, h, i: (b, 0, h, 0)),
        ],
        out_specs=pl.BlockSpec((1, t, hp * HEAD_DIM), lambda b, h, i: (b, i, h)),
        out_shape=jax.ShapeDtypeStruct((batch, seq, B_HEADS * HEAD_DIM), BF16),
        scratch_shapes=[pltpu.VMEM((hp, t, t), F32), pltpu.VMEM((hp, t, t), F32),
                        pltpu.VMEM((hp, 1, t), F32), pltpu.VMEM((hp, 1, t), F32),
                        pltpu.VMEM((hp, HEAD_DIM, t), F32)],
        compiler_params=_params("arbitrary", "arbitrary", "arbitrary"),
        name="mla_attn",
    )(qcat, kcat, vbt)


SB_HEADS_PER_STEP = 4


def _sb_kernel(qt_ref, k_ref, vt_ref, o_ref, r_ref, acc_ref, *, tq):
    qb = pl.program_id(2)
    tk = KEY_BLOCK
    sub = tq // tk
    tri_r = lax.broadcasted_iota(I32, (tk, tk), 0)
    tri_c = lax.broadcasted_iota(I32, (tk, tk), 1)
    later_keys = jnp.where(tri_c > tri_r, 1.0, 0.0).astype(BF16)
    r_ref[...] = jnp.zeros(r_ref.shape, F32)
    acc_ref[...] = jnp.zeros(acc_ref.shape, F32)

    def step(kb, key_offset):
        ks = pl.multiple_of(kb * tk, tk)
        heads = range(SB_HEADS_PER_STEP)
        if key_offset is not None:
            before = (lax.broadcasted_iota(I32, (tk, tq), 0) + key_offset
                      < lax.broadcasted_iota(I32, (tk, tq), 1))
        zs = [jnp.concatenate(
            [jnp.dot(k_ref[0, pl.ds(ks, tk), h * HEAD_DIM:(h + 1) * HEAD_DIM],
                     qt_ref[0, c, h * HEAD_DIM:(h + 1) * HEAD_DIM, :], preferred_element_type=F32)
             for c in range(sub)], axis=1) for h in heads]
        go = []
        for z in zs:
            neg_abs = lax.bitcast_convert_type(lax.bitcast_convert_type(z, I32) | INT_MIN, F32)
            g = jnp.maximum(z, 0.0) + jnp.log(1.0 + jnp.exp(neg_abs))
            if key_offset is not None:
                g = jnp.where(before, g, 0.0)
            go.append(g)
        later = []
        for g in go:
            hi = g.astype(BF16)
            lo = (g - hi.astype(F32)).astype(BF16)
            later.append(jnp.dot(later_keys, hi, preferred_element_type=F32)
                         + jnp.dot(later_keys, lo, preferred_element_type=F32))
        for h in heads:
            w = jnp.exp(zs[h] - (go[h] + later[h] + r_ref[h]))
            if key_offset is not None:
                w = jnp.where(before, w, 0.0)
            acc_ref[h] += jnp.dot(vt_ref[0, kb, h * HEAD_DIM:(h + 1) * HEAD_DIM, :], w.astype(BF16),
                                  preferred_element_type=F32)
            r_ref[h] += jnp.sum(go[h], axis=0, keepdims=True)

    for d in reversed(range(sub)):
        step(qb * sub + d, d * tk)

    def body(j, carry):
        step(qb * sub - 1 - j, None)
        return carry

    lax.fori_loop(0, qb * sub, body, 0)
    for h in range(SB_HEADS_PER_STEP):
        o_ref[0, :, h * HEAD_DIM:(h + 1) * HEAD_DIM] = acc_ref[h].T.astype(BF16)


def _sb(qc, kc, vct, batch, seq):
    tq = min(512, seq)
    hp = SB_HEADS_PER_STEP
    nkb = seq // KEY_BLOCK
    return pl.pallas_call(
        functools.partial(_sb_kernel, tq=tq),
        grid=(batch, C_HEADS // hp, seq // tq),
        in_specs=[
            pl.BlockSpec((1, tq // KEY_BLOCK, hp * HEAD_DIM, KEY_BLOCK), lambda b, h, i: (b, i, h, 0)),
            pl.BlockSpec((1, seq, hp * HEAD_DIM), lambda b, h, i: (b, 0, h)),
            pl.BlockSpec((1, nkb, hp * HEAD_DIM, KEY_BLOCK), lambda b, h, i: (b, 0, h, 0)),
        ],
        out_specs=pl.BlockSpec((1, tq, hp * HEAD_DIM), lambda b, h, i: (b, i, h)),
        out_shape=jax.ShapeDtypeStruct((batch, seq, C_HEADS * HEAD_DIM), BF16),
        scratch_shapes=[pltpu.VMEM((hp, 1, tq), F32), pltpu.VMEM((hp, HEAD_DIM, tq), F32)],
        compiler_params=_params("arbitrary", "arbitrary", "arbitrary"),
        name="sb_attn",
    )(qc, kc, vct)


def _dsa_kernel(qit_ref, ki_ref, wt_ref, qat_ref, ka_ref, vt_ref, o_ref,
                keys_ref, half_ref, bias_ref, thr_ref, lim_ref, m_ref, l_ref, acc_ref, s0_ref, s1_ref,
                *, t, topk, seq):
    qb = pl.program_id(1)
    nk = qb + 1
    row = lax.broadcasted_iota(I32, (t, t), 0)
    col = lax.broadcasted_iota(I32, (t, t), 1)
    wt = wt_ref[0]

    def score_block(kb, carry):
        ks = pl.multiple_of(kb * t, t)
        kix = ki_ref[0, pl.ds(ks, t), :]
        sc = jnp.zeros((t, t), F32)
        for h in range(IDX_HEADS):
            lg = jnp.dot(kix, qit_ref[0, 0, h * IDX_DIM:(h + 1) * IDX_DIM, :],
                         preferred_element_type=F32)
            sc = sc + wt[h:h + 1, :] * jnp.maximum(lg, 0.0)
        bits = lax.bitcast_convert_type(sc, I32)
        key = bits ^ ((bits >> 31) & 0x7FFFFFFF)
        visible = ((ks + row) >> CHUNK_SHIFT) <= ((qb * t + col) >> CHUNK_SHIFT)
        key = jnp.where(visible, key, INT_MIN)
        keys_ref[kb] = key
        half_ref[kb] = (key >> 16).astype(I16)
        return carry

    lax.fori_loop(0, nk, score_block, 0)

    i16_min, i16_max = -2 ** 15, 2 ** 15 - 1
    half_ref[nk] = jnp.full((t, t), i16_min, I16)

    def search16():
        def count16(cand):
            cand16 = cand.astype(I16)

            def block_pair(j, acc):
                for kb in (2 * j, 2 * j + 1):
                    hit = jnp.where(half_ref[kb] >= cand16, jnp.ones((), I16), jnp.zeros((), I16))
                    for i in range(t // 16):
                        acc = acc + hit[i * 16:(i + 1) * 16]
                return acc
            acc = lax.fori_loop(0, (nk + 1) // 2, block_pair, jnp.zeros((16, t), I16))
            return jnp.sum(acc.astype(I32), axis=0, keepdims=True)

        v0 = jnp.where(count16(jnp.zeros((1, t), I32)) >= topk, 0, i16_min).astype(I32)

        def bit(b, v):
            cand = v | lax.shift_left(jnp.int32(1), 14 - b)
            return jnp.where(count16(cand) >= topk, cand, v)

        return lax.fori_loop(0, 15, bit, v0)

    def count(pred):
        def block(kb, acc):
            hit = jnp.where(pred(keys_ref[kb], kb * t + row), 1, 0).astype(I32)
            return acc + jnp.sum(hit.reshape(t // 8, 8, t), axis=0)
        acc = lax.fori_loop(0, nk, block, jnp.zeros((8, t), I32))
        return jnp.sum(acc, axis=0, keepdims=True)

    thr_hi = search16()

    def low_half_block(kb, carry):
        key = keys_ref[kb]
        hi = key >> 16
        low = (key & 0xFFFF) + i16_min
        half_ref[kb] = jnp.where(hi == thr_hi, low, jnp.where(hi > thr_hi, i16_max, i16_min)).astype(I16)
        return carry

    lax.fori_loop(0, nk, low_half_block, 0)
    thr_lo = search16()
    thr = lax.shift_left(thr_hi, 16) + (thr_lo - i16_min)
    n_ge = count(lambda k, i: k >= thr)
    n_gt = count(lambda k, i: k > thr)
    tie = (thr != INT_MIN) & (n_ge > topk)
    need = topk - n_gt
    thr_ref[...] = thr
    lim_ref[...] = jnp.where(thr == INT_MIN, 0, seq).astype(I32)

    @pl.when(jnp.max(tie.astype(I32)) > 0)
    def _():
        nbits = seq.bit_length() - 1

        def lim_bit(b, lim):
            cand = lim | lax.shift_left(jnp.int32(1), nbits - 1 - b)
            below = count(lambda k, i: (k == thr) & (i < cand))
            return jnp.where(below < need, cand, lim)

        lim = lax.fori_loop(0, nbits, lim_bit, jnp.zeros((1, t), I32))
        lim_ref[...] = jnp.where(tie, lim + 1, lim_ref[...])

    def bias_block(kb, carry):
        key = keys_ref[kb]
        thr_q = thr_ref[...]
        at_thr = jnp.where((kb * t + row) < lim_ref[...], 0.0, NEG_INF)
        bias_ref[kb] = jnp.where(key > thr_q, 0.0, jnp.where(key == thr_q, at_thr, NEG_INF))
        return carry

    lax.fori_loop(0, nk, bias_block, 0)
    m_ref[...] = jnp.full(m_ref.shape, NEG_INF, F32)
    l_ref[...] = jnp.zeros(l_ref.shape, F32)
    acc_ref[...] = jnp.zeros(acc_ref.shape, F32)

    def produce(s_ref, kb):
        ks = pl.multiple_of(kb * t, t)
        bias = bias_ref[kb]
        for h in range(A_HEADS):
            s_ref[h] = jnp.dot(ka_ref[0, pl.ds(ks, t), h * HEAD_DIM:(h + 1) * HEAD_DIM],
                               qat_ref[0, 0, h * HEAD_DIM:(h + 1) * HEAD_DIM, :],
                               preferred_element_type=F32) + bias

    def consume(s_ref, kb):
        for h in range(A_HEADS):
            lo = h * HEAD_DIM
            s = s_ref[h]
            m_prev = m_ref[h]
            m_new = jnp.maximum(m_prev, jnp.max(s, axis=0, keepdims=True))
            m_safe = jnp.where(m_new == NEG_INF, 0.0, m_new)
            p = jnp.exp2(s - m_safe)
            alpha = jnp.exp2(m_prev - m_safe)
            l_ref[h] = alpha * l_ref[h] + jnp.sum(p, axis=0, keepdims=True)
            acc_ref[h] = alpha * acc_ref[h] + jnp.dot(vt_ref[0, kb, lo:lo + HEAD_DIM, :], p.astype(BF16),
                                                      preferred_element_type=F32)
            m_ref[h] = m_new

    produce(s0_ref, 0)

    def attend_pair(j, carry):
        produce(s1_ref, 2 * j + 1)
        consume(s0_ref, 2 * j)
        produce(s0_ref, jnp.minimum(2 * j + 2, nk - 1))
        consume(s1_ref, 2 * j + 1)
        return carry

    lax.fori_loop(0, nk // 2, attend_pair, 0)

    @pl.when(nk % 2 == 1)
    def _():
        consume(s0_ref, nk - 1)

    for h in range(A_HEADS):
        o_ref[0, :, h * HEAD_DIM:(h + 1) * HEAD_DIM] = (acc_ref[h] / l_ref[h]).T.astype(BF16)


def _dsa(qi, ki, wt, qa, ka, vat, batch, seq, topk):
    t = KEY_BLOCK
    nt = seq // t
    wa = A_HEADS * HEAD_DIM
    return pl.pallas_call(
        functools.partial(_dsa_kernel, t=t, topk=topk, seq=seq),
        grid=(batch, nt),
        in_specs=[
            pl.BlockSpec((1, 1, IDX_HEADS * IDX_DIM, t), lambda b, i: (b, i, 0, 0)),
            pl.BlockSpec((1, seq, IDX_DIM), lambda b, i: (b, 0, 0)),
            pl.BlockSpec((1, IDX_HEADS, t), lambda b, i: (b, 0, i)),
            pl.BlockSpec((1, 1, wa, t), lambda b, i: (b, i, 0, 0)),
            pl.BlockSpec((1, seq, wa), lambda b, i: (b, 0, 0)),
            pl.BlockSpec((1, nt, wa, t), lambda b, i: (b, 0, 0, 0)),
        ],
        out_specs=pl.BlockSpec((1, t, wa), lambda b, i: (b, i, 0)),
        out_shape=jax.ShapeDtypeStruct((batch, seq, wa), BF16),
        scratch_shapes=[
            pltpu.VMEM((nt, t, t), I32), pltpu.VMEM((nt + 1, t, t), I16), pltpu.VMEM((nt, t, t), F32),
            pltpu.VMEM((1, t), I32), pltpu.VMEM((1, t), I32),
            pltpu.VMEM((A_HEADS, 1, t), F32), pltpu.VMEM((A_HEADS, 1, t), F32),
            pltpu.VMEM((A_HEADS, HEAD_DIM, t), F32),
            pltpu.VMEM((A_HEADS, t, t), F32), pltpu.VMEM((A_HEADS, t, t), F32),
        ],
        compiler_params=_params("arbitrary", "arbitrary"),
        name="dsa_attn",
    )(qi, ki, wt, qa, ka, vat)


def _out_kernel(x_ref, mod_ref, oa_ref, ob_ref, oc_ref, w_ref, o_ref):
    wa = A_HEADS * HEAD_DIM
    wb = B_HEADS * HEAD_DIM
    mixed = (jnp.dot(oa_ref[...], w_ref[0:wa, :], preferred_element_type=F32)
             + jnp.dot(ob_ref[...], w_ref[wa:wa + wb, :], preferred_element_type=F32)
             + jnp.dot(oc_ref[...], w_ref[wa + wb:, :], preferred_element_type=F32))
    o_ref[...] = x_ref[...] + (1.0 + mod_ref[0, 5:6, :]) * mixed


def _out_proj(x2, mod, oa, ob, oc, w, seq):
    T, D = x2.shape
    tm = min(512, seq)
    per_seq = seq // tm
    return pl.pallas_call(
        _out_kernel,
        grid=(T // tm,),
        in_specs=[
            pl.BlockSpec((tm, D), lambda i: (i, 0)),
            pl.BlockSpec((1, N_MOD, D), lambda i: (i // per_seq, 0, 0)),
            pl.BlockSpec((tm, oa.shape[1]), lambda i: (i, 0)),
            pl.BlockSpec((tm, ob.shape[1]), lambda i: (i, 0)),
            pl.BlockSpec((tm, oc.shape[1]), lambda i: (i, 0)),
            pl.BlockSpec(w.shape, lambda i: (0, 0)),
        ],
        out_specs=pl.BlockSpec((tm, D), lambda i: (i, 0)),
        out_shape=jax.ShapeDtypeStruct((T, D), F32),
        compiler_params=_params("arbitrary"),
        name="out_proj",
    )(x2, mod, oa, ob, oc, w)


def _rope_tables(seq):
    def tables(dim):
        inv = 1.0 / (ROPE_THETA ** (jnp.arange(0, dim, 2, dtype=F32) / dim))
        ang = jnp.arange(seq, dtype=F32)[:, None] * inv[None, :]
        return jnp.cos(ang), jnp.sin(ang)

    def lane_tables(cos, sin, fill):
        half = cos.shape[1]
        rest = LANES - 2 * half
        zeros_h = jnp.zeros((seq, half), F32)
        zeros_r = jnp.zeros((seq, rest), F32)
        return [jnp.concatenate([cos, cos, jnp.full((seq, rest), fill, F32)], axis=1),
                jnp.concatenate([-sin, zeros_h, zeros_r], axis=1),
                jnp.concatenate([zeros_h, sin, zeros_r], axis=1)]

    cos_p, sin_p = tables(PARTIAL_ROPE_DIM)
    cos_m, sin_m = tables(MLA_ROPE)
    return jnp.stack(lane_tables(cos_p, sin_p, 1.0) + lane_tables(cos_m, sin_m, 0.0))


def _pad_cols(a, width):
    return jnp.pad(a, ((0, 0), (0, width - a.shape[1])))


def _layer_weights(w_in, w_uq, w_ukv):
    w_in_p = jnp.concatenate([
        w_in[:, 0:2560],
        w_in[:, 3280:4816],
        _pad_cols(w_in[:, 2640:3088], Q_RANK_PAD),
        w_in[:, 3088:3216],
        _pad_cols(w_in[:, 3216:3280], LANES),
        _pad_cols(w_in[:, 2560:2640], LANES),
    ], axis=1).astype(BF16)
    wuq = w_uq.reshape(MLA_Q_RANK, B_HEADS, MLA_NOPE + MLA_ROPE)
    wuq = jnp.pad(wuq, ((0, Q_RANK_PAD - MLA_Q_RANK), (0, 0), (0, 2 * LANES - MLA_NOPE - MLA_ROPE)))
    wuq = wuq.reshape(Q_RANK_PAD, B_HEADS * 2 * LANES).astype(BF16)
    wukv = w_ukv.reshape(MLA_KV_RANK, B_HEADS, MLA_NOPE + HEAD_DIM)
    wukv = jnp.concatenate([wukv[:, :, :MLA_NOPE].reshape(MLA_KV_RANK, -1),
                            wukv[:, :, MLA_NOPE:].reshape(MLA_KV_RANK, -1)], axis=1).astype(BF16)
    return w_in_p, wuq, wukv


def kernel(x, c, w_ada, b_ada, g_ffn1, w1_gate, w1_up, w1_down, g_mix, w_in, g_qa, g_ka, g_cq, g_ckv, w_uq, w_ukv, g_q_nope, g_k_nope, g_q_rope, g_k_rope, w_out, g_ffn2, w2_gate, w2_up, w2_down):
    batch, seq, d_model = x.shape
    depth = w_ada.shape[0]
    topk = min(TOPK_MAX, seq // 4)
    tabs = _rope_tables(seq)
    mods = _ada(c, w_ada, b_ada).reshape(depth, batch, N_MOD, d_model)
    x2 = x.reshape(batch * seq, d_model)

    for l in range(depth):
        mod = mods[l]
        x2 = _ffn(x2, mod, g_ffn1[l], w1_gate[l].astype(BF16), w1_up[l].astype(BF16),
                  w1_down[l].astype(BF16), 0, seq)

        w_in_p, wuq, wukv = _layer_weights(w_in[l], w_uq[l], w_ukv[l])
        gains = [g_qa[l][None, :], g_ka[l][None, :], _pad_cols(g_cq[l][None, :], Q_RANK_PAD),
                 g_ckv[l][None, :], g_q_nope[l][None, :], g_k_nope[l][None, :],
                 _pad_cols(g_q_rope[l][None, :], LANES), _pad_cols(g_k_rope[l][None, :], LANES)]
        (qa, ka, vat, qi, ki, wt, qcat, kcat, vbt, qc, kc, vct) = _prep(
            x2, mod, g_mix[l], w_in_p, tabs, gains, wuq, wukv, batch, seq)

        out_a = _dsa(qi, ki, wt, qa, ka, vat, batch, seq, topk)
        out_b = _mla(qcat, kcat, vbt, batch, seq)
        out_c = _sb(qc, kc, vct, batch, seq)
        x2 = _out_proj(x2, mod, out_a.reshape(batch * seq, -1), out_b.reshape(batch * seq, -1),
                       out_c.reshape(batch * seq, -1), w_out[l].astype(BF16), seq)

        x2 = _ffn(x2, mod, g_ffn2[l], w2_gate[l].astype(BF16), w2_up[l].astype(BF16),
                  w2_down[l].astype(BF16), 6, seq)

    return x2.reshape(batch, seq, d_model)
```

```python
import functools

import jax
import jax.numpy as jnp
from jax import lax
from jax.experimental import pallas as pl
from jax.experimental.pallas import tpu as pltpu

F32 = jnp.float32
BF16 = jnp.bfloat16
I32 = jnp.int32
I16 = jnp.int16

HEAD_DIM = 128
CHUNK = 64
CHUNK_SHIFT = 6
ROPE_THETA = 500000.0
PARTIAL_ROPE_DIM = HEAD_DIM // 4
NORM_EPS = 1e-6
D_FF = 5632
N_MOD = 9
A_HEADS = 4
IDX_HEADS = 16
IDX_DIM = 64
TOPK_MAX = 256
B_HEADS = 8
MLA_Q_RANK = 448
MLA_KV_RANK = 128
MLA_NOPE = 128
MLA_ROPE = 64
C_HEADS = 4

LANES = 128
KEY_BLOCK = 256
Q_RANK_PAD = 512
VMEM_LIMIT = 56 * 1024 * 1024
FFN_VMEM_LIMIT = 61 * 1024 * 1024

OFF_QA, OFF_KA, OFF_VA, OFF_QI = 0, 512, 1024, 1536
OFF_QC, OFF_KC, OFF_VC = 2560, 3072, 3584
OFF_CQ, OFF_CKV, OFF_KR, OFF_KIW = 4096, 4608, 4736, 4864
N_PROJ = 4992

LOG2E = 1.4426950408889634
A_SCALE = HEAD_DIM ** -0.5 * LOG2E
B_SCALE = (MLA_NOPE + MLA_ROPE) ** -0.5 * LOG2E
C_SCALE = HEAD_DIM ** -0.5
IDX_SCALE = (IDX_DIM ** -0.5) * (IDX_HEADS ** -0.5)

NT_DIMS = (((1,), (1,)), ((), ()))
NEG_INF = float("-inf")
INT_MIN = -2 ** 31


def _params(*sem, vmem_limit=VMEM_LIMIT):
    return pltpu.CompilerParams(dimension_semantics=sem, vmem_limit_bytes=vmem_limit)


def _ada_kernel(c_ref, w_ref, b_ref, o_ref):
    c = c_ref[...]
    ca = (c * jax.nn.sigmoid(c)).astype(BF16)
    o_ref[0] = jnp.dot(ca, w_ref[0].astype(BF16), preferred_element_type=F32) + b_ref[0]


def _ada(c, w_ada, b_ada):
    L, D, N = w_ada.shape
    B = c.shape[0]
    tn = 1024
    return pl.pallas_call(
        _ada_kernel,
        grid=(L, N // tn),
        in_specs=[
            pl.BlockSpec((B, D), lambda l, j: (0, 0)),
            pl.BlockSpec((1, D, tn), lambda l, j: (l, 0, j)),
            pl.BlockSpec((1, 1, tn), lambda l, j: (l, 0, j)),
        ],
        out_specs=pl.BlockSpec((1, B, tn), lambda l, j: (l, 0, j)),
        out_shape=jax.ShapeDtypeStruct((L, B, N), F32),
        compiler_params=_params("arbitrary", "arbitrary"),
        name="ada_mod",
    )(c, w_ada, b_ada.reshape(L, 1, N))


NORM_ROWS = 16


def _norm_mod_into(h_ref, x_ref, g, shift, scale):
    gain = g * (1.0 + scale)

    def rows(r, carry):
        sl = pl.ds(pl.multiple_of(r * NORM_ROWS, NORM_ROWS), NORM_ROWS)
        x = x_ref[sl, :]
        y = x * lax.rsqrt(jnp.mean(x * x, axis=-1, keepdims=True) + NORM_EPS)
        h_ref[sl, :] = (y * gain + shift).astype(BF16)
        return carry

    lax.fori_loop(0, x_ref.shape[0] // NORM_ROWS, rows, 0, unroll=8)


def _ffn_kernel(x_ref, mod_ref, g_ref, wg_ref, wu_ref, wd_ref, o_ref, h_ref, *, row):
    j = pl.program_id(1)

    @pl.when(j == 0)
    def _():
        _norm_mod_into(h_ref, x_ref, g_ref[...], mod_ref[0, row:row + 1, :], mod_ref[0, row + 1:row + 2, :])
        o_ref[...] = jnp.zeros(o_ref.shape, F32)

    h = h_ref[...]
    g = jnp.dot(h, wg_ref[...], preferred_element_type=F32)
    u = jnp.dot(h, wu_ref[...], preferred_element_type=F32)
    a = ((g * jax.nn.sigmoid(g)) * u).astype(BF16)
    o_ref[...] += jnp.dot(a, wd_ref[...], preferred_element_type=F32)

    @pl.when(j == pl.num_programs(1) - 1)
    def _():
        gate = mod_ref[0, row + 2:row + 3, :]
        o_ref[...] = x_ref[...] + (0.5 * (1.0 + gate)) * o_ref[...]


def _ffn(x2, mod, g, wg, wu, wd, row, seq):
    T, D = x2.shape
    F = wg.shape[1]
    tm = min(1024, seq)
    tf = 512
    per_seq = seq // tm
    return pl.pallas_call(
        functools.partial(_ffn_kernel, row=row),
        grid=(T // tm, F // tf),
        in_specs=[
            pl.BlockSpec((tm, D), lambda i, j: (i, 0)),
            pl.BlockSpec((1, N_MOD, D), lambda i, j: (i // per_seq, 0, 0)),
            pl.BlockSpec((1, D), lambda i, j: (0, 0)),
            pl.BlockSpec((D, tf), lambda i, j: (0, j)),
            pl.BlockSpec((D, tf), lambda i, j: (0, j)),
            pl.BlockSpec((tf, D), lambda i, j: (j, 0)),
        ],
        out_specs=pl.BlockSpec((tm, D), lambda i, j: (i, 0)),
        out_shape=jax.ShapeDtypeStruct((T, D), F32),
        scratch_shapes=[pltpu.VMEM((tm, D), BF16)],
        compiler_params=_params("arbitrary", "arbitrary", vmem_limit=FFN_VMEM_LIMIT),
        name="ffn",
    )(x2, mod, g.reshape(1, D), wg, wu, wd)


def _rms_lanes(x, g, n):
    return x * lax.rsqrt(jnp.sum(x * x, axis=-1, keepdims=True) / n + NORM_EPS) * g


def _rope_lanes(x, cos, sin_lo, sin_hi, half):
    return (x * cos + pltpu.roll(x, LANES - half, 1) * sin_lo + pltpu.roll(x, half, 1) * sin_hi)


def _prep_kernel(x_ref, mod_ref, gmix_ref, win_ref, tab_ref,
                 gqa_ref, gka_ref, gcq_ref, gckv_ref, gqn_ref, gkn_ref, gqr_ref, gkr_ref,
                 wuq_ref, wukv_ref,
                 qa_o, ka_o, vat_o, qi_o, ki_o, wt_o, qcat_o, kcat_o, vbt_o, qc_o, kc_o, vct_o, h_ref, t_ref):
    ca, sa_lo, sa_hi = tab_ref[0], tab_ref[1], tab_ref[2]
    cm, sm_lo, sm_hi = tab_ref[3], tab_ref[4], tab_ref[5]
    half_a = PARTIAL_ROPE_DIM // 2
    half_m = MLA_ROPE // 2
    wa, wc = A_HEADS * HEAD_DIM, C_HEADS * HEAD_DIM

    _norm_mod_into(h_ref, x_ref, gmix_ref[...], mod_ref[0, 3:4, :], mod_ref[0, 4:5, :])

    def transposed(v):
        t_ref[:, 0:v.shape[1]] = v
        return t_ref[:, 0:v.shape[1]].T

    def proj(lo, width):
        return jnp.dot(h_ref[...], win_ref[:, lo:lo + width], preferred_element_type=F32)

    cq = proj(OFF_CQ, Q_RANK_PAD)
    small = proj(OFF_CKV, N_PROJ - OFF_CKV)
    qa = proj(OFF_QA, wa)
    ka = proj(OFF_KA, wa)
    cq = _rms_lanes(cq, gcq_ref[...], MLA_Q_RANK)
    ckv = _rms_lanes(small[:, 0:MLA_KV_RANK], gckv_ref[...], MLA_KV_RANK)
    qb = jnp.dot(cq.astype(BF16), wuq_ref[...], preferred_element_type=F32)
    kvb = jnp.dot(ckv.astype(BF16), wukv_ref[...], preferred_element_type=F32)
    va = proj(OFF_VA, wa)
    qi = proj(OFF_QI, IDX_HEADS * IDX_DIM)
    qc = proj(OFF_QC, wc)
    kc = proj(OFF_KC, wc)
    vc = proj(OFF_VC, wc)

    for h in range(A_HEADS):
        lo = h * HEAD_DIM
        q = _rms_lanes(qa[:, lo:lo + HEAD_DIM], gqa_ref[...], HEAD_DIM)
        t_ref[:, lo:lo + HEAD_DIM] = _rope_lanes(q, ca, sa_lo, sa_hi, half_a) * A_SCALE
    qa_o[0, 0] = t_ref[:, 0:wa].T.astype(BF16)
    for h in range(A_HEADS):
        lo = h * HEAD_DIM
        k = _rms_lanes(ka[:, lo:lo + HEAD_DIM], gka_ref[...], HEAD_DIM)
        ka_o[0, :, lo:lo + HEAD_DIM] = _rope_lanes(k, ca, sa_lo, sa_hi, half_a).astype(BF16)
    vat_o[0, 0] = transposed(va).astype(BF16)
    qi_o[0, 0] = transposed(qi).astype(BF16)

    qc_o[0, 0] = transposed(qc * C_SCALE).astype(BF16)
    kc_o[0] = kc.astype(BF16)
    vct_o[0, 0] = transposed(vc).astype(BF16)

    kr = _rms_lanes(small[:, OFF_KR - OFF_CKV:OFF_KR - OFF_CKV + LANES], gkr_ref[...], MLA_ROPE)
    kr = _rope_lanes(kr, cm, sm_lo, sm_hi, half_m).astype(BF16)
    kiw = small[:, OFF_KIW - OFF_CKV:OFF_KIW - OFF_CKV + LANES]
    ki_o[0] = kiw[:, :IDX_DIM].astype(BF16)
    wt_o[0] = transposed(kiw)[IDX_DIM:IDX_DIM + IDX_HEADS, :] * IDX_SCALE
    for h in range(B_HEADS):
        lo = h * 2 * LANES
        kn = _rms_lanes(kvb[:, h * MLA_NOPE:(h + 1) * MLA_NOPE], gkn_ref[...], MLA_NOPE)
        kcat_o[0, :, lo:lo + MLA_NOPE] = kn.astype(BF16)
        kcat_o[0, :, lo + MLA_NOPE:lo + 2 * LANES] = kr
    vbt_o[0, 0] = transposed(kvb[:, B_HEADS * MLA_NOPE:]).astype(BF16)
    for h in range(B_HEADS):
        lo = h * 2 * LANES
        qn = _rms_lanes(qb[:, lo:lo + MLA_NOPE], gqn_ref[...], MLA_NOPE)
        t_ref[:, lo:lo + MLA_NOPE] = qn * B_SCALE
        qr = _rms_lanes(qb[:, lo + MLA_NOPE:lo + 2 * LANES], gqr_ref[...], MLA_ROPE)
        t_ref[:, lo + MLA_NOPE:lo + 2 * LANES] = _rope_lanes(qr, cm, sm_lo, sm_hi, half_m) * B_SCALE
    qcat_o[0, 0] = t_ref[...].T.astype(BF16)


def _resident(shape):
    return pl.BlockSpec(shape, lambda b, i: (0,) * len(shape), pipeline_mode=pl.Buffered(1))


def _prep(x2, mod, g_mix, w_in_p, tabs, gains, wuq, wukv, batch, seq):
    tm = KEY_BLOCK
    nt = seq // tm
    d_model = x2.shape[1]
    wa, wb, wc = A_HEADS * HEAD_DIM, B_HEADS * HEAD_DIM, C_HEADS * HEAD_DIM
    wcat = B_HEADS * 2 * LANES

    def tok(width, dtype):
        return (jax.ShapeDtypeStruct((batch, seq, width), dtype),
                pl.BlockSpec((1, tm, width), lambda b, i: (b, i, 0)))

    def tok_t(width):
        return (jax.ShapeDtypeStruct((batch, nt, width, tm), BF16),
                pl.BlockSpec((1, 1, width, tm), lambda b, i: (b, i, 0, 0)))

    outs = [tok_t(wa), tok(wa, BF16), tok_t(wa), tok_t(IDX_HEADS * IDX_DIM), tok(IDX_DIM, BF16),
            (jax.ShapeDtypeStruct((batch, IDX_HEADS, seq), F32),
             pl.BlockSpec((1, IDX_HEADS, tm), lambda b, i: (b, 0, i))),
            tok_t(wcat), tok(wcat, BF16), tok_t(wb), tok_t(wc), tok(wc, BF16), tok_t(wc)]
    return pl.pallas_call(
        _prep_kernel,
        grid=(batch, nt),
        in_specs=[pl.BlockSpec((tm, d_model), lambda b, i: (b * nt + i, 0)),
                  pl.BlockSpec((1, N_MOD, d_model), lambda b, i: (b, 0, 0)),
                  _resident((1, d_model)),
                  _resident(w_in_p.shape),
                  pl.BlockSpec((6, tm, LANES), lambda b, i: (0, i, 0))]
                 + [_resident(g.shape) for g in gains]
                 + [_resident(wuq.shape), _resident(wukv.shape)],
        out_specs=[o[1] for o in outs],
        out_shape=[o[0] for o in outs],
        scratch_shapes=[pltpu.VMEM((tm, d_model), BF16), pltpu.VMEM((tm, wcat), F32)],
        compiler_params=_params("arbitrary", "arbitrary"),
        name="head_prep",
    )(x2, mod, g_mix.reshape(1, d_model), w_in_p, tabs, *gains, wuq, wukv)


MLA_HEADS_PER_STEP = 4


def _mla_kernel(qt_ref, k_ref, vt_ref, o_ref, s0_ref, s1_ref, m_ref, l_ref, acc_ref, *, t):
    qb = pl.program_id(2)
    sub = t // KEY_BLOCK
    dq = 2 * LANES
    m_ref[...] = jnp.full(m_ref.shape, NEG_INF, F32)
    l_ref[...] = jnp.zeros(l_ref.shape, F32)
    acc_ref[...] = jnp.zeros(acc_ref.shape, F32)

    def produce(s_ref, kb):
        ks = pl.multiple_of(kb * t, t)
        for h in range(MLA_HEADS_PER_STEP):
            k = k_ref[0, pl.ds(ks, t), h * dq:(h + 1) * dq]
            for c in range(sub):
                s_ref[h, :, c * KEY_BLOCK:(c + 1) * KEY_BLOCK] = jnp.dot(
                    k, qt_ref[0, c, h * dq:(h + 1) * dq, :], preferred_element_type=F32)

    def consume(s_ref, kb, diagonal):
        for h in range(MLA_HEADS_PER_STEP):
            s = s_ref[h]
            if diagonal:
                kc = lax.broadcasted_iota(I32, (t, t), 0) >> CHUNK_SHIFT
                qc = lax.broadcasted_iota(I32, (t, t), 1) >> CHUNK_SHIFT
                s = jnp.where(kc <= qc, s, NEG_INF)
            m_prev = m_ref[h]
            m_new = jnp.maximum(m_prev, jnp.max(s, axis=0, keepdims=True))
            p = jnp.exp2(s - m_new)
            alpha = jnp.exp2(m_prev - m_new)
            l_ref[h] = alpha * l_ref[h] + jnp.sum(p, axis=0, keepdims=True)
            p = p.astype(BF16)
            pv = jnp.dot(vt_ref[0, kb * sub, h * HEAD_DIM:(h + 1) * HEAD_DIM, :], p[0:KEY_BLOCK],
                         preferred_element_type=F32)
            for c in range(1, sub):
                pv += jnp.dot(vt_ref[0, kb * sub + c, h * HEAD_DIM:(h + 1) * HEAD_DIM, :],
                              p[c * KEY_BLOCK:(c + 1) * KEY_BLOCK], preferred_element_type=F32)
            acc_ref[h] = alpha * acc_ref[h] + pv
            m_ref[h] = m_new

    produce(s0_ref, 0)

    def body(j, carry):
        produce(s1_ref, 2 * j + 1)
        consume(s0_ref, 2 * j, False)
        produce(s0_ref, 2 * j + 2)
        consume(s1_ref, 2 * j + 1, False)
        return carry

    lax.fori_loop(0, qb // 2, body, 0)

    @pl.when(qb % 2 == 0)
    def _():
        consume(s0_ref, qb, True)

    @pl.when(qb % 2 == 1)
    def _():
        produce(s1_ref, qb)
        consume(s0_ref, qb - 1, False)
        consume(s1_ref, qb, True)

    for h in range(MLA_HEADS_PER_STEP):
        o_ref[0, :, h * HEAD_DIM:(h + 1) * HEAD_DIM] = (acc_ref[h] / l_ref[h]).T.astype(BF16)


def _mla(qcat, kcat, vbt, batch, seq):
    t = min(512, seq)
    hp = MLA_HEADS_PER_STEP
    nkb = seq // KEY_BLOCK
    return pl.pallas_call(
        functools.partial(_mla_kernel, t=t),
        grid=(batch, B_HEADS // hp, seq // t),
        in_specs=[
            pl.BlockSpec((1, t // KEY_BLOCK, hp * 2 * LANES, KEY_BLOCK), lambda b, h, i: (b, i, h, 0)),
            pl.BlockSpec((1, seq, hp * 2 * LANES), lambda b, h, i: (b, 0, h)),
            pl.BlockSpec((1, nkb, hp * HEAD_DIM, KEY_BLOCK), lambda b, h, i: (b, 0, h, 0)),
        ],
        out_specs=pl.BlockSpec((1, t, hp * HEAD_DIM), lambda b, h, i: (b, i, h)),
        out_shape=jax.ShapeDtypeStruct((batch, seq, B_HEADS * HEAD_DIM), BF16),
        scratch_shapes=[pltpu.VMEM((hp, t, t), F32), pltpu.VMEM((hp, t, t), F32),
                        pltpu.VMEM((hp, 1, t), F32), pltpu.VMEM((hp, 1, t), F32),
                        pltpu.VMEM((hp, HEAD_DIM, t), F32)],
        compiler_params=_params("arbitrary", "arbitrary", "arbitrary"),
        name="mla_attn",
    )(qcat, kcat, vbt)


SB_HEADS_PER_STEP = 4


def _sb_kernel(qt_ref, k_ref, vt_ref, o_ref, z0_ref, z1_ref, r_ref, acc_ref, *, tq):
    qb = pl.program_id(2)
    tk = KEY_BLOCK
    sub = tq // tk
    assert sub % 2 == 0
    tri_r = lax.broadcasted_iota(I32, (tk, tk), 0)
    tri_c = lax.broadcasted_iota(I32, (tk, tk), 1)
    later_keys = jnp.where(tri_c > tri_r, 1.0, 0.0).astype(BF16)
    r_ref[...] = jnp.zeros(r_ref.shape, F32)
    acc_ref[...] = jnp.zeros(acc_ref.shape, F32)

    heads = range(SB_HEADS_PER_STEP)

    def produce(z_ref, kb):
        ks = pl.multiple_of(kb * tk, tk)
        for h in heads:
            k = k_ref[0, pl.ds(ks, tk), h * HEAD_DIM:(h + 1) * HEAD_DIM]
            for c in range(sub):
                z_ref[h, :, c * tk:(c + 1) * tk] = jnp.dot(
                    k, qt_ref[0, c, h * HEAD_DIM:(h + 1) * HEAD_DIM, :], preferred_element_type=F32)

    def consume(z_ref, kb, key_offset):
        if key_offset is not None:
            before = (lax.broadcasted_iota(I32, (tk, tq), 0) + key_offset
                      < lax.broadcasted_iota(I32, (tk, tq), 1))
        zs = [z_ref[h] for h in heads]
        go = []
        for z in zs:
            neg_abs = lax.bitcast_convert_type(lax.bitcast_convert_type(z, I32) | INT_MIN, F32)
            g = jnp.maximum(z, 0.0) + jnp.log(1.0 + jnp.exp(neg_abs))
            if key_offset is not None:
                g = jnp.where(before, g, 0.0)
            go.append(g)
        later = []
        for g in go:
            hi = g.astype(BF16)
            lo = (g - hi.astype(F32)).astype(BF16)
            later.append(jnp.dot(later_keys, hi, preferred_element_type=F32)
                         + jnp.dot(later_keys, lo, preferred_element_type=F32))
        for h in heads:
            w = jnp.exp(zs[h] - (go[h] + later[h] + r_ref[h]))
            if key_offset is not None:
                w = jnp.where(before, w, 0.0)
            acc_ref[h] += jnp.dot(vt_ref[0, kb, h * HEAD_DIM:(h + 1) * HEAD_DIM, :], w.astype(BF16),
                                  preferred_element_type=F32)
            r_ref[h] += jnp.sum(go[h], axis=0, keepdims=True)

    first = qb * sub
    bufs = (z0_ref, z1_ref)
    produce(bufs[0], first + sub - 1)
    for i, d in enumerate(reversed(range(sub))):
        if d > 0:
            produce(bufs[(i + 1) % 2], first + d - 1)
        else:
            @pl.when(qb > 0)
            def _():
                produce(bufs[(i + 1) % 2], first - 1)
        consume(bufs[i % 2], first + d, d * tk)

    def body(j, carry):
        top = first - 1 - 2 * j
        produce(z1_ref, top - 1)
        consume(z0_ref, top, None)
        produce(z0_ref, top - 2)
        consume(z1_ref, top - 1, None)
        return carry

    lax.fori_loop(0, first // 2 - 1, body, 0)

    @pl.when(qb > 0)
    def _():
        produce(z1_ref, 0)
        consume(z0_ref, 1, None)
        consume(z1_ref, 0, None)

    for h in range(SB_HEADS_PER_STEP):
        o_ref[0, :, h * HEAD_DIM:(h + 1) * HEAD_DIM] = acc_ref[h].T.astype(BF16)


def _sb(qc, kc, vct, batch, seq):
    tq = min(512, seq)
    hp = SB_HEADS_PER_STEP
    nkb = seq // KEY_BLOCK
    return pl.pallas_call(
        functools.partial(_sb_kernel, tq=tq),
        grid=(batch, C_HEADS // hp, seq // tq),
        in_specs=[
            pl.BlockSpec((1, tq // KEY_BLOCK, hp * HEAD_DIM, KEY_BLOCK), lambda b, h, i: (b, i, h, 0)),
            pl.BlockSpec((1, seq, hp * HEAD_DIM), lambda b, h, i: (b, 0, h)),
            pl.BlockSpec((1, nkb, hp * HEAD_DIM, KEY_BLOCK), lambda b, h, i: (b, 0, h, 0)),
        ],
        out_specs=pl.BlockSpec((1, tq, hp * HEAD_DIM), lambda b, h, i: (b, i, h)),
        out_shape=jax.ShapeDtypeStruct((batch, seq, C_HEADS * HEAD_DIM), BF16),
        scratch_shapes=[pltpu.VMEM((hp, KEY_BLOCK, tq), F32), pltpu.VMEM((hp, KEY_BLOCK, tq), F32),
                        pltpu.VMEM((hp, 1, tq), F32), pltpu.VMEM((hp, HEAD_DIM, tq), F32)],
        compiler_params=_params("arbitrary", "arbitrary", "arbitrary"),
        name="sb_attn",
    )(qc, kc, vct)


def _dsa_kernel(qit_ref, ki_ref, wt_ref, qat_ref, ka_ref, vt_ref, o_ref,
                keys_ref, half_ref, bias_ref, thr_ref, lim_ref, m_ref, l_ref, acc_ref, s0_ref, s1_ref,
                *, t, topk, seq):
    qb = pl.program_id(1)
    nk = qb + 1
    row = lax.broadcasted_iota(I32, (t, t), 0)
    col = lax.broadcasted_iota(I32, (t, t), 1)
    wt = wt_ref[0]

    def score_block(kb, carry):
        ks = pl.multiple_of(kb * t, t)
        kix = ki_ref[0, pl.ds(ks, t), :]
        sc = jnp.zeros((t, t), F32)
        for h in range(IDX_HEADS):
            lg = jnp.dot(kix, qit_ref[0, 0, h * IDX_DIM:(h + 1) * IDX_DIM, :],
                         preferred_element_type=F32)
            sc = sc + wt[h:h + 1, :] * jnp.maximum(lg, 0.0)
        bits = lax.bitcast_convert_type(sc, I32)
        key = bits ^ ((bits >> 31) & 0x7FFFFFFF)
        visible = ((ks + row) >> CHUNK_SHIFT) <= ((qb * t + col) >> CHUNK_SHIFT)
        key = jnp.where(visible, key, INT_MIN)
        keys_ref[kb] = key
        half_ref[kb] = (key >> 16).astype(I16)
        return carry

    lax.fori_loop(0, nk, score_block, 0)

    i16_min, i16_max = -2 ** 15, 2 ** 15 - 1
    half_ref[nk] = jnp.full((t, t), i16_min, I16)

    def search16():
        def count16(cand):
            cand16 = cand.astype(I16)

            def block_pair(j, acc):
                for kb in (2 * j, 2 * j + 1):
                    hit = jnp.where(half_ref[kb] >= cand16, jnp.ones((), I16), jnp.zeros((), I16))
                    for i in range(t // 16):
                        acc = acc + hit[i * 16:(i + 1) * 16]
                return acc
            acc = lax.fori_loop(0, (nk + 1) // 2, block_pair, jnp.zeros((16, t), I16))
            return jnp.sum(acc.astype(I32), axis=0, keepdims=True)

        v0 = jnp.where(count16(jnp.zeros((1, t), I32)) >= topk, 0, i16_min).astype(I32)

        def bit(b, v):
            cand = v | lax.shift_left(jnp.int32(1), 14 - b)
            return jnp.where(count16(cand) >= topk, cand, v)

        return lax.fori_loop(0, 15, bit, v0)

    def count(pred):
        def block(kb, acc):
            hit = jnp.where(pred(keys_ref[kb], kb * t + row), 1, 0).astype(I32)
            return acc + jnp.sum(hit.reshape(t // 8, 8, t), axis=0)
        acc = lax.fori_loop(0, nk, block, jnp.zeros((8, t), I32))
        return jnp.sum(acc, axis=0, keepdims=True)

    thr_hi = search16()

    def low_half_block(kb, carry):
        key = keys_ref[kb]
        hi = key >> 16
        low = (key & 0xFFFF) + i16_min
        half_ref[kb] = jnp.where(hi == thr_hi, low, jnp.where(hi > thr_hi, i16_max, i16_min)).astype(I16)
        return carry

    lax.fori_loop(0, nk, low_half_block, 0)
    thr_lo = search16()
    thr = lax.shift_left(thr_hi, 16) + (thr_lo - i16_min)
    n_ge = count(lambda k, i: k >= thr)
    n_gt = count(lambda k, i: k > thr)
    tie = (thr != INT_MIN) & (n_ge > topk)
    need = topk - n_gt
    thr_ref[...] = thr
    lim_ref[...] = jnp.where(thr == INT_MIN, 0, seq).astype(I32)

    @pl.when(jnp.max(tie.astype(I32)) > 0)
    def _():
        nbits = seq.bit_length() - 1

        def lim_bit(b, lim):
            cand = lim | lax.shift_left(jnp.int32(1), nbits - 1 - b)
            below = count(lambda k, i: (k == thr) & (i < cand))
            return jnp.where(below < need, cand, lim)

        lim = lax.fori_loop(0, nbits, lim_bit, jnp.zeros((1, t), I32))
        lim_ref[...] = jnp.where(tie, lim + 1, lim_ref[...])

    def bias_block(kb, carry):
        key = keys_ref[kb]
        thr_q = thr_ref[...]
        at_thr = jnp.where((kb * t + row) < lim_ref[...], 0.0, NEG_INF)
        bias_ref[kb] = jnp.where(key > thr_q, 0.0, jnp.where(key == thr_q, at_thr, NEG_INF))
        return carry

    lax.fori_loop(0, nk, bias_block, 0)
    m_ref[...] = jnp.full(m_ref.shape, NEG_INF, F32)
    l_ref[...] = jnp.zeros(l_ref.shape, F32)
    acc_ref[...] = jnp.zeros(acc_ref.shape, F32)

    def produce(s_ref, kb):
        ks = pl.multiple_of(kb * t, t)
        bias = bias_ref[kb]
        for h in range(A_HEADS):
            s_ref[h] = jnp.dot(ka_ref[0, pl.ds(ks, t), h * HEAD_DIM:(h + 1) * HEAD_DIM],
                               qat_ref[0, 0, h * HEAD_DIM:(h + 1) * HEAD_DIM, :],
                               preferred_element_type=F32) + bias

    def consume(s_ref, kb):
        for h in range(A_HEADS):
            lo = h * HEAD_DIM
            s = s_ref[h]
            m_prev = m_ref[h]
            m_new = jnp.maximum(m_prev, jnp.max(s, axis=0, keepdims=True))
            m_safe = jnp.where(m_new == NEG_INF, 0.0, m_new)
            p = jnp.exp2(s - m_safe)
            alpha = jnp.exp2(m_prev - m_safe)
            l_ref[h] = alpha * l_ref[h] + jnp.sum(p, axis=0, keepdims=True)
            acc_ref[h] = alpha * acc_ref[h] + jnp.dot(vt_ref[0, kb, lo:lo + HEAD_DIM, :], p.astype(BF16),
                                                      preferred_element_type=F32)
            m_ref[h] = m_new

    produce(s0_ref, 0)

    def attend_pair(j, carry):
        produce(s1_ref, 2 * j + 1)
        consume(s0_ref, 2 * j)
        produce(s0_ref, jnp.minimum(2 * j + 2, nk - 1))
        consume(s1_ref, 2 * j + 1)
        return carry

    lax.fori_loop(0, nk // 2, attend_pair, 0)

    @pl.when(nk % 2 == 1)
    def _():
        consume(s0_ref, nk - 1)

    for h in range(A_HEADS):
        o_ref[0, :, h * HEAD_DIM:(h + 1) * HEAD_DIM] = (acc_ref[h] / l_ref[h]).T.astype(BF16)


def _dsa(qi, ki, wt, qa, ka, vat, batch, seq, topk):
    t = KEY_BLOCK
    nt = seq // t
    wa = A_HEADS * HEAD_DIM
    return pl.pallas_call(
        functools.partial(_dsa_kernel, t=t, topk=topk, seq=seq),
        grid=(batch, nt),
        in_specs=[
            pl.BlockSpec((1, 1, IDX_HEADS * IDX_DIM, t), lambda b, i: (b, i, 0, 0)),
            pl.BlockSpec((1, seq, IDX_DIM), lambda b, i: (b, 0, 0)),
            pl.BlockSpec((1, IDX_HEADS, t), lambda b, i: (b, 0, i)),
            pl.BlockSpec((1, 1, wa, t), lambda b, i: (b, i, 0, 0)),
            pl.BlockSpec((1, seq, wa), lambda b, i: (b, 0, 0)),
            pl.BlockSpec((1, nt, wa, t), lambda b, i: (b, 0, 0, 0)),
        ],
        out_specs=pl.BlockSpec((1, t, wa), lambda b, i: (b, i, 0)),
        out_shape=jax.ShapeDtypeStruct((batch, seq, wa), BF16),
        scratch_shapes=[
            pltpu.VMEM((nt, t, t), I32), pltpu.VMEM((nt + 1, t, t), I16), pltpu.VMEM((nt, t, t), F32),
            pltpu.VMEM((1, t), I32), pltpu.VMEM((1, t), I32),
            pltpu.VMEM((A_HEADS, 1, t), F32), pltpu.VMEM((A_HEADS, 1, t), F32),
            pltpu.VMEM((A_HEADS, HEAD_DIM, t), F32),
            pltpu.VMEM((A_HEADS, t, t), F32), pltpu.VMEM((A_HEADS, t, t), F32),
        ],
        compiler_params=_params("arbitrary", "arbitrary"),
        name="dsa_attn",
    )(qi, ki, wt, qa, ka, vat)


def _out_kernel(x_ref, mod_ref, oa_ref, ob_ref, oc_ref, w_ref, o_ref):
    wa = A_HEADS * HEAD_DIM
    wb = B_HEADS * HEAD_DIM
    mixed = (jnp.dot(oa_ref[...], w_ref[0:wa, :], preferred_element_type=F32)
             + jnp.dot(ob_ref[...], w_ref[wa:wa + wb, :], preferred_element_type=F32)
             + jnp.dot(oc_ref[...], w_ref[wa + wb:, :], preferred_element_type=F32))
    o_ref[...] = x_ref[...] + (1.0 + mod_ref[0, 5:6, :]) * mixed


def _out_proj(x2, mod, oa, ob, oc, w, seq):
    T, D = x2.shape
    tm = min(512, seq)
    per_seq = seq // tm
    return pl.pallas_call(
        _out_kernel,
        grid=(T // tm,),
        in_specs=[
            pl.BlockSpec((tm, D), lambda i: (i, 0)),
            pl.BlockSpec((1, N_MOD, D), lambda i: (i // per_seq, 0, 0)),
            pl.BlockSpec((tm, oa.shape[1]), lambda i: (i, 0)),
            pl.BlockSpec((tm, ob.shape[1]), lambda i: (i, 0)),
            pl.BlockSpec((tm, oc.shape[1]), lambda i: (i, 0)),
            pl.BlockSpec(w.shape, lambda i: (0, 0)),
        ],
        out_specs=pl.BlockSpec((tm, D), lambda i: (i, 0)),
        out_shape=jax.ShapeDtypeStruct((T, D), F32),
        compiler_params=_params("arbitrary"),
        name="out_proj",
    )(x2, mod, oa, ob, oc, w)


def _rope_tables(seq):
    def tables(dim):
        inv = 1.0 / (ROPE_THETA ** (jnp.arange(0, dim, 2, dtype=F32) / dim))
        ang = jnp.arange(seq, dtype=F32)[:, None] * inv[None, :]
        return jnp.cos(ang), jnp.sin(ang)

    def lane_tables(cos, sin, fill):
        half = cos.shape[1]
        rest = LANES - 2 * half
        zeros_h = jnp.zeros((seq, half), F32)
        zeros_r = jnp.zeros((seq, rest), F32)
        return [jnp.concatenate([cos, cos, jnp.full((seq, rest), fill, F32)], axis=1),
                jnp.concatenate([-sin, zeros_h, zeros_r], axis=1),
                jnp.concatenate([zeros_h, sin, zeros_r], axis=1)]

    cos_p, sin_p = tables(PARTIAL_ROPE_DIM)
    cos_m, sin_m = tables(MLA_ROPE)
    return jnp.stack(lane_tables(cos_p, sin_p, 1.0) + lane_tables(cos_m, sin_m, 0.0))


def _pad_cols(a, width):
    return jnp.pad(a, ((0, 0), (0, width - a.shape[1])))


def _layer_weights(w_in, w_uq, w_ukv):
    w_in_p = jnp.concatenate([
        w_in[:, 0:2560],
        w_in[:, 3280:4816],
        _pad_cols(w_in[:, 2640:3088], Q_RANK_PAD),
        w_in[:, 3088:3216],
        _pad_cols(w_in[:, 3216:3280], LANES),
        _pad_cols(w_in[:, 2560:2640], LANES),
    ], axis=1).astype(BF16)
    wuq = w_uq.reshape(MLA_Q_RANK, B_HEADS, MLA_NOPE + MLA_ROPE)
    wuq = jnp.pad(wuq, ((0, Q_RANK_PAD - MLA_Q_RANK), (0, 0), (0, 2 * LANES - MLA_NOPE - MLA_ROPE)))
    wuq = wuq.reshape(Q_RANK_PAD, B_HEADS * 2 * LANES).astype(BF16)
    wukv = w_ukv.reshape(MLA_KV_RANK, B_HEADS, MLA_NOPE + HEAD_DIM)
    wukv = jnp.concatenate([wukv[:, :, :MLA_NOPE].reshape(MLA_KV_RANK, -1),
                            wukv[:, :, MLA_NOPE:].reshape(MLA_KV_RANK, -1)], axis=1).astype(BF16)
    return w_in_p, wuq, wukv


def kernel(x, c, w_ada, b_ada, g_ffn1, w1_gate, w1_up, w1_down, g_mix, w_in, g_qa, g_ka, g_cq, g_ckv, w_uq, w_ukv, g_q_nope, g_k_nope, g_q_rope, g_k_rope, w_out, g_ffn2, w2_gate, w2_up, w2_down):
    batch, seq, d_model = x.shape
    depth = w_ada.shape[0]
    topk = min(TOPK_MAX, seq // 4)
    tabs = _rope_tables(seq)
    mods = _ada(c, w_ada, b_ada).reshape(depth, batch, N_MOD, d_model)
    x2 = x.reshape(batch * seq, d_model)

    for l in range(depth):
        mod = mods[l]
        x2 = _ffn(x2, mod, g_ffn1[l], w1_gate[l].astype(BF16), w1_up[l].astype(BF16),
                  w1_down[l].astype(BF16), 0, seq)

        w_in_p, wuq, wukv = _layer_weights(w_in[l], w_uq[l], w_ukv[l])
        gains = [g_qa[l][None, :], g_ka[l][None, :], _pad_cols(g_cq[l][None, :], Q_RANK_PAD),
                 g_ckv[l][None, :], g_q_nope[l][None, :], g_k_nope[l][None, :],
                 _pad_cols(g_q_rope[l][None, :], LANES), _pad_cols(g_k_rope[l][None, :], LANES)]
        (qa, ka, vat, qi, ki, wt, qcat, kcat, vbt, qc, kc, vct) = _prep(
            x2, mod, g_mix[l], w_in_p, tabs, gains, wuq, wukv, batch, seq)

        out_a = _dsa(qi, ki, wt, qa, ka, vat, batch, seq, topk)
        out_b = _mla(qcat, kcat, vbt, batch, seq)
        out_c = _sb(qc, kc, vct, batch, seq)
        x2 = _out_proj(x2, mod, out_a.reshape(batch * seq, -1), out_b.reshape(batch * seq, -1),
                       out_c.reshape(batch * seq, -1), w_out[l].astype(BF16), seq)

        x2 = _ffn(x2, mod, g_ffn2[l], w2_gate[l].astype(BF16), w2_up[l].astype(BF16),
                  w2_down[l].astype(BF16), 6, seq)

    return x2.reshape(batch, seq, d_model)
```

```python
import functools

import jax
import jax.numpy as jnp
from jax import lax
from jax.experimental import pallas as pl
from jax.experimental.pallas import tpu as pltpu

F32 = jnp.float32
BF16 = jnp.bfloat16
I32 = jnp.int32
I16 = jnp.int16

HEAD_DIM = 128
CHUNK = 64
CHUNK_SHIFT = 6
ROPE_THETA = 500000.0
PARTIAL_ROPE_DIM = HEAD_DIM // 4
NORM_EPS = 1e-6
D_FF = 5632
N_MOD = 9
A_HEADS = 4
IDX_HEADS = 16
IDX_DIM = 64
TOPK_MAX = 256
B_HEADS = 8
MLA_Q_RANK = 448
MLA_KV_RANK = 128
MLA_NOPE = 128
MLA_ROPE = 64
C_HEADS = 4

LANES = 128
KEY_BLOCK = 256
Q_RANK_PAD = 512
VMEM_LIMIT = 56 * 1024 * 1024
FFN_VMEM_LIMIT = 61 * 1024 * 1024

OFF_QA, OFF_KA, OFF_VA, OFF_QI = 0, 512, 1024, 1536
OFF_QC, OFF_KC, OFF_VC = 2560, 3072, 3584
OFF_CQ, OFF_CKV, OFF_KR, OFF_KIW = 4096, 4608, 4736, 4864
N_PROJ = 4992

LOG2E = 1.4426950408889634
A_SCALE = HEAD_DIM ** -0.5 * LOG2E
B_SCALE = (MLA_NOPE + MLA_ROPE) ** -0.5 * LOG2E
C_SCALE = HEAD_DIM ** -0.5
IDX_SCALE = (IDX_DIM ** -0.5) * (IDX_HEADS ** -0.5)

NT_DIMS = (((1,), (1,)), ((), ()))
NEG_INF = float("-inf")
INT_MIN = -2 ** 31


def _params(*sem, vmem_limit=VMEM_LIMIT):
    return pltpu.CompilerParams(dimension_semantics=sem, vmem_limit_bytes=vmem_limit)


def _ada_kernel(c_ref, w_ref, b_ref, o_ref):
    c = c_ref[...]
    ca = (c * jax.nn.sigmoid(c)).astype(BF16)
    o_ref[0] = jnp.dot(ca, w_ref[0].astype(BF16), preferred_element_type=F32) + b_ref[0]


def _ada(c, w_ada, b_ada):
    L, D, N = w_ada.shape
    B = c.shape[0]
    tn = 1024
    return pl.pallas_call(
        _ada_kernel,
        grid=(L, N // tn),
        in_specs=[
            pl.BlockSpec((B, D), lambda l, j: (0, 0)),
            pl.BlockSpec((1, D, tn), lambda l, j: (l, 0, j)),
            pl.BlockSpec((1, 1, tn), lambda l, j: (l, 0, j)),
        ],
        out_specs=pl.BlockSpec((1, B, tn), lambda l, j: (l, 0, j)),
        out_shape=jax.ShapeDtypeStruct((L, B, N), F32),
        compiler_params=_params("arbitrary", "arbitrary"),
        name="ada_mod",
    )(c, w_ada, b_ada.reshape(L, 1, N))


NORM_ROWS = 16


def _norm_mod_into(h_ref, x_ref, g, shift, scale):
    gain = g * (1.0 + scale)

    def rows(r, carry):
        sl = pl.ds(pl.multiple_of(r * NORM_ROWS, NORM_ROWS), NORM_ROWS)
        x = x_ref[sl, :]
        y = x * lax.rsqrt(jnp.mean(x * x, axis=-1, keepdims=True) + NORM_EPS)
        h_ref[sl, :] = (y * gain + shift).astype(BF16)
        return carry

    lax.fori_loop(0, x_ref.shape[0] // NORM_ROWS, rows, 0, unroll=8)


def _ffn_kernel(x_ref, mod_ref, g_ref, wg_ref, wu_ref, wd_ref, o_ref, h_ref, *, row):
    j = pl.program_id(1)

    @pl.when(j == 0)
    def _():
        _norm_mod_into(h_ref, x_ref, g_ref[...], mod_ref[0, row:row + 1, :], mod_ref[0, row + 1:row + 2, :])
        o_ref[...] = jnp.zeros(o_ref.shape, F32)

    h = h_ref[...]
    g = jnp.dot(h, wg_ref[...], preferred_element_type=F32)
    u = jnp.dot(h, wu_ref[...], preferred_element_type=F32)
    a = ((g * jax.nn.sigmoid(g)) * u).astype(BF16)
    o_ref[...] += jnp.dot(a, wd_ref[...], preferred_element_type=F32)

    @pl.when(j == pl.num_programs(1) - 1)
    def _():
        gate = mod_ref[0, row + 2:row + 3, :]
        o_ref[...] = x_ref[...] + (0.5 * (1.0 + gate)) * o_ref[...]


def _ffn(x2, mod, g, wg, wu, wd, row, seq):
    T, D = x2.shape
    F = wg.shape[1]
    tm = min(1024, seq)
    tf = 512
    per_seq = seq // tm
    return pl.pallas_call(
        functools.partial(_ffn_kernel, row=row),
        grid=(T // tm, F // tf),
        in_specs=[
            pl.BlockSpec((tm, D), lambda i, j: (i, 0)),
            pl.BlockSpec((1, N_MOD, D), lambda i, j: (i // per_seq, 0, 0)),
            pl.BlockSpec((1, D), lambda i, j: (0, 0)),
            pl.BlockSpec((D, tf), lambda i, j: (0, j)),
            pl.BlockSpec((D, tf), lambda i, j: (0, j)),
            pl.BlockSpec((tf, D), lambda i, j: (j, 0)),
        ],
        out_specs=pl.BlockSpec((tm, D), lambda i, j: (i, 0)),
        out_shape=jax.ShapeDtypeStruct((T, D), F32),
        scratch_shapes=[pltpu.VMEM((tm, D), BF16)],
        compiler_params=_params("arbitrary", "arbitrary", vmem_limit=FFN_VMEM_LIMIT),
        name="ffn",
    )(x2, mod, g.reshape(1, D), wg, wu, wd)


def _rms_lanes(x, g, n):
    return x * lax.rsqrt(jnp.sum(x * x, axis=-1, keepdims=True) / n + NORM_EPS) * g


def _rope_lanes(x, cos, sin_lo, sin_hi, half):
    return (x * cos + pltpu.roll(x, LANES - half, 1) * sin_lo + pltpu.roll(x, half, 1) * sin_hi)


def _prep_kernel(x_ref, mod_ref, gmix_ref, win_ref, tab_ref,
                 gqa_ref, gka_ref, gcq_ref, gckv_ref, gqn_ref, gkn_ref, gqr_ref, gkr_ref,
                 wuq_ref, wukv_ref,
                 qa_o, ka_o, vat_o, qi_o, ki_o, wt_o, qcat_o, kcat_o, vbt_o, qc_o, kc_o, vct_o, h_ref, t_ref):
    ca, sa_lo, sa_hi = tab_ref[0], tab_ref[1], tab_ref[2]
    cm, sm_lo, sm_hi = tab_ref[3], tab_ref[4], tab_ref[5]
    half_a = PARTIAL_ROPE_DIM // 2
    half_m = MLA_ROPE // 2
    wa, wc = A_HEADS * HEAD_DIM, C_HEADS * HEAD_DIM

    _norm_mod_into(h_ref, x_ref, gmix_ref[...], mod_ref[0, 3:4, :], mod_ref[0, 4:5, :])

    def transposed(v):
        t_ref[:, 0:v.shape[1]] = v
        return t_ref[:, 0:v.shape[1]].T

    def proj(lo, width):
        return jnp.dot(h_ref[...], win_ref[:, lo:lo + width], preferred_element_type=F32)

    cq = proj(OFF_CQ, Q_RANK_PAD)
    small = proj(OFF_CKV, N_PROJ - OFF_CKV)
    qa = proj(OFF_QA, wa)
    ka = proj(OFF_KA, wa)
    cq = _rms_lanes(cq, gcq_ref[...], MLA_Q_RANK)
    ckv = _rms_lanes(small[:, 0:MLA_KV_RANK], gckv_ref[...], MLA_KV_RANK)
    qb = jnp.dot(cq.astype(BF16), wuq_ref[...], preferred_element_type=F32)
    kvb = jnp.dot(ckv.astype(BF16), wukv_ref[...], preferred_element_type=F32)
    va = proj(OFF_VA, wa)
    qi = proj(OFF_QI, IDX_HEADS * IDX_DIM)
    qc = proj(OFF_QC, wc)
    kc = proj(OFF_KC, wc)
    vc = proj(OFF_VC, wc)

    for h in range(A_HEADS):
        lo = h * HEAD_DIM
        q = _rms_lanes(qa[:, lo:lo + HEAD_DIM], gqa_ref[...], HEAD_DIM)
        t_ref[:, lo:lo + HEAD_DIM] = _rope_lanes(q, ca, sa_lo, sa_hi, half_a) * A_SCALE
    qa_o[0, 0] = t_ref[:, 0:wa].T.astype(BF16)
    for h in range(A_HEADS):
        lo = h * HEAD_DIM
        k = _rms_lanes(ka[:, lo:lo + HEAD_DIM], gka_ref[...], HEAD_DIM)
        ka_o[0, :, lo:lo + HEAD_DIM] = _rope_lanes(k, ca, sa_lo, sa_hi, half_a).astype(BF16)
    vat_o[0, 0] = transposed(va).astype(BF16)
    qi_o[0, 0] = transposed(qi).astype(BF16)

    qc_o[0, 0] = transposed(qc * C_SCALE).astype(BF16)
    kc_o[0] = kc.astype(BF16)
    vct_o[0, 0] = transposed(vc).astype(BF16)

    kr = _rms_lanes(small[:, OFF_KR - OFF_CKV:OFF_KR - OFF_CKV + LANES], gkr_ref[...], MLA_ROPE)
    kr = _rope_lanes(kr, cm, sm_lo, sm_hi, half_m).astype(BF16)
    kiw = small[:, OFF_KIW - OFF_CKV:OFF_KIW - OFF_CKV + LANES]
    ki_o[0] = kiw[:, :IDX_DIM].astype(BF16)
    wt_o[0] = transposed(kiw)[IDX_DIM:IDX_DIM + IDX_HEADS, :] * IDX_SCALE
    for h in range(B_HEADS):
        lo = h * 2 * LANES
        kn = _rms_lanes(kvb[:, h * MLA_NOPE:(h + 1) * MLA_NOPE], gkn_ref[...], MLA_NOPE)
        kcat_o[0, :, lo:lo + MLA_NOPE] = kn.astype(BF16)
        kcat_o[0, :, lo + MLA_NOPE:lo + 2 * LANES] = kr
    vbt_o[0, 0] = transposed(kvb[:, B_HEADS * MLA_NOPE:]).astype(BF16)
    for h in range(B_HEADS):
        lo = h * 2 * LANES
        qn = _rms_lanes(qb[:, lo:lo + MLA_NOPE], gqn_ref[...], MLA_NOPE)
        t_ref[:, lo:lo + MLA_NOPE] = qn * B_SCALE
        qr = _rms_lanes(qb[:, lo + MLA_NOPE:lo + 2 * LANES], gqr_ref[...], MLA_ROPE)
        t_ref[:, lo + MLA_NOPE:lo + 2 * LANES] = _rope_lanes(qr, cm, sm_lo, sm_hi, half_m) * B_SCALE
    qcat_o[0, 0] = t_ref[...].T.astype(BF16)


def _resident(shape):
    return pl.BlockSpec(shape, lambda b, i: (0,) * len(shape), pipeline_mode=pl.Buffered(1))


def _prep(x2, mod, g_mix, w_in_p, tabs, gains, wuq, wukv, batch, seq):
    tm = KEY_BLOCK
    nt = seq // tm
    d_model = x2.shape[1]
    wa, wb, wc = A_HEADS * HEAD_DIM, B_HEADS * HEAD_DIM, C_HEADS * HEAD_DIM
    wcat = B_HEADS * 2 * LANES

    def tok(width, dtype):
        return (jax.ShapeDtypeStruct((batch, seq, width), dtype),
                pl.BlockSpec((1, tm, width), lambda b, i: (b, i, 0)))

    def tok_t(width):
        return (jax.ShapeDtypeStruct((batch, nt, width, tm), BF16),
                pl.BlockSpec((1, 1, width, tm), lambda b, i: (b, i, 0, 0)))

    outs = [tok_t(wa), tok(wa, BF16), tok_t(wa), tok_t(IDX_HEADS * IDX_DIM), tok(IDX_DIM, BF16),
            (jax.ShapeDtypeStruct((batch, IDX_HEADS, seq), F32),
             pl.BlockSpec((1, IDX_HEADS, tm), lambda b, i: (b, 0, i))),
            tok_t(wcat), tok(wcat, BF16), tok_t(wb), tok_t(wc), tok(wc, BF16), tok_t(wc)]
    return pl.pallas_call(
        _prep_kernel,
        grid=(batch, nt),
        in_specs=[pl.BlockSpec((tm, d_model), lambda b, i: (b * nt + i, 0)),
                  pl.BlockSpec((1, N_MOD, d_model), lambda b, i: (b, 0, 0)),
                  _resident((1, d_model)),
                  _resident(w_in_p.shape),
                  pl.BlockSpec((6, tm, LANES), lambda b, i: (0, i, 0))]
                 + [_resident(g.shape) for g in gains]
                 + [_resident(wuq.shape), _resident(wukv.shape)],
        out_specs=[o[1] for o in outs],
        out_shape=[o[0] for o in outs],
        scratch_shapes=[pltpu.VMEM((tm, d_model), BF16), pltpu.VMEM((tm, wcat), F32)],
        compiler_params=_params("arbitrary", "arbitrary"),
        name="head_prep",
    )(x2, mod, g_mix.reshape(1, d_model), w_in_p, tabs, *gains, wuq, wukv)


MLA_HEADS_PER_STEP = 4


def _mla_kernel(qt_ref, k_ref, vt_ref, o_ref, s0_ref, s1_ref, m_ref, l_ref, acc_ref, *, t):
    qb = pl.program_id(2)
    sub = t // KEY_BLOCK
    dq = 2 * LANES
    m_ref[...] = jnp.full(m_ref.shape, NEG_INF, F32)
    l_ref[...] = jnp.zeros(l_ref.shape, F32)
    acc_ref[...] = jnp.zeros(acc_ref.shape, F32)

    def produce(s_ref, kb):
        ks = pl.multiple_of(kb * t, t)
        for h in range(MLA_HEADS_PER_STEP):
            k = k_ref[0, pl.ds(ks, t), h * dq:(h + 1) * dq]
            for c in range(sub):
                s_ref[h, :, c * KEY_BLOCK:(c + 1) * KEY_BLOCK] = jnp.dot(
                    k, qt_ref[0, c, h * dq:(h + 1) * dq, :], preferred_element_type=F32)

    def consume(s_ref, kb, diagonal):
        for h in range(MLA_HEADS_PER_STEP):
            s = s_ref[h]
            if diagonal:
                kc = lax.broadcasted_iota(I32, (t, t), 0) >> CHUNK_SHIFT
                qc = lax.broadcasted_iota(I32, (t, t), 1) >> CHUNK_SHIFT
                s = jnp.where(kc <= qc, s, NEG_INF)
            m_prev = m_ref[h]
            m_new = jnp.maximum(m_prev, jnp.max(s, axis=0, keepdims=True))
            p = jnp.exp2(s - m_new)
            alpha = jnp.exp2(m_prev - m_new)
            l_ref[h] = alpha * l_ref[h] + jnp.sum(p, axis=0, keepdims=True)
            p = p.astype(BF16)
            pv = jnp.dot(vt_ref[0, kb * sub, h * HEAD_DIM:(h + 1) * HEAD_DIM, :], p[0:KEY_BLOCK],
                         preferred_element_type=F32)
            for c in range(1, sub):
                pv += jnp.dot(vt_ref[0, kb * sub + c, h * HEAD_DIM:(h + 1) * HEAD_DIM, :],
                              p[c * KEY_BLOCK:(c + 1) * KEY_BLOCK], preferred_element_type=F32)
            acc_ref[h] = alpha * acc_ref[h] + pv
            m_ref[h] = m_new

    produce(s0_ref, 0)

    def body(j, carry):
        produce(s1_ref, 2 * j + 1)
        consume(s0_ref, 2 * j, False)
        produce(s0_ref, 2 * j + 2)
        consume(s1_ref, 2 * j + 1, False)
        return carry

    lax.fori_loop(0, qb // 2, body, 0)

    @pl.when(qb % 2 == 0)
    def _():
        consume(s0_ref, qb, True)

    @pl.when(qb % 2 == 1)
    def _():
        produce(s1_ref, qb)
        consume(s0_ref, qb - 1, False)
        consume(s1_ref, qb, True)

    for h in range(MLA_HEADS_PER_STEP):
        o_ref[0, :, h * HEAD_DIM:(h + 1) * HEAD_DIM] = (acc_ref[h] / l_ref[h]).T.astype(BF16)


def _mla(qcat, kcat, vbt, batch, seq):
    t = min(512, seq)
    hp = MLA_HEADS_PER_STEP
    nkb = seq // KEY_BLOCK
    return pl.pallas_call(
        functools.partial(_mla_kernel, t=t),
        grid=(batch, B_HEADS // hp, seq // t),
        in_specs=[
            pl.BlockSpec((1, t // KEY_BLOCK, hp * 2 * LANES, KEY_BLOCK), lambda b, h, i: (b, i, h, 0)),
            pl.BlockSpec((1, seq, hp * 2 * LANES), lambda b, h, i: (b, 0, h)),
            pl.BlockSpec((1, nkb, hp * HEAD_DIM, KEY_BLOCK), lambda b, h, i: (b, 0, h, 0)),
        ],
        out_specs=pl.BlockSpec((1, t, hp * HEAD_DIM), lambda b, h, i: (b, i, h)),
        out_shape=jax.ShapeDtypeStruct((batch, seq, B_HEADS * HEAD_DIM), BF16),
        scratch_shapes=[pltpu.VMEM((hp, t, t), F32), pltpu.VMEM((hp, t, t), F32),
                        pltpu.VMEM((hp, 1, t), F32), pltpu.VMEM((hp, 1, t), F32),
                        pltpu.VMEM((hp, HEAD_DIM, t), F32)],
        compiler_params=_params("arbitrary", "arbitrary", "arbitrary"),
        name="mla_attn",
    )(qcat, kcat, vbt)


SB_HEADS_PER_STEP = 4


def _sb_kernel(qt_ref, k_ref, vt_ref, o_ref, z0_ref, z1_ref, r_ref, acc_ref, *, tq):
    qb = pl.program_id(2)
    tk = KEY_BLOCK
    sub = tq // tk
    assert sub % 2 == 0
    tri_r = lax.broadcasted_iota(I32, (tk, tk), 0)
    tri_c = lax.broadcasted_iota(I32, (tk, tk), 1)
    later_keys = jnp.where(tri_c > tri_r, 1.0, 0.0).astype(BF16)
    r_ref[...] = jnp.zeros(r_ref.shape, F32)
    acc_ref[...] = jnp.zeros(acc_ref.shape, F32)

    heads = range(SB_HEADS_PER_STEP)

    def produce(z_ref, kb):
        ks = pl.multiple_of(kb * tk, tk)
        for h in heads:
            k = k_ref[0, pl.ds(ks, tk), h * HEAD_DIM:(h + 1) * HEAD_DIM]
            for c in range(sub):
                z_ref[h, :, c * tk:(c + 1) * tk] = jnp.dot(
                    k, qt_ref[0, c, h * HEAD_DIM:(h + 1) * HEAD_DIM, :], preferred_element_type=F32)

    def consume(z_ref, kb, key_offset):
        if key_offset is not None:
            before = (lax.broadcasted_iota(I32, (tk, tq), 0) + key_offset
                      < lax.broadcasted_iota(I32, (tk, tq), 1))
        zs = [z_ref[h] for h in heads]
        go = []
        for z in zs:
            neg_abs = lax.bitcast_convert_type(lax.bitcast_convert_type(z, I32) | INT_MIN, F32)
            g = jnp.maximum(z, 0.0) + jnp.log(1.0 + jnp.exp(neg_abs))
            if key_offset is not None:
                g = jnp.where(before, g, 0.0)
            go.append(g)
        later = []
        for g in go:
            hi = g.astype(BF16)
            lo = (g - hi.astype(F32)).astype(BF16)
            later.append(jnp.dot(later_keys, hi, preferred_element_type=F32)
                         + jnp.dot(later_keys, lo, preferred_element_type=F32))
        for h in heads:
            w = jnp.exp(zs[h] - (go[h] + later[h] + r_ref[h]))
            if key_offset is not None:
                w = jnp.where(before, w, 0.0)
            acc_ref[h] += jnp.dot(vt_ref[0, kb, h * HEAD_DIM:(h + 1) * HEAD_DIM, :], w.astype(BF16),
                                  preferred_element_type=F32)
            r_ref[h] += jnp.sum(go[h], axis=0, keepdims=True)

    first = qb * sub
    bufs = (z0_ref, z1_ref)
    produce(bufs[0], first + sub - 1)
    for i, d in enumerate(reversed(range(sub))):
        if d > 0:
            produce(bufs[(i + 1) % 2], first + d - 1)
        else:
            @pl.when(qb > 0)
            def _():
                produce(bufs[(i + 1) % 2], first - 1)
        consume(bufs[i % 2], first + d, d * tk)

    def body(j, carry):
        top = first - 1 - 2 * j
        produce(z1_ref, top - 1)
        consume(z0_ref, top, None)
        produce(z0_ref, top - 2)
        consume(z1_ref, top - 1, None)
        return carry

    lax.fori_loop(0, first // 2 - 1, body, 0)

    @pl.when(qb > 0)
    def _():
        produce(z1_ref, 0)
        consume(z0_ref, 1, None)
        consume(z1_ref, 0, None)

    for h in range(SB_HEADS_PER_STEP):
        o_ref[0, :, h * HEAD_DIM:(h + 1) * HEAD_DIM] = acc_ref[h].T.astype(BF16)


def _sb(qc, kc, vct, batch, seq):
    tq = min(512, seq)
    hp = SB_HEADS_PER_STEP
    nkb = seq // KEY_BLOCK
    return pl.pallas_call(
        functools.partial(_sb_kernel, tq=tq),
        grid=(batch, C_HEADS // hp, seq // tq),
        in_specs=[
            pl.BlockSpec((1, tq // KEY_BLOCK, hp * HEAD_DIM, KEY_BLOCK), lambda b, h, i: (b, i, h, 0)),
            pl.BlockSpec((1, seq, hp * HEAD_DIM), lambda b, h, i: (b, 0, h)),
            pl.BlockSpec((1, nkb, hp * HEAD_DIM, KEY_BLOCK), lambda b, h, i: (b, 0, h, 0)),
        ],
        out_specs=pl.BlockSpec((1, tq, hp * HEAD_DIM), lambda b, h, i: (b, i, h)),
        out_shape=jax.ShapeDtypeStruct((batch, seq, C_HEADS * HEAD_DIM), BF16),
        scratch_shapes=[pltpu.VMEM((hp, KEY_BLOCK, tq), F32), pltpu.VMEM((hp, KEY_BLOCK, tq), F32),
                        pltpu.VMEM((hp, 1, tq), F32), pltpu.VMEM((hp, HEAD_DIM, tq), F32)],
        compiler_params=_params("arbitrary", "arbitrary", "arbitrary"),
        name="sb_attn",
    )(qc, kc, vct)


SEARCH_GROUP = 2


def _dsa_kernel(qit_ref, ki_ref, wt_ref, qat_ref, ka_ref, vt_ref, o_ref,
                keys_ref, half_ref, bias_ref, thr_ref, lim_ref, m_ref, l_ref, acc_ref, s0_ref, s1_ref,
                *, t, topk, seq):
    qb = pl.program_id(1)
    nk = qb + 1
    row = lax.broadcasted_iota(I32, (t, t), 0)
    col = lax.broadcasted_iota(I32, (t, t), 1)
    wt = wt_ref[0]

    def score_block(kb):
        ks = pl.multiple_of(kb * t, t)
        kix = ki_ref[0, pl.ds(ks, t), :]
        sc = jnp.zeros((t, t), F32)
        for h in range(IDX_HEADS):
            lg = jnp.dot(kix, qit_ref[0, 0, h * IDX_DIM:(h + 1) * IDX_DIM, :],
                         preferred_element_type=F32)
            sc = sc + wt[h:h + 1, :] * jnp.maximum(lg, 0.0)
        bits = lax.bitcast_convert_type(sc, I32)
        key = bits ^ ((bits >> 31) & 0x7FFFFFFF)
        visible = ((ks + row) >> CHUNK_SHIFT) <= ((qb * t + col) >> CHUNK_SHIFT)
        key = jnp.where(visible, key, INT_MIN)
        keys_ref[kb] = key
        half_ref[kb] = (key >> 16).astype(I16)

    def score_pair(j, carry):
        score_block(2 * j)
        score_block(2 * j + 1)
        return carry

    lax.fori_loop(0, nk // 2, score_pair, 0)

    @pl.when(nk % 2 == 1)
    def _():
        score_block(nk - 1)

    i16_min, i16_max = -2 ** 15, 2 ** 15 - 1
    for pad in range(SEARCH_GROUP - 1):
        half_ref[nk + pad] = jnp.full((t, t), i16_min, I16)

    def search16():
        def count16(cand):
            cand16 = cand.astype(I16)

            def block_group(j, acc):
                parts = [acc]
                for kb in range(SEARCH_GROUP):
                    hit = jnp.where(half_ref[SEARCH_GROUP * j + kb] >= cand16, jnp.ones((), I16), jnp.zeros((), I16))
                    parts += [hit[i * 16:(i + 1) * 16] for i in range(t // 16)]
                while len(parts) > 1:
                    parts = [a + b for a, b in zip(parts[0::2], parts[1::2])] + parts[len(parts) & ~1:]
                return parts[0]
            groups = (nk + SEARCH_GROUP - 1) // SEARCH_GROUP
            acc = lax.fori_loop(0, groups, block_group, jnp.zeros((16, t), I16))
            return jnp.sum(acc.astype(I32), axis=0, keepdims=True)

        c0 = count16(jnp.zeros((1, t), I32))
        v0 = jnp.where(c0 >= topk, 0, i16_min).astype(I32)
        n0 = jnp.where(c0 >= topk, c0, 0)

        def bit(b, carry):
            v, n = carry
            cand = v | lax.shift_left(jnp.int32(1), 14 - b)
            c = count16(cand)
            return jnp.where(c >= topk, cand, v), jnp.where(c >= topk, c, n)

        return lax.fori_loop(0, 15, bit, (v0, n0))

    def count(pred):
        def block(kb, acc):
            hit = jnp.where(pred(keys_ref[kb], kb * t + row), 1, 0).astype(I32)
            parts = [acc] + [hit[i * 8:(i + 1) * 8] for i in range(t // 8)]
            while len(parts) > 1:
                parts = [a + b for a, b in zip(parts[0::2], parts[1::2])] + parts[len(parts) & ~1:]
            return parts[0]
        acc = lax.fori_loop(0, nk, block, jnp.zeros((8, t), I32))
        return jnp.sum(acc, axis=0, keepdims=True)

    thr_hi, n_hi = search16()

    def low_half_block(kb, carry):
        key = keys_ref[kb]
        hi = key >> 16
        low = (key & 0xFFFF) + i16_min
        half_ref[kb] = jnp.where(hi == thr_hi, low, jnp.where(hi > thr_hi, i16_max, i16_min)).astype(I16)
        return carry

    lax.fori_loop(0, nk, low_half_block, 0)
    thr_lo, n_lo = search16()
    thr = lax.shift_left(thr_hi, 16) + (thr_lo - i16_min)
    n_ge = jnp.where(thr_lo > i16_min, n_lo, n_hi)
    tie = (thr != INT_MIN) & (n_ge > topk)
    thr_ref[...] = thr
    lim_ref[...] = jnp.where(thr == INT_MIN, 0, seq).astype(I32)

    @pl.when(jnp.max(tie.astype(I32)) > 0)
    def _():
        nbits = seq.bit_length() - 1
        need = topk - count(lambda k, i: k > thr)

        def lim_bit(b, lim):
            cand = lim | lax.shift_left(jnp.int32(1), nbits - 1 - b)
            below = count(lambda k, i: (k == thr) & (i < cand))
            return jnp.where(below < need, cand, lim)

        lim = lax.fori_loop(0, nbits, lim_bit, jnp.zeros((1, t), I32))
        lim_ref[...] = jnp.where(tie, lim + 1, lim_ref[...])

    def bias_block(kb, carry):
        key = keys_ref[kb]
        thr_q = thr_ref[...]
        at_thr = jnp.where((kb * t + row) < lim_ref[...], 0.0, NEG_INF)
        bias_ref[kb] = jnp.where(key > thr_q, 0.0, jnp.where(key == thr_q, at_thr, NEG_INF))
        return carry

    lax.fori_loop(0, nk, bias_block, 0)
    m_ref[...] = jnp.full(m_ref.shape, NEG_INF, F32)
    l_ref[...] = jnp.zeros(l_ref.shape, F32)
    acc_ref[...] = jnp.zeros(acc_ref.shape, F32)

    def produce(s_ref, kb):
        ks = pl.multiple_of(kb * t, t)
        bias = bias_ref[kb]
        for h in range(A_HEADS):
            s_ref[h] = jnp.dot(ka_ref[0, pl.ds(ks, t), h * HEAD_DIM:(h + 1) * HEAD_DIM],
                               qat_ref[0, 0, h * HEAD_DIM:(h + 1) * HEAD_DIM, :],
                               preferred_element_type=F32) + bias

    def consume(s_ref, kb):
        for h in range(A_HEADS):
            lo = h * HEAD_DIM
            s = s_ref[h]
            m_prev = m_ref[h]
            m_new = jnp.maximum(m_prev, jnp.max(s, axis=0, keepdims=True))
            m_safe = jnp.where(m_new == NEG_INF, 0.0, m_new)
            p = jnp.exp2(s - m_safe)
            alpha = jnp.exp2(m_prev - m_safe)
            l_ref[h] = alpha * l_ref[h] + jnp.sum(p, axis=0, keepdims=True)
            acc_ref[h] = alpha * acc_ref[h] + jnp.dot(vt_ref[0, kb, lo:lo + HEAD_DIM, :], p.astype(BF16),
                                                      preferred_element_type=F32)
            m_ref[h] = m_new

    produce(s0_ref, 0)

    def attend_pair(j, carry):
        produce(s1_ref, 2 * j + 1)
        consume(s0_ref, 2 * j)
        produce(s0_ref, jnp.minimum(2 * j + 2, nk - 1))
        consume(s1_ref, 2 * j + 1)
        return carry

    lax.fori_loop(0, nk // 2, attend_pair, 0)

    @pl.when(nk % 2 == 1)
    def _():
        consume(s0_ref, nk - 1)

    for h in range(A_HEADS):
        o_ref[0, :, h * HEAD_DIM:(h + 1) * HEAD_DIM] = (acc_ref[h] / l_ref[h]).T.astype(BF16)


def _dsa(qi, ki, wt, qa, ka, vat, batch, seq, topk):
    t = KEY_BLOCK
    nt = seq // t
    wa = A_HEADS * HEAD_DIM
    return pl.pallas_call(
        functools.partial(_dsa_kernel, t=t, topk=topk, seq=seq),
        grid=(batch, nt),
        in_specs=[
            pl.BlockSpec((1, 1, IDX_HEADS * IDX_DIM, t), lambda b, i: (b, i, 0, 0)),
            pl.BlockSpec((1, seq, IDX_DIM), lambda b, i: (b, 0, 0)),
            pl.BlockSpec((1, IDX_HEADS, t), lambda b, i: (b, 0, i)),
            pl.BlockSpec((1, 1, wa, t), lambda b, i: (b, i, 0, 0)),
            pl.BlockSpec((1, seq, wa), lambda b, i: (b, 0, 0)),
            pl.BlockSpec((1, nt, wa, t), lambda b, i: (b, 0, 0, 0)),
        ],
        out_specs=pl.BlockSpec((1, t, wa), lambda b, i: (b, i, 0)),
        out_shape=jax.ShapeDtypeStruct((batch, seq, wa), BF16),
        scratch_shapes=[
            pltpu.VMEM((nt, t, t), I32), pltpu.VMEM((nt + SEARCH_GROUP - 1, t, t), I16), pltpu.VMEM((nt, t, t), F32),
            pltpu.VMEM((1, t), I32), pltpu.VMEM((1, t), I32),
            pltpu.VMEM((A_HEADS, 1, t), F32), pltpu.VMEM((A_HEADS, 1, t), F32),
            pltpu.VMEM((A_HEADS, HEAD_DIM, t), F32),
            pltpu.VMEM((A_HEADS, t, t), F32), pltpu.VMEM((A_HEADS, t, t), F32),
        ],
        compiler_params=_params("arbitrary", "arbitrary"),
        name="dsa_attn",
    )(qi, ki, wt, qa, ka, vat)


def _out_kernel(x_ref, mod_ref, oa_ref, ob_ref, oc_ref, w_ref, o_ref):
    wa = A_HEADS * HEAD_DIM
    wb = B_HEADS * HEAD_DIM
    mixed = (jnp.dot(oa_ref[...], w_ref[0:wa, :], preferred_element_type=F32)
             + jnp.dot(ob_ref[...], w_ref[wa:wa + wb, :], preferred_element_type=F32)
             + jnp.dot(oc_ref[...], w_ref[wa + wb:, :], preferred_element_type=F32))
    o_ref[...] = x_ref[...] + (1.0 + mod_ref[0, 5:6, :]) * mixed


def _out_proj(x2, mod, oa, ob, oc, w, seq):
    T, D = x2.shape
    tm = min(512, seq)
    per_seq = seq // tm
    return pl.pallas_call(
        _out_kernel,
        grid=(T // tm,),
        in_specs=[
            pl.BlockSpec((tm, D), lambda i: (i, 0)),
            pl.BlockSpec((1, N_MOD, D), lambda i: (i // per_seq, 0, 0)),
            pl.BlockSpec((tm, oa.shape[1]), lambda i: (i, 0)),
            pl.BlockSpec((tm, ob.shape[1]), lambda i: (i, 0)),
            pl.BlockSpec((tm, oc.shape[1]), lambda i: (i, 0)),
            pl.BlockSpec(w.shape, lambda i: (0, 0)),
        ],
        out_specs=pl.BlockSpec((tm, D), lambda i: (i, 0)),
        out_shape=jax.ShapeDtypeStruct((T, D), F32),
        compiler_params=_params("arbitrary"),
        name="out_proj",
    )(x2, mod, oa, ob, oc, w)


def _rope_tables(seq):
    def tables(dim):
        inv = 1.0 / (ROPE_THETA ** (jnp.arange(0, dim, 2, dtype=F32) / dim))
        ang = jnp.arange(seq, dtype=F32)[:, None] * inv[None, :]
        return jnp.cos(ang), jnp.sin(ang)

    def lane_tables(cos, sin, fill):
        half = cos.shape[1]
        rest = LANES - 2 * half
        zeros_h = jnp.zeros((seq, half), F32)
        zeros_r = jnp.zeros((seq, rest), F32)
        return [jnp.concatenate([cos, cos, jnp.full((seq, rest), fill, F32)], axis=1),
                jnp.concatenate([-sin, zeros_h, zeros_r], axis=1),
                jnp.concatenate([zeros_h, sin, zeros_r], axis=1)]

    cos_p, sin_p = tables(PARTIAL_ROPE_DIM)
    cos_m, sin_m = tables(MLA_ROPE)
    return jnp.stack(lane_tables(cos_p, sin_p, 1.0) + lane_tables(cos_m, sin_m, 0.0))


def _pad_cols(a, width):
    return jnp.pad(a, ((0, 0), (0, width - a.shape[1])))


def _layer_weights(w_in, w_uq, w_ukv):
    w_in_p = jnp.concatenate([
        w_in[:, 0:2560],
        w_in[:, 3280:4816],
        _pad_cols(w_in[:, 2640:3088], Q_RANK_PAD),
        w_in[:, 3088:3216],
        _pad_cols(w_in[:, 3216:3280], LANES),
        _pad_cols(w_in[:, 2560:2640], LANES),
    ], axis=1).astype(BF16)
    wuq = w_uq.reshape(MLA_Q_RANK, B_HEADS, MLA_NOPE + MLA_ROPE)
    wuq = jnp.pad(wuq, ((0, Q_RANK_PAD - MLA_Q_RANK), (0, 0), (0, 2 * LANES - MLA_NOPE - MLA_ROPE)))
    wuq = wuq.reshape(Q_RANK_PAD, B_HEADS * 2 * LANES).astype(BF16)
    wukv = w_ukv.reshape(MLA_KV_RANK, B_HEADS, MLA_NOPE + HEAD_DIM)
    wukv = jnp.concatenate([wukv[:, :, :MLA_NOPE].reshape(MLA_KV_RANK, -1),
                            wukv[:, :, MLA_NOPE:].reshape(MLA_KV_RANK, -1)], axis=1).astype(BF16)
    return w_in_p, wuq, wukv


def kernel(x, c, w_ada, b_ada, g_ffn1, w1_gate, w1_up, w1_down, g_mix, w_in, g_qa, g_ka, g_cq, g_ckv, w_uq, w_ukv, g_q_nope, g_k_nope, g_q_rope, g_k_rope, w_out, g_ffn2, w2_gate, w2_up, w2_down):
    batch, seq, d_model = x.shape
    depth = w_ada.shape[0]
    topk = min(TOPK_MAX, seq // 4)
    tabs = _rope_tables(seq)
    mods = _ada(c, w_ada, b_ada).reshape(depth, batch, N_MOD, d_model)
    x2 = x.reshape(batch * seq, d_model)

    for l in range(depth):
        mod = mods[l]
        x2 = _ffn(x2, mod, g_ffn1[l], w1_gate[l].astype(BF16), w1_up[l].astype(BF16),
                  w1_down[l].astype(BF16), 0, seq)

        w_in_p, wuq, wukv = _layer_weights(w_in[l], w_uq[l], w_ukv[l])
        gains = [g_qa[l][None, :], g_ka[l][None, :], _pad_cols(g_cq[l][None, :], Q_RANK_PAD),
                 g_ckv[l][None, :], g_q_nope[l][None, :], g_k_nope[l][None, :],
                 _pad_cols(g_q_rope[l][None, :], LANES), _pad_cols(g_k_rope[l][None, :], LANES)]
        (qa, ka, vat, qi, ki, wt, qcat, kcat, vbt, qc, kc, vct) = _prep(
            x2, mod, g_mix[l], w_in_p, tabs, gains, wuq, wukv, batch, seq)

        out_a = _dsa(qi, ki, wt, qa, ka, vat, batch, seq, topk)
        out_b = _mla(qcat, kcat, vbt, batch, seq)
        out_c = _sb(qc, kc, vct, batch, seq)
        x2 = _out_proj(x2, mod, out_a.reshape(batch * seq, -1), out_b.reshape(batch * seq, -1),
                       out_c.reshape(batch * seq, -1), w_out[l].astype(BF16), seq)

        x2 = _ffn(x2, mod, g_ffn2[l], w2_gate[l].astype(BF16), w2_up[l].astype(BF16),
                  w2_down[l].astype(BF16), 6, seq)

    return x2.reshape(batch, seq, d_model)
```

```python
import functools

import jax
import jax.numpy as jnp
from jax import lax
from jax.experimental import pallas as pl
from jax.experimental.pallas import tpu as pltpu

F32 = jnp.float32
BF16 = jnp.bfloat16
I32 = jnp.int32
I16 = jnp.int16

HEAD_DIM = 128
CHUNK = 64
CHUNK_SHIFT = 6
ROPE_THETA = 500000.0
PARTIAL_ROPE_DIM = HEAD_DIM // 4
NORM_EPS = 1e-6
D_FF = 5632
N_MOD = 9
A_HEADS = 4
IDX_HEADS = 16
IDX_DIM = 64
TOPK_MAX = 256
B_HEADS = 8
MLA_Q_RANK = 448
MLA_KV_RANK = 128
MLA_NOPE = 128
MLA_ROPE = 64
C_HEADS = 4

LANES = 128
KEY_BLOCK = 256
Q_RANK_PAD = 512
VMEM_LIMIT = 56 * 1024 * 1024
FFN_VMEM_LIMIT = 61 * 1024 * 1024

OFF_QA, OFF_KA, OFF_VA, OFF_QI = 0, 512, 1024, 1536
OFF_QC, OFF_KC, OFF_VC = 2560, 3072, 3584
OFF_CQ, OFF_CKV, OFF_KR, OFF_KIW = 4096, 4608, 4736, 4864
N_PROJ = 4992

LOG2E = 1.4426950408889634
A_SCALE = HEAD_DIM ** -0.5 * LOG2E
B_SCALE = (MLA_NOPE + MLA_ROPE) ** -0.5 * LOG2E
C_SCALE = HEAD_DIM ** -0.5
IDX_SCALE = (IDX_DIM ** -0.5) * (IDX_HEADS ** -0.5)

NT_DIMS = (((1,), (1,)), ((), ()))
NEG_INF = float("-inf")
INT_MIN = -2 ** 31


def _params(*sem, vmem_limit=VMEM_LIMIT):
    return pltpu.CompilerParams(dimension_semantics=sem, vmem_limit_bytes=vmem_limit)


def _ada_kernel(c_ref, w_ref, b_ref, o_ref):
    c = c_ref[...]
    ca = (c * jax.nn.sigmoid(c)).astype(BF16)
    o_ref[0] = jnp.dot(ca, w_ref[0].astype(BF16), preferred_element_type=F32) + b_ref[0]


def _ada(c, w_ada, b_ada):
    L, D, N = w_ada.shape
    B = c.shape[0]
    tn = 1024
    return pl.pallas_call(
        _ada_kernel,
        grid=(L, N // tn),
        in_specs=[
            pl.BlockSpec((B, D), lambda l, j: (0, 0)),
            pl.BlockSpec((1, D, tn), lambda l, j: (l, 0, j)),
            pl.BlockSpec((1, 1, tn), lambda l, j: (l, 0, j)),
        ],
        out_specs=pl.BlockSpec((1, B, tn), lambda l, j: (l, 0, j)),
        out_shape=jax.ShapeDtypeStruct((L, B, N), F32),
        compiler_params=_params("arbitrary", "arbitrary"),
        name="ada_mod",
    )(c, w_ada, b_ada.reshape(L, 1, N))


NORM_ROWS = 16


def _norm_mod_into(h_ref, x_ref, g, shift, scale):
    gain = g * (1.0 + scale)

    def rows(r, carry):
        sl = pl.ds(pl.multiple_of(r * NORM_ROWS, NORM_ROWS), NORM_ROWS)
        x = x_ref[sl, :]
        y = x * lax.rsqrt(jnp.mean(x * x, axis=-1, keepdims=True) + NORM_EPS)
        h_ref[sl, :] = (y * gain + shift).astype(BF16)
        return carry

    lax.fori_loop(0, x_ref.shape[0] // NORM_ROWS, rows, 0, unroll=8)


def _ffn_kernel(x_ref, mod_ref, g_ref, wg_ref, wu_ref, wd_ref, o_ref, h_ref, *, row):
    j = pl.program_id(1)

    @pl.when(j == 0)
    def _():
        _norm_mod_into(h_ref, x_ref, g_ref[...], mod_ref[0, row:row + 1, :], mod_ref[0, row + 1:row + 2, :])
        o_ref[...] = jnp.zeros(o_ref.shape, F32)

    h = h_ref[...]
    g = jnp.dot(h, wg_ref[...], preferred_element_type=F32)
    u = jnp.dot(h, wu_ref[...], preferred_element_type=F32)
    a = ((g * jax.nn.sigmoid(g)) * u).astype(BF16)
    o_ref[...] += jnp.dot(a, wd_ref[...], preferred_element_type=F32)

    @pl.when(j == pl.num_programs(1) - 1)
    def _():
        gate = mod_ref[0, row + 2:row + 3, :]
        o_ref[...] = x_ref[...] + (0.5 * (1.0 + gate)) * o_ref[...]


def _ffn(x2, mod, g, wg, wu, wd, row, seq):
    T, D = x2.shape
    F = wg.shape[1]
    tm = min(1024, seq)
    tf = 512
    per_seq = seq // tm
    return pl.pallas_call(
        functools.partial(_ffn_kernel, row=row),
        grid=(T // tm, F // tf),
        in_specs=[
            pl.BlockSpec((tm, D), lambda i, j: (i, 0)),
            pl.BlockSpec((1, N_MOD, D), lambda i, j: (i // per_seq, 0, 0)),
            pl.BlockSpec((1, D), lambda i, j: (0, 0)),
            pl.BlockSpec((D, tf), lambda i, j: (0, j)),
            pl.BlockSpec((D, tf), lambda i, j: (0, j)),
            pl.BlockSpec((tf, D), lambda i, j: (j, 0)),
        ],
        out_specs=pl.BlockSpec((tm, D), lambda i, j: (i, 0)),
        out_shape=jax.ShapeDtypeStruct((T, D), F32),
        scratch_shapes=[pltpu.VMEM((tm, D), BF16)],
        compiler_params=_params("arbitrary", "arbitrary", vmem_limit=FFN_VMEM_LIMIT),
        name="ffn",
    )(x2, mod, g.reshape(1, D), wg, wu, wd)


def _rms_lanes(x, g, n):
    return x * lax.rsqrt(jnp.sum(x * x, axis=-1, keepdims=True) / n + NORM_EPS) * g


def _rope_lanes(x, cos, sin_lo, sin_hi, half):
    return (x * cos + pltpu.roll(x, LANES - half, 1) * sin_lo + pltpu.roll(x, half, 1) * sin_hi)


def _prep_kernel(x_ref, mod_ref, gmix_ref, win_ref, tab_ref,
                 gqa_ref, gka_ref, gcq_ref, gckv_ref, gqn_ref, gkn_ref, gqr_ref, gkr_ref,
                 wuq_ref, wukv_ref,
                 qa_o, ka_o, vat_o, qi_o, ki_o, wt_o, qcat_o, kcat_o, vbt_o, qc_o, kc_o, vct_o, h_ref, t_ref):
    ca, sa_lo, sa_hi = tab_ref[0], tab_ref[1], tab_ref[2]
    cm, sm_lo, sm_hi = tab_ref[3], tab_ref[4], tab_ref[5]
    half_a = PARTIAL_ROPE_DIM // 2
    half_m = MLA_ROPE // 2
    wa, wc = A_HEADS * HEAD_DIM, C_HEADS * HEAD_DIM

    _norm_mod_into(h_ref, x_ref, gmix_ref[...], mod_ref[0, 3:4, :], mod_ref[0, 4:5, :])

    def transposed(v):
        t_ref[:, 0:v.shape[1]] = v
        return t_ref[:, 0:v.shape[1]].T

    def proj(lo, width):
        return jnp.dot(h_ref[...], win_ref[:, lo:lo + width], preferred_element_type=F32)

    cq = proj(OFF_CQ, Q_RANK_PAD)
    small = proj(OFF_CKV, N_PROJ - OFF_CKV)
    qa = proj(OFF_QA, wa)
    ka = proj(OFF_KA, wa)
    cq = _rms_lanes(cq, gcq_ref[...], MLA_Q_RANK)
    ckv = _rms_lanes(small[:, 0:MLA_KV_RANK], gckv_ref[...], MLA_KV_RANK)
    qb = jnp.dot(cq.astype(BF16), wuq_ref[...], preferred_element_type=F32)
    kvb = jnp.dot(ckv.astype(BF16), wukv_ref[...], preferred_element_type=F32)
    va = proj(OFF_VA, wa)
    qi = proj(OFF_QI, IDX_HEADS * IDX_DIM)
    qc = proj(OFF_QC, wc)
    kc = proj(OFF_KC, wc)
    vc = proj(OFF_VC, wc)

    for h in range(A_HEADS):
        lo = h * HEAD_DIM
        q = _rms_lanes(qa[:, lo:lo + HEAD_DIM], gqa_ref[...], HEAD_DIM)
        t_ref[:, lo:lo + HEAD_DIM] = _rope_lanes(q, ca, sa_lo, sa_hi, half_a) * A_SCALE
    qa_o[0, 0] = t_ref[:, 0:wa].T.astype(BF16)
    for h in range(A_HEADS):
        lo = h * HEAD_DIM
        k = _rms_lanes(ka[:, lo:lo + HEAD_DIM], gka_ref[...], HEAD_DIM)
        ka_o[0, :, lo:lo + HEAD_DIM] = _rope_lanes(k, ca, sa_lo, sa_hi, half_a).astype(BF16)
    vat_o[0, 0] = transposed(va).astype(BF16)
    qi_o[0, 0] = transposed(qi).astype(BF16)

    qc_o[0, 0] = transposed(qc * C_SCALE).astype(BF16)
    kc_o[0] = kc.astype(BF16)
    vct_o[0, 0] = transposed(vc).astype(BF16)

    kr = _rms_lanes(small[:, OFF_KR - OFF_CKV:OFF_KR - OFF_CKV + LANES], gkr_ref[...], MLA_ROPE)
    kr = _rope_lanes(kr, cm, sm_lo, sm_hi, half_m).astype(BF16)
    kiw = small[:, OFF_KIW - OFF_CKV:OFF_KIW - OFF_CKV + LANES]
    ki_o[0] = kiw[:, :IDX_DIM].astype(BF16)
    wt_o[0] = transposed(kiw)[IDX_DIM:IDX_DIM + IDX_HEADS, :] * IDX_SCALE
    for h in range(B_HEADS):
        lo = h * 2 * LANES
        kn = _rms_lanes(kvb[:, h * MLA_NOPE:(h + 1) * MLA_NOPE], gkn_ref[...], MLA_NOPE)
        kcat_o[0, :, lo:lo + MLA_NOPE] = kn.astype(BF16)
        kcat_o[0, :, lo + MLA_NOPE:lo + 2 * LANES] = kr
    vbt_o[0, 0] = transposed(kvb[:, B_HEADS * MLA_NOPE:]).astype(BF16)
    for h in range(B_HEADS):
        lo = h * 2 * LANES
        qn = _rms_lanes(qb[:, lo:lo + MLA_NOPE], gqn_ref[...], MLA_NOPE)
        t_ref[:, lo:lo + MLA_NOPE] = qn * B_SCALE
        qr = _rms_lanes(qb[:, lo + MLA_NOPE:lo + 2 * LANES], gqr_ref[...], MLA_ROPE)
        t_ref[:, lo + MLA_NOPE:lo + 2 * LANES] = _rope_lanes(qr, cm, sm_lo, sm_hi, half_m) * B_SCALE
    qcat_o[0, 0] = t_ref[...].T.astype(BF16)


def _resident(shape):
    return pl.BlockSpec(shape, lambda b, i: (0,) * len(shape), pipeline_mode=pl.Buffered(1))


def _prep(x2, mod, g_mix, w_in_p, tabs, gains, wuq, wukv, batch, seq):
    tm = KEY_BLOCK
    nt = seq // tm
    d_model = x2.shape[1]
    wa, wb, wc = A_HEADS * HEAD_DIM, B_HEADS * HEAD_DIM, C_HEADS * HEAD_DIM
    wcat = B_HEADS * 2 * LANES

    def tok(width, dtype):
        return (jax.ShapeDtypeStruct((batch, seq, width), dtype),
                pl.BlockSpec((1, tm, width), lambda b, i: (b, i, 0)))

    def tok_t(width):
        return (jax.ShapeDtypeStruct((batch, nt, width, tm), BF16),
                pl.BlockSpec((1, 1, width, tm), lambda b, i: (b, i, 0, 0)))

    outs = [tok_t(wa), tok(wa, BF16), tok_t(wa), tok_t(IDX_HEADS * IDX_DIM), tok(IDX_DIM, BF16),
            (jax.ShapeDtypeStruct((batch, IDX_HEADS, seq), F32),
             pl.BlockSpec((1, IDX_HEADS, tm), lambda b, i: (b, 0, i))),
            tok_t(wcat), tok(wcat, BF16), tok_t(wb), tok_t(wc), tok(wc, BF16), tok_t(wc)]
    return pl.pallas_call(
        _prep_kernel,
        grid=(batch, nt),
        in_specs=[pl.BlockSpec((tm, d_model), lambda b, i: (b * nt + i, 0)),
                  pl.BlockSpec((1, N_MOD, d_model), lambda b, i: (b, 0, 0)),
                  _resident((1, d_model)),
                  _resident(w_in_p.shape),
                  pl.BlockSpec((6, tm, LANES), lambda b, i: (0, i, 0))]
                 + [_resident(g.shape) for g in gains]
                 + [_resident(wuq.shape), _resident(wukv.shape)],
        out_specs=[o[1] for o in outs],
        out_shape=[o[0] for o in outs],
        scratch_shapes=[pltpu.VMEM((tm, d_model), BF16), pltpu.VMEM((tm, wcat), F32)],
        compiler_params=_params("arbitrary", "arbitrary"),
        name="head_prep",
    )(x2, mod, g_mix.reshape(1, d_model), w_in_p, tabs, *gains, wuq, wukv)


MLA_HEADS_PER_STEP = 4


def _mla_kernel(qt_ref, k_ref, vt_ref, o_ref, s0_ref, s1_ref, m_ref, l_ref, acc_ref, *, t):
    qb = pl.program_id(2)
    sub = t // KEY_BLOCK
    dq = 2 * LANES
    m_ref[...] = jnp.full(m_ref.shape, NEG_INF, F32)
    l_ref[...] = jnp.zeros(l_ref.shape, F32)
    acc_ref[...] = jnp.zeros(acc_ref.shape, F32)

    def produce(s_ref, kb):
        ks = pl.multiple_of(kb * t, t)
        for h in range(MLA_HEADS_PER_STEP):
            k = k_ref[0, pl.ds(ks, t), h * dq:(h + 1) * dq]
            for c in range(sub):
                s_ref[h, :, c * KEY_BLOCK:(c + 1) * KEY_BLOCK] = jnp.dot(
                    k, qt_ref[0, c, h * dq:(h + 1) * dq, :], preferred_element_type=F32)

    def consume(s_ref, kb, diagonal):
        for h in range(MLA_HEADS_PER_STEP):
            s = s_ref[h]
            if diagonal:
                kc = lax.broadcasted_iota(I32, (t, t), 0) >> CHUNK_SHIFT
                qc = lax.broadcasted_iota(I32, (t, t), 1) >> CHUNK_SHIFT
                s = jnp.where(kc <= qc, s, NEG_INF)
            m_prev = m_ref[h]
            m_new = jnp.maximum(m_prev, jnp.max(s, axis=0, keepdims=True))
            p = jnp.exp2(s - m_new)
            alpha = jnp.exp2(m_prev - m_new)
            l_ref[h] = alpha * l_ref[h] + jnp.sum(p, axis=0, keepdims=True)
            p = p.astype(BF16)
            pv = jnp.dot(vt_ref[0, kb * sub, h * HEAD_DIM:(h + 1) * HEAD_DIM, :], p[0:KEY_BLOCK],
                         preferred_element_type=F32)
            for c in range(1, sub):
                pv += jnp.dot(vt_ref[0, kb * sub + c, h * HEAD_DIM:(h + 1) * HEAD_DIM, :],
                              p[c * KEY_BLOCK:(c + 1) * KEY_BLOCK], preferred_element_type=F32)
            acc_ref[h] = alpha * acc_ref[h] + pv
            m_ref[h] = m_new

    produce(s0_ref, 0)

    def body(j, carry):
        produce(s1_ref, 2 * j + 1)
        consume(s0_ref, 2 * j, False)
        produce(s0_ref, 2 * j + 2)
        consume(s1_ref, 2 * j + 1, False)
        return carry

    lax.fori_loop(0, qb // 2, body, 0)

    @pl.when(qb % 2 == 0)
    def _():
        consume(s0_ref, qb, True)

    @pl.when(qb % 2 == 1)
    def _():
        produce(s1_ref, qb)
        consume(s0_ref, qb - 1, False)
        consume(s1_ref, qb, True)

    for h in range(MLA_HEADS_PER_STEP):
        o_ref[0, :, h * HEAD_DIM:(h + 1) * HEAD_DIM] = (acc_ref[h] / l_ref[h]).T.astype(BF16)


def _mla(qcat, kcat, vbt, batch, seq):
    t = min(512, seq)
    hp = MLA_HEADS_PER_STEP
    nkb = seq // KEY_BLOCK
    return pl.pallas_call(
        functools.partial(_mla_kernel, t=t),
        grid=(batch, B_HEADS // hp, seq // t),
        in_specs=[
            pl.BlockSpec((1, t // KEY_BLOCK, hp * 2 * LANES, KEY_BLOCK), lambda b, h, i: (b, i, h, 0)),
            pl.BlockSpec((1, seq, hp * 2 * LANES), lambda b, h, i: (b, 0, h)),
            pl.BlockSpec((1, nkb, hp * HEAD_DIM, KEY_BLOCK), lambda b, h, i: (b, 0, h, 0)),
        ],
        out_specs=pl.BlockSpec((1, t, hp * HEAD_DIM), lambda b, h, i: (b, i, h)),
        out_shape=jax.ShapeDtypeStruct((batch, seq, B_HEADS * HEAD_DIM), BF16),
        scratch_shapes=[pltpu.VMEM((hp, t, t), F32), pltpu.VMEM((hp, t, t), F32),
                        pltpu.VMEM((hp, 1, t), F32), pltpu.VMEM((hp, 1, t), F32),
                        pltpu.VMEM((hp, HEAD_DIM, t), F32)],
        compiler_params=_params("arbitrary", "arbitrary", "arbitrary"),
        name="mla_attn",
    )(qcat, kcat, vbt)


SB_HEADS_PER_STEP = 4


def _sb_kernel(qt_ref, k_ref, vt_ref, o_ref, z0_ref, z1_ref, r_ref, acc_ref, *, tq):
    qb = pl.program_id(2)
    tk = KEY_BLOCK
    sub = tq // tk
    assert sub % 2 == 0
    tri_r = lax.broadcasted_iota(I32, (tk, tk), 0)
    tri_c = lax.broadcasted_iota(I32, (tk, tk), 1)
    later_keys = jnp.where(tri_c > tri_r, 1.0, 0.0).astype(BF16)
    r_ref[...] = jnp.zeros(r_ref.shape, F32)
    acc_ref[...] = jnp.zeros(acc_ref.shape, F32)

    heads = range(SB_HEADS_PER_STEP)

    def produce(z_ref, kb):
        ks = pl.multiple_of(kb * tk, tk)
        for h in heads:
            k = k_ref[0, pl.ds(ks, tk), h * HEAD_DIM:(h + 1) * HEAD_DIM]
            for c in range(sub):
                z_ref[h, :, c * tk:(c + 1) * tk] = jnp.dot(
                    k, qt_ref[0, c, h * HEAD_DIM:(h + 1) * HEAD_DIM, :], preferred_element_type=F32)

    def consume(z_ref, kb, key_offset):
        if key_offset is not None:
            before = (lax.broadcasted_iota(I32, (tk, tq), 0) + key_offset
                      < lax.broadcasted_iota(I32, (tk, tq), 1))
        zs = [z_ref[h] for h in heads]
        go = []
        for z in zs:
            neg_abs = lax.bitcast_convert_type(lax.bitcast_convert_type(z, I32) | INT_MIN, F32)
            g = jnp.maximum(z, 0.0) + jnp.log(1.0 + jnp.exp(neg_abs))
            if key_offset is not None:
                g = jnp.where(before, g, 0.0)
            go.append(g)
        later = [jnp.dot(later_keys, g.astype(BF16), preferred_element_type=F32) for g in go]
        for h in heads:
            w = jnp.exp(zs[h] - (go[h] + later[h] + r_ref[h]))
            if key_offset is not None:
                w = jnp.where(before, w, 0.0)
            acc_ref[h] += jnp.dot(vt_ref[0, kb, h * HEAD_DIM:(h + 1) * HEAD_DIM, :], w.astype(BF16),
                                  preferred_element_type=F32)
            r_ref[h] += jnp.sum(go[h], axis=0, keepdims=True)

    first = qb * sub
    bufs = (z0_ref, z1_ref)
    produce(bufs[0], first + sub - 1)
    for i, d in enumerate(reversed(range(sub))):
        if d > 0:
            produce(bufs[(i + 1) % 2], first + d - 1)
        else:
            @pl.when(qb > 0)
            def _():
                produce(bufs[(i + 1) % 2], first - 1)
        consume(bufs[i % 2], first + d, d * tk)

    def body(j, carry):
        top = first - 1 - 2 * j
        produce(z1_ref, top - 1)
        consume(z0_ref, top, None)
        produce(z0_ref, top - 2)
        consume(z1_ref, top - 1, None)
        return carry

    lax.fori_loop(0, first // 2 - 1, body, 0)

    @pl.when(qb > 0)
    def _():
        produce(z1_ref, 0)
        consume(z0_ref, 1, None)
        consume(z1_ref, 0, None)

    for h in range(SB_HEADS_PER_STEP):
        o_ref[0, :, h * HEAD_DIM:(h + 1) * HEAD_DIM] = acc_ref[h].T.astype(BF16)


def _sb(qc, kc, vct, batch, seq):
    tq = min(512, seq)
    hp = SB_HEADS_PER_STEP
    nkb = seq // KEY_BLOCK
    return pl.pallas_call(
        functools.partial(_sb_kernel, tq=tq),
        grid=(batch, C_HEADS // hp, seq // tq),
        in_specs=[
            pl.BlockSpec((1, tq // KEY_BLOCK, hp * HEAD_DIM, KEY_BLOCK), lambda b, h, i: (b, i, h, 0)),
            pl.BlockSpec((1, seq, hp * HEAD_DIM), lambda b, h, i: (b, 0, h)),
            pl.BlockSpec((1, nkb, hp * HEAD_DIM, KEY_BLOCK), lambda b, h, i: (b, 0, h, 0)),
        ],
        out_specs=pl.BlockSpec((1, tq, hp * HEAD_DIM), lambda b, h, i: (b, i, h)),
        out_shape=jax.ShapeDtypeStruct((batch, seq, C_HEADS * HEAD_DIM), BF16),
        scratch_shapes=[pltpu.VMEM((hp, KEY_BLOCK, tq), F32), pltpu.VMEM((hp, KEY_BLOCK, tq), F32),
                        pltpu.VMEM((hp, 1, tq), F32), pltpu.VMEM((hp, HEAD_DIM, tq), F32)],
        compiler_params=_params("arbitrary", "arbitrary", "arbitrary"),
        name="sb_attn",
    )(qc, kc, vct)


SEARCH_GROUP = 2


def _dsa_kernel(qit_ref, ki_ref, wt_ref, qat_ref, ka_ref, vt_ref, o_ref,
                keys_ref, half_ref, bias_ref, thr_ref, lim_ref, m_ref, l_ref, acc_ref, s0_ref, s1_ref,
                *, t, topk, seq):
    qb = pl.program_id(1)
    nk = qb + 1
    row = lax.broadcasted_iota(I32, (t, t), 0)
    col = lax.broadcasted_iota(I32, (t, t), 1)
    wt = wt_ref[0]

    def score_block(kb):
        ks = pl.multiple_of(kb * t, t)
        kix = ki_ref[0, pl.ds(ks, t), :]
        sc = jnp.zeros((t, t), F32)
        for h in range(IDX_HEADS):
            lg = jnp.dot(kix, qit_ref[0, 0, h * IDX_DIM:(h + 1) * IDX_DIM, :],
                         preferred_element_type=F32)
            sc = sc + wt[h:h + 1, :] * jnp.maximum(lg, 0.0)
        bits = lax.bitcast_convert_type(sc, I32)
        key = bits ^ ((bits >> 31) & 0x7FFFFFFF)
        visible = ((ks + row) >> CHUNK_SHIFT) <= ((qb * t + col) >> CHUNK_SHIFT)
        key = jnp.where(visible, key, INT_MIN)
        keys_ref[kb] = key
        half_ref[kb] = (key >> 16).astype(I16)

    def score_pair(j, carry):
        score_block(2 * j)
        score_block(2 * j + 1)
        return carry

    lax.fori_loop(0, nk // 2, score_pair, 0)

    @pl.when(nk % 2 == 1)
    def _():
        score_block(nk - 1)

    i16_min, i16_max = -2 ** 15, 2 ** 15 - 1
    for pad in range(SEARCH_GROUP - 1):
        half_ref[nk + pad] = jnp.full((t, t), i16_min, I16)

    def search16():
        def count16(cand):
            cand16 = cand.astype(I16)

            def block_group(j, acc):
                parts = [acc]
                for kb in range(SEARCH_GROUP):
                    hit = jnp.where(half_ref[SEARCH_GROUP * j + kb] >= cand16, jnp.ones((), I16), jnp.zeros((), I16))
                    parts += [hit[i * 16:(i + 1) * 16] for i in range(t // 16)]
                while len(parts) > 1:
                    parts = [a + b for a, b in zip(parts[0::2], parts[1::2])] + parts[len(parts) & ~1:]
                return parts[0]
            groups = (nk + SEARCH_GROUP - 1) // SEARCH_GROUP
            acc = lax.fori_loop(0, groups, block_group, jnp.zeros((16, t), I16))
            return jnp.sum(acc.astype(I32), axis=0, keepdims=True)

        c0 = count16(jnp.zeros((1, t), I32))
        v0 = jnp.where(c0 >= topk, 0, i16_min).astype(I32)
        n0 = jnp.where(c0 >= topk, c0, 0)

        def bit(b, carry):
            v, n = carry
            cand = v | lax.shift_left(jnp.int32(1), 14 - b)
            c = count16(cand)
            return jnp.where(c >= topk, cand, v), jnp.where(c >= topk, c, n)

        return lax.fori_loop(0, 15, bit, (v0, n0))

    def count(pred):
        def block(kb, acc):
            hit = jnp.where(pred(keys_ref[kb], kb * t + row), 1, 0).astype(I32)
            parts = [acc] + [hit[i * 8:(i + 1) * 8] for i in range(t // 8)]
            while len(parts) > 1:
                parts = [a + b for a, b in zip(parts[0::2], parts[1::2])] + parts[len(parts) & ~1:]
            return parts[0]
        acc = lax.fori_loop(0, nk, block, jnp.zeros((8, t), I32))
        return jnp.sum(acc, axis=0, keepdims=True)

    thr_hi, n_hi = search16()

    def low_half_block(kb, carry):
        key = keys_ref[kb]
        hi = key >> 16
        low = (key & 0xFFFF) + i16_min
        half_ref[kb] = jnp.where(hi == thr_hi, low, jnp.where(hi > thr_hi, i16_max, i16_min)).astype(I16)
        return carry

    lax.fori_loop(0, nk, low_half_block, 0)
    thr_lo, n_lo = search16()
    thr = lax.shift_left(thr_hi, 16) + (thr_lo - i16_min)
    n_ge = jnp.where(thr_lo > i16_min, n_lo, n_hi)
    tie = (thr != INT_MIN) & (n_ge > topk)
    thr_ref[...] = thr
    lim_ref[...] = jnp.where(thr == INT_MIN, 0, seq).astype(I32)

    @pl.when(jnp.max(tie.astype(I32)) > 0)
    def _():
        nbits = seq.bit_length() - 1
        need = topk - count(lambda k, i: k > thr)

        def lim_bit(b, lim):
            cand = lim | lax.shift_left(jnp.int32(1), nbits - 1 - b)
            below = count(lambda k, i: (k == thr) & (i < cand))
            return jnp.where(below < need, cand, lim)

        lim = lax.fori_loop(0, nbits, lim_bit, jnp.zeros((1, t), I32))
        lim_ref[...] = jnp.where(tie, lim + 1, lim_ref[...])

    def bias_block(kb, carry):
        key = keys_ref[kb]
        thr_q = thr_ref[...]
        at_thr = jnp.where((kb * t + row) < lim_ref[...], 0.0, NEG_INF)
        bias_ref[kb] = jnp.where(key > thr_q, 0.0, jnp.where(key == thr_q, at_thr, NEG_INF))
        return carry

    lax.fori_loop(0, nk, bias_block, 0)
    m_ref[...] = jnp.full(m_ref.shape, NEG_INF, F32)
    l_ref[...] = jnp.zeros(l_ref.shape, F32)
    acc_ref[...] = jnp.zeros(acc_ref.shape, F32)

    def produce(s_ref, kb):
        ks = pl.multiple_of(kb * t, t)
        bias = bias_ref[kb]
        for h in range(A_HEADS):
            s_ref[h] = jnp.dot(ka_ref[0, pl.ds(ks, t), h * HEAD_DIM:(h + 1) * HEAD_DIM],
                               qat_ref[0, 0, h * HEAD_DIM:(h + 1) * HEAD_DIM, :],
                               preferred_element_type=F32) + bias

    def consume(s_ref, kb):
        for h in range(A_HEADS):
            lo = h * HEAD_DIM
            s = s_ref[h]
            m_prev = m_ref[h]
            m_new = jnp.maximum(m_prev, jnp.max(s, axis=0, keepdims=True))
            m_safe = jnp.where(m_new == NEG_INF, 0.0, m_new)
            p = jnp.exp2(s - m_safe)
            alpha = jnp.exp2(m_prev - m_safe)
            l_ref[h] = alpha * l_ref[h] + jnp.sum(p, axis=0, keepdims=True)
            acc_ref[h] = alpha * acc_ref[h] + jnp.dot(vt_ref[0, kb, lo:lo + HEAD_DIM, :], p.astype(BF16),
                                                      preferred_element_type=F32)
            m_ref[h] = m_new

    produce(s0_ref, 0)

    def attend_pair(j, carry):
        produce(s1_ref, 2 * j + 1)
        consume(s0_ref, 2 * j)
        produce(s0_ref, jnp.minimum(2 * j + 2, nk - 1))
        consume(s1_ref, 2 * j + 1)
        return carry

    lax.fori_loop(0, nk // 2, attend_pair, 0)

    @pl.when(nk % 2 == 1)
    def _():
        consume(s0_ref, nk - 1)

    for h in range(A_HEADS):
        o_ref[0, :, h * HEAD_DIM:(h + 1) * HEAD_DIM] = (acc_ref[h] / l_ref[h]).T.astype(BF16)


def _dsa(qi, ki, wt, qa, ka, vat, batch, seq, topk):
    t = KEY_BLOCK
    nt = seq // t
    wa = A_HEADS * HEAD_DIM
    return pl.pallas_call(
        functools.partial(_dsa_kernel, t=t, topk=topk, seq=seq),
        grid=(batch, nt),
        in_specs=[
            pl.BlockSpec((1, 1, IDX_HEADS * IDX_DIM, t), lambda b, i: (b, i, 0, 0)),
            pl.BlockSpec((1, seq, IDX_DIM), lambda b, i: (b, 0, 0)),
            pl.BlockSpec((1, IDX_HEADS, t), lambda b, i: (b, 0, i)),
            pl.BlockSpec((1, 1, wa, t), lambda b, i: (b, i, 0, 0)),
            pl.BlockSpec((1, seq, wa), lambda b, i: (b, 0, 0)),
            pl.BlockSpec((1, nt, wa, t), lambda b, i: (b, 0, 0, 0)),
        ],
        out_specs=pl.BlockSpec((1, t, wa), lambda b, i: (b, i, 0)),
        out_shape=jax.ShapeDtypeStruct((batch, seq, wa), BF16),
        scratch_shapes=[
            pltpu.VMEM((nt, t, t), I32), pltpu.VMEM((nt + SEARCH_GROUP - 1, t, t), I16), pltpu.VMEM((nt, t, t), F32),
            pltpu.VMEM((1, t), I32), pltpu.VMEM((1, t), I32),
            pltpu.VMEM((A_HEADS, 1, t), F32), pltpu.VMEM((A_HEADS, 1, t), F32),
            pltpu.VMEM((A_HEADS, HEAD_DIM, t), F32),
            pltpu.VMEM((A_HEADS, t, t), F32), pltpu.VMEM((A_HEADS, t, t), F32),
        ],
        compiler_params=_params("arbitrary", "arbitrary"),
        name="dsa_attn",
    )(qi, ki, wt, qa, ka, vat)


def _out_kernel(x_ref, mod_ref, oa_ref, ob_ref, oc_ref, w_ref, o_ref):
    wa = A_HEADS * HEAD_DIM
    wb = B_HEADS * HEAD_DIM
    mixed = (jnp.dot(oa_ref[...], w_ref[0:wa, :], preferred_element_type=F32)
             + jnp.dot(ob_ref[...], w_ref[wa:wa + wb, :], preferred_element_type=F32)
             + jnp.dot(oc_ref[...], w_ref[wa + wb:, :], preferred_element_type=F32))
    o_ref[...] = x_ref[...] + (1.0 + mod_ref[0, 5:6, :]) * mixed


def _out_proj(x2, mod, oa, ob, oc, w, seq):
    T, D = x2.shape
    tm = min(512, seq)
    per_seq = seq // tm
    return pl.pallas_call(
        _out_kernel,
        grid=(T // tm,),
        in_specs=[
            pl.BlockSpec((tm, D), lambda i: (i, 0)),
            pl.BlockSpec((1, N_MOD, D), lambda i: (i // per_seq, 0, 0)),
            pl.BlockSpec((tm, oa.shape[1]), lambda i: (i, 0)),
            pl.BlockSpec((tm, ob.shape[1]), lambda i: (i, 0)),
            pl.BlockSpec((tm, oc.shape[1]), lambda i: (i, 0)),
            pl.BlockSpec(w.shape, lambda i: (0, 0)),
        ],
        out_specs=pl.BlockSpec((tm, D), lambda i: (i, 0)),
        out_shape=jax.ShapeDtypeStruct((T, D), F32),
        compiler_params=_params("arbitrary"),
        name="out_proj",
    )(x2, mod, oa, ob, oc, w)


def _rope_tables(seq):
    def tables(dim):
        inv = 1.0 / (ROPE_THETA ** (jnp.arange(0, dim, 2, dtype=F32) / dim))
        ang = jnp.arange(seq, dtype=F32)[:, None] * inv[None, :]
        return jnp.cos(ang), jnp.sin(ang)

    def lane_tables(cos, sin, fill):
        half = cos.shape[1]
        rest = LANES - 2 * half
        zeros_h = jnp.zeros((seq, half), F32)
        zeros_r = jnp.zeros((seq, rest), F32)
        return [jnp.concatenate([cos, cos, jnp.full((seq, rest), fill, F32)], axis=1),
                jnp.concatenate([-sin, zeros_h, zeros_r], axis=1),
                jnp.concatenate([zeros_h, sin, zeros_r], axis=1)]

    cos_p, sin_p = tables(PARTIAL_ROPE_DIM)
    cos_m, sin_m = tables(MLA_ROPE)
    return jnp.stack(lane_tables(cos_p, sin_p, 1.0) + lane_tables(cos_m, sin_m, 0.0))


def _pad_cols(a, width):
    return jnp.pad(a, ((0, 0), (0, width - a.shape[1])))


def _layer_weights(w_in, w_uq, w_ukv):
    w_in_p = jnp.concatenate([
        w_in[:, 0:2560],
        w_in[:, 3280:4816],
        _pad_cols(w_in[:, 2640:3088], Q_RANK_PAD),
        w_in[:, 3088:3216],
        _pad_cols(w_in[:, 3216:3280], LANES),
        _pad_cols(w_in[:, 2560:2640], LANES),
    ], axis=1).astype(BF16)
    wuq = w_uq.reshape(MLA_Q_RANK, B_HEADS, MLA_NOPE + MLA_ROPE)
    wuq = jnp.pad(wuq, ((0, Q_RANK_PAD - MLA_Q_RANK), (0, 0), (0, 2 * LANES - MLA_NOPE - MLA_ROPE)))
    wuq = wuq.reshape(Q_RANK_PAD, B_HEADS * 2 * LANES).astype(BF16)
    wukv = w_ukv.reshape(MLA_KV_RANK, B_HEADS, MLA_NOPE + HEAD_DIM)
    wukv = jnp.concatenate([wukv[:, :, :MLA_NOPE].reshape(MLA_KV_RANK, -1),
                            wukv[:, :, MLA_NOPE:].reshape(MLA_KV_RANK, -1)], axis=1).astype(BF16)
    return w_in_p, wuq, wukv


def kernel(x, c, w_ada, b_ada, g_ffn1, w1_gate, w1_up, w1_down, g_mix, w_in, g_qa, g_ka, g_cq, g_ckv, w_uq, w_ukv, g_q_nope, g_k_nope, g_q_rope, g_k_rope, w_out, g_ffn2, w2_gate, w2_up, w2_down):
    batch, seq, d_model = x.shape
    depth = w_ada.shape[0]
    topk = min(TOPK_MAX, seq // 4)
    tabs = _rope_tables(seq)
    mods = _ada(c, w_ada, b_ada).reshape(depth, batch, N_MOD, d_model)
    x2 = x.reshape(batch * seq, d_model)

    for l in range(depth):
        mod = mods[l]
        x2 = _ffn(x2, mod, g_ffn1[l], w1_gate[l].astype(BF16), w1_up[l].astype(BF16),
                  w1_down[l].astype(BF16), 0, seq)

        w_in_p, wuq, wukv = _layer_weights(w_in[l], w_uq[l], w_ukv[l])
        gains = [g_qa[l][None, :], g_ka[l][None, :], _pad_cols(g_cq[l][None, :], Q_RANK_PAD),
                 g_ckv[l][None, :], g_q_nope[l][None, :], g_k_nope[l][None, :],
                 _pad_cols(g_q_rope[l][None, :], LANES), _pad_cols(g_k_rope[l][None, :], LANES)]
        (qa, ka, vat, qi, ki, wt, qcat, kcat, vbt, qc, kc, vct) = _prep(
            x2, mod, g_mix[l], w_in_p, tabs, gains, wuq, wukv, batch, seq)

        out_a = _dsa(qi, ki, wt, qa, ka, vat, batch, seq, topk)
        out_b = _mla(qcat, kcat, vbt, batch, seq)
        out_c = _sb(qc, kc, vct, batch, seq)
        x2 = _out_proj(x2, mod, out_a.reshape(batch * seq, -1), out_b.reshape(batch * seq, -1),
                       out_c.reshape(batch * seq, -1), w_out[l].astype(BF16), seq)

        x2 = _ffn(x2, mod, g_ffn2[l], w2_gate[l].astype(BF16), w2_up[l].astype(BF16),
                  w2_down[l].astype(BF16), 6, seq)

    return x2.reshape(batch, seq, d_model)
```

```python
import functools

import jax
import jax.numpy as jnp
from jax import lax
from jax.experimental import pallas as pl
from jax.experimental.pallas import tpu as pltpu

F32 = jnp.float32
BF16 = jnp.bfloat16
I32 = jnp.int32
I16 = jnp.int16

HEAD_DIM = 128
CHUNK = 64
CHUNK_SHIFT = 6
ROPE_THETA = 500000.0
PARTIAL_ROPE_DIM = HEAD_DIM // 4
NORM_EPS = 1e-6
D_FF = 5632
N_MOD = 9
A_HEADS = 4
IDX_HEADS = 16
IDX_DIM = 64
TOPK_MAX = 256
B_HEADS = 8
MLA_Q_RANK = 448
MLA_KV_RANK = 128
MLA_NOPE = 128
MLA_ROPE = 64
C_HEADS = 4

LANES = 128
KEY_BLOCK = 256
Q_RANK_PAD = 512
VMEM_LIMIT = 56 * 1024 * 1024
FFN_VMEM_LIMIT = 61 * 1024 * 1024

OFF_QA, OFF_KA, OFF_VA, OFF_QI = 0, 512, 1024, 1536
OFF_QC, OFF_KC, OFF_VC = 2560, 3072, 3584
OFF_CQ, OFF_CKV, OFF_KR, OFF_KIW = 4096, 4608, 4736, 4864
N_PROJ = 4992

LOG2E = 1.4426950408889634
A_SCALE = HEAD_DIM ** -0.5 * LOG2E
B_SCALE = (MLA_NOPE + MLA_ROPE) ** -0.5 * LOG2E
C_SCALE = HEAD_DIM ** -0.5
IDX_SCALE = (IDX_DIM ** -0.5) * (IDX_HEADS ** -0.5)

NT_DIMS = (((1,), (1,)), ((), ()))
NEG_INF = float("-inf")
INT_MIN = -2 ** 31


def _params(*sem, vmem_limit=VMEM_LIMIT):
    return pltpu.CompilerParams(dimension_semantics=sem, vmem_limit_bytes=vmem_limit)


def _ada_kernel(c_ref, w_ref, b_ref, o_ref):
    c = c_ref[...]
    ca = (c * jax.nn.sigmoid(c)).astype(BF16)
    o_ref[0] = jnp.dot(ca, w_ref[0].astype(BF16), preferred_element_type=F32) + b_ref[0]


def _ada(c, w_ada, b_ada):
    L, D, N = w_ada.shape
    B = c.shape[0]
    tn = 2048
    return pl.pallas_call(
        _ada_kernel,
        grid=(L, N // tn),
        in_specs=[
            pl.BlockSpec((B, D), lambda l, j: (0, 0)),
            pl.BlockSpec((1, D, tn), lambda l, j: (l, 0, j)),
            pl.BlockSpec((1, 1, tn), lambda l, j: (l, 0, j)),
        ],
        out_specs=pl.BlockSpec((1, B, tn), lambda l, j: (l, 0, j)),
        out_shape=jax.ShapeDtypeStruct((L, B, N), F32),
        compiler_params=_params("arbitrary", "arbitrary"),
        name="ada_mod",
    )(c, w_ada, b_ada.reshape(L, 1, N))


NORM_ROWS = 16


def _norm_mod_into(h_ref, x_ref, g, shift, scale):
    gain = g * (1.0 + scale)

    def rows(r, carry):
        sl = pl.ds(pl.multiple_of(r * NORM_ROWS, NORM_ROWS), NORM_ROWS)
        x = x_ref[sl, :]
        y = x * lax.rsqrt(jnp.mean(x * x, axis=-1, keepdims=True) + NORM_EPS)
        h_ref[sl, :] = (y * gain + shift).astype(BF16)
        return carry

    lax.fori_loop(0, x_ref.shape[0] // NORM_ROWS, rows, 0, unroll=8)


def _ffn_kernel(x_ref, mod_ref, g_ref, wg_ref, wu_ref, wd_ref, o_ref, h_ref, *, row):
    j = pl.program_id(1)

    @pl.when(j == 0)
    def _():
        _norm_mod_into(h_ref, x_ref, g_ref[...], mod_ref[0, row:row + 1, :], mod_ref[0, row + 1:row + 2, :])
        o_ref[...] = jnp.zeros(o_ref.shape, F32)

    h = h_ref[...]
    g = jnp.dot(h, wg_ref[...], preferred_element_type=F32)
    u = jnp.dot(h, wu_ref[...], preferred_element_type=F32)
    a = ((g * jax.nn.sigmoid(g)) * u).astype(BF16)
    o_ref[...] += jnp.dot(a, wd_ref[...], preferred_element_type=F32)

    @pl.when(j == pl.num_programs(1) - 1)
    def _():
        gate = mod_ref[0, row + 2:row + 3, :]
        o_ref[...] = x_ref[...] + (0.5 * (1.0 + gate)) * o_ref[...]


def _ffn(x2, mod, g, wg, wu, wd, row, seq):
    T, D = x2.shape
    F = wg.shape[1]
    tm = min(1024, seq)
    tf = 512
    per_seq = seq // tm
    return pl.pallas_call(
        functools.partial(_ffn_kernel, row=row),
        grid=(T // tm, F // tf),
        in_specs=[
            pl.BlockSpec((tm, D), lambda i, j: (i, 0)),
            pl.BlockSpec((1, N_MOD, D), lambda i, j: (i // per_seq, 0, 0)),
            pl.BlockSpec((1, D), lambda i, j: (0, 0)),
            pl.BlockSpec((D, tf), lambda i, j: (0, j)),
            pl.BlockSpec((D, tf), lambda i, j: (0, j)),
            pl.BlockSpec((tf, D), lambda i, j: (j, 0)),
        ],
        out_specs=pl.BlockSpec((tm, D), lambda i, j: (i, 0)),
        out_shape=jax.ShapeDtypeStruct((T, D), F32),
        scratch_shapes=[pltpu.VMEM((tm, D), BF16)],
        compiler_params=_params("arbitrary", "arbitrary", vmem_limit=FFN_VMEM_LIMIT),
        name="ffn",
    )(x2, mod, g.reshape(1, D), wg, wu, wd)


def _rms_lanes(x, g, n):
    return x * lax.rsqrt(jnp.sum(x * x, axis=-1, keepdims=True) / n + NORM_EPS) * g


def _rope_lanes(x, cos, sin_lo, sin_hi, half):
    return (x * cos + pltpu.roll(x, LANES - half, 1) * sin_lo + pltpu.roll(x, half, 1) * sin_hi)


def _prep_kernel(x_ref, mod_ref, gmix_ref, win_ref, tab_ref,
                 gqa_ref, gka_ref, gcq_ref, gckv_ref, gqn_ref, gkn_ref, gqr_ref, gkr_ref,
                 wuq_ref, wukv_ref,
                 qa_o, ka_o, vat_o, qi_o, ki_o, wt_o, qcat_o, kcat_o, vbt_o, qc_o, kc_o, vct_o, h_ref, t_ref):
    ca, sa_lo, sa_hi = tab_ref[0], tab_ref[1], tab_ref[2]
    cm, sm_lo, sm_hi = tab_ref[3], tab_ref[4], tab_ref[5]
    half_a = PARTIAL_ROPE_DIM // 2
    half_m = MLA_ROPE // 2
    wa, wc = A_HEADS * HEAD_DIM, C_HEADS * HEAD_DIM

    _norm_mod_into(h_ref, x_ref, gmix_ref[...], mod_ref[0, 3:4, :], mod_ref[0, 4:5, :])

    def transposed(v):
        t_ref[:, 0:v.shape[1]] = v
        return t_ref[:, 0:v.shape[1]].T

    def proj(lo, width):
        return jnp.dot(h_ref[...], win_ref[:, lo:lo + width], preferred_element_type=F32)

    cq = proj(OFF_CQ, Q_RANK_PAD)
    small = proj(OFF_CKV, N_PROJ - OFF_CKV)
    qa = proj(OFF_QA, wa)
    ka = proj(OFF_KA, wa)
    cq = _rms_lanes(cq, gcq_ref[...], MLA_Q_RANK)
    ckv = _rms_lanes(small[:, 0:MLA_KV_RANK], gckv_ref[...], MLA_KV_RANK)
    qb = jnp.dot(cq.astype(BF16), wuq_ref[...], preferred_element_type=F32)
    kvb = jnp.dot(ckv.astype(BF16), wukv_ref[...], preferred_element_type=F32)
    va = proj(OFF_VA, wa)
    qi = proj(OFF_QI, IDX_HEADS * IDX_DIM)
    qc = proj(OFF_QC, wc)
    kc = proj(OFF_KC, wc)
    vc = proj(OFF_VC, wc)

    for h in range(A_HEADS):
        lo = h * HEAD_DIM
        q = _rms_lanes(qa[:, lo:lo + HEAD_DIM], gqa_ref[...], HEAD_DIM)
        t_ref[:, lo:lo + HEAD_DIM] = _rope_lanes(q, ca, sa_lo, sa_hi, half_a) * A_SCALE
    qa_o[0, 0] = t_ref[:, 0:wa].T.astype(BF16)
    for h in range(A_HEADS):
        lo = h * HEAD_DIM
        k = _rms_lanes(ka[:, lo:lo + HEAD_DIM], gka_ref[...], HEAD_DIM)
        ka_o[0, :, lo:lo + HEAD_DIM] = _rope_lanes(k, ca, sa_lo, sa_hi, half_a).astype(BF16)
    vat_o[0, 0] = transposed(va).astype(BF16)
    qi_o[0, 0] = transposed(qi).astype(BF16)

    qc_o[0, 0] = transposed(qc * C_SCALE).astype(BF16)
    kc_o[0] = kc.astype(BF16)
    vct_o[0, 0] = transposed(vc).astype(BF16)

    kr = _rms_lanes(small[:, OFF_KR - OFF_CKV:OFF_KR - OFF_CKV + LANES], gkr_ref[...], MLA_ROPE)
    kr = _rope_lanes(kr, cm, sm_lo, sm_hi, half_m).astype(BF16)
    kiw = small[:, OFF_KIW - OFF_CKV:OFF_KIW - OFF_CKV + LANES]
    ki_o[0] = kiw[:, :IDX_DIM].astype(BF16)
    wt_o[0] = transposed(kiw)[IDX_DIM:IDX_DIM + IDX_HEADS, :] * IDX_SCALE
    for h in range(B_HEADS):
        lo = h * 2 * LANES
        kn = _rms_lanes(kvb[:, h * MLA_NOPE:(h + 1) * MLA_NOPE], gkn_ref[...], MLA_NOPE)
        kcat_o[0, :, lo:lo + MLA_NOPE] = kn.astype(BF16)
        kcat_o[0, :, lo + MLA_NOPE:lo + 2 * LANES] = kr
    vbt_o[0, 0] = transposed(kvb[:, B_HEADS * MLA_NOPE:]).astype(BF16)
    for h in range(B_HEADS):
        lo = h * 2 * LANES
        qn = _rms_lanes(qb[:, lo:lo + MLA_NOPE], gqn_ref[...], MLA_NOPE)
        t_ref[:, lo:lo + MLA_NOPE] = qn * B_SCALE
        qr = _rms_lanes(qb[:, lo + MLA_NOPE:lo + 2 * LANES], gqr_ref[...], MLA_ROPE)
        t_ref[:, lo + MLA_NOPE:lo + 2 * LANES] = _rope_lanes(qr, cm, sm_lo, sm_hi, half_m) * B_SCALE
    qcat_o[0, 0] = t_ref[...].T.astype(BF16)


def _resident(shape):
    return pl.BlockSpec(shape, lambda b, i: (0,) * len(shape), pipeline_mode=pl.Buffered(1))


def _prep(x2, mod, g_mix, w_in_p, tabs, gains, wuq, wukv, batch, seq):
    tm = KEY_BLOCK
    nt = seq // tm
    d_model = x2.shape[1]
    wa, wb, wc = A_HEADS * HEAD_DIM, B_HEADS * HEAD_DIM, C_HEADS * HEAD_DIM
    wcat = B_HEADS * 2 * LANES

    def tok(width, dtype):
        return (jax.ShapeDtypeStruct((batch, seq, width), dtype),
                pl.BlockSpec((1, tm, width), lambda b, i: (b, i, 0)))

    def tok_t(width):
        return (jax.ShapeDtypeStruct((batch, nt, width, tm), BF16),
                pl.BlockSpec((1, 1, width, tm), lambda b, i: (b, i, 0, 0)))

    outs = [tok_t(wa), tok(wa, BF16), tok_t(wa), tok_t(IDX_HEADS * IDX_DIM), tok(IDX_DIM, BF16),
            (jax.ShapeDtypeStruct((batch, IDX_HEADS, seq), F32),
             pl.BlockSpec((1, IDX_HEADS, tm), lambda b, i: (b, 0, i))),
            tok_t(wcat), tok(wcat, BF16), tok_t(wb), tok_t(wc), tok(wc, BF16), tok_t(wc)]
    return pl.pallas_call(
        _prep_kernel,
        grid=(batch, nt),
        in_specs=[pl.BlockSpec((tm, d_model), lambda b, i: (b * nt + i, 0)),
                  pl.BlockSpec((1, N_MOD, d_model), lambda b, i: (b, 0, 0)),
                  _resident((1, d_model)),
                  _resident(w_in_p.shape),
                  pl.BlockSpec((6, tm, LANES), lambda b, i: (0, i, 0))]
                 + [_resident(g.shape) for g in gains]
                 + [_resident(wuq.shape), _resident(wukv.shape)],
        out_specs=[o[1] for o in outs],
        out_shape=[o[0] for o in outs],
        scratch_shapes=[pltpu.VMEM((tm, d_model), BF16), pltpu.VMEM((tm, wcat), F32)],
        compiler_params=_params("arbitrary", "arbitrary"),
        name="head_prep",
    )(x2, mod, g_mix.reshape(1, d_model), w_in_p, tabs, *gains, wuq, wukv)


MLA_HEADS_PER_STEP = 4


def _mla_kernel(qt_ref, k_ref, vt_ref, o_ref, s0_ref, s1_ref, m_ref, l_ref, acc_ref, *, t):
    qb = pl.program_id(2)
    sub = t // KEY_BLOCK
    dq = 2 * LANES
    m_ref[...] = jnp.full(m_ref.shape, NEG_INF, F32)
    l_ref[...] = jnp.zeros(l_ref.shape, F32)
    acc_ref[...] = jnp.zeros(acc_ref.shape, F32)

    def produce(s_ref, kb):
        ks = pl.multiple_of(kb * t, t)
        for h in range(MLA_HEADS_PER_STEP):
            k = k_ref[0, pl.ds(ks, t), h * dq:(h + 1) * dq]
            for c in range(sub):
                s_ref[h, :, c * KEY_BLOCK:(c + 1) * KEY_BLOCK] = jnp.dot(
                    k, qt_ref[0, c, h * dq:(h + 1) * dq, :], preferred_element_type=F32)

    def consume(s_ref, kb, diagonal):
        for h in range(MLA_HEADS_PER_STEP):
            s = s_ref[h]
            if diagonal:
                kc = lax.broadcasted_iota(I32, (t, t), 0) >> CHUNK_SHIFT
                qc = lax.broadcasted_iota(I32, (t, t), 1) >> CHUNK_SHIFT
                s = jnp.where(kc <= qc, s, NEG_INF)
            m_prev = m_ref[h]
            m_new = jnp.maximum(m_prev, jnp.max(s, axis=0, keepdims=True))
            p = jnp.exp2(s - m_new)
            alpha = jnp.exp2(m_prev - m_new)
            l_ref[h] = alpha * l_ref[h] + jnp.sum(p, axis=0, keepdims=True)
            p = p.astype(BF16)
            pv = jnp.dot(vt_ref[0, kb * sub, h * HEAD_DIM:(h + 1) * HEAD_DIM, :], p[0:KEY_BLOCK],
                         preferred_element_type=F32)
            for c in range(1, sub):
                pv += jnp.dot(vt_ref[0, kb * sub + c, h * HEAD_DIM:(h + 1) * HEAD_DIM, :],
                              p[c * KEY_BLOCK:(c + 1) * KEY_BLOCK], preferred_element_type=F32)
            acc_ref[h] = alpha * acc_ref[h] + pv
            m_ref[h] = m_new

    produce(s0_ref, 0)

    def body(j, carry):
        produce(s1_ref, 2 * j + 1)
        consume(s0_ref, 2 * j, False)
        produce(s0_ref, 2 * j + 2)
        consume(s1_ref, 2 * j + 1, False)
        return carry

    lax.fori_loop(0, qb // 2, body, 0)

    @pl.when(qb % 2 == 0)
    def _():
        consume(s0_ref, qb, True)

    @pl.when(qb % 2 == 1)
    def _():
        produce(s1_ref, qb)
        consume(s0_ref, qb - 1, False)
        consume(s1_ref, qb, True)

    for h in range(MLA_HEADS_PER_STEP):
        o_ref[0, :, h * HEAD_DIM:(h + 1) * HEAD_DIM] = (acc_ref[h] / l_ref[h]).T.astype(BF16)


def _mla(qcat, kcat, vbt, batch, seq):
    t = min(512, seq)
    hp = MLA_HEADS_PER_STEP
    nkb = seq // KEY_BLOCK
    return pl.pallas_call(
        functools.partial(_mla_kernel, t=t),
        grid=(batch, B_HEADS // hp, seq // t),
        in_specs=[
            pl.BlockSpec((1, t // KEY_BLOCK, hp * 2 * LANES, KEY_BLOCK), lambda b, h, i: (b, i, h, 0)),
            pl.BlockSpec((1, seq, hp * 2 * LANES), lambda b, h, i: (b, 0, h)),
            pl.BlockSpec((1, nkb, hp * HEAD_DIM, KEY_BLOCK), lambda b, h, i: (b, 0, h, 0)),
        ],
        out_specs=pl.BlockSpec((1, t, hp * HEAD_DIM), lambda b, h, i: (b, i, h)),
        out_shape=jax.ShapeDtypeStruct((batch, seq, B_HEADS * HEAD_DIM), BF16),
        scratch_shapes=[pltpu.VMEM((hp, t, t), F32), pltpu.VMEM((hp, t, t), F32),
                        pltpu.VMEM((hp, 1, t), F32), pltpu.VMEM((hp, 1, t), F32),
                        pltpu.VMEM((hp, HEAD_DIM, t), F32)],
        compiler_params=_params("arbitrary", "arbitrary", "arbitrary"),
        name="mla_attn",
    )(qcat, kcat, vbt)


SB_HEADS_PER_STEP = 4


def _sb_kernel(qt_ref, k_ref, vt_ref, o_ref, z0_ref, z1_ref, r_ref, acc_ref, *, tq):
    qb = pl.program_id(2)
    tk = KEY_BLOCK
    sub = tq // tk
    assert sub % 2 == 0
    tri_r = lax.broadcasted_iota(I32, (tk, tk), 0)
    tri_c = lax.broadcasted_iota(I32, (tk, tk), 1)
    later_keys = jnp.where(tri_c > tri_r, 1.0, 0.0).astype(BF16)
    r_ref[...] = jnp.zeros(r_ref.shape, F32)
    acc_ref[...] = jnp.zeros(acc_ref.shape, F32)

    heads = range(SB_HEADS_PER_STEP)

    def produce(z_ref, kb):
        ks = pl.multiple_of(kb * tk, tk)
        for h in heads:
            k = k_ref[0, pl.ds(ks, tk), h * HEAD_DIM:(h + 1) * HEAD_DIM]
            for c in range(sub):
                z_ref[h, :, c * tk:(c + 1) * tk] = jnp.dot(
                    k, qt_ref[0, c, h * HEAD_DIM:(h + 1) * HEAD_DIM, :], preferred_element_type=F32)

    def consume(z_ref, kb, key_offset):
        if key_offset is not None:
            before = (lax.broadcasted_iota(I32, (tk, tq), 0) + key_offset
                      < lax.broadcasted_iota(I32, (tk, tq), 1))
        zs = [z_ref[h] for h in heads]
        go = []
        for z in zs:
            neg_abs = lax.bitcast_convert_type(lax.bitcast_convert_type(z, I32) | INT_MIN, F32)
            g = jnp.maximum(z, 0.0) + jnp.log(1.0 + jnp.exp(neg_abs))
            if key_offset is not None:
                g = jnp.where(before, g, 0.0)
            go.append(g)
        later = [jnp.dot(later_keys, g.astype(BF16), preferred_element_type=F32) for g in go]
        for h in heads:
            w = jnp.exp(zs[h] - (go[h] + later[h] + r_ref[h]))
            if key_offset is not None:
                w = jnp.where(before, w, 0.0)
            acc_ref[h] += jnp.dot(vt_ref[0, kb, h * HEAD_DIM:(h + 1) * HEAD_DIM, :], w.astype(BF16),
                                  preferred_element_type=F32)
            r_ref[h] += jnp.sum(go[h], axis=0, keepdims=True)

    first = qb * sub
    bufs = (z0_ref, z1_ref)
    produce(bufs[0], first + sub - 1)
    for i, d in enumerate(reversed(range(sub))):
        if d > 0:
            produce(bufs[(i + 1) % 2], first + d - 1)
        else:
            @pl.when(qb > 0)
            def _():
                produce(bufs[(i + 1) % 2], first - 1)
        consume(bufs[i % 2], first + d, d * tk)

    def body(j, carry):
        top = first - 1 - 2 * j
        produce(z1_ref, top - 1)
        consume(z0_ref, top, None)
        produce(z0_ref, top - 2)
        consume(z1_ref, top - 1, None)
        return carry

    lax.fori_loop(0, first // 2 - 1, body, 0)

    @pl.when(qb > 0)
    def _():
        produce(z1_ref, 0)
        consume(z0_ref, 1, None)
        consume(z1_ref, 0, None)

    for h in range(SB_HEADS_PER_STEP):
        o_ref[0, :, h * HEAD_DIM:(h + 1) * HEAD_DIM] = acc_ref[h].T.astype(BF16)


def _sb(qc, kc, vct, batch, seq):
    tq = min(512, seq)
    hp = SB_HEADS_PER_STEP
    nkb = seq // KEY_BLOCK
    return pl.pallas_call(
        functools.partial(_sb_kernel, tq=tq),
        grid=(batch, C_HEADS // hp, seq // tq),
        in_specs=[
            pl.BlockSpec((1, tq // KEY_BLOCK, hp * HEAD_DIM, KEY_BLOCK), lambda b, h, i: (b, i, h, 0)),
            pl.BlockSpec((1, seq, hp * HEAD_DIM), lambda b, h, i: (b, 0, h)),
            pl.BlockSpec((1, nkb, hp * HEAD_DIM, KEY_BLOCK), lambda b, h, i: (b, 0, h, 0)),
        ],
        out_specs=pl.BlockSpec((1, tq, hp * HEAD_DIM), lambda b, h, i: (b, i, h)),
        out_shape=jax.ShapeDtypeStruct((batch, seq, C_HEADS * HEAD_DIM), BF16),
        scratch_shapes=[pltpu.VMEM((hp, KEY_BLOCK, tq), F32), pltpu.VMEM((hp, KEY_BLOCK, tq), F32),
                        pltpu.VMEM((hp, 1, tq), F32), pltpu.VMEM((hp, HEAD_DIM, tq), F32)],
        compiler_params=_params("arbitrary", "arbitrary", "arbitrary"),
        name="sb_attn",
    )(qc, kc, vct)


SEARCH_GROUP = 2


def _dsa_kernel(qit_ref, ki_ref, wt_ref, qat_ref, ka_ref, vt_ref, o_ref,
                keys_ref, half_ref, thr_ref, lim_ref, m_ref, l_ref, acc_ref, s0_ref, s1_ref,
                *, t, topk, seq):
    qb = pl.program_id(1)
    nk = qb + 1
    row = lax.broadcasted_iota(I32, (t, t), 0)
    col = lax.broadcasted_iota(I32, (t, t), 1)
    wt = wt_ref[0]

    def score_block(kb):
        ks = pl.multiple_of(kb * t, t)
        kix = ki_ref[0, pl.ds(ks, t), :]
        sc = jnp.zeros((t, t), F32)
        for h in range(IDX_HEADS):
            lg = jnp.dot(kix, qit_ref[0, 0, h * IDX_DIM:(h + 1) * IDX_DIM, :],
                         preferred_element_type=F32)
            sc = sc + wt[h:h + 1, :] * jnp.maximum(lg, 0.0)
        bits = lax.bitcast_convert_type(sc, I32)
        key = bits ^ ((bits >> 31) & 0x7FFFFFFF)
        visible = ((ks + row) >> CHUNK_SHIFT) <= ((qb * t + col) >> CHUNK_SHIFT)
        key = jnp.where(visible, key, INT_MIN)
        keys_ref[kb] = key
        half_ref[kb] = (key >> 16).astype(I16)

    def score_pair(j, carry):
        score_block(2 * j)
        score_block(2 * j + 1)
        return carry

    lax.fori_loop(0, nk // 2, score_pair, 0)

    @pl.when(nk % 2 == 1)
    def _():
        score_block(nk - 1)

    i16_min, i16_max = -2 ** 15, 2 ** 15 - 1
    for pad in range(SEARCH_GROUP - 1):
        half_ref[nk + pad] = jnp.full((t, t), i16_min, I16)

    def search16():
        def count16(cand):
            cand16 = cand.astype(I16)

            def block_group(j, acc):
                parts = [acc]
                for kb in range(SEARCH_GROUP):
                    hit = jnp.where(half_ref[SEARCH_GROUP * j + kb] >= cand16, jnp.ones((), I16), jnp.zeros((), I16))
                    parts += [hit[i * 16:(i + 1) * 16] for i in range(t // 16)]
                while len(parts) > 1:
                    parts = [a + b for a, b in zip(parts[0::2], parts[1::2])] + parts[len(parts) & ~1:]
                return parts[0]
            groups = (nk + SEARCH_GROUP - 1) // SEARCH_GROUP
            acc = lax.fori_loop(0, groups, block_group, jnp.zeros((16, t), I16))
            return jnp.sum(acc.astype(I32), axis=0, keepdims=True)

        c0 = count16(jnp.zeros((1, t), I32))
        v0 = jnp.where(c0 >= topk, 0, i16_min).astype(I32)
        n0 = jnp.where(c0 >= topk, c0, 0)

        def bit(b, carry):
            v, n = carry
            cand = v | lax.shift_left(jnp.int32(1), 14 - b)
            c = count16(cand)
            return jnp.where(c >= topk, cand, v), jnp.where(c >= topk, c, n)

        return lax.fori_loop(0, 15, bit, (v0, n0))

    def count(pred):
        def block(kb, acc):
            hit = jnp.where(pred(keys_ref[kb], kb * t + row), 1, 0).astype(I32)
            parts = [acc] + [hit[i * 8:(i + 1) * 8] for i in range(t // 8)]
            while len(parts) > 1:
                parts = [a + b for a, b in zip(parts[0::2], parts[1::2])] + parts[len(parts) & ~1:]
            return parts[0]
        acc = lax.fori_loop(0, nk, block, jnp.zeros((8, t), I32))
        return jnp.sum(acc, axis=0, keepdims=True)

    thr_hi, n_hi = search16()

    def low_half_block(kb, carry):
        key = keys_ref[kb]
        hi = key >> 16
        low = (key & 0xFFFF) + i16_min
        half_ref[kb] = jnp.where(hi == thr_hi, low, jnp.where(hi > thr_hi, i16_max, i16_min)).astype(I16)
        return carry

    lax.fori_loop(0, nk, low_half_block, 0)
    thr_lo, n_lo = search16()
    thr = lax.shift_left(thr_hi, 16) + (thr_lo - i16_min)
    n_ge = jnp.where(thr_lo > i16_min, n_lo, n_hi)
    tie = (thr != INT_MIN) & (n_ge > topk)
    thr_ref[...] = thr
    lim_ref[...] = jnp.where(thr == INT_MIN, 0, seq).astype(I32)

    @pl.when(jnp.max(tie.astype(I32)) > 0)
    def _():
        nbits = seq.bit_length() - 1
        need = topk - count(lambda k, i: k > thr)

        def lim_bit(b, lim):
            cand = lim | lax.shift_left(jnp.int32(1), nbits - 1 - b)
            below = count(lambda k, i: (k == thr) & (i < cand))
            return jnp.where(below < need, cand, lim)

        lim = lax.fori_loop(0, nbits, lim_bit, jnp.zeros((1, t), I32))
        lim_ref[...] = jnp.where(tie, lim + 1, lim_ref[...])

    m_ref[...] = jnp.full(m_ref.shape, NEG_INF, F32)
    l_ref[...] = jnp.zeros(l_ref.shape, F32)
    acc_ref[...] = jnp.zeros(acc_ref.shape, F32)

    def produce(s_ref, kb):
        ks = pl.multiple_of(kb * t, t)
        key = keys_ref[kb]
        thr_q = thr_ref[...]
        at_thr = jnp.where((ks + row) < lim_ref[...], 0.0, NEG_INF)
        bias = jnp.where(key > thr_q, 0.0, jnp.where(key == thr_q, at_thr, NEG_INF))
        for h in range(A_HEADS):
            s_ref[h] = jnp.dot(ka_ref[0, pl.ds(ks, t), h * HEAD_DIM:(h + 1) * HEAD_DIM],
                               qat_ref[0, 0, h * HEAD_DIM:(h + 1) * HEAD_DIM, :],
                               preferred_element_type=F32) + bias

    def consume(s_ref, kb):
        for h in range(A_HEADS):
            lo = h * HEAD_DIM
            s = s_ref[h]
            m_prev = m_ref[h]
            m_new = jnp.maximum(m_prev, jnp.max(s, axis=0, keepdims=True))
            m_safe = jnp.where(m_new == NEG_INF, 0.0, m_new)
            p = jnp.exp2(s - m_safe)
            alpha = jnp.exp2(m_prev - m_safe)
            l_ref[h] = alpha * l_ref[h] + jnp.sum(p, axis=0, keepdims=True)
            acc_ref[h] = alpha * acc_ref[h] + jnp.dot(vt_ref[0, kb, lo:lo + HEAD_DIM, :], p.astype(BF16),
                                                      preferred_element_type=F32)
            m_ref[h] = m_new

    produce(s0_ref, 0)

    def attend_pair(j, carry):
        produce(s1_ref, 2 * j + 1)
        consume(s0_ref, 2 * j)
        produce(s0_ref, jnp.minimum(2 * j + 2, nk - 1))
        consume(s1_ref, 2 * j + 1)
        return carry

    lax.fori_loop(0, nk // 2, attend_pair, 0)

    @pl.when(nk % 2 == 1)
    def _():
        consume(s0_ref, nk - 1)

    for h in range(A_HEADS):
        o_ref[0, :, h * HEAD_DIM:(h + 1) * HEAD_DIM] = (acc_ref[h] / l_ref[h]).T.astype(BF16)


def _dsa(qi, ki, wt, qa, ka, vat, batch, seq, topk):
    t = KEY_BLOCK
    nt = seq // t
    wa = A_HEADS * HEAD_DIM
    return pl.pallas_call(
        functools.partial(_dsa_kernel, t=t, topk=topk, seq=seq),
        grid=(batch, nt),
        in_specs=[
            pl.BlockSpec((1, 1, IDX_HEADS * IDX_DIM, t), lambda b, i: (b, i, 0, 0)),
            pl.BlockSpec((1, seq, IDX_DIM), lambda b, i: (b, 0, 0)),
            pl.BlockSpec((1, IDX_HEADS, t), lambda b, i: (b, 0, i)),
            pl.BlockSpec((1, 1, wa, t), lambda b, i: (b, i, 0, 0)),
            pl.BlockSpec((1, seq, wa), lambda b, i: (b, 0, 0)),
            pl.BlockSpec((1, nt, wa, t), lambda b, i: (b, 0, 0, 0)),
        ],
        out_specs=pl.BlockSpec((1, t, wa), lambda b, i: (b, i, 0)),
        out_shape=jax.ShapeDtypeStruct((batch, seq, wa), BF16),
        scratch_shapes=[
            pltpu.VMEM((nt, t, t), I32), pltpu.VMEM((nt + SEARCH_GROUP - 1, t, t), I16),
            pltpu.VMEM((1, t), I32), pltpu.VMEM((1, t), I32),
            pltpu.VMEM((A_HEADS, 1, t), F32), pltpu.VMEM((A_HEADS, 1, t), F32),
            pltpu.VMEM((A_HEADS, HEAD_DIM, t), F32),
            pltpu.VMEM((A_HEADS, t, t), F32), pltpu.VMEM((A_HEADS, t, t), F32),
        ],
        compiler_params=_params("arbitrary", "arbitrary"),
        name="dsa_attn",
    )(qi, ki, wt, qa, ka, vat)


def _out_kernel(x_ref, mod_ref, oa_ref, ob_ref, oc_ref, w_ref, o_ref):
    wa = A_HEADS * HEAD_DIM
    wb = B_HEADS * HEAD_DIM
    mixed = (jnp.dot(oa_ref[...], w_ref[0:wa, :], preferred_element_type=F32)
             + jnp.dot(ob_ref[...], w_ref[wa:wa + wb, :], preferred_element_type=F32)
             + jnp.dot(oc_ref[...], w_ref[wa + wb:, :], preferred_element_type=F32))
    o_ref[...] = x_ref[...] + (1.0 + mod_ref[0, 5:6, :]) * mixed


def _out_proj(x2, mod, oa, ob, oc, w, seq):
    T, D = x2.shape
    tm = min(512, seq)
    per_seq = seq // tm
    return pl.pallas_call(
        _out_kernel,
        grid=(T // tm,),
        in_specs=[
            pl.BlockSpec((tm, D), lambda i: (i, 0)),
            pl.BlockSpec((1, N_MOD, D), lambda i: (i // per_seq, 0, 0)),
            pl.BlockSpec((tm, oa.shape[1]), lambda i: (i, 0)),
            pl.BlockSpec((tm, ob.shape[1]), lambda i: (i, 0)),
            pl.BlockSpec((tm, oc.shape[1]), lambda i: (i, 0)),
            pl.BlockSpec(w.shape, lambda i: (0, 0)),
        ],
        out_specs=pl.BlockSpec((tm, D), lambda i: (i, 0)),
        out_shape=jax.ShapeDtypeStruct((T, D), F32),
        compiler_params=_params("arbitrary"),
        name="out_proj",
    )(x2, mod, oa, ob, oc, w)


def _rope_tables(seq):
    def tables(dim):
        inv = 1.0 / (ROPE_THETA ** (jnp.arange(0, dim, 2, dtype=F32) / dim))
        ang = jnp.arange(seq, dtype=F32)[:, None] * inv[None, :]
        return jnp.cos(ang), jnp.sin(ang)

    def lane_tables(cos, sin, fill):
        half = cos.shape[1]
        rest = LANES - 2 * half
        zeros_h = jnp.zeros((seq, half), F32)
        zeros_r = jnp.zeros((seq, rest), F32)
        return [jnp.concatenate([cos, cos, jnp.full((seq, rest), fill, F32)], axis=1),
                jnp.concatenate([-sin, zeros_h, zeros_r], axis=1),
                jnp.concatenate([zeros_h, sin, zeros_r], axis=1)]

    cos_p, sin_p = tables(PARTIAL_ROPE_DIM)
    cos_m, sin_m = tables(MLA_ROPE)
    return jnp.stack(lane_tables(cos_p, sin_p, 1.0) + lane_tables(cos_m, sin_m, 0.0))


def _pad_cols(a, width):
    return jnp.pad(a, ((0, 0), (0, width - a.shape[1])))


def _layer_weights(w_in, w_uq, w_ukv):
    w_in_p = jnp.concatenate([
        w_in[:, 0:2560],
        w_in[:, 3280:4816],
        _pad_cols(w_in[:, 2640:3088], Q_RANK_PAD),
        w_in[:, 3088:3216],
        _pad_cols(w_in[:, 3216:3280], LANES),
        _pad_cols(w_in[:, 2560:2640], LANES),
    ], axis=1).astype(BF16)
    wuq = w_uq.reshape(MLA_Q_RANK, B_HEADS, MLA_NOPE + MLA_ROPE)
    wuq = jnp.pad(wuq, ((0, Q_RANK_PAD - MLA_Q_RANK), (0, 0), (0, 2 * LANES - MLA_NOPE - MLA_ROPE)))
    wuq = wuq.reshape(Q_RANK_PAD, B_HEADS * 2 * LANES).astype(BF16)
    wukv = w_ukv.reshape(MLA_KV_RANK, B_HEADS, MLA_NOPE + HEAD_DIM)
    wukv = jnp.concatenate([wukv[:, :, :MLA_NOPE].reshape(MLA_KV_RANK, -1),
                            wukv[:, :, MLA_NOPE:].reshape(MLA_KV_RANK, -1)], axis=1).astype(BF16)
    return w_in_p, wuq, wukv


def kernel(x, c, w_ada, b_ada, g_ffn1, w1_gate, w1_up, w1_down, g_mix, w_in, g_qa, g_ka, g_cq, g_ckv, w_uq, w_ukv, g_q_nope, g_k_nope, g_q_rope, g_k_rope, w_out, g_ffn2, w2_gate, w2_up, w2_down):
    batch, seq, d_model = x.shape
    depth = w_ada.shape[0]
    topk = min(TOPK_MAX, seq // 4)
    tabs = _rope_tables(seq)
    mods = _ada(c, w_ada, b_ada).reshape(depth, batch, N_MOD, d_model)
    x2 = x.reshape(batch * seq, d_model)

    for l in range(depth):
        mod = mods[l]
        x2 = _ffn(x2, mod, g_ffn1[l], w1_gate[l].astype(BF16), w1_up[l].astype(BF16),
                  w1_down[l].astype(BF16), 0, seq)

        w_in_p, wuq, wukv = _layer_weights(w_in[l], w_uq[l], w_ukv[l])
        gains = [g_qa[l][None, :], g_ka[l][None, :], _pad_cols(g_cq[l][None, :], Q_RANK_PAD),
                 g_ckv[l][None, :], g_q_nope[l][None, :], g_k_nope[l][None, :],
                 _pad_cols(g_q_rope[l][None, :], LANES), _pad_cols(g_k_rope[l][None, :], LANES)]
        (qa, ka, vat, qi, ki, wt, qcat, kcat, vbt, qc, kc, vct) = _prep(
            x2, mod, g_mix[l], w_in_p, tabs, gains, wuq, wukv, batch, seq)

        out_a = _dsa(qi, ki, wt, qa, ka, vat, batch, seq, topk)
        out_b = _mla(qcat, kcat, vbt, batch, seq)
        out_c = _sb(qc, kc, vct, batch, seq)
        x2 = _out_proj(x2, mod, out_a.reshape(batch * seq, -1), out_b.reshape(batch * seq, -1),
                       out_c.reshape(batch * seq, -1), w_out[l].astype(BF16), seq)

        x2 = _ffn(x2, mod, g_ffn2[l], w2_gate[l].astype(BF16), w2_up[l].astype(BF16),
                  w2_down[l].astype(BF16), 6, seq)

    return x2.reshape(batch, seq, d_model)
```

```python
import functools

import jax
import jax.numpy as jnp
from jax import lax
from jax.experimental import pallas as pl
from jax.experimental.pallas import tpu as pltpu

F32 = jnp.float32
BF16 = jnp.bfloat16
I32 = jnp.int32
I16 = jnp.int16

HEAD_DIM = 128
CHUNK = 64
CHUNK_SHIFT = 6
ROPE_THETA = 500000.0
PARTIAL_ROPE_DIM = HEAD_DIM // 4
NORM_EPS = 1e-6
D_FF = 5632
N_MOD = 9
A_HEADS = 4
IDX_HEADS = 16
IDX_DIM = 64
TOPK_MAX = 256
B_HEADS = 8
MLA_Q_RANK = 448
MLA_KV_RANK = 128
MLA_NOPE = 128
MLA_ROPE = 64
C_HEADS = 4

LANES = 128
KEY_BLOCK = 256
Q_RANK_PAD = 512
VMEM_LIMIT = 56 * 1024 * 1024
FFN_VMEM_LIMIT = 61 * 1024 * 1024

OFF_QA, OFF_KA, OFF_VA, OFF_QI = 0, 512, 1024, 1536
OFF_QC, OFF_KC, OFF_VC = 2560, 3072, 3584
OFF_CQ, OFF_CKV, OFF_KR, OFF_KIW = 4096, 4608, 4736, 4864
N_PROJ = 4992

LOG2E = 1.4426950408889634
A_SCALE = HEAD_DIM ** -0.5 * LOG2E
B_SCALE = (MLA_NOPE + MLA_ROPE) ** -0.5 * LOG2E
C_SCALE = HEAD_DIM ** -0.5
IDX_SCALE = (IDX_DIM ** -0.5) * (IDX_HEADS ** -0.5)

NT_DIMS = (((1,), (1,)), ((), ()))
NEG_INF = float("-inf")
INT_MIN = -2 ** 31


def _params(*sem, vmem_limit=VMEM_LIMIT):
    return pltpu.CompilerParams(dimension_semantics=sem, vmem_limit_bytes=vmem_limit)


def _ada_kernel(c_ref, w_ref, b_ref, o_ref):
    c = c_ref[...]
    ca = (c * jax.nn.sigmoid(c)).astype(BF16)
    o_ref[0] = jnp.dot(ca, w_ref[0].astype(BF16), preferred_element_type=F32) + b_ref[0]


def _ada(c, w_ada, b_ada):
    L, D, N = w_ada.shape
    B = c.shape[0]
    tn = 2048
    return pl.pallas_call(
        _ada_kernel,
        grid=(L, N // tn),
        in_specs=[
            pl.BlockSpec((B, D), lambda l, j: (0, 0)),
            pl.BlockSpec((1, D, tn), lambda l, j: (l, 0, j)),
            pl.BlockSpec((1, 1, tn), lambda l, j: (l, 0, j)),
        ],
        out_specs=pl.BlockSpec((1, B, tn), lambda l, j: (l, 0, j)),
        out_shape=jax.ShapeDtypeStruct((L, B, N), F32),
        compiler_params=_params("arbitrary", "arbitrary"),
        name="ada_mod",
    )(c, w_ada, b_ada.reshape(L, 1, N))


NORM_ROWS = 16


def _norm_mod_into(h_ref, x_ref, g, shift, scale):
    gain = g * (1.0 + scale)

    def rows(r, carry):
        sl = pl.ds(pl.multiple_of(r * NORM_ROWS, NORM_ROWS), NORM_ROWS)
        x = x_ref[sl, :]
        y = x * lax.rsqrt(jnp.mean(x * x, axis=-1, keepdims=True) + NORM_EPS)
        h_ref[sl, :] = (y * gain + shift).astype(BF16)
        return carry

    lax.fori_loop(0, x_ref.shape[0] // NORM_ROWS, rows, 0, unroll=8)


def _ffn_kernel(x_ref, mod_ref, g_ref, wg_ref, wu_ref, wd_ref, o_ref, h_ref, *, row):
    j = pl.program_id(1)

    @pl.when(j == 0)
    def _():
        _norm_mod_into(h_ref, x_ref, g_ref[...], mod_ref[0, row:row + 1, :], mod_ref[0, row + 1:row + 2, :])
        o_ref[...] = jnp.zeros(o_ref.shape, F32)

    h = h_ref[...]
    g = jnp.dot(h, wg_ref[...], preferred_element_type=F32)
    u = jnp.dot(h, wu_ref[...], preferred_element_type=F32)
    a = ((g * jax.nn.sigmoid(g)) * u).astype(BF16)
    o_ref[...] += jnp.dot(a, wd_ref[...], preferred_element_type=F32)

    @pl.when(j == pl.num_programs(1) - 1)
    def _():
        gate = mod_ref[0, row + 2:row + 3, :]
        o_ref[...] = x_ref[...] + (0.5 * (1.0 + gate)) * o_ref[...]


def _ffn(x2, mod, g, wg, wu, wd, row, seq):
    T, D = x2.shape
    F = wg.shape[1]
    tm = min(1024, seq)
    tf = 512
    per_seq = seq // tm
    return pl.pallas_call(
        functools.partial(_ffn_kernel, row=row),
        grid=(T // tm, F // tf),
        in_specs=[
            pl.BlockSpec((tm, D), lambda i, j: (i, 0)),
            pl.BlockSpec((1, N_MOD, D), lambda i, j: (i // per_seq, 0, 0)),
            pl.BlockSpec((1, D), lambda i, j: (0, 0)),
            pl.BlockSpec((D, tf), lambda i, j: (0, j)),
            pl.BlockSpec((D, tf), lambda i, j: (0, j)),
            pl.BlockSpec((tf, D), lambda i, j: (j, 0)),
        ],
        out_specs=pl.BlockSpec((tm, D), lambda i, j: (i, 0)),
        out_shape=jax.ShapeDtypeStruct((T, D), F32),
        scratch_shapes=[pltpu.VMEM((tm, D), BF16)],
        compiler_params=_params("arbitrary", "arbitrary", vmem_limit=FFN_VMEM_LIMIT),
        name="ffn",
    )(x2, mod, g.reshape(1, D), wg, wu, wd)


def _rms_lanes(x, g, n):
    return x * lax.rsqrt(jnp.sum(x * x, axis=-1, keepdims=True) / n + NORM_EPS) * g


def _rope_lanes(x, cos, sin_lo, sin_hi, half):
    return (x * cos + pltpu.roll(x, LANES - half, 1) * sin_lo + pltpu.roll(x, half, 1) * sin_hi)


def _prep_kernel(x_ref, mod_ref, gmix_ref, win_ref, tab_ref,
                 gqa_ref, gka_ref, gcq_ref, gckv_ref, gqn_ref, gkn_ref, gqr_ref, gkr_ref,
                 wuq_ref, wukv_ref,
                 qa_o, ka_o, vat_o, qi_o, ki_o, wt_o, qcat_o, kcat_o, vbt_o, qc_o, kc_o, vct_o, h_ref, t_ref):
    ca, sa_lo, sa_hi = tab_ref[0], tab_ref[1], tab_ref[2]
    cm, sm_lo, sm_hi = tab_ref[3], tab_ref[4], tab_ref[5]
    half_a = PARTIAL_ROPE_DIM // 2
    half_m = MLA_ROPE // 2
    wa, wc = A_HEADS * HEAD_DIM, C_HEADS * HEAD_DIM

    _norm_mod_into(h_ref, x_ref, gmix_ref[...], mod_ref[0, 3:4, :], mod_ref[0, 4:5, :])

    def transposed(v):
        t_ref[:, 0:v.shape[1]] = v
        return t_ref[:, 0:v.shape[1]].T

    def proj(lo, width):
        return jnp.dot(h_ref[...], win_ref[:, lo:lo + width], preferred_element_type=F32)

    cq = proj(OFF_CQ, Q_RANK_PAD)
    small = proj(OFF_CKV, N_PROJ - OFF_CKV)
    qa = proj(OFF_QA, wa)
    ka = proj(OFF_KA, wa)
    cq = _rms_lanes(cq, gcq_ref[...], MLA_Q_RANK)
    ckv = _rms_lanes(small[:, 0:MLA_KV_RANK], gckv_ref[...], MLA_KV_RANK)
    qb = jnp.dot(cq.astype(BF16), wuq_ref[...], preferred_element_type=F32)
    kvb = jnp.dot(ckv.astype(BF16), wukv_ref[...], preferred_element_type=F32)
    va = proj(OFF_VA, wa)
    qi = proj(OFF_QI, IDX_HEADS * IDX_DIM)
    qc = proj(OFF_QC, wc)
    kc = proj(OFF_KC, wc)
    vc = proj(OFF_VC, wc)

    for h in range(A_HEADS):
        lo = h * HEAD_DIM
        q = _rms_lanes(qa[:, lo:lo + HEAD_DIM], gqa_ref[...], HEAD_DIM)
        t_ref[:, lo:lo + HEAD_DIM] = _rope_lanes(q, ca, sa_lo, sa_hi, half_a) * A_SCALE
    qa_o[0, 0] = t_ref[:, 0:wa].T.astype(BF16)
    for h in range(A_HEADS):
        lo = h * HEAD_DIM
        k = _rms_lanes(ka[:, lo:lo + HEAD_DIM], gka_ref[...], HEAD_DIM)
        ka_o[0, :, lo:lo + HEAD_DIM] = _rope_lanes(k, ca, sa_lo, sa_hi, half_a).astype(BF16)
    vat_o[0, 0] = transposed(va).astype(BF16)
    qi_o[0, 0] = transposed(qi).astype(BF16)

    qc_o[0, 0] = transposed(qc * C_SCALE).astype(BF16)
    kc_o[0] = kc.astype(BF16)
    vct_o[0, 0] = transposed(vc).astype(BF16)

    kr = _rms_lanes(small[:, OFF_KR - OFF_CKV:OFF_KR - OFF_CKV + LANES], gkr_ref[...], MLA_ROPE)
    kr = _rope_lanes(kr, cm, sm_lo, sm_hi, half_m).astype(BF16)
    kiw = small[:, OFF_KIW - OFF_CKV:OFF_KIW - OFF_CKV + LANES]
    ki_o[0] = kiw[:, :IDX_DIM].astype(BF16)
    wt_o[0] = transposed(kiw)[IDX_DIM:IDX_DIM + IDX_HEADS, :] * IDX_SCALE
    for h in range(B_HEADS):
        lo = h * 2 * LANES
        kn = _rms_lanes(kvb[:, h * MLA_NOPE:(h + 1) * MLA_NOPE], gkn_ref[...], MLA_NOPE)
        kcat_o[0, :, lo:lo + MLA_NOPE] = kn.astype(BF16)
        kcat_o[0, :, lo + MLA_NOPE:lo + 2 * LANES] = kr
    vbt_o[0, 0] = transposed(kvb[:, B_HEADS * MLA_NOPE:]).astype(BF16)
    for h in range(B_HEADS):
        lo = h * 2 * LANES
        qn = _rms_lanes(qb[:, lo:lo + MLA_NOPE], gqn_ref[...], MLA_NOPE)
        t_ref[:, lo:lo + MLA_NOPE] = qn * B_SCALE
        qr = _rms_lanes(qb[:, lo + MLA_NOPE:lo + 2 * LANES], gqr_ref[...], MLA_ROPE)
        t_ref[:, lo + MLA_NOPE:lo + 2 * LANES] = _rope_lanes(qr, cm, sm_lo, sm_hi, half_m) * B_SCALE
    qcat_o[0, 0] = t_ref[...].T.astype(BF16)


def _resident(shape):
    return pl.BlockSpec(shape, lambda b, i: (0,) * len(shape), pipeline_mode=pl.Buffered(1))


def _prep(x2, mod, g_mix, w_in_p, tabs, gains, wuq, wukv, batch, seq):
    tm = KEY_BLOCK
    nt = seq // tm
    d_model = x2.shape[1]
    wa, wb, wc = A_HEADS * HEAD_DIM, B_HEADS * HEAD_DIM, C_HEADS * HEAD_DIM
    wcat = B_HEADS * 2 * LANES

    def tok(width, dtype):
        return (jax.ShapeDtypeStruct((batch, seq, width), dtype),
                pl.BlockSpec((1, tm, width), lambda b, i: (b, i, 0)))

    def tok_t(width):
        return (jax.ShapeDtypeStruct((batch, nt, width, tm), BF16),
                pl.BlockSpec((1, 1, width, tm), lambda b, i: (b, i, 0, 0)))

    outs = [tok_t(wa), tok(wa, BF16), tok_t(wa), tok_t(IDX_HEADS * IDX_DIM), tok(IDX_DIM, BF16),
            (jax.ShapeDtypeStruct((batch, IDX_HEADS, seq), F32),
             pl.BlockSpec((1, IDX_HEADS, tm), lambda b, i: (b, 0, i))),
            tok_t(wcat), tok(wcat, BF16), tok_t(wb), tok_t(wc), tok(wc, BF16), tok_t(wc)]
    return pl.pallas_call(
        _prep_kernel,
        grid=(batch, nt),
        in_specs=[pl.BlockSpec((tm, d_model), lambda b, i: (b * nt + i, 0)),
                  pl.BlockSpec((1, N_MOD, d_model), lambda b, i: (b, 0, 0)),
                  _resident((1, d_model)),
                  _resident(w_in_p.shape),
                  pl.BlockSpec((6, tm, LANES), lambda b, i: (0, i, 0))]
                 + [_resident(g.shape) for g in gains]
                 + [_resident(wuq.shape), _resident(wukv.shape)],
        out_specs=[o[1] for o in outs],
        out_shape=[o[0] for o in outs],
        scratch_shapes=[pltpu.VMEM((tm, d_model), BF16), pltpu.VMEM((tm, wcat), F32)],
        compiler_params=_params("arbitrary", "arbitrary"),
        name="head_prep",
    )(x2, mod, g_mix.reshape(1, d_model), w_in_p, tabs, *gains, wuq, wukv)


MLA_HEADS_PER_STEP = 4


def _mla_kernel(qt_ref, k_ref, vt_ref, o_ref, s0_ref, s1_ref, m_ref, l_ref, acc_ref, *, t):
    qb = pl.program_id(2)
    sub = t // KEY_BLOCK
    dq = 2 * LANES
    m_ref[...] = jnp.full(m_ref.shape, NEG_INF, F32)
    l_ref[...] = jnp.zeros(l_ref.shape, F32)
    acc_ref[...] = jnp.zeros(acc_ref.shape, F32)

    def produce(s_ref, kb):
        ks = pl.multiple_of(kb * t, t)
        for h in range(MLA_HEADS_PER_STEP):
            k = k_ref[0, pl.ds(ks, t), h * dq:(h + 1) * dq]
            for c in range(sub):
                s_ref[h, :, c * KEY_BLOCK:(c + 1) * KEY_BLOCK] = jnp.dot(
                    k, qt_ref[0, c, h * dq:(h + 1) * dq, :], preferred_element_type=F32)

    def consume(s_ref, kb, diagonal):
        for h in range(MLA_HEADS_PER_STEP):
            s = s_ref[h]
            if diagonal:
                kc = lax.broadcasted_iota(I32, (t, t), 0) >> CHUNK_SHIFT
                qc = lax.broadcasted_iota(I32, (t, t), 1) >> CHUNK_SHIFT
                s = jnp.where(kc <= qc, s, NEG_INF)
            m_prev = m_ref[h]
            m_new = jnp.maximum(m_prev, jnp.max(s, axis=0, keepdims=True))
            p = jnp.exp2(s - m_new)
            alpha = jnp.exp2(m_prev - m_new)
            l_ref[h] = alpha * l_ref[h] + jnp.sum(p, axis=0, keepdims=True)
            p = p.astype(BF16)
            pv = jnp.dot(vt_ref[0, kb * sub, h * HEAD_DIM:(h + 1) * HEAD_DIM, :], p[0:KEY_BLOCK],
                         preferred_element_type=F32)
            for c in range(1, sub):
                pv += jnp.dot(vt_ref[0, kb * sub + c, h * HEAD_DIM:(h + 1) * HEAD_DIM, :],
                              p[c * KEY_BLOCK:(c + 1) * KEY_BLOCK], preferred_element_type=F32)
            acc_ref[h] = alpha * acc_ref[h] + pv
            m_ref[h] = m_new

    produce(s0_ref, 0)

    def body(j, carry):
        produce(s1_ref, 2 * j + 1)
        consume(s0_ref, 2 * j, False)
        produce(s0_ref, 2 * j + 2)
        consume(s1_ref, 2 * j + 1, False)
        return carry

    lax.fori_loop(0, qb // 2, body, 0)

    @pl.when(qb % 2 == 0)
    def _():
        consume(s0_ref, qb, True)

    @pl.when(qb % 2 == 1)
    def _():
        produce(s1_ref, qb)
        consume(s0_ref, qb - 1, False)
        consume(s1_ref, qb, True)

    for h in range(MLA_HEADS_PER_STEP):
        o_ref[0, :, h * HEAD_DIM:(h + 1) * HEAD_DIM] = (acc_ref[h] / l_ref[h]).T.astype(BF16)


def _mla(qcat, kcat, vbt, batch, seq):
    t = min(512, seq)
    hp = MLA_HEADS_PER_STEP
    nkb = seq // KEY_BLOCK
    return pl.pallas_call(
        functools.partial(_mla_kernel, t=t),
        grid=(batch, B_HEADS // hp, seq // t),
        in_specs=[
            pl.BlockSpec((1, t // KEY_BLOCK, hp * 2 * LANES, KEY_BLOCK), lambda b, h, i: (b, i, h, 0)),
            pl.BlockSpec((1, seq, hp * 2 * LANES), lambda b, h, i: (b, 0, h)),
            pl.BlockSpec((1, nkb, hp * HEAD_DIM, KEY_BLOCK), lambda b, h, i: (b, 0, h, 0)),
        ],
        out_specs=pl.BlockSpec((1, t, hp * HEAD_DIM), lambda b, h, i: (b, i, h)),
        out_shape=jax.ShapeDtypeStruct((batch, seq, B_HEADS * HEAD_DIM), BF16),
        scratch_shapes=[pltpu.VMEM((hp, t, t), F32), pltpu.VMEM((hp, t, t), F32),
                        pltpu.VMEM((hp, 1, t), F32), pltpu.VMEM((hp, 1, t), F32),
                        pltpu.VMEM((hp, HEAD_DIM, t), F32)],
        compiler_params=_params("arbitrary", "arbitrary", "arbitrary"),
        name="mla_attn",
    )(qcat, kcat, vbt)


SB_HEADS_PER_STEP = 4


def _sb_kernel(qt_ref, k_ref, vt_ref, o_ref, z0_ref, z1_ref, r_ref, acc_ref, *, tq):
    qb = pl.program_id(2)
    tk = KEY_BLOCK
    sub = tq // tk
    assert sub % 2 == 0
    tri_r = lax.broadcasted_iota(I32, (tk, tk), 0)
    tri_c = lax.broadcasted_iota(I32, (tk, tk), 1)
    later_keys = jnp.where(tri_c > tri_r, 1.0, 0.0).astype(BF16)
    r_ref[...] = jnp.zeros(r_ref.shape, F32)
    acc_ref[...] = jnp.zeros(acc_ref.shape, F32)

    heads = range(SB_HEADS_PER_STEP)

    def produce(z_ref, kb):
        ks = pl.multiple_of(kb * tk, tk)
        for h in heads:
            k = k_ref[0, pl.ds(ks, tk), h * HEAD_DIM:(h + 1) * HEAD_DIM]
            for c in range(sub):
                z_ref[h, :, c * tk:(c + 1) * tk] = jnp.dot(
                    k, qt_ref[0, c, h * HEAD_DIM:(h + 1) * HEAD_DIM, :], preferred_element_type=F32)

    def consume(z_ref, kb, key_offset):
        if key_offset is not None:
            before = (lax.broadcasted_iota(I32, (tk, tq), 0) + key_offset
                      < lax.broadcasted_iota(I32, (tk, tq), 1))
        zs = [z_ref[h] for h in heads]
        go = []
        for z in zs:
            neg_abs = lax.bitcast_convert_type(lax.bitcast_convert_type(z, I32) | INT_MIN, F32)
            g = jnp.maximum(z, 0.0) + jnp.log(1.0 + jnp.exp(neg_abs))
            if key_offset is not None:
                g = jnp.where(before, g, 0.0)
            go.append(g)
        later = [jnp.dot(later_keys, g.astype(BF16), preferred_element_type=F32) for g in go]
        for h in heads:
            w = jnp.exp(zs[h] - (go[h] + later[h] + r_ref[h]))
            if key_offset is not None:
                w = jnp.where(before, w, 0.0)
            acc_ref[h] += jnp.dot(vt_ref[0, kb, h * HEAD_DIM:(h + 1) * HEAD_DIM, :], w.astype(BF16),
                                  preferred_element_type=F32)
            r_ref[h] += jnp.sum(go[h], axis=0, keepdims=True)

    first = qb * sub
    bufs = (z0_ref, z1_ref)
    produce(bufs[0], first + sub - 1)
    for i, d in enumerate(reversed(range(sub))):
        if d > 0:
            produce(bufs[(i + 1) % 2], first + d - 1)
        else:
            @pl.when(qb > 0)
            def _():
                produce(bufs[(i + 1) % 2], first - 1)
        consume(bufs[i % 2], first + d, d * tk)

    def body(j, carry):
        top = first - 1 - 2 * j
        produce(z1_ref, top - 1)
        consume(z0_ref, top, None)
        produce(z0_ref, top - 2)
        consume(z1_ref, top - 1, None)
        return carry

    lax.fori_loop(0, first // 2 - 1, body, 0)

    @pl.when(qb > 0)
    def _():
        produce(z1_ref, 0)
        consume(z0_ref, 1, None)
        consume(z1_ref, 0, None)

    for h in range(SB_HEADS_PER_STEP):
        o_ref[0, :, h * HEAD_DIM:(h + 1) * HEAD_DIM] = acc_ref[h].T.astype(BF16)


def _sb(qc, kc, vct, batch, seq):
    tq = min(512, seq)
    hp = SB_HEADS_PER_STEP
    nkb = seq // KEY_BLOCK
    return pl.pallas_call(
        functools.partial(_sb_kernel, tq=tq),
        grid=(batch, C_HEADS // hp, seq // tq),
        in_specs=[
            pl.BlockSpec((1, tq // KEY_BLOCK, hp * HEAD_DIM, KEY_BLOCK), lambda b, h, i: (b, i, h, 0)),
            pl.BlockSpec((1, seq, hp * HEAD_DIM), lambda b, h, i: (b, 0, h)),
            pl.BlockSpec((1, nkb, hp * HEAD_DIM, KEY_BLOCK), lambda b, h, i: (b, 0, h, 0)),
        ],
        out_specs=pl.BlockSpec((1, tq, hp * HEAD_DIM), lambda b, h, i: (b, i, h)),
        out_shape=jax.ShapeDtypeStruct((batch, seq, C_HEADS * HEAD_DIM), BF16),
        scratch_shapes=[pltpu.VMEM((hp, KEY_BLOCK, tq), F32), pltpu.VMEM((hp, KEY_BLOCK, tq), F32),
                        pltpu.VMEM((hp, 1, tq), F32), pltpu.VMEM((hp, HEAD_DIM, tq), F32)],
        compiler_params=_params("arbitrary", "arbitrary", "arbitrary"),
        name="sb_attn",
    )(qc, kc, vct)


SEARCH_GROUP = 2


def _dsa_kernel(qit_ref, ki_ref, wt_ref, qat_ref, ka_ref, vt_ref, o_ref,
                keys_ref, half_ref, thr_ref, lim_ref, m_ref, l_ref, acc_ref, s0_ref, s1_ref,
                *, t, tq, topk, seq):
    qb = pl.program_id(1)
    sub = tq // t
    nk = (qb + 1) * sub
    row = lax.broadcasted_iota(I32, (t, tq), 0)
    col = lax.broadcasted_iota(I32, (t, tq), 1)
    wt = wt_ref[0]

    def score_block(kb):
        ks = pl.multiple_of(kb * t, t)
        kix = ki_ref[0, pl.ds(ks, t), :]
        sc = jnp.zeros((t, tq), F32)
        for h in range(IDX_HEADS):
            lg = jnp.concatenate(
                [jnp.dot(kix, qit_ref[0, c, h * IDX_DIM:(h + 1) * IDX_DIM, :], preferred_element_type=F32)
                 for c in range(sub)], axis=1)
            sc = sc + wt[h:h + 1, :] * jnp.maximum(lg, 0.0)
        bits = lax.bitcast_convert_type(sc, I32)
        key = bits ^ ((bits >> 31) & 0x7FFFFFFF)
        visible = ((ks + row) >> CHUNK_SHIFT) <= ((qb * tq + col) >> CHUNK_SHIFT)
        key = jnp.where(visible, key, INT_MIN)
        keys_ref[kb] = key
        half_ref[kb] = (key >> 16).astype(I16)

    def score_pair(j, carry):
        score_block(2 * j)
        score_block(2 * j + 1)
        return carry

    assert sub % 2 == 0
    lax.fori_loop(0, nk // 2, score_pair, 0)

    i16_min, i16_max = -2 ** 15, 2 ** 15 - 1
    for pad in range(SEARCH_GROUP - 1):
        half_ref[nk + pad] = jnp.full((t, tq), i16_min, I16)

    def search16():
        def count16(cand):
            cand16 = cand.astype(I16)

            def block_group(j, acc):
                parts = [acc]
                for kb in range(SEARCH_GROUP):
                    hit = jnp.where(half_ref[SEARCH_GROUP * j + kb] >= cand16, jnp.ones((), I16), jnp.zeros((), I16))
                    parts += [hit[i * 16:(i + 1) * 16] for i in range(t // 16)]
                while len(parts) > 1:
                    parts = [a + b for a, b in zip(parts[0::2], parts[1::2])] + parts[len(parts) & ~1:]
                return parts[0]
            groups = (nk + SEARCH_GROUP - 1) // SEARCH_GROUP
            acc = lax.fori_loop(0, groups, block_group, jnp.zeros((16, tq), I16))
            return jnp.sum(acc.astype(I32), axis=0, keepdims=True)

        c0 = count16(jnp.zeros((1, tq), I32))
        v0 = jnp.where(c0 >= topk, 0, i16_min).astype(I32)
        n0 = jnp.where(c0 >= topk, c0, 0)

        def bit(b, carry):
            v, n = carry
            cand = v | lax.shift_left(jnp.int32(1), 14 - b)
            c = count16(cand)
            return jnp.where(c >= topk, cand, v), jnp.where(c >= topk, c, n)

        return lax.fori_loop(0, 15, bit, (v0, n0))

    def count(pred):
        def block(kb, acc):
            hit = jnp.where(pred(keys_ref[kb], kb * t + row), 1, 0).astype(I32)
            parts = [acc] + [hit[i * 8:(i + 1) * 8] for i in range(t // 8)]
            while len(parts) > 1:
                parts = [a + b for a, b in zip(parts[0::2], parts[1::2])] + parts[len(parts) & ~1:]
            return parts[0]
        acc = lax.fori_loop(0, nk, block, jnp.zeros((8, tq), I32))
        return jnp.sum(acc, axis=0, keepdims=True)

    thr_hi, n_hi = search16()

    def low_half_block(kb, carry):
        key = keys_ref[kb]
        hi = key >> 16
        low = (key & 0xFFFF) + i16_min
        half_ref[kb] = jnp.where(hi == thr_hi, low, jnp.where(hi > thr_hi, i16_max, i16_min)).astype(I16)
        return carry

    lax.fori_loop(0, nk, low_half_block, 0)
    thr_lo, n_lo = search16()
    thr = lax.shift_left(thr_hi, 16) + (thr_lo - i16_min)
    n_ge = jnp.where(thr_lo > i16_min, n_lo, n_hi)
    tie = (thr != INT_MIN) & (n_ge > topk)
    thr_ref[...] = thr
    lim_ref[...] = jnp.where(thr == INT_MIN, 0, seq).astype(I32)

    @pl.when(jnp.max(tie.astype(I32)) > 0)
    def _():
        nbits = seq.bit_length() - 1
        need = topk - count(lambda k, i: k > thr)

        def lim_bit(b, lim):
            cand = lim | lax.shift_left(jnp.int32(1), nbits - 1 - b)
            below = count(lambda k, i: (k == thr) & (i < cand))
            return jnp.where(below < need, cand, lim)

        lim = lax.fori_loop(0, nbits, lim_bit, jnp.zeros((1, tq), I32))
        lim_ref[...] = jnp.where(tie, lim + 1, lim_ref[...])

    m_ref[...] = jnp.full(m_ref.shape, NEG_INF, F32)
    l_ref[...] = jnp.zeros(l_ref.shape, F32)
    acc_ref[...] = jnp.zeros(acc_ref.shape, F32)

    def produce(s_ref, kb):
        ks = pl.multiple_of(kb * t, t)
        key = keys_ref[kb]
        thr_q = thr_ref[...]
        at_thr = jnp.where((ks + row) < lim_ref[...], 0.0, NEG_INF)
        bias = jnp.where(key > thr_q, 0.0, jnp.where(key == thr_q, at_thr, NEG_INF))
        for h in range(A_HEADS):
            k = ka_ref[0, pl.ds(ks, t), h * HEAD_DIM:(h + 1) * HEAD_DIM]
            for c in range(sub):
                s_ref[h, :, c * t:(c + 1) * t] = jnp.dot(
                    k, qat_ref[0, c, h * HEAD_DIM:(h + 1) * HEAD_DIM, :],
                    preferred_element_type=F32) + bias[:, c * t:(c + 1) * t]

    def consume(s_ref, kb):
        for h in range(A_HEADS):
            lo = h * HEAD_DIM
            s = s_ref[h]
            m_prev = m_ref[h]
            m_new = jnp.maximum(m_prev, jnp.max(s, axis=0, keepdims=True))
            m_safe = jnp.where(m_new == NEG_INF, 0.0, m_new)
            p = jnp.exp2(s - m_safe)
            alpha = jnp.exp2(m_prev - m_safe)
            l_ref[h] = alpha * l_ref[h] + jnp.sum(p, axis=0, keepdims=True)
            acc_ref[h] = alpha * acc_ref[h] + jnp.dot(vt_ref[0, kb, lo:lo + HEAD_DIM, :], p.astype(BF16),
                                                      preferred_element_type=F32)
            m_ref[h] = m_new

    produce(s0_ref, 0)

    def attend_pair(j, carry):
        produce(s1_ref, 2 * j + 1)
        consume(s0_ref, 2 * j)
        produce(s0_ref, 2 * j + 2)
        consume(s1_ref, 2 * j + 1)
        return carry

    lax.fori_loop(0, nk // 2 - 1, attend_pair, 0)
    produce(s1_ref, nk - 1)
    consume(s0_ref, nk - 2)
    consume(s1_ref, nk - 1)

    for h in range(A_HEADS):
        o_ref[0, :, h * HEAD_DIM:(h + 1) * HEAD_DIM] = (acc_ref[h] / l_ref[h]).T.astype(BF16)


def _dsa(qi, ki, wt, qa, ka, vat, batch, seq, topk):
    t = KEY_BLOCK
    tq = min(512, seq)
    nt = seq // t
    wa = A_HEADS * HEAD_DIM
    return pl.pallas_call(
        functools.partial(_dsa_kernel, t=t, tq=tq, topk=topk, seq=seq),
        grid=(batch, seq // tq),
        in_specs=[
            pl.BlockSpec((1, tq // t, IDX_HEADS * IDX_DIM, t), lambda b, i: (b, i, 0, 0)),
            pl.BlockSpec((1, seq, IDX_DIM), lambda b, i: (b, 0, 0)),
            pl.BlockSpec((1, IDX_HEADS, tq), lambda b, i: (b, 0, i)),
            pl.BlockSpec((1, tq // t, wa, t), lambda b, i: (b, i, 0, 0)),
            pl.BlockSpec((1, seq, wa), lambda b, i: (b, 0, 0)),
            pl.BlockSpec((1, nt, wa, t), lambda b, i: (b, 0, 0, 0)),
        ],
        out_specs=pl.BlockSpec((1, tq, wa), lambda b, i: (b, i, 0)),
        out_shape=jax.ShapeDtypeStruct((batch, seq, wa), BF16),
        scratch_shapes=[
            pltpu.VMEM((nt, t, tq), I32), pltpu.VMEM((nt + SEARCH_GROUP - 1, t, tq), I16),
            pltpu.VMEM((1, tq), I32), pltpu.VMEM((1, tq), I32),
            pltpu.VMEM((A_HEADS, 1, tq), F32), pltpu.VMEM((A_HEADS, 1, tq), F32),
            pltpu.VMEM((A_HEADS, HEAD_DIM, tq), F32),
            pltpu.VMEM((A_HEADS, t, tq), F32), pltpu.VMEM((A_HEADS, t, tq), F32),
        ],
        compiler_params=_params("arbitrary", "arbitrary"),
        name="dsa_attn",
    )(qi, ki, wt, qa, ka, vat)


def _out_kernel(x_ref, mod_ref, oa_ref, ob_ref, oc_ref, w_ref, o_ref):
    wa = A_HEADS * HEAD_DIM
    wb = B_HEADS * HEAD_DIM
    mixed = (jnp.dot(oa_ref[...], w_ref[0:wa, :], preferred_element_type=F32)
             + jnp.dot(ob_ref[...], w_ref[wa:wa + wb, :], preferred_element_type=F32)
             + jnp.dot(oc_ref[...], w_ref[wa + wb:, :], preferred_element_type=F32))
    o_ref[...] = x_ref[...] + (1.0 + mod_ref[0, 5:6, :]) * mixed


def _out_proj(x2, mod, oa, ob, oc, w, seq):
    T, D = x2.shape
    tm = min(512, seq)
    per_seq = seq // tm
    return pl.pallas_call(
        _out_kernel,
        grid=(T // tm,),
        in_specs=[
            pl.BlockSpec((tm, D), lambda i: (i, 0)),
            pl.BlockSpec((1, N_MOD, D), lambda i: (i // per_seq, 0, 0)),
            pl.BlockSpec((tm, oa.shape[1]), lambda i: (i, 0)),
            pl.BlockSpec((tm, ob.shape[1]), lambda i: (i, 0)),
            pl.BlockSpec((tm, oc.shape[1]), lambda i: (i, 0)),
            pl.BlockSpec(w.shape, lambda i: (0, 0)),
        ],
        out_specs=pl.BlockSpec((tm, D), lambda i: (i, 0)),
        out_shape=jax.ShapeDtypeStruct((T, D), F32),
        compiler_params=_params("arbitrary"),
        name="out_proj",
    )(x2, mod, oa, ob, oc, w)


def _rope_tables(seq):
    def tables(dim):
        inv = 1.0 / (ROPE_THETA ** (jnp.arange(0, dim, 2, dtype=F32) / dim))
        ang = jnp.arange(seq, dtype=F32)[:, None] * inv[None, :]
        return jnp.cos(ang), jnp.sin(ang)

    def lane_tables(cos, sin, fill):
        half = cos.shape[1]
        rest = LANES - 2 * half
        zeros_h = jnp.zeros((seq, half), F32)
        zeros_r = jnp.zeros((seq, rest), F32)
        return [jnp.concatenate([cos, cos, jnp.full((seq, rest), fill, F32)], axis=1),
                jnp.concatenate([-sin, zeros_h, zeros_r], axis=1),
                jnp.concatenate([zeros_h, sin, zeros_r], axis=1)]

    cos_p, sin_p = tables(PARTIAL_ROPE_DIM)
    cos_m, sin_m = tables(MLA_ROPE)
    return jnp.stack(lane_tables(cos_p, sin_p, 1.0) + lane_tables(cos_m, sin_m, 0.0))


def _pad_cols(a, width):
    return jnp.pad(a, ((0, 0), (0, width - a.shape[1])))


def _layer_weights(w_in, w_uq, w_ukv):
    w_in_p = jnp.concatenate([
        w_in[:, 0:2560],
        w_in[:, 3280:4816],
        _pad_cols(w_in[:, 2640:3088], Q_RANK_PAD),
        w_in[:, 3088:3216],
        _pad_cols(w_in[:, 3216:3280], LANES),
        _pad_cols(w_in[:, 2560:2640], LANES),
    ], axis=1).astype(BF16)
    wuq = w_uq.reshape(MLA_Q_RANK, B_HEADS, MLA_NOPE + MLA_ROPE)
    wuq = jnp.pad(wuq, ((0, Q_RANK_PAD - MLA_Q_RANK), (0, 0), (0, 2 * LANES - MLA_NOPE - MLA_ROPE)))
    wuq = wuq.reshape(Q_RANK_PAD, B_HEADS * 2 * LANES).astype(BF16)
    wukv = w_ukv.reshape(MLA_KV_RANK, B_HEADS, MLA_NOPE + HEAD_DIM)
    wukv = jnp.concatenate([wukv[:, :, :MLA_NOPE].reshape(MLA_KV_RANK, -1),
                            wukv[:, :, MLA_NOPE:].reshape(MLA_KV_RANK, -1)], axis=1).astype(BF16)
    return w_in_p, wuq, wukv


def kernel(x, c, w_ada, b_ada, g_ffn1, w1_gate, w1_up, w1_down, g_mix, w_in, g_qa, g_ka, g_cq, g_ckv, w_uq, w_ukv, g_q_nope, g_k_nope, g_q_rope, g_k_rope, w_out, g_ffn2, w2_gate, w2_up, w2_down):
    batch, seq, d_model = x.shape
    depth = w_ada.shape[0]
    topk = min(TOPK_MAX, seq // 4)
    tabs = _rope_tables(seq)
    mods = _ada(c, w_ada, b_ada).reshape(depth, batch, N_MOD, d_model)
    x2 = x.reshape(batch * seq, d_model)

    for l in range(depth):
        mod = mods[l]
        x2 = _ffn(x2, mod, g_ffn1[l], w1_gate[l].astype(BF16), w1_up[l].astype(BF16),
                  w1_down[l].astype(BF16), 0, seq)

        w_in_p, wuq, wukv = _layer_weights(w_in[l], w_uq[l], w_ukv[l])
        gains = [g_qa[l][None, :], g_ka[l][None, :], _pad_cols(g_cq[l][None, :], Q_RANK_PAD),
                 g_ckv[l][None, :], g_q_nope[l][None, :], g_k_nope[l][None, :],
                 _pad_cols(g_q_rope[l][None, :], LANES), _pad_cols(g_k_rope[l][None, :], LANES)]
        (qa, ka, vat, qi, ki, wt, qcat, kcat, vbt, qc, kc, vct) = _prep(
            x2, mod, g_mix[l], w_in_p, tabs, gains, wuq, wukv, batch, seq)

        out_a = _dsa(qi, ki, wt, qa, ka, vat, batch, seq, topk)
        out_b = _mla(qcat, kcat, vbt, batch, seq)
        out_c = _sb(qc, kc, vct, batch, seq)
        x2 = _out_proj(x2, mod, out_a.reshape(batch * seq, -1), out_b.reshape(batch * seq, -1),
                       out_c.reshape(batch * seq, -1), w_out[l].astype(BF16), seq)

        x2 = _ffn(x2, mod, g_ffn2[l], w2_gate[l].astype(BF16), w2_up[l].astype(BF16),
                  w2_down[l].astype(BF16), 6, seq)

    return x2.reshape(batch, seq, d_model)
```

```python
import functools

import jax
import jax.numpy as jnp
from jax import lax
from jax.experimental import pallas as pl
from jax.experimental.pallas import tpu as pltpu

F32 = jnp.float32
BF16 = jnp.bfloat16
I32 = jnp.int32
I16 = jnp.int16

HEAD_DIM = 128
CHUNK = 64
CHUNK_SHIFT = 6
ROPE_THETA = 500000.0
PARTIAL_ROPE_DIM = HEAD_DIM // 4
NORM_EPS = 1e-6
D_FF = 5632
N_MOD = 9
A_HEADS = 4
IDX_HEADS = 16
IDX_DIM = 64
TOPK_MAX = 256
B_HEADS = 8
MLA_Q_RANK = 448
MLA_KV_RANK = 128
MLA_NOPE = 128
MLA_ROPE = 64
C_HEADS = 4

LANES = 128
KEY_BLOCK = 256
Q_RANK_PAD = 512
VMEM_LIMIT = 56 * 1024 * 1024
FFN_VMEM_LIMIT = 61 * 1024 * 1024

OFF_QA, OFF_KA, OFF_VA, OFF_QI = 0, 512, 1024, 1536
OFF_QC, OFF_KC, OFF_VC = 2560, 3072, 3584
OFF_CQ, OFF_CKV, OFF_KR, OFF_KIW = 4096, 4608, 4736, 4864
N_PROJ = 4992

LOG2E = 1.4426950408889634
A_SCALE = HEAD_DIM ** -0.5 * LOG2E
B_SCALE = (MLA_NOPE + MLA_ROPE) ** -0.5 * LOG2E
C_SCALE = HEAD_DIM ** -0.5
IDX_SCALE = (IDX_DIM ** -0.5) * (IDX_HEADS ** -0.5)

NT_DIMS = (((1,), (1,)), ((), ()))
NEG_INF = float("-inf")
INT_MIN = -2 ** 31


def _params(*sem, vmem_limit=VMEM_LIMIT):
    return pltpu.CompilerParams(dimension_semantics=sem, vmem_limit_bytes=vmem_limit)


def _ada_kernel(c_ref, w_ref, b_ref, o_ref):
    c = c_ref[...]
    ca = (c * jax.nn.sigmoid(c)).astype(BF16)
    o_ref[0] = jnp.dot(ca, w_ref[0].astype(BF16), preferred_element_type=F32) + b_ref[0]


def _ada(c, w_ada, b_ada):
    L, D, N = w_ada.shape
    B = c.shape[0]
    tn = 2048
    return pl.pallas_call(
        _ada_kernel,
        grid=(L, N // tn),
        in_specs=[
            pl.BlockSpec((B, D), lambda l, j: (0, 0)),
            pl.BlockSpec((1, D, tn), lambda l, j: (l, 0, j)),
            pl.BlockSpec((1, 1, tn), lambda l, j: (l, 0, j)),
        ],
        out_specs=pl.BlockSpec((1, B, tn), lambda l, j: (l, 0, j)),
        out_shape=jax.ShapeDtypeStruct((L, B, N), F32),
        compiler_params=_params("arbitrary", "arbitrary"),
        name="ada_mod",
    )(c, w_ada, b_ada.reshape(L, 1, N))


NORM_ROWS = 16


def _norm_mod_into(h_ref, x_ref, g, shift, scale):
    gain = g * (1.0 + scale)

    def rows(r, carry):
        sl = pl.ds(pl.multiple_of(r * NORM_ROWS, NORM_ROWS), NORM_ROWS)
        x = x_ref[sl, :]
        y = x * lax.rsqrt(jnp.mean(x * x, axis=-1, keepdims=True) + NORM_EPS)
        h_ref[sl, :] = (y * gain + shift).astype(BF16)
        return carry

    lax.fori_loop(0, x_ref.shape[0] // NORM_ROWS, rows, 0, unroll=8)


def _ffn_kernel(x_ref, mod_ref, g_ref, wg_ref, wu_ref, wd_ref, o_ref, h_ref, *, row):
    j = pl.program_id(1)

    @pl.when(j == 0)
    def _():
        _norm_mod_into(h_ref, x_ref, g_ref[...], mod_ref[0, row:row + 1, :], mod_ref[0, row + 1:row + 2, :])
        o_ref[...] = jnp.zeros(o_ref.shape, F32)

    h = h_ref[...]
    g = jnp.dot(h, wg_ref[...], preferred_element_type=F32)
    u = jnp.dot(h, wu_ref[...], preferred_element_type=F32)
    a = ((g * jax.nn.sigmoid(g)) * u).astype(BF16)
    o_ref[...] += jnp.dot(a, wd_ref[...], preferred_element_type=F32)

    @pl.when(j == pl.num_programs(1) - 1)
    def _():
        gate = mod_ref[0, row + 2:row + 3, :]
        o_ref[...] = x_ref[...] + (0.5 * (1.0 + gate)) * o_ref[...]


def _ffn(x2, mod, g, wg, wu, wd, row, seq):
    T, D = x2.shape
    F = wg.shape[1]
    tm = min(1024, seq)
    tf = 512
    per_seq = seq // tm
    return pl.pallas_call(
        functools.partial(_ffn_kernel, row=row),
        grid=(T // tm, F // tf),
        in_specs=[
            pl.BlockSpec((tm, D), lambda i, j: (i, 0)),
            pl.BlockSpec((1, N_MOD, D), lambda i, j: (i // per_seq, 0, 0)),
            pl.BlockSpec((1, D), lambda i, j: (0, 0)),
            pl.BlockSpec((D, tf), lambda i, j: (0, j)),
            pl.BlockSpec((D, tf), lambda i, j: (0, j)),
            pl.BlockSpec((tf, D), lambda i, j: (j, 0)),
        ],
        out_specs=pl.BlockSpec((tm, D), lambda i, j: (i, 0)),
        out_shape=jax.ShapeDtypeStruct((T, D), F32),
        scratch_shapes=[pltpu.VMEM((tm, D), BF16)],
        compiler_params=_params("arbitrary", "arbitrary", vmem_limit=FFN_VMEM_LIMIT),
        name="ffn",
    )(x2, mod, g.reshape(1, D), wg, wu, wd)


def _rms_lanes(x, g, n):
    return x * lax.rsqrt(jnp.sum(x * x, axis=-1, keepdims=True) / n + NORM_EPS) * g


def _rope_lanes(x, cos, sin_lo, sin_hi, half):
    return (x * cos + pltpu.roll(x, LANES - half, 1) * sin_lo + pltpu.roll(x, half, 1) * sin_hi)


def _prep_kernel(x_ref, mod_ref, gmix_ref, win_ref, tab_ref,
                 gqa_ref, gka_ref, gcq_ref, gckv_ref, gqn_ref, gkn_ref, gqr_ref, gkr_ref,
                 wuq_ref, wukv_ref,
                 qa_o, ka_o, vat_o, qi_o, ki_o, wt_o, qcat_o, kcat_o, vbt_o, qc_o, kc_o, vct_o, h_ref, t_ref):
    ca, sa_lo, sa_hi = tab_ref[0], tab_ref[1], tab_ref[2]
    cm, sm_lo, sm_hi = tab_ref[3], tab_ref[4], tab_ref[5]
    half_a = PARTIAL_ROPE_DIM // 2
    half_m = MLA_ROPE // 2
    wa, wc = A_HEADS * HEAD_DIM, C_HEADS * HEAD_DIM

    _norm_mod_into(h_ref, x_ref, gmix_ref[...], mod_ref[0, 3:4, :], mod_ref[0, 4:5, :])

    def transposed(v):
        t_ref[:, 0:v.shape[1]] = v
        return t_ref[:, 0:v.shape[1]].T

    def proj(lo, width):
        return jnp.dot(h_ref[...], win_ref[:, lo:lo + width], preferred_element_type=F32)

    cq = proj(OFF_CQ, Q_RANK_PAD)
    small = proj(OFF_CKV, N_PROJ - OFF_CKV)
    qa = proj(OFF_QA, wa)
    ka = proj(OFF_KA, wa)
    cq = _rms_lanes(cq, gcq_ref[...], MLA_Q_RANK)
    ckv = _rms_lanes(small[:, 0:MLA_KV_RANK], gckv_ref[...], MLA_KV_RANK)
    qb = jnp.dot(cq.astype(BF16), wuq_ref[...], preferred_element_type=F32)
    kvb = jnp.dot(ckv.astype(BF16), wukv_ref[...], preferred_element_type=F32)
    va = proj(OFF_VA, wa)
    qi = proj(OFF_QI, IDX_HEADS * IDX_DIM)
    qc = proj(OFF_QC, wc)
    kc = proj(OFF_KC, wc)
    vc = proj(OFF_VC, wc)

    for h in range(A_HEADS):
        lo = h * HEAD_DIM
        q = _rms_lanes(qa[:, lo:lo + HEAD_DIM], gqa_ref[...], HEAD_DIM)
        t_ref[:, lo:lo + HEAD_DIM] = _rope_lanes(q, ca, sa_lo, sa_hi, half_a) * A_SCALE
    qa_o[0, 0] = t_ref[:, 0:wa].T.astype(BF16)
    for h in range(A_HEADS):
        lo = h * HEAD_DIM
        k = _rms_lanes(ka[:, lo:lo + HEAD_DIM], gka_ref[...], HEAD_DIM)
        ka_o[0, :, lo:lo + HEAD_DIM] = _rope_lanes(k, ca, sa_lo, sa_hi, half_a).astype(BF16)
    vat_o[0, 0] = transposed(va).astype(BF16)
    qi_o[0, 0] = transposed(qi).astype(BF16)

    qc_o[0, 0] = transposed(qc * C_SCALE).astype(BF16)
    kc_o[0] = kc.astype(BF16)
    vct_o[0, 0] = transposed(vc).astype(BF16)

    kr = _rms_lanes(small[:, OFF_KR - OFF_CKV:OFF_KR - OFF_CKV + LANES], gkr_ref[...], MLA_ROPE)
    kr = _rope_lanes(kr, cm, sm_lo, sm_hi, half_m).astype(BF16)
    kiw = small[:, OFF_KIW - OFF_CKV:OFF_KIW - OFF_CKV + LANES]
    ki_o[0] = kiw[:, :IDX_DIM].astype(BF16)
    wt_o[0] = transposed(kiw)[IDX_DIM:IDX_DIM + IDX_HEADS, :] * IDX_SCALE
    for h in range(B_HEADS):
        lo = h * 2 * LANES
        kn = _rms_lanes(kvb[:, h * MLA_NOPE:(h + 1) * MLA_NOPE], gkn_ref[...], MLA_NOPE)
        kcat_o[0, :, lo:lo + MLA_NOPE] = kn.astype(BF16)
        kcat_o[0, :, lo + MLA_NOPE:lo + 2 * LANES] = kr
    vbt_o[0, 0] = transposed(kvb[:, B_HEADS * MLA_NOPE:]).astype(BF16)
    for h in range(B_HEADS):
        lo = h * 2 * LANES
        qn = _rms_lanes(qb[:, lo:lo + MLA_NOPE], gqn_ref[...], MLA_NOPE)
        t_ref[:, lo:lo + MLA_NOPE] = qn * B_SCALE
        qr = _rms_lanes(qb[:, lo + MLA_NOPE:lo + 2 * LANES], gqr_ref[...], MLA_ROPE)
        t_ref[:, lo + MLA_NOPE:lo + 2 * LANES] = _rope_lanes(qr, cm, sm_lo, sm_hi, half_m) * B_SCALE
    qcat_o[0, 0] = t_ref[...].T.astype(BF16)


def _resident(shape):
    return pl.BlockSpec(shape, lambda b, i: (0,) * len(shape), pipeline_mode=pl.Buffered(1))


def _prep(x2, mod, g_mix, w_in_p, tabs, gains, wuq, wukv, batch, seq):
    tm = KEY_BLOCK
    nt = seq // tm
    d_model = x2.shape[1]
    wa, wb, wc = A_HEADS * HEAD_DIM, B_HEADS * HEAD_DIM, C_HEADS * HEAD_DIM
    wcat = B_HEADS * 2 * LANES

    def tok(width, dtype):
        return (jax.ShapeDtypeStruct((batch, seq, width), dtype),
                pl.BlockSpec((1, tm, width), lambda b, i: (b, i, 0)))

    def tok_t(width):
        return (jax.ShapeDtypeStruct((batch, nt, width, tm), BF16),
                pl.BlockSpec((1, 1, width, tm), lambda b, i: (b, i, 0, 0)))

    outs = [tok_t(wa), tok(wa, BF16), tok_t(wa), tok_t(IDX_HEADS * IDX_DIM), tok(IDX_DIM, BF16),
            (jax.ShapeDtypeStruct((batch, IDX_HEADS, seq), F32),
             pl.BlockSpec((1, IDX_HEADS, tm), lambda b, i: (b, 0, i))),
            tok_t(wcat), tok(wcat, BF16), tok_t(wb), tok_t(wc), tok(wc, BF16), tok_t(wc)]
    return pl.pallas_call(
        _prep_kernel,
        grid=(batch, nt),
        in_specs=[pl.BlockSpec((tm, d_model), lambda b, i: (b * nt + i, 0)),
                  pl.BlockSpec((1, N_MOD, d_model), lambda b, i: (b, 0, 0)),
                  _resident((1, d_model)),
                  _resident(w_in_p.shape),
                  pl.BlockSpec((6, tm, LANES), lambda b, i: (0, i, 0))]
                 + [_resident(g.shape) for g in gains]
                 + [_resident(wuq.shape), _resident(wukv.shape)],
        out_specs=[o[1] for o in outs],
        out_shape=[o[0] for o in outs],
        scratch_shapes=[pltpu.VMEM((tm, d_model), BF16), pltpu.VMEM((tm, wcat), F32)],
        compiler_params=_params("arbitrary", "arbitrary"),
        name="head_prep",
    )(x2, mod, g_mix.reshape(1, d_model), w_in_p, tabs, *gains, wuq, wukv)


MLA_HEADS_PER_STEP = 4


def _mla_kernel(qt_ref, k_ref, vt_ref, o_ref, s0_ref, s1_ref, m_ref, l_ref, acc_ref, *, t):
    qb = pl.program_id(2)
    sub = t // KEY_BLOCK
    dq = 2 * LANES
    m_ref[...] = jnp.full(m_ref.shape, NEG_INF, F32)
    l_ref[...] = jnp.zeros(l_ref.shape, F32)
    acc_ref[...] = jnp.zeros(acc_ref.shape, F32)

    def produce(s_ref, kb):
        ks = pl.multiple_of(kb * t, t)
        for h in range(MLA_HEADS_PER_STEP):
            k = k_ref[0, pl.ds(ks, t), h * dq:(h + 1) * dq]
            for c in range(sub):
                s_ref[h, :, c * KEY_BLOCK:(c + 1) * KEY_BLOCK] = jnp.dot(
                    k, qt_ref[0, c, h * dq:(h + 1) * dq, :], preferred_element_type=F32)

    def consume(s_ref, kb, diagonal):
        for h in range(MLA_HEADS_PER_STEP):
            s = s_ref[h]
            if diagonal:
                kc = lax.broadcasted_iota(I32, (t, t), 0) >> CHUNK_SHIFT
                qc = lax.broadcasted_iota(I32, (t, t), 1) >> CHUNK_SHIFT
                s = jnp.where(kc <= qc, s, NEG_INF)
            m_prev = m_ref[h]
            m_new = jnp.maximum(m_prev, jnp.max(s, axis=0, keepdims=True))
            p = jnp.exp2(s - m_new)
            alpha = jnp.exp2(m_prev - m_new)
            l_ref[h] = alpha * l_ref[h] + jnp.sum(p, axis=0, keepdims=True)
            p = p.astype(BF16)
            pv = jnp.dot(vt_ref[0, kb * sub, h * HEAD_DIM:(h + 1) * HEAD_DIM, :], p[0:KEY_BLOCK],
                         preferred_element_type=F32)
            for c in range(1, sub):
                pv += jnp.dot(vt_ref[0, kb * sub + c, h * HEAD_DIM:(h + 1) * HEAD_DIM, :],
                              p[c * KEY_BLOCK:(c + 1) * KEY_BLOCK], preferred_element_type=F32)
            acc_ref[h] = alpha * acc_ref[h] + pv
            m_ref[h] = m_new

    produce(s0_ref, 0)

    def body(j, carry):
        produce(s1_ref, 2 * j + 1)
        consume(s0_ref, 2 * j, False)
        produce(s0_ref, 2 * j + 2)
        consume(s1_ref, 2 * j + 1, False)
        return carry

    lax.fori_loop(0, qb // 2, body, 0)

    @pl.when(qb % 2 == 0)
    def _():
        consume(s0_ref, qb, True)

    @pl.when(qb % 2 == 1)
    def _():
        produce(s1_ref, qb)
        consume(s0_ref, qb - 1, False)
        consume(s1_ref, qb, True)

    for h in range(MLA_HEADS_PER_STEP):
        o_ref[0, :, h * HEAD_DIM:(h + 1) * HEAD_DIM] = (acc_ref[h] / l_ref[h]).T.astype(BF16)


def _mla(qcat, kcat, vbt, batch, seq):
    t = min(512, seq)
    hp = MLA_HEADS_PER_STEP
    nkb = seq // KEY_BLOCK
    return pl.pallas_call(
        functools.partial(_mla_kernel, t=t),
        grid=(batch, B_HEADS // hp, seq // t),
        in_specs=[
            pl.BlockSpec((1, t // KEY_BLOCK, hp * 2 * LANES, KEY_BLOCK), lambda b, h, i: (b, i, h, 0)),
            pl.BlockSpec((1, seq, hp * 2 * LANES), lambda b, h, i: (b, 0, h)),
            pl.BlockSpec((1, nkb, hp * HEAD_DIM, KEY_BLOCK), lambda b, h, i: (b, 0, h, 0)),
        ],
        out_specs=pl.BlockSpec((1, t, hp * HEAD_DIM), lambda b, h, i: (b, i, h)),
        out_shape=jax.ShapeDtypeStruct((batch, seq, B_HEADS * HEAD_DIM), BF16),
        scratch_shapes=[pltpu.VMEM((hp, t, t), F32), pltpu.VMEM((hp, t, t), F32),
                        pltpu.VMEM((hp, 1, t), F32), pltpu.VMEM((hp, 1, t), F32),
                        pltpu.VMEM((hp, HEAD_DIM, t), F32)],
        compiler_params=_params("arbitrary", "arbitrary", "arbitrary"),
        name="mla_attn",
    )(qcat, kcat, vbt)


SB_HEADS_PER_STEP = 4


def _sb_kernel(qt_ref, k_ref, vt_ref, o_ref, z0_ref, z1_ref, r_ref, acc_ref, *, tq):
    qb = pl.program_id(2)
    tk = KEY_BLOCK
    sub = tq // tk
    assert sub % 2 == 0
    tri_r = lax.broadcasted_iota(I32, (tk, tk), 0)
    tri_c = lax.broadcasted_iota(I32, (tk, tk), 1)
    later_keys = jnp.where(tri_c > tri_r, 1.0, 0.0).astype(BF16)
    r_ref[...] = jnp.zeros(r_ref.shape, F32)
    acc_ref[...] = jnp.zeros(acc_ref.shape, F32)

    heads = range(SB_HEADS_PER_STEP)

    def produce(z_ref, kb):
        ks = pl.multiple_of(kb * tk, tk)
        for h in heads:
            k = k_ref[0, pl.ds(ks, tk), h * HEAD_DIM:(h + 1) * HEAD_DIM]
            for c in range(sub):
                z_ref[h, :, c * tk:(c + 1) * tk] = jnp.dot(
                    k, qt_ref[0, c, h * HEAD_DIM:(h + 1) * HEAD_DIM, :], preferred_element_type=F32)

    def consume(z_ref, kb, key_offset):
        if key_offset is not None:
            before = (lax.broadcasted_iota(I32, (tk, tq), 0) + key_offset
                      < lax.broadcasted_iota(I32, (tk, tq), 1))
        zs = [z_ref[h] for h in heads]
        go = []
        for z in zs:
            neg_abs = lax.bitcast_convert_type(lax.bitcast_convert_type(z, I32) | INT_MIN, F32)
            g = jnp.maximum(z, 0.0) + jnp.log(1.0 + jnp.exp(neg_abs))
            if key_offset is not None:
                g = jnp.where(before, g, 0.0)
            go.append(g)
        later = [jnp.dot(later_keys, g.astype(BF16), preferred_element_type=F32) for g in go]
        for h in heads:
            w = jnp.exp(zs[h] - (go[h] + later[h] + r_ref[h]))
            if key_offset is not None:
                w = jnp.where(before, w, 0.0)
            acc_ref[h] += jnp.dot(vt_ref[0, kb, h * HEAD_DIM:(h + 1) * HEAD_DIM, :], w.astype(BF16),
                                  preferred_element_type=F32)
            r_ref[h] += jnp.sum(go[h], axis=0, keepdims=True)

    first = qb * sub
    bufs = (z0_ref, z1_ref)
    produce(bufs[0], first + sub - 1)
    for i, d in enumerate(reversed(range(sub))):
        if d > 0:
            produce(bufs[(i + 1) % 2], first + d - 1)
        else:
            @pl.when(qb > 0)
            def _():
                produce(bufs[(i + 1) % 2], first - 1)
        consume(bufs[i % 2], first + d, d * tk)

    def body(j, carry):
        top = first - 1 - 2 * j
        produce(z1_ref, top - 1)
        consume(z0_ref, top, None)
        produce(z0_ref, top - 2)
        consume(z1_ref, top - 1, None)
        return carry

    lax.fori_loop(0, first // 2 - 1, body, 0)

    @pl.when(qb > 0)
    def _():
        produce(z1_ref, 0)
        consume(z0_ref, 1, None)
        consume(z1_ref, 0, None)

    for h in range(SB_HEADS_PER_STEP):
        o_ref[0, :, h * HEAD_DIM:(h + 1) * HEAD_DIM] = acc_ref[h].T.astype(BF16)


def _sb(qc, kc, vct, batch, seq):
    tq = min(512, seq)
    hp = SB_HEADS_PER_STEP
    nkb = seq // KEY_BLOCK
    return pl.pallas_call(
        functools.partial(_sb_kernel, tq=tq),
        grid=(batch, C_HEADS // hp, seq // tq),
        in_specs=[
            pl.BlockSpec((1, tq // KEY_BLOCK, hp * HEAD_DIM, KEY_BLOCK), lambda b, h, i: (b, i, h, 0)),
            pl.BlockSpec((1, seq, hp * HEAD_DIM), lambda b, h, i: (b, 0, h)),
            pl.BlockSpec((1, nkb, hp * HEAD_DIM, KEY_BLOCK), lambda b, h, i: (b, 0, h, 0)),
        ],
        out_specs=pl.BlockSpec((1, tq, hp * HEAD_DIM), lambda b, h, i: (b, i, h)),
        out_shape=jax.ShapeDtypeStruct((batch, seq, C_HEADS * HEAD_DIM), BF16),
        scratch_shapes=[pltpu.VMEM((hp, KEY_BLOCK, tq), F32), pltpu.VMEM((hp, KEY_BLOCK, tq), F32),
                        pltpu.VMEM((hp, 1, tq), F32), pltpu.VMEM((hp, HEAD_DIM, tq), F32)],
        compiler_params=_params("arbitrary", "arbitrary", "arbitrary"),
        name="sb_attn",
    )(qc, kc, vct)


SEARCH_GROUP = 2


def _dsa_kernel(qit_ref, ki_ref, wt_ref, qat_ref, ka_ref, vt_ref, o_ref,
                keys_ref, half_ref, thr_ref, lim_ref, m_ref, l_ref, acc_ref, s0_ref, s1_ref,
                *, t, tq, topk, seq):
    qb = pl.program_id(1)
    sub = tq // t
    nk = (qb + 1) * sub
    row = lax.broadcasted_iota(I32, (t, tq), 0)
    col = lax.broadcasted_iota(I32, (t, tq), 1)
    wt = wt_ref[0]

    def score_block(kb):
        ks = pl.multiple_of(kb * t, t)
        kix = ki_ref[0, pl.ds(ks, t), :]
        sc = jnp.zeros((t, tq), F32)
        for h in range(IDX_HEADS):
            lg = jnp.concatenate(
                [jnp.dot(kix, qit_ref[0, c, h * IDX_DIM:(h + 1) * IDX_DIM, :], preferred_element_type=F32)
                 for c in range(sub)], axis=1)
            sc = sc + wt[h:h + 1, :] * jnp.maximum(lg, 0.0)
        bits = lax.bitcast_convert_type(sc, I32)
        key = bits ^ ((bits >> 31) & 0x7FFFFFFF)
        visible = ((ks + row) >> CHUNK_SHIFT) <= ((qb * tq + col) >> CHUNK_SHIFT)
        key = jnp.where(visible, key, INT_MIN)
        keys_ref[kb] = key
        half_ref[kb] = (key >> 16).astype(I16)

    def score_pair(j, carry):
        score_block(2 * j)
        score_block(2 * j + 1)
        return carry

    assert sub % 2 == 0
    lax.fori_loop(0, nk // 2, score_pair, 0)

    i16_min, i16_max = -2 ** 15, 2 ** 15 - 1
    for pad in range(SEARCH_GROUP - 1):
        half_ref[nk + pad] = jnp.full((t, tq), i16_min, I16)

    def search16():
        def count16(cand):
            cand16 = cand.astype(I16)

            def block_group(j, accs):
                out = []
                for c, acc in enumerate(accs):
                    parts = [acc]
                    for kb in range(SEARCH_GROUP):
                        hit = jnp.where(half_ref[SEARCH_GROUP * j + kb, :, c * t:(c + 1) * t] >= cand16[:, c * t:(c + 1) * t],
                                        jnp.ones((), I16), jnp.zeros((), I16))
                        parts += [hit[i * 16:(i + 1) * 16] for i in range(t // 16)]
                    while len(parts) > 1:
                        parts = [a + b for a, b in zip(parts[0::2], parts[1::2])] + parts[len(parts) & ~1:]
                    out.append(parts[0])
                return tuple(out)
            groups = (nk + SEARCH_GROUP - 1) // SEARCH_GROUP
            accs = lax.fori_loop(0, groups, block_group,
                                 tuple(jnp.zeros((16, t), I16) for _ in range(sub)))
            return jnp.concatenate([jnp.sum(a.astype(I32), axis=0, keepdims=True) for a in accs], axis=1)

        c0 = count16(jnp.zeros((1, tq), I32))
        v0 = jnp.where(c0 >= topk, 0, i16_min).astype(I32)
        n0 = jnp.where(c0 >= topk, c0, 0)

        def bit(b, carry):
            v, n = carry
            cand = v | lax.shift_left(jnp.int32(1), 14 - b)
            c = count16(cand)
            return jnp.where(c >= topk, cand, v), jnp.where(c >= topk, c, n)

        return lax.fori_loop(0, 15, bit, (v0, n0))

    def count(pred):
        def block(kb, acc):
            hit = jnp.where(pred(keys_ref[kb], kb * t + row), 1, 0).astype(I32)
            parts = [acc] + [hit[i * 8:(i + 1) * 8] for i in range(t // 8)]
            while len(parts) > 1:
                parts = [a + b for a, b in zip(parts[0::2], parts[1::2])] + parts[len(parts) & ~1:]
            return parts[0]
        acc = lax.fori_loop(0, nk, block, jnp.zeros((8, tq), I32))
        return jnp.sum(acc, axis=0, keepdims=True)

    thr_hi, n_hi = search16()

    def low_half_block(kb, carry):
        key = keys_ref[kb]
        hi = key >> 16
        low = (key & 0xFFFF) + i16_min
        half_ref[kb] = jnp.where(hi == thr_hi, low, jnp.where(hi > thr_hi, i16_max, i16_min)).astype(I16)
        return carry

    lax.fori_loop(0, nk, low_half_block, 0)
    thr_lo, n_lo = search16()
    thr = lax.shift_left(thr_hi, 16) + (thr_lo - i16_min)
    n_ge = jnp.where(thr_lo > i16_min, n_lo, n_hi)
    tie = (thr != INT_MIN) & (n_ge > topk)
    thr_ref[...] = thr
    lim_ref[...] = jnp.where(thr == INT_MIN, 0, seq).astype(I32)

    @pl.when(jnp.max(tie.astype(I32)) > 0)
    def _():
        nbits = seq.bit_length() - 1
        need = topk - count(lambda k, i: k > thr)

        def lim_bit(b, lim):
            cand = lim | lax.shift_left(jnp.int32(1), nbits - 1 - b)
            below = count(lambda k, i: (k == thr) & (i < cand))
            return jnp.where(below < need, cand, lim)

        lim = lax.fori_loop(0, nbits, lim_bit, jnp.zeros((1, tq), I32))
        lim_ref[...] = jnp.where(tie, lim + 1, lim_ref[...])

    m_ref[...] = jnp.full(m_ref.shape, NEG_INF, F32)
    l_ref[...] = jnp.zeros(l_ref.shape, F32)
    acc_ref[...] = jnp.zeros(acc_ref.shape, F32)

    def produce(s_ref, kb):
        ks = pl.multiple_of(kb * t, t)
        key = keys_ref[kb]
        thr_q = thr_ref[...]
        at_thr = jnp.where((ks + row) < lim_ref[...], 0.0, NEG_INF)
        bias = jnp.where(key > thr_q, 0.0, jnp.where(key == thr_q, at_thr, NEG_INF))
        for h in range(A_HEADS):
            k = ka_ref[0, pl.ds(ks, t), h * HEAD_DIM:(h + 1) * HEAD_DIM]
            for c in range(sub):
                s_ref[h, :, c * t:(c + 1) * t] = jnp.dot(
                    k, qat_ref[0, c, h * HEAD_DIM:(h + 1) * HEAD_DIM, :],
                    preferred_element_type=F32) + bias[:, c * t:(c + 1) * t]

    def consume(s_ref, kb):
        for h in range(A_HEADS):
            lo = h * HEAD_DIM
            s = s_ref[h]
            m_prev = m_ref[h]
            m_new = jnp.maximum(m_prev, jnp.max(s, axis=0, keepdims=True))
            m_safe = jnp.where(m_new == NEG_INF, 0.0, m_new)
            p = jnp.exp2(s - m_safe)
            alpha = jnp.exp2(m_prev - m_safe)
            l_ref[h] = alpha * l_ref[h] + jnp.sum(p, axis=0, keepdims=True)
            acc_ref[h] = alpha * acc_ref[h] + jnp.dot(vt_ref[0, kb, lo:lo + HEAD_DIM, :], p.astype(BF16),
                                                      preferred_element_type=F32)
            m_ref[h] = m_new

    produce(s0_ref, 0)

    def attend_pair(j, carry):
        produce(s1_ref, 2 * j + 1)
        consume(s0_ref, 2 * j)
        produce(s0_ref, 2 * j + 2)
        consume(s1_ref, 2 * j + 1)
        return carry

    lax.fori_loop(0, nk // 2 - 1, attend_pair, 0)
    produce(s1_ref, nk - 1)
    consume(s0_ref, nk - 2)
    consume(s1_ref, nk - 1)

    for h in range(A_HEADS):
        o_ref[0, :, h * HEAD_DIM:(h + 1) * HEAD_DIM] = (acc_ref[h] / l_ref[h]).T.astype(BF16)


def _dsa(qi, ki, wt, qa, ka, vat, batch, seq, topk):
    t = KEY_BLOCK
    tq = min(512, seq)
    nt = seq // t
    wa = A_HEADS * HEAD_DIM
    return pl.pallas_call(
        functools.partial(_dsa_kernel, t=t, tq=tq, topk=topk, seq=seq),
        grid=(batch, seq // tq),
        in_specs=[
            pl.BlockSpec((1, tq // t, IDX_HEADS * IDX_DIM, t), lambda b, i: (b, i, 0, 0)),
            pl.BlockSpec((1, seq, IDX_DIM), lambda b, i: (b, 0, 0)),
            pl.BlockSpec((1, IDX_HEADS, tq), lambda b, i: (b, 0, i)),
            pl.BlockSpec((1, tq // t, wa, t), lambda b, i: (b, i, 0, 0)),
            pl.BlockSpec((1, seq, wa), lambda b, i: (b, 0, 0)),
            pl.BlockSpec((1, nt, wa, t), lambda b, i: (b, 0, 0, 0)),
        ],
        out_specs=pl.BlockSpec((1, tq, wa), lambda b, i: (b, i, 0)),
        out_shape=jax.ShapeDtypeStruct((batch, seq, wa), BF16),
        scratch_shapes=[
            pltpu.VMEM((nt, t, tq), I32), pltpu.VMEM((nt + SEARCH_GROUP - 1, t, tq), I16),
            pltpu.VMEM((1, tq), I32), pltpu.VMEM((1, tq), I32),
            pltpu.VMEM((A_HEADS, 1, tq), F32), pltpu.VMEM((A_HEADS, 1, tq), F32),
            pltpu.VMEM((A_HEADS, HEAD_DIM, tq), F32),
            pltpu.VMEM((A_HEADS, t, tq), F32), pltpu.VMEM((A_HEADS, t, tq), F32),
        ],
        compiler_params=_params("arbitrary", "arbitrary"),
        name="dsa_attn",
    )(qi, ki, wt, qa, ka, vat)


def _out_kernel(x_ref, mod_ref, oa_ref, ob_ref, oc_ref, w_ref, o_ref):
    wa = A_HEADS * HEAD_DIM
    wb = B_HEADS * HEAD_DIM
    mixed = (jnp.dot(oa_ref[...], w_ref[0:wa, :], preferred_element_type=F32)
             + jnp.dot(ob_ref[...], w_ref[wa:wa + wb, :], preferred_element_type=F32)
             + jnp.dot(oc_ref[...], w_ref[wa + wb:, :], preferred_element_type=F32))
    o_ref[...] = x_ref[...] + (1.0 + mod_ref[0, 5:6, :]) * mixed


def _out_proj(x2, mod, oa, ob, oc, w, seq):
    T, D = x2.shape
    tm = min(512, seq)
    per_seq = seq // tm
    return pl.pallas_call(
        _out_kernel,
        grid=(T // tm,),
        in_specs=[
            pl.BlockSpec((tm, D), lambda i: (i, 0)),
            pl.BlockSpec((1, N_MOD, D), lambda i: (i // per_seq, 0, 0)),
            pl.BlockSpec((tm, oa.shape[1]), lambda i: (i, 0)),
            pl.BlockSpec((tm, ob.shape[1]), lambda i: (i, 0)),
            pl.BlockSpec((tm, oc.shape[1]), lambda i: (i, 0)),
            pl.BlockSpec(w.shape, lambda i: (0, 0)),
        ],
        out_specs=pl.BlockSpec((tm, D), lambda i: (i, 0)),
        out_shape=jax.ShapeDtypeStruct((T, D), F32),
        compiler_params=_params("arbitrary"),
        name="out_proj",
    )(x2, mod, oa, ob, oc, w)


def _rope_tables(seq):
    def tables(dim):
        inv = 1.0 / (ROPE_THETA ** (jnp.arange(0, dim, 2, dtype=F32) / dim))
        ang = jnp.arange(seq, dtype=F32)[:, None] * inv[None, :]
        return jnp.cos(ang), jnp.sin(ang)

    def lane_tables(cos, sin, fill):
        half = cos.shape[1]
        rest = LANES - 2 * half
        zeros_h = jnp.zeros((seq, half), F32)
        zeros_r = jnp.zeros((seq, rest), F32)
        return [jnp.concatenate([cos, cos, jnp.full((seq, rest), fill, F32)], axis=1),
                jnp.concatenate([-sin, zeros_h, zeros_r], axis=1),
                jnp.concatenate([zeros_h, sin, zeros_r], axis=1)]

    cos_p, sin_p = tables(PARTIAL_ROPE_DIM)
    cos_m, sin_m = tables(MLA_ROPE)
    return jnp.stack(lane_tables(cos_p, sin_p, 1.0) + lane_tables(cos_m, sin_m, 0.0))


def _pad_cols(a, width):
    return jnp.pad(a, ((0, 0), (0, width - a.shape[1])))


def _layer_weights(w_in, w_uq, w_ukv):
    w_in_p = jnp.concatenate([
        w_in[:, 0:2560],
        w_in[:, 3280:4816],
        _pad_cols(w_in[:, 2640:3088], Q_RANK_PAD),
        w_in[:, 3088:3216],
        _pad_cols(w_in[:, 3216:3280], LANES),
        _pad_cols(w_in[:, 2560:2640], LANES),
    ], axis=1).astype(BF16)
    wuq = w_uq.reshape(MLA_Q_RANK, B_HEADS, MLA_NOPE + MLA_ROPE)
    wuq = jnp.pad(wuq, ((0, Q_RANK_PAD - MLA_Q_RANK), (0, 0), (0, 2 * LANES - MLA_NOPE - MLA_ROPE)))
    wuq = wuq.reshape(Q_RANK_PAD, B_HEADS * 2 * LANES).astype(BF16)
    wukv = w_ukv.reshape(MLA_KV_RANK, B_HEADS, MLA_NOPE + HEAD_DIM)
    wukv = jnp.concatenate([wukv[:, :, :MLA_NOPE].reshape(MLA_KV_RANK, -1),
                            wukv[:, :, MLA_NOPE:].reshape(MLA_KV_RANK, -1)], axis=1).astype(BF16)
    return w_in_p, wuq, wukv


def kernel(x, c, w_ada, b_ada, g_ffn1, w1_gate, w1_up, w1_down, g_mix, w_in, g_qa, g_ka, g_cq, g_ckv, w_uq, w_ukv, g_q_nope, g_k_nope, g_q_rope, g_k_rope, w_out, g_ffn2, w2_gate, w2_up, w2_down):
    batch, seq, d_model = x.shape
    depth = w_ada.shape[0]
    topk = min(TOPK_MAX, seq // 4)
    tabs = _rope_tables(seq)
    mods = _ada(c, w_ada, b_ada).reshape(depth, batch, N_MOD, d_model)
    x2 = x.reshape(batch * seq, d_model)

    for l in range(depth):
        mod = mods[l]
        x2 = _ffn(x2, mod, g_ffn1[l], w1_gate[l].astype(BF16), w1_up[l].astype(BF16),
                  w1_down[l].astype(BF16), 0, seq)

        w_in_p, wuq, wukv = _layer_weights(w_in[l], w_uq[l], w_ukv[l])
        gains = [g_qa[l][None, :], g_ka[l][None, :], _pad_cols(g_cq[l][None, :], Q_RANK_PAD),
                 g_ckv[l][None, :], g_q_nope[l][None, :], g_k_nope[l][None, :],
                 _pad_cols(g_q_rope[l][None, :], LANES), _pad_cols(g_k_rope[l][None, :], LANES)]
        (qa, ka, vat, qi, ki, wt, qcat, kcat, vbt, qc, kc, vct) = _prep(
            x2, mod, g_mix[l], w_in_p, tabs, gains, wuq, wukv, batch, seq)

        out_a = _dsa(qi, ki, wt, qa, ka, vat, batch, seq, topk)
        out_b = _mla(qcat, kcat, vbt, batch, seq)
        out_c = _sb(qc, kc, vct, batch, seq)
        x2 = _out_proj(x2, mod, out_a.reshape(batch * seq, -1), out_b.reshape(batch * seq, -1),
                       out_c.reshape(batch * seq, -1), w_out[l].astype(BF16), seq)

        x2 = _ffn(x2, mod, g_ffn2[l], w2_gate[l].astype(BF16), w2_up[l].astype(BF16),
                  w2_down[l].astype(BF16), 6, seq)

    return x2.reshape(batch, seq, d_model)
```

```python
import functools

import jax
import jax.numpy as jnp
from jax import lax
from jax.experimental import pallas as pl
from jax.experimental.pallas import tpu as pltpu

F32 = jnp.float32
BF16 = jnp.bfloat16
I32 = jnp.int32
I16 = jnp.int16

HEAD_DIM = 128
CHUNK = 64
CHUNK_SHIFT = 6
ROPE_THETA = 500000.0
PARTIAL_ROPE_DIM = HEAD_DIM // 4
NORM_EPS = 1e-6
D_FF = 5632
N_MOD = 9
A_HEADS = 4
IDX_HEADS = 16
IDX_DIM = 64
TOPK_MAX = 256
B_HEADS = 8
MLA_Q_RANK = 448
MLA_KV_RANK = 128
MLA_NOPE = 128
MLA_ROPE = 64
C_HEADS = 4

LANES = 128
KEY_BLOCK = 256
Q_RANK_PAD = 512
VMEM_LIMIT = 56 * 1024 * 1024
FFN_VMEM_LIMIT = 61 * 1024 * 1024

OFF_QA, OFF_KA, OFF_VA, OFF_QI = 0, 512, 1024, 1536
OFF_QC, OFF_KC, OFF_VC = 2560, 3072, 3584
OFF_CQ, OFF_CKV, OFF_KR, OFF_KIW = 4096, 4608, 4736, 4864
N_PROJ = 4992

LOG2E = 1.4426950408889634
A_SCALE = HEAD_DIM ** -0.5 * LOG2E
B_SCALE = (MLA_NOPE + MLA_ROPE) ** -0.5 * LOG2E
C_SCALE = HEAD_DIM ** -0.5
IDX_SCALE = (IDX_DIM ** -0.5) * (IDX_HEADS ** -0.5)

NT_DIMS = (((1,), (1,)), ((), ()))
NEG_INF = float("-inf")
INT_MIN = -2 ** 31


def _params(*sem, vmem_limit=VMEM_LIMIT):
    return pltpu.CompilerParams(dimension_semantics=sem, vmem_limit_bytes=vmem_limit)


def _ada_kernel(c_ref, w_ref, b_ref, o_ref):
    c = c_ref[...]
    ca = (c * jax.nn.sigmoid(c)).astype(BF16)
    o_ref[0] = jnp.dot(ca, w_ref[0].astype(BF16), preferred_element_type=F32) + b_ref[0]


def _ada(c, w_ada, b_ada):
    L, D, N = w_ada.shape
    B = c.shape[0]
    tn = 2048
    return pl.pallas_call(
        _ada_kernel,
        grid=(L, N // tn),
        in_specs=[
            pl.BlockSpec((B, D), lambda l, j: (0, 0)),
            pl.BlockSpec((1, D, tn), lambda l, j: (l, 0, j)),
            pl.BlockSpec((1, 1, tn), lambda l, j: (l, 0, j)),
        ],
        out_specs=pl.BlockSpec((1, B, tn), lambda l, j: (l, 0, j)),
        out_shape=jax.ShapeDtypeStruct((L, B, N), F32),
        compiler_params=_params("arbitrary", "arbitrary"),
        name="ada_mod",
    )(c, w_ada, b_ada.reshape(L, 1, N))


NORM_ROWS = 16


def _norm_mod_into(h_ref, x_ref, g, shift, scale):
    gain = g * (1.0 + scale)

    def rows(r, carry):
        sl = pl.ds(pl.multiple_of(r * NORM_ROWS, NORM_ROWS), NORM_ROWS)
        x = x_ref[sl, :]
        y = x * lax.rsqrt(jnp.mean(x * x, axis=-1, keepdims=True) + NORM_EPS)
        h_ref[sl, :] = (y * gain + shift).astype(BF16)
        return carry

    lax.fori_loop(0, x_ref.shape[0] // NORM_ROWS, rows, 0, unroll=8)


def _ffn_kernel(x_ref, mod_ref, g_ref, wg_ref, wu_ref, wd_ref, o_ref, h_ref, *, row):
    j = pl.program_id(1)

    @pl.when(j == 0)
    def _():
        _norm_mod_into(h_ref, x_ref, g_ref[...], mod_ref[0, row:row + 1, :], mod_ref[0, row + 1:row + 2, :])
        o_ref[...] = jnp.zeros(o_ref.shape, F32)

    h = h_ref[...]
    g = jnp.dot(h, wg_ref[...], preferred_element_type=F32)
    u = jnp.dot(h, wu_ref[...], preferred_element_type=F32)
    a = ((g * jax.nn.sigmoid(g)) * u).astype(BF16)
    o_ref[...] += jnp.dot(a, wd_ref[...], preferred_element_type=F32)

    @pl.when(j == pl.num_programs(1) - 1)
    def _():
        gate = mod_ref[0, row + 2:row + 3, :]
        o_ref[...] = x_ref[...] + (0.5 * (1.0 + gate)) * o_ref[...]


def _ffn(x2, mod, g, wg, wu, wd, row, seq):
    T, D = x2.shape
    F = wg.shape[1]
    tm = min(1024, seq)
    tf = 512
    per_seq = seq // tm
    return pl.pallas_call(
        functools.partial(_ffn_kernel, row=row),
        grid=(T // tm, F // tf),
        in_specs=[
            pl.BlockSpec((tm, D), lambda i, j: (i, 0)),
            pl.BlockSpec((1, N_MOD, D), lambda i, j: (i // per_seq, 0, 0)),
            pl.BlockSpec((1, D), lambda i, j: (0, 0)),
            pl.BlockSpec((D, tf), lambda i, j: (0, j)),
            pl.BlockSpec((D, tf), lambda i, j: (0, j)),
            pl.BlockSpec((tf, D), lambda i, j: (j, 0)),
        ],
        out_specs=pl.BlockSpec((tm, D), lambda i, j: (i, 0)),
        out_shape=jax.ShapeDtypeStruct((T, D), F32),
        scratch_shapes=[pltpu.VMEM((tm, D), BF16)],
        compiler_params=_params("arbitrary", "arbitrary", vmem_limit=FFN_VMEM_LIMIT),
        name="ffn",
    )(x2, mod, g.reshape(1, D), wg, wu, wd)


def _rms_lanes(x, g, n):
    return x * lax.rsqrt(jnp.sum(x * x, axis=-1, keepdims=True) / n + NORM_EPS) * g


def _rope_lanes(x, cos, sin_lo, sin_hi, half):
    return (x * cos + pltpu.roll(x, LANES - half, 1) * sin_lo + pltpu.roll(x, half, 1) * sin_hi)


def _prep_kernel(x_ref, mod_ref, gmix_ref, win_ref, tab_ref,
                 gqa_ref, gka_ref, gcq_ref, gckv_ref, gqn_ref, gkn_ref, gqr_ref, gkr_ref,
                 wuq_ref, wukv_ref,
                 qa_o, ka_o, vat_o, qi_o, ki_o, wt_o, qcat_o, kcat_o, vbt_o, qc_o, kc_o, vct_o, h_ref, t_ref):
    ca, sa_lo, sa_hi = tab_ref[0], tab_ref[1], tab_ref[2]
    cm, sm_lo, sm_hi = tab_ref[3], tab_ref[4], tab_ref[5]
    half_a = PARTIAL_ROPE_DIM // 2
    half_m = MLA_ROPE // 2
    wa, wc = A_HEADS * HEAD_DIM, C_HEADS * HEAD_DIM

    _norm_mod_into(h_ref, x_ref, gmix_ref[...], mod_ref[0, 3:4, :], mod_ref[0, 4:5, :])

    def transposed(v):
        t_ref[:, 0:v.shape[1]] = v
        return t_ref[:, 0:v.shape[1]].T

    def proj(lo, width):
        return jnp.dot(h_ref[...], win_ref[:, lo:lo + width], preferred_element_type=F32)

    cq = proj(OFF_CQ, Q_RANK_PAD)
    small = proj(OFF_CKV, N_PROJ - OFF_CKV)
    qa = proj(OFF_QA, wa)
    ka = proj(OFF_KA, wa)
    cq = _rms_lanes(cq, gcq_ref[...], MLA_Q_RANK)
    ckv = _rms_lanes(small[:, 0:MLA_KV_RANK], gckv_ref[...], MLA_KV_RANK)
    qb = jnp.dot(cq.astype(BF16), wuq_ref[...], preferred_element_type=F32)
    kvb = jnp.dot(ckv.astype(BF16), wukv_ref[...], preferred_element_type=F32)
    va = proj(OFF_VA, wa)
    qi = proj(OFF_QI, IDX_HEADS * IDX_DIM)
    qc = proj(OFF_QC, wc)
    kc = proj(OFF_KC, wc)
    vc = proj(OFF_VC, wc)

    for h in range(A_HEADS):
        lo = h * HEAD_DIM
        q = _rms_lanes(qa[:, lo:lo + HEAD_DIM], gqa_ref[...], HEAD_DIM)
        t_ref[:, lo:lo + HEAD_DIM] = _rope_lanes(q, ca, sa_lo, sa_hi, half_a) * A_SCALE
    qa_o[0, 0] = t_ref[:, 0:wa].T.astype(BF16)
    for h in range(A_HEADS):
        lo = h * HEAD_DIM
        k = _rms_lanes(ka[:, lo:lo + HEAD_DIM], gka_ref[...], HEAD_DIM)
        ka_o[0, :, lo:lo + HEAD_DIM] = _rope_lanes(k, ca, sa_lo, sa_hi, half_a).astype(BF16)
    vat_o[0, 0] = transposed(va).astype(BF16)
    qi_o[0, 0] = transposed(qi).astype(BF16)

    qc_o[0, 0] = transposed(qc * C_SCALE).astype(BF16)
    kc_o[0] = kc.astype(BF16)
    vct_o[0, 0] = transposed(vc).astype(BF16)

    kr = _rms_lanes(small[:, OFF_KR - OFF_CKV:OFF_KR - OFF_CKV + LANES], gkr_ref[...], MLA_ROPE)
    kr = _rope_lanes(kr, cm, sm_lo, sm_hi, half_m).astype(BF16)
    kiw = small[:, OFF_KIW - OFF_CKV:OFF_KIW - OFF_CKV + LANES]
    ki_o[0] = kiw[:, :IDX_DIM].astype(BF16)
    wt_o[0] = transposed(kiw)[IDX_DIM:IDX_DIM + IDX_HEADS, :] * IDX_SCALE
    for h in range(B_HEADS):
        lo = h * 2 * LANES
        kn = _rms_lanes(kvb[:, h * MLA_NOPE:(h + 1) * MLA_NOPE], gkn_ref[...], MLA_NOPE)
        kcat_o[0, :, lo:lo + MLA_NOPE] = kn.astype(BF16)
        kcat_o[0, :, lo + MLA_NOPE:lo + 2 * LANES] = kr
    vbt_o[0, 0] = transposed(kvb[:, B_HEADS * MLA_NOPE:]).astype(BF16)
    for h in range(B_HEADS):
        lo = h * 2 * LANES
        qn = _rms_lanes(qb[:, lo:lo + MLA_NOPE], gqn_ref[...], MLA_NOPE)
        t_ref[:, lo:lo + MLA_NOPE] = qn * B_SCALE
        qr = _rms_lanes(qb[:, lo + MLA_NOPE:lo + 2 * LANES], gqr_ref[...], MLA_ROPE)
        t_ref[:, lo + MLA_NOPE:lo + 2 * LANES] = _rope_lanes(qr, cm, sm_lo, sm_hi, half_m) * B_SCALE
    qcat_o[0, 0] = t_ref[...].T.astype(BF16)


def _resident(shape):
    return pl.BlockSpec(shape, lambda b, i: (0,) * len(shape), pipeline_mode=pl.Buffered(1))


def _prep(x2, mod, g_mix, w_in_p, tabs, gains, wuq, wukv, batch, seq):
    tm = KEY_BLOCK
    nt = seq // tm
    d_model = x2.shape[1]
    wa, wb, wc = A_HEADS * HEAD_DIM, B_HEADS * HEAD_DIM, C_HEADS * HEAD_DIM
    wcat = B_HEADS * 2 * LANES

    def tok(width, dtype):
        return (jax.ShapeDtypeStruct((batch, seq, width), dtype),
                pl.BlockSpec((1, tm, width), lambda b, i: (b, i, 0)))

    def tok_t(width):
        return (jax.ShapeDtypeStruct((batch, nt, width, tm), BF16),
                pl.BlockSpec((1, 1, width, tm), lambda b, i: (b, i, 0, 0)))

    outs = [tok_t(wa), tok(wa, BF16), tok_t(wa), tok_t(IDX_HEADS * IDX_DIM), tok(IDX_DIM, BF16),
            (jax.ShapeDtypeStruct((batch, IDX_HEADS, seq), F32),
             pl.BlockSpec((1, IDX_HEADS, tm), lambda b, i: (b, 0, i))),
            tok_t(wcat), tok(wcat, BF16), tok_t(wb), tok_t(wc), tok(wc, BF16), tok_t(wc)]
    return pl.pallas_call(
        _prep_kernel,
        grid=(batch, nt),
        in_specs=[pl.BlockSpec((tm, d_model), lambda b, i: (b * nt + i, 0)),
                  pl.BlockSpec((1, N_MOD, d_model), lambda b, i: (b, 0, 0)),
                  _resident((1, d_model)),
                  _resident(w_in_p.shape),
                  pl.BlockSpec((6, tm, LANES), lambda b, i: (0, i, 0))]
                 + [_resident(g.shape) for g in gains]
                 + [_resident(wuq.shape), _resident(wukv.shape)],
        out_specs=[o[1] for o in outs],
        out_shape=[o[0] for o in outs],
        scratch_shapes=[pltpu.VMEM((tm, d_model), BF16), pltpu.VMEM((tm, wcat), F32)],
        compiler_params=_params("arbitrary", "arbitrary"),
        name="head_prep",
    )(x2, mod, g_mix.reshape(1, d_model), w_in_p, tabs, *gains, wuq, wukv)


MLA_HEADS_PER_STEP = 4


def _mla_kernel(qt_ref, k_ref, vt_ref, o_ref, s0_ref, s1_ref, m_ref, l_ref, acc_ref, *, t):
    qb = pl.program_id(2)
    sub = t // KEY_BLOCK
    dq = 2 * LANES
    m_ref[...] = jnp.full(m_ref.shape, NEG_INF, F32)
    l_ref[...] = jnp.zeros(l_ref.shape, F32)
    acc_ref[...] = jnp.zeros(acc_ref.shape, F32)

    def produce(s_ref, kb):
        ks = pl.multiple_of(kb * t, t)
        for h in range(MLA_HEADS_PER_STEP):
            k = k_ref[0, pl.ds(ks, t), h * dq:(h + 1) * dq]
            for c in range(sub):
                s_ref[h, :, c * KEY_BLOCK:(c + 1) * KEY_BLOCK] = jnp.dot(
                    k, qt_ref[0, c, h * dq:(h + 1) * dq, :], preferred_element_type=F32)

    def consume(s_ref, kb, diagonal):
        for h in range(MLA_HEADS_PER_STEP):
            s = s_ref[h]
            if diagonal:
                kc = lax.broadcasted_iota(I32, (t, t), 0) >> CHUNK_SHIFT
                qc = lax.broadcasted_iota(I32, (t, t), 1) >> CHUNK_SHIFT
                s = jnp.where(kc <= qc, s, NEG_INF)
            m_prev = m_ref[h]
            m_new = jnp.maximum(m_prev, jnp.max(s, axis=0, keepdims=True))
            p = jnp.exp2(s - m_new)
            alpha = jnp.exp2(m_prev - m_new)
            l_ref[h] = alpha * l_ref[h] + jnp.sum(p, axis=0, keepdims=True)
            p = p.astype(BF16)
            pv = jnp.dot(vt_ref[0, kb * sub, h * HEAD_DIM:(h + 1) * HEAD_DIM, :], p[0:KEY_BLOCK],
                         preferred_element_type=F32)
            for c in range(1, sub):
                pv += jnp.dot(vt_ref[0, kb * sub + c, h * HEAD_DIM:(h + 1) * HEAD_DIM, :],
                              p[c * KEY_BLOCK:(c + 1) * KEY_BLOCK], preferred_element_type=F32)
            acc_ref[h] = alpha * acc_ref[h] + pv
            m_ref[h] = m_new

    produce(s0_ref, 0)

    def body(j, carry):
        produce(s1_ref, 2 * j + 1)
        consume(s0_ref, 2 * j, False)
        produce(s0_ref, 2 * j + 2)
        consume(s1_ref, 2 * j + 1, False)
        return carry

    lax.fori_loop(0, qb // 2, body, 0)

    @pl.when(qb % 2 == 0)
    def _():
        consume(s0_ref, qb, True)

    @pl.when(qb % 2 == 1)
    def _():
        produce(s1_ref, qb)
        consume(s0_ref, qb - 1, False)
        consume(s1_ref, qb, True)

    for h in range(MLA_HEADS_PER_STEP):
        o_ref[0, :, h * HEAD_DIM:(h + 1) * HEAD_DIM] = (acc_ref[h] / l_ref[h]).T.astype(BF16)


def _mla(qcat, kcat, vbt, batch, seq):
    t = min(512, seq)
    hp = MLA_HEADS_PER_STEP
    nkb = seq // KEY_BLOCK
    return pl.pallas_call(
        functools.partial(_mla_kernel, t=t),
        grid=(batch, B_HEADS // hp, seq // t),
        in_specs=[
            pl.BlockSpec((1, t // KEY_BLOCK, hp * 2 * LANES, KEY_BLOCK), lambda b, h, i: (b, i, h, 0)),
            pl.BlockSpec((1, seq, hp * 2 * LANES), lambda b, h, i: (b, 0, h)),
            pl.BlockSpec((1, nkb, hp * HEAD_DIM, KEY_BLOCK), lambda b, h, i: (b, 0, h, 0)),
        ],
        out_specs=pl.BlockSpec((1, t, hp * HEAD_DIM), lambda b, h, i: (b, i, h)),
        out_shape=jax.ShapeDtypeStruct((batch, seq, B_HEADS * HEAD_DIM), BF16),
        scratch_shapes=[pltpu.VMEM((hp, t, t), F32), pltpu.VMEM((hp, t, t), F32),
                        pltpu.VMEM((hp, 1, t), F32), pltpu.VMEM((hp, 1, t), F32),
                        pltpu.VMEM((hp, HEAD_DIM, t), F32)],
        compiler_params=_params("arbitrary", "arbitrary", "arbitrary"),
        name="mla_attn",
    )(qcat, kcat, vbt)


SB_HEADS_PER_STEP = 4


def _sb_kernel(qt_ref, k_ref, vt_ref, o_ref, z0_ref, z1_ref, r_ref, acc_ref, *, tq):
    qb = pl.program_id(2)
    tk = KEY_BLOCK
    sub = tq // tk
    assert sub % 2 == 0
    tri_r = lax.broadcasted_iota(I32, (tk, tk), 0)
    tri_c = lax.broadcasted_iota(I32, (tk, tk), 1)
    later_keys = jnp.where(tri_c > tri_r, 1.0, 0.0).astype(BF16)
    r_ref[...] = jnp.zeros(r_ref.shape, F32)
    acc_ref[...] = jnp.zeros(acc_ref.shape, F32)

    heads = range(SB_HEADS_PER_STEP)

    def produce(z_ref, kb):
        ks = pl.multiple_of(kb * tk, tk)
        for h in heads:
            k = k_ref[0, pl.ds(ks, tk), h * HEAD_DIM:(h + 1) * HEAD_DIM]
            for c in range(sub):
                z_ref[h, :, c * tk:(c + 1) * tk] = jnp.dot(
                    k, qt_ref[0, c, h * HEAD_DIM:(h + 1) * HEAD_DIM, :], preferred_element_type=F32)

    def consume(z_ref, kb, key_offset):
        if key_offset is not None:
            before = (lax.broadcasted_iota(I32, (tk, tq), 0) + key_offset
                      < lax.broadcasted_iota(I32, (tk, tq), 1))
        zs = [z_ref[h] for h in heads]
        go = []
        for z in zs:
            neg_abs = lax.bitcast_convert_type(lax.bitcast_convert_type(z, I32) | INT_MIN, F32)
            g = jnp.maximum(z, 0.0) + jnp.log(1.0 + jnp.exp(neg_abs))
            if key_offset is not None:
                g = jnp.where(before, g, 0.0)
            go.append(g)
        later = [jnp.dot(later_keys, g.astype(BF16), preferred_element_type=F32) for g in go]
        for h in heads:
            w = jnp.exp(zs[h] - (go[h] + later[h] + r_ref[h]))
            if key_offset is not None:
                w = jnp.where(before, w, 0.0)
            acc_ref[h] += jnp.dot(vt_ref[0, kb, h * HEAD_DIM:(h + 1) * HEAD_DIM, :], w.astype(BF16),
                                  preferred_element_type=F32)
            r_ref[h] += jnp.sum(go[h], axis=0, keepdims=True)

    first = qb * sub
    bufs = (z0_ref, z1_ref)
    produce(bufs[0], first + sub - 1)
    for i, d in enumerate(reversed(range(sub))):
        if d > 0:
            produce(bufs[(i + 1) % 2], first + d - 1)
        else:
            @pl.when(qb > 0)
            def _():
                produce(bufs[(i + 1) % 2], first - 1)
        consume(bufs[i % 2], first + d, d * tk)

    def body(j, carry):
        top = first - 1 - 2 * j
        produce(z1_ref, top - 1)
        consume(z0_ref, top, None)
        produce(z0_ref, top - 2)
        consume(z1_ref, top - 1, None)
        return carry

    lax.fori_loop(0, first // 2 - 1, body, 0)

    @pl.when(qb > 0)
    def _():
        produce(z1_ref, 0)
        consume(z0_ref, 1, None)
        consume(z1_ref, 0, None)

    for h in range(SB_HEADS_PER_STEP):
        o_ref[0, :, h * HEAD_DIM:(h + 1) * HEAD_DIM] = acc_ref[h].T.astype(BF16)


def _sb(qc, kc, vct, batch, seq):
    tq = min(512, seq)
    hp = SB_HEADS_PER_STEP
    nkb = seq // KEY_BLOCK
    return pl.pallas_call(
        functools.partial(_sb_kernel, tq=tq),
        grid=(batch, C_HEADS // hp, seq // tq),
        in_specs=[
            pl.BlockSpec((1, tq // KEY_BLOCK, hp * HEAD_DIM, KEY_BLOCK), lambda b, h, i: (b, i, h, 0)),
            pl.BlockSpec((1, seq, hp * HEAD_DIM), lambda b, h, i: (b, 0, h)),
            pl.BlockSpec((1, nkb, hp * HEAD_DIM, KEY_BLOCK), lambda b, h, i: (b, 0, h, 0)),
        ],
        out_specs=pl.BlockSpec((1, tq, hp * HEAD_DIM), lambda b, h, i: (b, i, h)),
        out_shape=jax.ShapeDtypeStruct((batch, seq, C_HEADS * HEAD_DIM), BF16),
        scratch_shapes=[pltpu.VMEM((hp, KEY_BLOCK, tq), F32), pltpu.VMEM((hp, KEY_BLOCK, tq), F32),
                        pltpu.VMEM((hp, 1, tq), F32), pltpu.VMEM((hp, HEAD_DIM, tq), F32)],
        compiler_params=_params("arbitrary", "arbitrary", "arbitrary"),
        name="sb_attn",
    )(qc, kc, vct)


SEARCH_GROUP = 2


def _dsa_kernel(qit_ref, ki_ref, wt_ref, qat_ref, ka_ref, vt_ref, o_ref,
                keys_ref, half_ref, thr_ref, m_ref, l_ref, acc_ref, s0_ref, s1_ref,
                *, t, tq, topk, seq):
    qb = pl.program_id(1)
    sub = tq // t
    nk = (qb + 1) * sub
    row = lax.broadcasted_iota(I32, (t, tq), 0)
    col = lax.broadcasted_iota(I32, (t, tq), 1)
    wt = wt_ref[0]

    def score_block(kb):
        ks = pl.multiple_of(kb * t, t)
        kix = ki_ref[0, pl.ds(ks, t), :]
        sc = jnp.zeros((t, tq), F32)
        for h in range(IDX_HEADS):
            lg = jnp.concatenate(
                [jnp.dot(kix, qit_ref[0, c, h * IDX_DIM:(h + 1) * IDX_DIM, :], preferred_element_type=F32)
                 for c in range(sub)], axis=1)
            sc = sc + wt[h:h + 1, :] * jnp.maximum(lg, 0.0)
        bits = lax.bitcast_convert_type(sc, I32)
        key = bits ^ ((bits >> 31) & 0x7FFFFFFF)
        visible = ((ks + row) >> CHUNK_SHIFT) <= ((qb * tq + col) >> CHUNK_SHIFT)
        key = jnp.where(visible, key, INT_MIN)
        keys_ref[kb] = key
        half_ref[kb] = (key >> 16).astype(I16)

    def score_pair(j, carry):
        score_block(2 * j)
        score_block(2 * j + 1)
        return carry

    assert sub % 2 == 0
    lax.fori_loop(0, nk // 2, score_pair, 0)

    i16_min, i16_max = -2 ** 15, 2 ** 15 - 1
    for pad in range(SEARCH_GROUP - 1):
        half_ref[nk + pad] = jnp.full((t, tq), i16_min, I16)

    def search16():
        def count16(cand):
            cand16 = cand.astype(I16)

            def block_group(j, accs):
                out = []
                for c, acc in enumerate(accs):
                    parts = [acc]
                    for kb in range(SEARCH_GROUP):
                        hit = jnp.where(half_ref[SEARCH_GROUP * j + kb, :, c * t:(c + 1) * t] >= cand16[:, c * t:(c + 1) * t],
                                        jnp.ones((), I16), jnp.zeros((), I16))
                        parts += [hit[i * 16:(i + 1) * 16] for i in range(t // 16)]
                    while len(parts) > 1:
                        parts = [a + b for a, b in zip(parts[0::2], parts[1::2])] + parts[len(parts) & ~1:]
                    out.append(parts[0])
                return tuple(out)
            groups = (nk + SEARCH_GROUP - 1) // SEARCH_GROUP
            accs = lax.fori_loop(0, groups, block_group,
                                 tuple(jnp.zeros((16, t), I16) for _ in range(sub)))
            return jnp.concatenate([jnp.sum(a.astype(I32), axis=0, keepdims=True) for a in accs], axis=1)

        c0 = count16(jnp.zeros((1, tq), I32))
        v0 = jnp.where(c0 >= topk, 0, i16_min).astype(I32)
        n0 = jnp.where(c0 >= topk, c0, 0)

        def bit(b, carry):
            v, n = carry
            cand = v | lax.shift_left(jnp.int32(1), 14 - b)
            c = count16(cand)
            return jnp.where(c >= topk, cand, v), jnp.where(c >= topk, c, n)

        return lax.fori_loop(0, 15, bit, (v0, n0))

    def count(pred):
        def block(kb, acc):
            hit = jnp.where(pred(keys_ref[kb], kb * t + row), 1, 0).astype(I32)
            parts = [acc] + [hit[i * 8:(i + 1) * 8] for i in range(t // 8)]
            while len(parts) > 1:
                parts = [a + b for a, b in zip(parts[0::2], parts[1::2])] + parts[len(parts) & ~1:]
            return parts[0]
        acc = lax.fori_loop(0, nk, block, jnp.zeros((8, tq), I32))
        return jnp.sum(acc, axis=0, keepdims=True)

    thr_hi, n_hi = search16()

    def low_half_block(kb, carry):
        key = keys_ref[kb]
        hi = key >> 16
        low = (key & 0xFFFF) + i16_min
        half_ref[kb] = jnp.where(hi == thr_hi, low, jnp.where(hi > thr_hi, i16_max, i16_min)).astype(I16)
        return carry

    lax.fori_loop(0, nk, low_half_block, 0)
    thr_lo, n_lo = search16()
    thr = lax.shift_left(thr_hi, 16) + (thr_lo - i16_min)
    n_ge = jnp.where(thr_lo > i16_min, n_lo, n_hi)
    tie = (thr != INT_MIN) & (n_ge > topk)
    thr_ref[...] = jnp.maximum(thr, INT_MIN + 1)

    @pl.when(jnp.max(tie.astype(I32)) > 0)
    def _():
        nbits = seq.bit_length() - 1
        need = topk - count(lambda k, i: k > thr)

        def lim_bit(b, lim):
            cand = lim | lax.shift_left(jnp.int32(1), nbits - 1 - b)
            below = count(lambda k, i: (k == thr) & (i < cand))
            return jnp.where(below < need, cand, lim)

        lim = lax.fori_loop(0, nbits, lim_bit, jnp.zeros((1, tq), I32)) + 1
        lim = jnp.where(tie, lim, seq)

        def demote_block(kb, carry):
            key = keys_ref[kb]
            drop = (key == thr) & ((kb * t + row) >= lim)
            keys_ref[kb] = jnp.where(drop, key - 1, key)
            return carry

        lax.fori_loop(0, nk, demote_block, 0)

    m_ref[...] = jnp.full(m_ref.shape, NEG_INF, F32)
    l_ref[...] = jnp.zeros(l_ref.shape, F32)
    acc_ref[...] = jnp.zeros(acc_ref.shape, F32)

    def produce(s_ref, kb):
        ks = pl.multiple_of(kb * t, t)
        bias = jnp.where(keys_ref[kb] >= thr_ref[...], 0.0, NEG_INF)
        for h in range(A_HEADS):
            k = ka_ref[0, pl.ds(ks, t), h * HEAD_DIM:(h + 1) * HEAD_DIM]
            for c in range(sub):
                s_ref[h, :, c * t:(c + 1) * t] = jnp.dot(
                    k, qat_ref[0, c, h * HEAD_DIM:(h + 1) * HEAD_DIM, :],
                    preferred_element_type=F32) + bias[:, c * t:(c + 1) * t]

    def consume(s_ref, kb):
        for h in range(A_HEADS):
            lo = h * HEAD_DIM
            s = s_ref[h]
            m_prev = m_ref[h]
            m_new = jnp.maximum(m_prev, jnp.max(s, axis=0, keepdims=True))
            m_safe = jnp.where(m_new == NEG_INF, 0.0, m_new)
            p = jnp.exp2(s - m_safe)
            alpha = jnp.exp2(m_prev - m_safe)
            l_ref[h] = alpha * l_ref[h] + jnp.sum(p, axis=0, keepdims=True)
            acc_ref[h] = alpha * acc_ref[h] + jnp.dot(vt_ref[0, kb, lo:lo + HEAD_DIM, :], p.astype(BF16),
                                                      preferred_element_type=F32)
            m_ref[h] = m_new

    produce(s0_ref, 0)

    def attend_pair(j, carry):
        produce(s1_ref, 2 * j + 1)
        consume(s0_ref, 2 * j)
        produce(s0_ref, 2 * j + 2)
        consume(s1_ref, 2 * j + 1)
        return carry

    lax.fori_loop(0, nk // 2 - 1, attend_pair, 0)
    produce(s1_ref, nk - 1)
    consume(s0_ref, nk - 2)
    consume(s1_ref, nk - 1)

    for h in range(A_HEADS):
        o_ref[0, :, h * HEAD_DIM:(h + 1) * HEAD_DIM] = (acc_ref[h] / l_ref[h]).T.astype(BF16)


def _dsa(qi, ki, wt, qa, ka, vat, batch, seq, topk):
    t = KEY_BLOCK
    tq = min(512, seq)
    nt = seq // t
    wa = A_HEADS * HEAD_DIM
    return pl.pallas_call(
        functools.partial(_dsa_kernel, t=t, tq=tq, topk=topk, seq=seq),
        grid=(batch, seq // tq),
        in_specs=[
            pl.BlockSpec((1, tq // t, IDX_HEADS * IDX_DIM, t), lambda b, i: (b, i, 0, 0)),
            pl.BlockSpec((1, seq, IDX_DIM), lambda b, i: (b, 0, 0)),
            pl.BlockSpec((1, IDX_HEADS, tq), lambda b, i: (b, 0, i)),
            pl.BlockSpec((1, tq // t, wa, t), lambda b, i: (b, i, 0, 0)),
            pl.BlockSpec((1, seq, wa), lambda b, i: (b, 0, 0)),
            pl.BlockSpec((1, nt, wa, t), lambda b, i: (b, 0, 0, 0)),
        ],
        out_specs=pl.BlockSpec((1, tq, wa), lambda b, i: (b, i, 0)),
        out_shape=jax.ShapeDtypeStruct((batch, seq, wa), BF16),
        scratch_shapes=[
            pltpu.VMEM((nt, t, tq), I32), pltpu.VMEM((nt + SEARCH_GROUP - 1, t, tq), I16),
            pltpu.VMEM((1, tq), I32),
            pltpu.VMEM((A_HEADS, 1, tq), F32), pltpu.VMEM((A_HEADS, 1, tq), F32),
            pltpu.VMEM((A_HEADS, HEAD_DIM, tq), F32),
            pltpu.VMEM((A_HEADS, t, tq), F32), pltpu.VMEM((A_HEADS, t, tq), F32),
        ],
        compiler_params=_params("arbitrary", "arbitrary"),
        name="dsa_attn",
    )(qi, ki, wt, qa, ka, vat)


def _out_kernel(x_ref, mod_ref, oa_ref, ob_ref, oc_ref, w_ref, o_ref):
    wa = A_HEADS * HEAD_DIM
    wb = B_HEADS * HEAD_DIM
    mixed = (jnp.dot(oa_ref[...], w_ref[0:wa, :], preferred_element_type=F32)
             + jnp.dot(ob_ref[...], w_ref[wa:wa + wb, :], preferred_element_type=F32)
             + jnp.dot(oc_ref[...], w_ref[wa + wb:, :], preferred_element_type=F32))
    o_ref[...] = x_ref[...] + (1.0 + mod_ref[0, 5:6, :]) * mixed


def _out_proj(x2, mod, oa, ob, oc, w, seq):
    T, D = x2.shape
    tm = min(512, seq)
    per_seq = seq // tm
    return pl.pallas_call(
        _out_kernel,
        grid=(T // tm,),
        in_specs=[
            pl.BlockSpec((tm, D), lambda i: (i, 0)),
            pl.BlockSpec((1, N_MOD, D), lambda i: (i // per_seq, 0, 0)),
            pl.BlockSpec((tm, oa.shape[1]), lambda i: (i, 0)),
            pl.BlockSpec((tm, ob.shape[1]), lambda i: (i, 0)),
            pl.BlockSpec((tm, oc.shape[1]), lambda i: (i, 0)),
            pl.BlockSpec(w.shape, lambda i: (0, 0)),
        ],
        out_specs=pl.BlockSpec((tm, D), lambda i: (i, 0)),
        out_shape=jax.ShapeDtypeStruct((T, D), F32),
        compiler_params=_params("arbitrary"),
        name="out_proj",
    )(x2, mod, oa, ob, oc, w)


def _rope_tables(seq):
    def tables(dim):
        inv = 1.0 / (ROPE_THETA ** (jnp.arange(0, dim, 2, dtype=F32) / dim))
        ang = jnp.arange(seq, dtype=F32)[:, None] * inv[None, :]
        return jnp.cos(ang), jnp.sin(ang)

    def lane_tables(cos, sin, fill):
        half = cos.shape[1]
        rest = LANES - 2 * half
        zeros_h = jnp.zeros((seq, half), F32)
        zeros_r = jnp.zeros((seq, rest), F32)
        return [jnp.concatenate([cos, cos, jnp.full((seq, rest), fill, F32)], axis=1),
                jnp.concatenate([-sin, zeros_h, zeros_r], axis=1),
                jnp.concatenate([zeros_h, sin, zeros_r], axis=1)]

    cos_p, sin_p = tables(PARTIAL_ROPE_DIM)
    cos_m, sin_m = tables(MLA_ROPE)
    return jnp.stack(lane_tables(cos_p, sin_p, 1.0) + lane_tables(cos_m, sin_m, 0.0))


def _pad_cols(a, width):
    return jnp.pad(a, ((0, 0), (0, width - a.shape[1])))


def _layer_weights(w_in, w_uq, w_ukv):
    w_in_p = jnp.concatenate([
        w_in[:, 0:2560],
        w_in[:, 3280:4816],
        _pad_cols(w_in[:, 2640:3088], Q_RANK_PAD),
        w_in[:, 3088:3216],
        _pad_cols(w_in[:, 3216:3280], LANES),
        _pad_cols(w_in[:, 2560:2640], LANES),
    ], axis=1).astype(BF16)
    wuq = w_uq.reshape(MLA_Q_RANK, B_HEADS, MLA_NOPE + MLA_ROPE)
    wuq = jnp.pad(wuq, ((0, Q_RANK_PAD - MLA_Q_RANK), (0, 0), (0, 2 * LANES - MLA_NOPE - MLA_ROPE)))
    wuq = wuq.reshape(Q_RANK_PAD, B_HEADS * 2 * LANES).astype(BF16)
    wukv = w_ukv.reshape(MLA_KV_RANK, B_HEADS, MLA_NOPE + HEAD_DIM)
    wukv = jnp.concatenate([wukv[:, :, :MLA_NOPE].reshape(MLA_KV_RANK, -1),
                            wukv[:, :, MLA_NOPE:].reshape(MLA_KV_RANK, -1)], axis=1).astype(BF16)
    return w_in_p, wuq, wukv


def kernel(x, c, w_ada, b_ada, g_ffn1, w1_gate, w1_up, w1_down, g_mix, w_in, g_qa, g_ka, g_cq, g_ckv, w_uq, w_ukv, g_q_nope, g_k_nope, g_q_rope, g_k_rope, w_out, g_ffn2, w2_gate, w2_up, w2_down):
    batch, seq, d_model = x.shape
    depth = w_ada.shape[0]
    topk = min(TOPK_MAX, seq // 4)
    tabs = _rope_tables(seq)
    mods = _ada(c, w_ada, b_ada).reshape(depth, batch, N_MOD, d_model)
    x2 = x.reshape(batch * seq, d_model)

    for l in range(depth):
        mod = mods[l]
        x2 = _ffn(x2, mod, g_ffn1[l], w1_gate[l].astype(BF16), w1_up[l].astype(BF16),
                  w1_down[l].astype(BF16), 0, seq)

        w_in_p, wuq, wukv = _layer_weights(w_in[l], w_uq[l], w_ukv[l])
        gains = [g_qa[l][None, :], g_ka[l][None, :], _pad_cols(g_cq[l][None, :], Q_RANK_PAD),
                 g_ckv[l][None, :], g_q_nope[l][None, :], g_k_nope[l][None, :],
                 _pad_cols(g_q_rope[l][None, :], LANES), _pad_cols(g_k_rope[l][None, :], LANES)]
        (qa, ka, vat, qi, ki, wt, qcat, kcat, vbt, qc, kc, vct) = _prep(
            x2, mod, g_mix[l], w_in_p, tabs, gains, wuq, wukv, batch, seq)

        out_a = _dsa(qi, ki, wt, qa, ka, vat, batch, seq, topk)
        out_b = _mla(qcat, kcat, vbt, batch, seq)
        out_c = _sb(qc, kc, vct, batch, seq)
        x2 = _out_proj(x2, mod, out_a.reshape(batch * seq, -1), out_b.reshape(batch * seq, -1),
                       out_c.reshape(batch * seq, -1), w_out[l].astype(BF16), seq)

        x2 = _ffn(x2, mod, g_ffn2[l], w2_gate[l].astype(BF16), w2_up[l].astype(BF16),
                  w2_down[l].astype(BF16), 6, seq)

    return x2.reshape(batch, seq, d_model)
```

```python
import functools

import jax
import jax.numpy as jnp
from jax import lax
from jax.experimental import pallas as pl
from jax.experimental.pallas import tpu as pltpu

F32 = jnp.float32
BF16 = jnp.bfloat16
I32 = jnp.int32
I16 = jnp.int16

HEAD_DIM = 128
CHUNK = 64
CHUNK_SHIFT = 6
ROPE_THETA = 500000.0
PARTIAL_ROPE_DIM = HEAD_DIM // 4
NORM_EPS = 1e-6
D_FF = 5632
N_MOD = 9
A_HEADS = 4
IDX_HEADS = 16
IDX_DIM = 64
TOPK_MAX = 256
B_HEADS = 8
MLA_Q_RANK = 448
MLA_KV_RANK = 128
MLA_NOPE = 128
MLA_ROPE = 64
C_HEADS = 4

LANES = 128
KEY_BLOCK = 256
Q_RANK_PAD = 512
VMEM_LIMIT = 56 * 1024 * 1024
FFN_VMEM_LIMIT = 61 * 1024 * 1024

OFF_QA, OFF_KA, OFF_VA, OFF_QI = 0, 512, 1024, 1536
OFF_QC, OFF_KC, OFF_VC = 2560, 3072, 3584
OFF_CQ, OFF_CKV, OFF_KR, OFF_KIW = 4096, 4608, 4736, 4864
N_PROJ = 4992

LOG2E = 1.4426950408889634
A_SCALE = HEAD_DIM ** -0.5 * LOG2E
B_SCALE = (MLA_NOPE + MLA_ROPE) ** -0.5 * LOG2E
C_SCALE = HEAD_DIM ** -0.5
IDX_SCALE = (IDX_DIM ** -0.5) * (IDX_HEADS ** -0.5)

NT_DIMS = (((1,), (1,)), ((), ()))
NEG_INF = float("-inf")
INT_MIN = -2 ** 31


def _params(*sem, vmem_limit=VMEM_LIMIT):
    return pltpu.CompilerParams(dimension_semantics=sem, vmem_limit_bytes=vmem_limit)


def _ada_kernel(c_ref, w_ref, b_ref, o_ref):
    c = c_ref[...]
    ca = (c * jax.nn.sigmoid(c)).astype(BF16)
    o_ref[0] = jnp.dot(ca, w_ref[0].astype(BF16), preferred_element_type=F32) + b_ref[0]


def _ada(c, w_ada, b_ada):
    L, D, N = w_ada.shape
    B = c.shape[0]
    tn = 2048
    return pl.pallas_call(
        _ada_kernel,
        grid=(L, N // tn),
        in_specs=[
            pl.BlockSpec((B, D), lambda l, j: (0, 0)),
            pl.BlockSpec((1, D, tn), lambda l, j: (l, 0, j)),
            pl.BlockSpec((1, 1, tn), lambda l, j: (l, 0, j)),
        ],
        out_specs=pl.BlockSpec((1, B, tn), lambda l, j: (l, 0, j)),
        out_shape=jax.ShapeDtypeStruct((L, B, N), F32),
        compiler_params=_params("arbitrary", "arbitrary"),
        name="ada_mod",
    )(c, w_ada, b_ada.reshape(L, 1, N))


NORM_ROWS = 16


def _norm_mod_into(h_ref, x_ref, g, shift, scale):
    gain = g * (1.0 + scale)

    def rows(r, carry):
        sl = pl.ds(pl.multiple_of(r * NORM_ROWS, NORM_ROWS), NORM_ROWS)
        x = x_ref[sl, :]
        y = x * lax.rsqrt(jnp.mean(x * x, axis=-1, keepdims=True) + NORM_EPS)
        h_ref[sl, :] = (y * gain + shift).astype(BF16)
        return carry

    lax.fori_loop(0, x_ref.shape[0] // NORM_ROWS, rows, 0, unroll=8)


FFN_TF = 512


def _gate_up_weights(w_gate, w_up):
    d, f = w_gate.shape
    nb = f // FFN_TF
    both = jnp.stack([w_gate.reshape(d, nb, FFN_TF), w_up.reshape(d, nb, FFN_TF)], axis=2)
    return both.reshape(d, 2 * f).astype(BF16)


def _ffn_kernel(x_ref, mod_ref, g_ref, wgu_ref, wd_ref, o_ref, h_ref, *, row):
    j = pl.program_id(1)

    @pl.when(j == 0)
    def _():
        _norm_mod_into(h_ref, x_ref, g_ref[...], mod_ref[0, row:row + 1, :], mod_ref[0, row + 1:row + 2, :])
        o_ref[...] = jnp.zeros(o_ref.shape, F32)

    gu = jnp.dot(h_ref[...], wgu_ref[...], preferred_element_type=F32)
    g = gu[:, :FFN_TF]
    u = gu[:, FFN_TF:]
    a = ((g * jax.nn.sigmoid(g)) * u).astype(BF16)
    o_ref[...] += jnp.dot(a, wd_ref[...], preferred_element_type=F32)

    @pl.when(j == pl.num_programs(1) - 1)
    def _():
        gate = mod_ref[0, row + 2:row + 3, :]
        o_ref[...] = x_ref[...] + (0.5 * (1.0 + gate)) * o_ref[...]


def _ffn(x2, mod, g, wgu, wd, row, seq):
    T, D = x2.shape
    F = wd.shape[0]
    tm = min(1024, seq)
    tf = FFN_TF
    per_seq = seq // tm
    return pl.pallas_call(
        functools.partial(_ffn_kernel, row=row),
        grid=(T // tm, F // tf),
        in_specs=[
            pl.BlockSpec((tm, D), lambda i, j: (i, 0)),
            pl.BlockSpec((1, N_MOD, D), lambda i, j: (i // per_seq, 0, 0)),
            pl.BlockSpec((1, D), lambda i, j: (0, 0)),
            pl.BlockSpec((D, 2 * tf), lambda i, j: (0, j)),
            pl.BlockSpec((tf, D), lambda i, j: (j, 0)),
        ],
        out_specs=pl.BlockSpec((tm, D), lambda i, j: (i, 0)),
        out_shape=jax.ShapeDtypeStruct((T, D), F32),
        scratch_shapes=[pltpu.VMEM((tm, D), BF16)],
        compiler_params=_params("arbitrary", "arbitrary", vmem_limit=FFN_VMEM_LIMIT),
        name="ffn",
    )(x2, mod, g.reshape(1, D), wgu, wd)


def _rms_lanes(x, g, n):
    return x * lax.rsqrt(jnp.sum(x * x, axis=-1, keepdims=True) / n + NORM_EPS) * g


def _rope_lanes(x, cos, sin_lo, sin_hi, half):
    return (x * cos + pltpu.roll(x, LANES - half, 1) * sin_lo + pltpu.roll(x, half, 1) * sin_hi)


def _prep_kernel(x_ref, mod_ref, gmix_ref, win_ref, tab_ref,
                 gqa_ref, gka_ref, gcq_ref, gckv_ref, gqn_ref, gkn_ref, gqr_ref, gkr_ref,
                 wuq_ref, wukv_ref,
                 qa_o, ka_o, vat_o, qi_o, ki_o, wt_o, qcat_o, kcat_o, vbt_o, qc_o, kc_o, vct_o, h_ref, t_ref):
    ca, sa_lo, sa_hi = tab_ref[0], tab_ref[1], tab_ref[2]
    cm, sm_lo, sm_hi = tab_ref[3], tab_ref[4], tab_ref[5]
    half_a = PARTIAL_ROPE_DIM // 2
    half_m = MLA_ROPE // 2
    wa, wc = A_HEADS * HEAD_DIM, C_HEADS * HEAD_DIM

    _norm_mod_into(h_ref, x_ref, gmix_ref[...], mod_ref[0, 3:4, :], mod_ref[0, 4:5, :])

    def transposed(v):
        t_ref[:, 0:v.shape[1]] = v
        return t_ref[:, 0:v.shape[1]].T

    def proj(lo, width):
        return jnp.dot(h_ref[...], win_ref[:, lo:lo + width], preferred_element_type=F32)

    cq = proj(OFF_CQ, Q_RANK_PAD)
    small = proj(OFF_CKV, N_PROJ - OFF_CKV)
    qa = proj(OFF_QA, wa)
    ka = proj(OFF_KA, wa)
    cq = _rms_lanes(cq, gcq_ref[...], MLA_Q_RANK)
    ckv = _rms_lanes(small[:, 0:MLA_KV_RANK], gckv_ref[...], MLA_KV_RANK)
    qb = jnp.dot(cq.astype(BF16), wuq_ref[...], preferred_element_type=F32)
    kvb = jnp.dot(ckv.astype(BF16), wukv_ref[...], preferred_element_type=F32)
    va = proj(OFF_VA, wa)
    qi = proj(OFF_QI, IDX_HEADS * IDX_DIM)
    qc = proj(OFF_QC, wc)
    kc = proj(OFF_KC, wc)
    vc = proj(OFF_VC, wc)

    for h in range(A_HEADS):
        lo = h * HEAD_DIM
        q = _rms_lanes(qa[:, lo:lo + HEAD_DIM], gqa_ref[...], HEAD_DIM)
        t_ref[:, lo:lo + HEAD_DIM] = _rope_lanes(q, ca, sa_lo, sa_hi, half_a) * A_SCALE
    qa_o[0, 0] = t_ref[:, 0:wa].T.astype(BF16)
    for h in range(A_HEADS):
        lo = h * HEAD_DIM
        k = _rms_lanes(ka[:, lo:lo + HEAD_DIM], gka_ref[...], HEAD_DIM)
        ka_o[0, :, lo:lo + HEAD_DIM] = _rope_lanes(k, ca, sa_lo, sa_hi, half_a).astype(BF16)
    vat_o[0, 0] = transposed(va).astype(BF16)
    qi_o[0, 0] = transposed(qi).astype(BF16)

    qc_o[0, 0] = transposed(qc * C_SCALE).astype(BF16)
    kc_o[0] = kc.astype(BF16)
    vct_o[0, 0] = transposed(vc).astype(BF16)

    kr = _rms_lanes(small[:, OFF_KR - OFF_CKV:OFF_KR - OFF_CKV + LANES], gkr_ref[...], MLA_ROPE)
    kr = _rope_lanes(kr, cm, sm_lo, sm_hi, half_m).astype(BF16)
    kiw = small[:, OFF_KIW - OFF_CKV:OFF_KIW - OFF_CKV + LANES]
    ki_o[0] = kiw[:, :IDX_DIM].astype(BF16)
    wt_o[0] = transposed(kiw)[IDX_DIM:IDX_DIM + IDX_HEADS, :] * IDX_SCALE
    for h in range(B_HEADS):
        lo = h * 2 * LANES
        kn = _rms_lanes(kvb[:, h * MLA_NOPE:(h + 1) * MLA_NOPE], gkn_ref[...], MLA_NOPE)
        kcat_o[0, :, lo:lo + MLA_NOPE] = kn.astype(BF16)
        kcat_o[0, :, lo + MLA_NOPE:lo + 2 * LANES] = kr
    vbt_o[0, 0] = transposed(kvb[:, B_HEADS * MLA_NOPE:]).astype(BF16)
    for h in range(B_HEADS):
        lo = h * 2 * LANES
        qn = _rms_lanes(qb[:, lo:lo + MLA_NOPE], gqn_ref[...], MLA_NOPE)
        t_ref[:, lo:lo + MLA_NOPE] = qn * B_SCALE
        qr = _rms_lanes(qb[:, lo + MLA_NOPE:lo + 2 * LANES], gqr_ref[...], MLA_ROPE)
        t_ref[:, lo + MLA_NOPE:lo + 2 * LANES] = _rope_lanes(qr, cm, sm_lo, sm_hi, half_m) * B_SCALE
    qcat_o[0, 0] = t_ref[...].T.astype(BF16)


def _resident(shape):
    return pl.BlockSpec(shape, lambda b, i: (0,) * len(shape), pipeline_mode=pl.Buffered(1))


def _prep(x2, mod, g_mix, w_in_p, tabs, gains, wuq, wukv, batch, seq):
    tm = KEY_BLOCK
    nt = seq // tm
    d_model = x2.shape[1]
    wa, wb, wc = A_HEADS * HEAD_DIM, B_HEADS * HEAD_DIM, C_HEADS * HEAD_DIM
    wcat = B_HEADS * 2 * LANES

    def tok(width, dtype):
        return (jax.ShapeDtypeStruct((batch, seq, width), dtype),
                pl.BlockSpec((1, tm, width), lambda b, i: (b, i, 0)))

    def tok_t(width):
        return (jax.ShapeDtypeStruct((batch, nt, width, tm), BF16),
                pl.BlockSpec((1, 1, width, tm), lambda b, i: (b, i, 0, 0)))

    outs = [tok_t(wa), tok(wa, BF16), tok_t(wa), tok_t(IDX_HEADS * IDX_DIM), tok(IDX_DIM, BF16),
            (jax.ShapeDtypeStruct((batch, IDX_HEADS, seq), F32),
             pl.BlockSpec((1, IDX_HEADS, tm), lambda b, i: (b, 0, i))),
            tok_t(wcat), tok(wcat, BF16), tok_t(wb), tok_t(wc), tok(wc, BF16), tok_t(wc)]
    return pl.pallas_call(
        _prep_kernel,
        grid=(batch, nt),
        in_specs=[pl.BlockSpec((tm, d_model), lambda b, i: (b * nt + i, 0)),
                  pl.BlockSpec((1, N_MOD, d_model), lambda b, i: (b, 0, 0)),
                  _resident((1, d_model)),
                  _resident(w_in_p.shape),
                  pl.BlockSpec((6, tm, LANES), lambda b, i: (0, i, 0))]
                 + [_resident(g.shape) for g in gains]
                 + [_resident(wuq.shape), _resident(wukv.shape)],
        out_specs=[o[1] for o in outs],
        out_shape=[o[0] for o in outs],
        scratch_shapes=[pltpu.VMEM((tm, d_model), BF16), pltpu.VMEM((tm, wcat), F32)],
        compiler_params=_params("arbitrary", "arbitrary"),
        name="head_prep",
    )(x2, mod, g_mix.reshape(1, d_model), w_in_p, tabs, *gains, wuq, wukv)


MLA_HEADS_PER_STEP = 4


def _mla_kernel(qt_ref, k_ref, vt_ref, o_ref, s0_ref, s1_ref, m_ref, l_ref, acc_ref, *, t):
    qb = pl.program_id(2)
    sub = t // KEY_BLOCK
    dq = 2 * LANES
    m_ref[...] = jnp.full(m_ref.shape, NEG_INF, F32)
    l_ref[...] = jnp.zeros(l_ref.shape, F32)
    acc_ref[...] = jnp.zeros(acc_ref.shape, F32)

    def produce(s_ref, kb):
        ks = pl.multiple_of(kb * t, t)
        for h in range(MLA_HEADS_PER_STEP):
            k = k_ref[0, pl.ds(ks, t), h * dq:(h + 1) * dq]
            for c in range(sub):
                s_ref[h, :, c * KEY_BLOCK:(c + 1) * KEY_BLOCK] = jnp.dot(
                    k, qt_ref[0, c, h * dq:(h + 1) * dq, :], preferred_element_type=F32)

    def consume(s_ref, kb, diagonal):
        for h in range(MLA_HEADS_PER_STEP):
            s = s_ref[h]
            if diagonal:
                kc = lax.broadcasted_iota(I32, (t, t), 0) >> CHUNK_SHIFT
                qc = lax.broadcasted_iota(I32, (t, t), 1) >> CHUNK_SHIFT
                s = jnp.where(kc <= qc, s, NEG_INF)
            m_prev = m_ref[h]
            m_new = jnp.maximum(m_prev, jnp.max(s, axis=0, keepdims=True))
            p = jnp.exp2(s - m_new)
            alpha = jnp.exp2(m_prev - m_new)
            l_ref[h] = alpha * l_ref[h] + jnp.sum(p, axis=0, keepdims=True)
            p = p.astype(BF16)
            pv = jnp.dot(vt_ref[0, kb * sub, h * HEAD_DIM:(h + 1) * HEAD_DIM, :], p[0:KEY_BLOCK],
                         preferred_element_type=F32)
            for c in range(1, sub):
                pv += jnp.dot(vt_ref[0, kb * sub + c, h * HEAD_DIM:(h + 1) * HEAD_DIM, :],
                              p[c * KEY_BLOCK:(c + 1) * KEY_BLOCK], preferred_element_type=F32)
            acc_ref[h] = alpha * acc_ref[h] + pv
            m_ref[h] = m_new

    produce(s0_ref, 0)

    def body(j, carry):
        produce(s1_ref, 2 * j + 1)
        consume(s0_ref, 2 * j, False)
        produce(s0_ref, 2 * j + 2)
        consume(s1_ref, 2 * j + 1, False)
        return carry

    lax.fori_loop(0, qb // 2, body, 0)

    @pl.when(qb % 2 == 0)
    def _():
        consume(s0_ref, qb, True)

    @pl.when(qb % 2 == 1)
    def _():
        produce(s1_ref, qb)
        consume(s0_ref, qb - 1, False)
        consume(s1_ref, qb, True)

    for h in range(MLA_HEADS_PER_STEP):
        o_ref[0, :, h * HEAD_DIM:(h + 1) * HEAD_DIM] = (acc_ref[h] / l_ref[h]).T.astype(BF16)


def _mla(qcat, kcat, vbt, batch, seq):
    t = min(512, seq)
    hp = MLA_HEADS_PER_STEP
    nkb = seq // KEY_BLOCK
    return pl.pallas_call(
        functools.partial(_mla_kernel, t=t),
        grid=(batch, B_HEADS // hp, seq // t),
        in_specs=[
            pl.BlockSpec((1, t // KEY_BLOCK, hp * 2 * LANES, KEY_BLOCK), lambda b, h, i: (b, i, h, 0)),
            pl.BlockSpec((1, seq, hp * 2 * LANES), lambda b, h, i: (b, 0, h)),
            pl.BlockSpec((1, nkb, hp * HEAD_DIM, KEY_BLOCK), lambda b, h, i: (b, 0, h, 0)),
        ],
        out_specs=pl.BlockSpec((1, t, hp * HEAD_DIM), lambda b, h, i: (b, i, h)),
        out_shape=jax.ShapeDtypeStruct((batch, seq, B_HEADS * HEAD_DIM), BF16),
        scratch_shapes=[pltpu.VMEM((hp, t, t), F32), pltpu.VMEM((hp, t, t), F32),
                        pltpu.VMEM((hp, 1, t), F32), pltpu.VMEM((hp, 1, t), F32),
                        pltpu.VMEM((hp, HEAD_DIM, t), F32)],
        compiler_params=_params("arbitrary", "arbitrary", "arbitrary"),
        name="mla_attn",
    )(qcat, kcat, vbt)


SB_HEADS_PER_STEP = 4


def _sb_kernel(qt_ref, k_ref, vt_ref, o_ref, z0_ref, z1_ref, r_ref, acc_ref, *, tq):
    qb = pl.program_id(2)
    tk = KEY_BLOCK
    sub = tq // tk
    assert sub % 2 == 0
    tri_r = lax.broadcasted_iota(I32, (tk, tk), 0)
    tri_c = lax.broadcasted_iota(I32, (tk, tk), 1)
    later_keys = jnp.where(tri_c > tri_r, 1.0, 0.0).astype(BF16)
    r_ref[...] = jnp.zeros(r_ref.shape, F32)
    acc_ref[...] = jnp.zeros(acc_ref.shape, F32)

    heads = range(SB_HEADS_PER_STEP)

    def produce(z_ref, kb):
        ks = pl.multiple_of(kb * tk, tk)
        for h in heads:
            k = k_ref[0, pl.ds(ks, tk), h * HEAD_DIM:(h + 1) * HEAD_DIM]
            for c in range(sub):
                z_ref[h, :, c * tk:(c + 1) * tk] = jnp.dot(
                    k, qt_ref[0, c, h * HEAD_DIM:(h + 1) * HEAD_DIM, :], preferred_element_type=F32)

    def consume(z_ref, kb, key_offset):
        if key_offset is not None:
            before = (lax.broadcasted_iota(I32, (tk, tq), 0) + key_offset
                      < lax.broadcasted_iota(I32, (tk, tq), 1))
        zs = [z_ref[h] for h in heads]
        go = []
        for z in zs:
            neg_abs = lax.bitcast_convert_type(lax.bitcast_convert_type(z, I32) | INT_MIN, F32)
            g = jnp.maximum(z, 0.0) + jnp.log(1.0 + jnp.exp(neg_abs))
            if key_offset is not None:
                g = jnp.where(before, g, 0.0)
            go.append(g)
        later = [jnp.dot(later_keys, g.astype(BF16), preferred_element_type=F32) for g in go]
        for h in heads:
            w = jnp.exp(zs[h] - (go[h] + later[h] + r_ref[h]))
            if key_offset is not None:
                w = jnp.where(before, w, 0.0)
            acc_ref[h] += jnp.dot(vt_ref[0, kb, h * HEAD_DIM:(h + 1) * HEAD_DIM, :], w.astype(BF16),
                                  preferred_element_type=F32)
            r_ref[h] += jnp.sum(go[h], axis=0, keepdims=True)

    first = qb * sub
    bufs = (z0_ref, z1_ref)
    produce(bufs[0], first + sub - 1)
    for i, d in enumerate(reversed(range(sub))):
        if d > 0:
            produce(bufs[(i + 1) % 2], first + d - 1)
        else:
            @pl.when(qb > 0)
            def _():
                produce(bufs[(i + 1) % 2], first - 1)
        consume(bufs[i % 2], first + d, d * tk)

    def body(j, carry):
        top = first - 1 - 2 * j
        produce(z1_ref, top - 1)
        consume(z0_ref, top, None)
        produce(z0_ref, top - 2)
        consume(z1_ref, top - 1, None)
        return carry

    lax.fori_loop(0, first // 2 - 1, body, 0)

    @pl.when(qb > 0)
    def _():
        produce(z1_ref, 0)
        consume(z0_ref, 1, None)
        consume(z1_ref, 0, None)

    for h in range(SB_HEADS_PER_STEP):
        o_ref[0, :, h * HEAD_DIM:(h + 1) * HEAD_DIM] = acc_ref[h].T.astype(BF16)


def _sb(qc, kc, vct, batch, seq):
    tq = min(512, seq)
    hp = SB_HEADS_PER_STEP
    nkb = seq // KEY_BLOCK
    return pl.pallas_call(
        functools.partial(_sb_kernel, tq=tq),
        grid=(batch, C_HEADS // hp, seq // tq),
        in_specs=[
            pl.BlockSpec((1, tq // KEY_BLOCK, hp * HEAD_DIM, KEY_BLOCK), lambda b, h, i: (b, i, h, 0)),
            pl.BlockSpec((1, seq, hp * HEAD_DIM), lambda b, h, i: (b, 0, h)),
            pl.BlockSpec((1, nkb, hp * HEAD_DIM, KEY_BLOCK), lambda b, h, i: (b, 0, h, 0)),
        ],
        out_specs=pl.BlockSpec((1, tq, hp * HEAD_DIM), lambda b, h, i: (b, i, h)),
        out_shape=jax.ShapeDtypeStruct((batch, seq, C_HEADS * HEAD_DIM), BF16),
        scratch_shapes=[pltpu.VMEM((hp, KEY_BLOCK, tq), F32), pltpu.VMEM((hp, KEY_BLOCK, tq), F32),
                        pltpu.VMEM((hp, 1, tq), F32), pltpu.VMEM((hp, HEAD_DIM, tq), F32)],
        compiler_params=_params("arbitrary", "arbitrary", "arbitrary"),
        name="sb_attn",
    )(qc, kc, vct)


SEARCH_GROUP = 2


def _dsa_kernel(qit_ref, ki_ref, wt_ref, qat_ref, ka_ref, vt_ref, o_ref,
                keys_ref, half_ref, thr_ref, m_ref, l_ref, acc_ref, s0_ref, s1_ref,
                *, t, tq, topk, seq):
    qb = pl.program_id(1)
    sub = tq // t
    nk = (qb + 1) * sub
    row = lax.broadcasted_iota(I32, (t, tq), 0)
    col = lax.broadcasted_iota(I32, (t, tq), 1)
    wt = wt_ref[0]

    def score_block(kb):
        ks = pl.multiple_of(kb * t, t)
        kix = ki_ref[0, pl.ds(ks, t), :]
        sc = jnp.zeros((t, tq), F32)
        for h in range(IDX_HEADS):
            lg = jnp.concatenate(
                [jnp.dot(kix, qit_ref[0, c, h * IDX_DIM:(h + 1) * IDX_DIM, :], preferred_element_type=F32)
                 for c in range(sub)], axis=1)
            sc = sc + wt[h:h + 1, :] * jnp.maximum(lg, 0.0)
        bits = lax.bitcast_convert_type(sc, I32)
        key = bits ^ ((bits >> 31) & 0x7FFFFFFF)
        visible = ((ks + row) >> CHUNK_SHIFT) <= ((qb * tq + col) >> CHUNK_SHIFT)
        key = jnp.where(visible, key, INT_MIN)
        keys_ref[kb] = key
        half_ref[kb] = (key >> 16).astype(I16)

    def score_pair(j, carry):
        score_block(2 * j)
        score_block(2 * j + 1)
        return carry

    assert sub % 2 == 0
    lax.fori_loop(0, nk // 2, score_pair, 0)

    i16_min, i16_max = -2 ** 15, 2 ** 15 - 1
    for pad in range(SEARCH_GROUP - 1):
        half_ref[nk + pad] = jnp.full((t, tq), i16_min, I16)

    def search16():
        def count16(cand):
            cand16 = cand.astype(I16)

            def block_group(j, accs):
                out = []
                for c, acc in enumerate(accs):
                    parts = [acc]
                    for kb in range(SEARCH_GROUP):
                        hit = jnp.where(half_ref[SEARCH_GROUP * j + kb, :, c * t:(c + 1) * t] >= cand16[:, c * t:(c + 1) * t],
                                        jnp.ones((), I16), jnp.zeros((), I16))
                        parts += [hit[i * 16:(i + 1) * 16] for i in range(t // 16)]
                    while len(parts) > 1:
                        parts = [a + b for a, b in zip(parts[0::2], parts[1::2])] + parts[len(parts) & ~1:]
                    out.append(parts[0])
                return tuple(out)
            groups = (nk + SEARCH_GROUP - 1) // SEARCH_GROUP
            accs = lax.fori_loop(0, groups, block_group,
                                 tuple(jnp.zeros((16, t), I16) for _ in range(sub)))
            return jnp.concatenate([jnp.sum(a.astype(I32), axis=0, keepdims=True) for a in accs], axis=1)

        c0 = count16(jnp.zeros((1, tq), I32))
        v0 = jnp.where(c0 >= topk, 0, i16_min).astype(I32)
        n0 = jnp.where(c0 >= topk, c0, 0)

        def bit(b, carry):
            v, n = carry
            cand = v | lax.shift_left(jnp.int32(1), 14 - b)
            c = count16(cand)
            return jnp.where(c >= topk, cand, v), jnp.where(c >= topk, c, n)

        return lax.fori_loop(0, 15, bit, (v0, n0))

    def count(pred):
        def block(kb, acc):
            hit = jnp.where(pred(keys_ref[kb], kb * t + row), 1, 0).astype(I32)
            parts = [acc] + [hit[i * 8:(i + 1) * 8] for i in range(t // 8)]
            while len(parts) > 1:
                parts = [a + b for a, b in zip(parts[0::2], parts[1::2])] + parts[len(parts) & ~1:]
            return parts[0]
        acc = lax.fori_loop(0, nk, block, jnp.zeros((8, tq), I32))
        return jnp.sum(acc, axis=0, keepdims=True)

    thr_hi, n_hi = search16()

    def low_half_block(kb, carry):
        key = keys_ref[kb]
        hi = key >> 16
        low = (key & 0xFFFF) + i16_min
        half_ref[kb] = jnp.where(hi == thr_hi, low, jnp.where(hi > thr_hi, i16_max, i16_min)).astype(I16)
        return carry

    lax.fori_loop(0, nk, low_half_block, 0)
    thr_lo, n_lo = search16()
    thr = lax.shift_left(thr_hi, 16) + (thr_lo - i16_min)
    n_ge = jnp.where(thr_lo > i16_min, n_lo, n_hi)
    tie = (thr != INT_MIN) & (n_ge > topk)
    thr_ref[...] = jnp.maximum(thr, INT_MIN + 1)

    @pl.when(jnp.max(tie.astype(I32)) > 0)
    def _():
        nbits = seq.bit_length() - 1
        need = topk - count(lambda k, i: k > thr)

        def lim_bit(b, lim):
            cand = lim | lax.shift_left(jnp.int32(1), nbits - 1 - b)
            below = count(lambda k, i: (k == thr) & (i < cand))
            return jnp.where(below < need, cand, lim)

        lim = lax.fori_loop(0, nbits, lim_bit, jnp.zeros((1, tq), I32)) + 1
        lim = jnp.where(tie, lim, seq)

        def demote_block(kb, carry):
            key = keys_ref[kb]
            drop = (key == thr) & ((kb * t + row) >= lim)
            keys_ref[kb] = jnp.where(drop, key - 1, key)
            return carry

        lax.fori_loop(0, nk, demote_block, 0)

    m_ref[...] = jnp.full(m_ref.shape, NEG_INF, F32)
    l_ref[...] = jnp.zeros(l_ref.shape, F32)
    acc_ref[...] = jnp.zeros(acc_ref.shape, F32)

    def produce(s_ref, kb):
        ks = pl.multiple_of(kb * t, t)
        bias = jnp.where(keys_ref[kb] >= thr_ref[...], 0.0, NEG_INF)
        for h in range(A_HEADS):
            k = ka_ref[0, pl.ds(ks, t), h * HEAD_DIM:(h + 1) * HEAD_DIM]
            for c in range(sub):
                s_ref[h, :, c * t:(c + 1) * t] = jnp.dot(
                    k, qat_ref[0, c, h * HEAD_DIM:(h + 1) * HEAD_DIM, :],
                    preferred_element_type=F32) + bias[:, c * t:(c + 1) * t]

    def consume(s_ref, kb):
        for h in range(A_HEADS):
            lo = h * HEAD_DIM
            s = s_ref[h]
            m_prev = m_ref[h]
            m_new = jnp.maximum(m_prev, jnp.max(s, axis=0, keepdims=True))
            m_safe = jnp.where(m_new == NEG_INF, 0.0, m_new)
            p = jnp.exp2(s - m_safe)
            alpha = jnp.exp2(m_prev - m_safe)
            l_ref[h] = alpha * l_ref[h] + jnp.sum(p, axis=0, keepdims=True)
            acc_ref[h] = alpha * acc_ref[h] + jnp.dot(vt_ref[0, kb, lo:lo + HEAD_DIM, :], p.astype(BF16),
                                                      preferred_element_type=F32)
            m_ref[h] = m_new

    produce(s0_ref, 0)

    def attend_pair(j, carry):
        produce(s1_ref, 2 * j + 1)
        consume(s0_ref, 2 * j)
        produce(s0_ref, 2 * j + 2)
        consume(s1_ref, 2 * j + 1)
        return carry

    lax.fori_loop(0, nk // 2 - 1, attend_pair, 0)
    produce(s1_ref, nk - 1)
    consume(s0_ref, nk - 2)
    consume(s1_ref, nk - 1)

    for h in range(A_HEADS):
        o_ref[0, :, h * HEAD_DIM:(h + 1) * HEAD_DIM] = (acc_ref[h] / l_ref[h]).T.astype(BF16)


def _dsa(qi, ki, wt, qa, ka, vat, batch, seq, topk):
    t = KEY_BLOCK
    tq = min(512, seq)
    nt = seq // t
    wa = A_HEADS * HEAD_DIM
    return pl.pallas_call(
        functools.partial(_dsa_kernel, t=t, tq=tq, topk=topk, seq=seq),
        grid=(batch, seq // tq),
        in_specs=[
            pl.BlockSpec((1, tq // t, IDX_HEADS * IDX_DIM, t), lambda b, i: (b, i, 0, 0)),
            pl.BlockSpec((1, seq, IDX_DIM), lambda b, i: (b, 0, 0)),
            pl.BlockSpec((1, IDX_HEADS, tq), lambda b, i: (b, 0, i)),
            pl.BlockSpec((1, tq // t, wa, t), lambda b, i: (b, i, 0, 0)),
            pl.BlockSpec((1, seq, wa), lambda b, i: (b, 0, 0)),
            pl.BlockSpec((1, nt, wa, t), lambda b, i: (b, 0, 0, 0)),
        ],
        out_specs=pl.BlockSpec((1, tq, wa), lambda b, i: (b, i, 0)),
        out_shape=jax.ShapeDtypeStruct((batch, seq, wa), BF16),
        scratch_shapes=[
            pltpu.VMEM((nt, t, tq), I32), pltpu.VMEM((nt + SEARCH_GROUP - 1, t, tq), I16),
            pltpu.VMEM((1, tq), I32),
            pltpu.VMEM((A_HEADS, 1, tq), F32), pltpu.VMEM((A_HEADS, 1, tq), F32),
            pltpu.VMEM((A_HEADS, HEAD_DIM, tq), F32),
            pltpu.VMEM((A_HEADS, t, tq), F32), pltpu.VMEM((A_HEADS, t, tq), F32),
        ],
        compiler_params=_params("arbitrary", "arbitrary"),
        name="dsa_attn",
    )(qi, ki, wt, qa, ka, vat)


def _out_kernel(x_ref, mod_ref, oa_ref, ob_ref, oc_ref, w_ref, o_ref):
    wa = A_HEADS * HEAD_DIM
    wb = B_HEADS * HEAD_DIM
    mixed = (jnp.dot(oa_ref[...], w_ref[0:wa, :], preferred_element_type=F32)
             + jnp.dot(ob_ref[...], w_ref[wa:wa + wb, :], preferred_element_type=F32)
             + jnp.dot(oc_ref[...], w_ref[wa + wb:, :], preferred_element_type=F32))
    o_ref[...] = x_ref[...] + (1.0 + mod_ref[0, 5:6, :]) * mixed


def _out_proj(x2, mod, oa, ob, oc, w, seq):
    T, D = x2.shape
    tm = min(512, seq)
    per_seq = seq // tm
    return pl.pallas_call(
        _out_kernel,
        grid=(T // tm,),
        in_specs=[
            pl.BlockSpec((tm, D), lambda i: (i, 0)),
            pl.BlockSpec((1, N_MOD, D), lambda i: (i // per_seq, 0, 0)),
            pl.BlockSpec((tm, oa.shape[1]), lambda i: (i, 0)),
            pl.BlockSpec((tm, ob.shape[1]), lambda i: (i, 0)),
            pl.BlockSpec((tm, oc.shape[1]), lambda i: (i, 0)),
            pl.BlockSpec(w.shape, lambda i: (0, 0)),
        ],
        out_specs=pl.BlockSpec((tm, D), lambda i: (i, 0)),
        out_shape=jax.ShapeDtypeStruct((T, D), F32),
        compiler_params=_params("arbitrary"),
        name="out_proj",
    )(x2, mod, oa, ob, oc, w)


def _rope_tables(seq):
    def tables(dim):
        inv = 1.0 / (ROPE_THETA ** (jnp.arange(0, dim, 2, dtype=F32) / dim))
        ang = jnp.arange(seq, dtype=F32)[:, None] * inv[None, :]
        return jnp.cos(ang), jnp.sin(ang)

    def lane_tables(cos, sin, fill):
        half = cos.shape[1]
        rest = LANES - 2 * half
        zeros_h = jnp.zeros((seq, half), F32)
        zeros_r = jnp.zeros((seq, rest), F32)
        return [jnp.concatenate([cos, cos, jnp.full((seq, rest), fill, F32)], axis=1),
                jnp.concatenate([-sin, zeros_h, zeros_r], axis=1),
                jnp.concatenate([zeros_h, sin, zeros_r], axis=1)]

    cos_p, sin_p = tables(PARTIAL_ROPE_DIM)
    cos_m, sin_m = tables(MLA_ROPE)
    return jnp.stack(lane_tables(cos_p, sin_p, 1.0) + lane_tables(cos_m, sin_m, 0.0))


def _pad_cols(a, width):
    return jnp.pad(a, ((0, 0), (0, width - a.shape[1])))


def _layer_weights(w_in, w_uq, w_ukv):
    w_in_p = jnp.concatenate([
        w_in[:, 0:2560],
        w_in[:, 3280:4816],
        _pad_cols(w_in[:, 2640:3088], Q_RANK_PAD),
        w_in[:, 3088:3216],
        _pad_cols(w_in[:, 3216:3280], LANES),
        _pad_cols(w_in[:, 2560:2640], LANES),
    ], axis=1).astype(BF16)
    wuq = w_uq.reshape(MLA_Q_RANK, B_HEADS, MLA_NOPE + MLA_ROPE)
    wuq = jnp.pad(wuq, ((0, Q_RANK_PAD - MLA_Q_RANK), (0, 0), (0, 2 * LANES - MLA_NOPE - MLA_ROPE)))
    wuq = wuq.reshape(Q_RANK_PAD, B_HEADS * 2 * LANES).astype(BF16)
    wukv = w_ukv.reshape(MLA_KV_RANK, B_HEADS, MLA_NOPE + HEAD_DIM)
    wukv = jnp.concatenate([wukv[:, :, :MLA_NOPE].reshape(MLA_KV_RANK, -1),
                            wukv[:, :, MLA_NOPE:].reshape(MLA_KV_RANK, -1)], axis=1).astype(BF16)
    return w_in_p, wuq, wukv


def kernel(x, c, w_ada, b_ada, g_ffn1, w1_gate, w1_up, w1_down, g_mix, w_in, g_qa, g_ka, g_cq, g_ckv, w_uq, w_ukv, g_q_nope, g_k_nope, g_q_rope, g_k_rope, w_out, g_ffn2, w2_gate, w2_up, w2_down):
    batch, seq, d_model = x.shape
    depth = w_ada.shape[0]
    topk = min(TOPK_MAX, seq // 4)
    tabs = _rope_tables(seq)
    mods = _ada(c, w_ada, b_ada).reshape(depth, batch, N_MOD, d_model)
    x2 = x.reshape(batch * seq, d_model)

    for l in range(depth):
        mod = mods[l]
        x2 = _ffn(x2, mod, g_ffn1[l], _gate_up_weights(w1_gate[l], w1_up[l]), w1_down[l].astype(BF16), 0, seq)

        w_in_p, wuq, wukv = _layer_weights(w_in[l], w_uq[l], w_ukv[l])
        gains = [g_qa[l][None, :], g_ka[l][None, :], _pad_cols(g_cq[l][None, :], Q_RANK_PAD),
                 g_ckv[l][None, :], g_q_nope[l][None, :], g_k_nope[l][None, :],
                 _pad_cols(g_q_rope[l][None, :], LANES), _pad_cols(g_k_rope[l][None, :], LANES)]
        (qa, ka, vat, qi, ki, wt, qcat, kcat, vbt, qc, kc, vct) = _prep(
            x2, mod, g_mix[l], w_in_p, tabs, gains, wuq, wukv, batch, seq)

        out_a = _dsa(qi, ki, wt, qa, ka, vat, batch, seq, topk)
        out_b = _mla(qcat, kcat, vbt, batch, seq)
        out_c = _sb(qc, kc, vct, batch, seq)
        x2 = _out_proj(x2, mod, out_a.reshape(batch * seq, -1), out_b.reshape(batch * seq, -1),
                       out_c.reshape(batch * seq, -1), w_out[l].astype(BF16), seq)

        x2 = _ffn(x2, mod, g_ffn2[l], _gate_up_weights(w2_gate[l], w2_up[l]), w2_down[l].astype(BF16), 6, seq)

    return x2.reshape(batch, seq, d_model)
```

```python
import functools

import jax
import jax.numpy as jnp
from jax import lax
from jax.experimental import pallas as pl
from jax.experimental.pallas import tpu as pltpu

F32 = jnp.float32
BF16 = jnp.bfloat16
I32 = jnp.int32
I16 = jnp.int16

HEAD_DIM = 128
CHUNK = 64
CHUNK_SHIFT = 6
ROPE_THETA = 500000.0
PARTIAL_ROPE_DIM = HEAD_DIM // 4
NORM_EPS = 1e-6
D_FF = 5632
N_MOD = 9
A_HEADS = 4
IDX_HEADS = 16
IDX_DIM = 64
TOPK_MAX = 256
B_HEADS = 8
MLA_Q_RANK = 448
MLA_KV_RANK = 128
MLA_NOPE = 128
MLA_ROPE = 64
C_HEADS = 4

LANES = 128
KEY_BLOCK = 256
Q_RANK_PAD = 512
VMEM_LIMIT = 56 * 1024 * 1024
FFN_VMEM_LIMIT = 61 * 1024 * 1024

OFF_QA, OFF_KA, OFF_VA, OFF_QI = 0, 512, 1024, 1536
OFF_QC, OFF_KC, OFF_VC = 2560, 3072, 3584
OFF_CQ, OFF_CKV, OFF_KR, OFF_KIW = 4096, 4608, 4736, 4864
N_PROJ = 4992

LOG2E = 1.4426950408889634
A_SCALE = HEAD_DIM ** -0.5 * LOG2E
B_SCALE = (MLA_NOPE + MLA_ROPE) ** -0.5 * LOG2E
C_SCALE = HEAD_DIM ** -0.5
IDX_SCALE = (IDX_DIM ** -0.5) * (IDX_HEADS ** -0.5)

NT_DIMS = (((1,), (1,)), ((), ()))
NEG_INF = float("-inf")
INT_MIN = -2 ** 31


def _params(*sem, vmem_limit=VMEM_LIMIT):
    return pltpu.CompilerParams(dimension_semantics=sem, vmem_limit_bytes=vmem_limit)


def _ada_kernel(c_ref, w_ref, b_ref, o_ref):
    c = c_ref[...]
    ca = (c * jax.nn.sigmoid(c)).astype(BF16)
    o_ref[0] = jnp.dot(ca, w_ref[0].astype(BF16), preferred_element_type=F32) + b_ref[0]


def _ada(c, w_ada, b_ada):
    L, D, N = w_ada.shape
    B = c.shape[0]
    tn = 2048
    return pl.pallas_call(
        _ada_kernel,
        grid=(L, N // tn),
        in_specs=[
            pl.BlockSpec((B, D), lambda l, j: (0, 0)),
            pl.BlockSpec((1, D, tn), lambda l, j: (l, 0, j)),
            pl.BlockSpec((1, 1, tn), lambda l, j: (l, 0, j)),
        ],
        out_specs=pl.BlockSpec((1, B, tn), lambda l, j: (l, 0, j)),
        out_shape=jax.ShapeDtypeStruct((L, B, N), F32),
        compiler_params=_params("arbitrary", "arbitrary"),
        name="ada_mod",
    )(c, w_ada, b_ada.reshape(L, 1, N))


NORM_ROWS = 16


def _norm_mod_into(h_ref, x_ref, g, shift, scale):
    gain = g * (1.0 + scale)

    def rows(r, carry):
        sl = pl.ds(pl.multiple_of(r * NORM_ROWS, NORM_ROWS), NORM_ROWS)
        x = x_ref[sl, :]
        y = x * lax.rsqrt(jnp.mean(x * x, axis=-1, keepdims=True) + NORM_EPS)
        h_ref[sl, :] = (y * gain + shift).astype(BF16)
        return carry

    lax.fori_loop(0, x_ref.shape[0] // NORM_ROWS, rows, 0, unroll=8)


def _ffn_kernel(x_ref, mod_ref, g_ref, wg_ref, wu_ref, wd_ref, o_ref, h_ref, *, row):
    j = pl.program_id(1)

    @pl.when(j == 0)
    def _():
        _norm_mod_into(h_ref, x_ref, g_ref[...], mod_ref[0, row:row + 1, :], mod_ref[0, row + 1:row + 2, :])
        o_ref[...] = jnp.zeros(o_ref.shape, F32)

    h = h_ref[...]
    g = jnp.dot(h, wg_ref[...], preferred_element_type=F32)
    u = jnp.dot(h, wu_ref[...], preferred_element_type=F32)
    a = ((g * jax.nn.sigmoid(g)) * u).astype(BF16)
    o_ref[...] += jnp.dot(a, wd_ref[...], preferred_element_type=F32)

    @pl.when(j == pl.num_programs(1) - 1)
    def _():
        gate = mod_ref[0, row + 2:row + 3, :]
        o_ref[...] = x_ref[...] + (0.5 * (1.0 + gate)) * o_ref[...]


def _ffn(x2, mod, g, wg, wu, wd, row, seq):
    T, D = x2.shape
    F = wg.shape[1]
    tm = min(1024, seq)
    tf = 512
    per_seq = seq // tm
    return pl.pallas_call(
        functools.partial(_ffn_kernel, row=row),
        grid=(T // tm, F // tf),
        in_specs=[
            pl.BlockSpec((tm, D), lambda i, j: (i, 0)),
            pl.BlockSpec((1, N_MOD, D), lambda i, j: (i // per_seq, 0, 0)),
            pl.BlockSpec((1, D), lambda i, j: (0, 0)),
            pl.BlockSpec((D, tf), lambda i, j: (0, j)),
            pl.BlockSpec((D, tf), lambda i, j: (0, j)),
            pl.BlockSpec((tf, D), lambda i, j: (j, 0)),
        ],
        out_specs=pl.BlockSpec((tm, D), lambda i, j: (i, 0)),
        out_shape=jax.ShapeDtypeStruct((T, D), F32),
        scratch_shapes=[pltpu.VMEM((tm, D), BF16)],
        compiler_params=_params("arbitrary", "arbitrary", vmem_limit=FFN_VMEM_LIMIT),
        name="ffn",
    )(x2, mod, g.reshape(1, D), wg, wu, wd)


def _rms_lanes(x, g, n):
    return x * lax.rsqrt(jnp.sum(x * x, axis=-1, keepdims=True) / n + NORM_EPS) * g


def _rope_lanes(x, cos, sin_lo, sin_hi, half):
    return (x * cos + pltpu.roll(x, LANES - half, 1) * sin_lo + pltpu.roll(x, half, 1) * sin_hi)


def _prep_kernel(x_ref, mod_ref, gmix_ref, win_ref, tab_ref,
                 gqa_ref, gka_ref, gcq_ref, gckv_ref, gqn_ref, gkn_ref, gqr_ref, gkr_ref,
                 wuq_ref, wukv_ref,
                 qa_o, ka_o, vat_o, qi_o, ki_o, wt_o, qcat_o, kcat_o, vbt_o, qc_o, kc_o, vct_o, h_ref, t_ref):
    ca, sa_lo, sa_hi = tab_ref[0], tab_ref[1], tab_ref[2]
    cm, sm_lo, sm_hi = tab_ref[3], tab_ref[4], tab_ref[5]
    half_a = PARTIAL_ROPE_DIM // 2
    half_m = MLA_ROPE // 2
    wa, wc = A_HEADS * HEAD_DIM, C_HEADS * HEAD_DIM

    _norm_mod_into(h_ref, x_ref, gmix_ref[...], mod_ref[0, 3:4, :], mod_ref[0, 4:5, :])

    def transposed(v):
        t_ref[:, 0:v.shape[1]] = v
        return t_ref[:, 0:v.shape[1]].T

    def proj(lo, width):
        return jnp.dot(h_ref[...], win_ref[:, lo:lo + width], preferred_element_type=F32)

    cq = proj(OFF_CQ, Q_RANK_PAD)
    small = proj(OFF_CKV, N_PROJ - OFF_CKV)
    qa = proj(OFF_QA, wa)
    ka = proj(OFF_KA, wa)
    cq = _rms_lanes(cq, gcq_ref[...], MLA_Q_RANK)
    ckv = _rms_lanes(small[:, 0:MLA_KV_RANK], gckv_ref[...], MLA_KV_RANK)
    qb = jnp.dot(cq.astype(BF16), wuq_ref[...], preferred_element_type=F32)
    kvb = jnp.dot(ckv.astype(BF16), wukv_ref[...], preferred_element_type=F32)
    va = proj(OFF_VA, wa)
    qi = proj(OFF_QI, IDX_HEADS * IDX_DIM)
    qc = proj(OFF_QC, wc)
    kc = proj(OFF_KC, wc)
    vc = proj(OFF_VC, wc)

    for h in range(A_HEADS):
        lo = h * HEAD_DIM
        q = _rms_lanes(qa[:, lo:lo + HEAD_DIM], gqa_ref[...], HEAD_DIM)
        t_ref[:, lo:lo + HEAD_DIM] = _rope_lanes(q, ca, sa_lo, sa_hi, half_a) * A_SCALE
    qa_o[0, 0] = t_ref[:, 0:wa].T.astype(BF16)
    for h in range(A_HEADS):
        lo = h * HEAD_DIM
        k = _rms_lanes(ka[:, lo:lo + HEAD_DIM], gka_ref[...], HEAD_DIM)
        ka_o[0, :, lo:lo + HEAD_DIM] = _rope_lanes(k, ca, sa_lo, sa_hi, half_a).astype(BF16)
    vat_o[0, 0] = transposed(va).astype(BF16)
    qi_o[0, 0] = transposed(qi).astype(BF16)

    qc_o[0, 0] = transposed(qc * C_SCALE).astype(BF16)
    kc_o[0] = kc.astype(BF16)
    vct_o[0, 0] = transposed(vc).astype(BF16)

    kr = _rms_lanes(small[:, OFF_KR - OFF_CKV:OFF_KR - OFF_CKV + LANES], gkr_ref[...], MLA_ROPE)
    kr = _rope_lanes(kr, cm, sm_lo, sm_hi, half_m).astype(BF16)
    kiw = small[:, OFF_KIW - OFF_CKV:OFF_KIW - OFF_CKV + LANES]
    ki_o[0] = kiw[:, :IDX_DIM].astype(BF16)
    wt_o[0] = transposed(kiw)[IDX_DIM:IDX_DIM + IDX_HEADS, :] * IDX_SCALE
    for h in range(B_HEADS):
        lo = h * 2 * LANES
        kn = _rms_lanes(kvb[:, h * MLA_NOPE:(h + 1) * MLA_NOPE], gkn_ref[...], MLA_NOPE)
        kcat_o[0, :, lo:lo + MLA_NOPE] = kn.astype(BF16)
        kcat_o[0, :, lo + MLA_NOPE:lo + 2 * LANES] = kr
    vbt_o[0, 0] = transposed(kvb[:, B_HEADS * MLA_NOPE:]).astype(BF16)
    for h in range(B_HEADS):
        lo = h * 2 * LANES
        qn = _rms_lanes(qb[:, lo:lo + MLA_NOPE], gqn_ref[...], MLA_NOPE)
        t_ref[:, lo:lo + MLA_NOPE] = qn * B_SCALE
        qr = _rms_lanes(qb[:, lo + MLA_NOPE:lo + 2 * LANES], gqr_ref[...], MLA_ROPE)
        t_ref[:, lo + MLA_NOPE:lo + 2 * LANES] = _rope_lanes(qr, cm, sm_lo, sm_hi, half_m) * B_SCALE
    qcat_o[0, 0] = t_ref[...].T.astype(BF16)


def _resident(shape):
    return pl.BlockSpec(shape, lambda b, i: (0,) * len(shape), pipeline_mode=pl.Buffered(1))


def _prep(x2, mod, g_mix, w_in_p, tabs, gains, wuq, wukv, batch, seq):
    tm = KEY_BLOCK
    nt = seq // tm
    d_model = x2.shape[1]
    wa, wb, wc = A_HEADS * HEAD_DIM, B_HEADS * HEAD_DIM, C_HEADS * HEAD_DIM
    wcat = B_HEADS * 2 * LANES

    def tok(width, dtype):
        return (jax.ShapeDtypeStruct((batch, seq, width), dtype),
                pl.BlockSpec((1, tm, width), lambda b, i: (b, i, 0)))

    def tok_t(width):
        return (jax.ShapeDtypeStruct((batch, nt, width, tm), BF16),
                pl.BlockSpec((1, 1, width, tm), lambda b, i: (b, i, 0, 0)))

    outs = [tok_t(wa), tok(wa, BF16), tok_t(wa), tok_t(IDX_HEADS * IDX_DIM), tok(IDX_DIM, BF16),
            (jax.ShapeDtypeStruct((batch, IDX_HEADS, seq), F32),
             pl.BlockSpec((1, IDX_HEADS, tm), lambda b, i: (b, 0, i))),
            tok_t(wcat), tok(wcat, BF16), tok_t(wb), tok_t(wc), tok(wc, BF16), tok_t(wc)]
    return pl.pallas_call(
        _prep_kernel,
        grid=(batch, nt),
        in_specs=[pl.BlockSpec((tm, d_model), lambda b, i: (b * nt + i, 0)),
                  pl.BlockSpec((1, N_MOD, d_model), lambda b, i: (b, 0, 0)),
                  _resident((1, d_model)),
                  _resident(w_in_p.shape),
                  pl.BlockSpec((6, tm, LANES), lambda b, i: (0, i, 0))]
                 + [_resident(g.shape) for g in gains]
                 + [_resident(wuq.shape), _resident(wukv.shape)],
        out_specs=[o[1] for o in outs],
        out_shape=[o[0] for o in outs],
        scratch_shapes=[pltpu.VMEM((tm, d_model), BF16), pltpu.VMEM((tm, wcat), F32)],
        compiler_params=_params("arbitrary", "arbitrary"),
        name="head_prep",
    )(x2, mod, g_mix.reshape(1, d_model), w_in_p, tabs, *gains, wuq, wukv)


MLA_HEADS_PER_STEP = 4


def _mla_kernel(qt_ref, k_ref, vt_ref, o_ref, s0_ref, s1_ref, m_ref, l_ref, acc_ref, *, t):
    qb = pl.program_id(2)
    sub = t // KEY_BLOCK
    dq = 2 * LANES
    m_ref[...] = jnp.full(m_ref.shape, NEG_INF, F32)
    l_ref[...] = jnp.zeros(l_ref.shape, F32)
    acc_ref[...] = jnp.zeros(acc_ref.shape, F32)

    def produce(s_ref, kb):
        ks = pl.multiple_of(kb * t, t)
        for h in range(MLA_HEADS_PER_STEP):
            k = k_ref[0, pl.ds(ks, t), h * dq:(h + 1) * dq]
            for c in range(sub):
                s_ref[h, :, c * KEY_BLOCK:(c + 1) * KEY_BLOCK] = jnp.dot(
                    k, qt_ref[0, c, h * dq:(h + 1) * dq, :], preferred_element_type=F32)

    def consume(s_ref, kb, diagonal):
        for h in range(MLA_HEADS_PER_STEP):
            s = s_ref[h]
            if diagonal:
                kc = lax.broadcasted_iota(I32, (t, t), 0) >> CHUNK_SHIFT
                qc = lax.broadcasted_iota(I32, (t, t), 1) >> CHUNK_SHIFT
                s = jnp.where(kc <= qc, s, NEG_INF)
            m_prev = m_ref[h]
            m_new = jnp.maximum(m_prev, jnp.max(s, axis=0, keepdims=True))
            p = jnp.exp2(s - m_new)
            alpha = jnp.exp2(m_prev - m_new)
            l_ref[h] = alpha * l_ref[h] + jnp.sum(p, axis=0, keepdims=True)
            p = p.astype(BF16)
            pv = jnp.dot(vt_ref[0, kb * sub, h * HEAD_DIM:(h + 1) * HEAD_DIM, :], p[0:KEY_BLOCK],
                         preferred_element_type=F32)
            for c in range(1, sub):
                pv += jnp.dot(vt_ref[0, kb * sub + c, h * HEAD_DIM:(h + 1) * HEAD_DIM, :],
                              p[c * KEY_BLOCK:(c + 1) * KEY_BLOCK], preferred_element_type=F32)
            acc_ref[h] = alpha * acc_ref[h] + pv
            m_ref[h] = m_new

    produce(s0_ref, 0)

    def body(j, carry):
        produce(s1_ref, 2 * j + 1)
        consume(s0_ref, 2 * j, False)
        produce(s0_ref, 2 * j + 2)
        consume(s1_ref, 2 * j + 1, False)
        return carry

    lax.fori_loop(0, qb // 2, body, 0)

    @pl.when(qb % 2 == 0)
    def _():
        consume(s0_ref, qb, True)

    @pl.when(qb % 2 == 1)
    def _():
        produce(s1_ref, qb)
        consume(s0_ref, qb - 1, False)
        consume(s1_ref, qb, True)

    for h in range(MLA_HEADS_PER_STEP):
        o_ref[0, :, h * HEAD_DIM:(h + 1) * HEAD_DIM] = (acc_ref[h] / l_ref[h]).T.astype(BF16)


def _mla(qcat, kcat, vbt, batch, seq):
    t = min(512, seq)
    hp = MLA_HEADS_PER_STEP
    nkb = seq // KEY_BLOCK
    return pl.pallas_call(
        functools.partial(_mla_kernel, t=t),
        grid=(batch, B_HEADS // hp, seq // t),
        in_specs=[
            pl.BlockSpec((1, t // KEY_BLOCK, hp * 2 * LANES, KEY_BLOCK), lambda b, h, i: (b, i, h, 0)),
            pl.BlockSpec((1, seq, hp * 2 * LANES), lambda b, h, i: (b, 0, h)),
            pl.BlockSpec((1, nkb, hp * HEAD_DIM, KEY_BLOCK), lambda b, h, i: (b, 0, h, 0)),
        ],
        out_specs=pl.BlockSpec((1, t, hp * HEAD_DIM), lambda b, h, i: (b, i, h)),
        out_shape=jax.ShapeDtypeStruct((batch, seq, B_HEADS * HEAD_DIM), BF16),
        scratch_shapes=[pltpu.VMEM((hp, t, t), F32), pltpu.VMEM((hp, t, t), F32),
                        pltpu.VMEM((hp, 1, t), F32), pltpu.VMEM((hp, 1, t), F32),
                        pltpu.VMEM((hp, HEAD_DIM, t), F32)],
        compiler_params=_params("arbitrary", "arbitrary", "arbitrary"),
        name="mla_attn",
    )(qcat, kcat, vbt)


SB_HEADS_PER_STEP = 4


def _sb_kernel(qt_ref, k_ref, vt_ref, o_ref, z0_ref, z1_ref, r_ref, acc_ref, *, tq):
    qb = pl.program_id(2)
    tk = KEY_BLOCK
    sub = tq // tk
    assert sub % 2 == 0
    tri_r = lax.broadcasted_iota(I32, (tk, tk), 0)
    tri_c = lax.broadcasted_iota(I32, (tk, tk), 1)
    later_keys = jnp.where(tri_c > tri_r, 1.0, 0.0).astype(BF16)
    r_ref[...] = jnp.zeros(r_ref.shape, F32)
    acc_ref[...] = jnp.zeros(acc_ref.shape, F32)

    heads = range(SB_HEADS_PER_STEP)

    def produce(z_ref, kb):
        ks = pl.multiple_of(kb * tk, tk)
        for h in heads:
            k = k_ref[0, pl.ds(ks, tk), h * HEAD_DIM:(h + 1) * HEAD_DIM]
            for c in range(sub):
                z_ref[h, :, c * tk:(c + 1) * tk] = jnp.dot(
                    k, qt_ref[0, c, h * HEAD_DIM:(h + 1) * HEAD_DIM, :], preferred_element_type=F32)

    def consume(z_ref, kb, key_offset):
        if key_offset is not None:
            before = (lax.broadcasted_iota(I32, (tk, tq), 0) + key_offset
                      < lax.broadcasted_iota(I32, (tk, tq), 1))
        zs = [z_ref[h] for h in heads]
        go = []
        for z in zs:
            neg_abs = lax.bitcast_convert_type(lax.bitcast_convert_type(z, I32) | INT_MIN, F32)
            g = jnp.maximum(z, 0.0) + jnp.log(1.0 + jnp.exp(neg_abs))
            if key_offset is not None:
                g = jnp.where(before, g, 0.0)
            go.append(g)
        later = [jnp.dot(later_keys, g.astype(BF16), preferred_element_type=F32) for g in go]
        for h in heads:
            w = jnp.exp(zs[h] - (go[h] + later[h] + r_ref[h]))
            if key_offset is not None:
                w = jnp.where(before, w, 0.0)
            acc_ref[h] += jnp.dot(vt_ref[0, kb, h * HEAD_DIM:(h + 1) * HEAD_DIM, :], w.astype(BF16),
                                  preferred_element_type=F32)
            r_ref[h] += jnp.sum(go[h], axis=0, keepdims=True)

    first = qb * sub
    bufs = (z0_ref, z1_ref)
    produce(bufs[0], first + sub - 1)
    for i, d in enumerate(reversed(range(sub))):
        if d > 0:
            produce(bufs[(i + 1) % 2], first + d - 1)
        else:
            @pl.when(qb > 0)
            def _():
                produce(bufs[(i + 1) % 2], first - 1)
        consume(bufs[i % 2], first + d, d * tk)

    def body(j, carry):
        top = first - 1 - 2 * j
        produce(z1_ref, top - 1)
        consume(z0_ref, top, None)
        produce(z0_ref, top - 2)
        consume(z1_ref, top - 1, None)
        return carry

    lax.fori_loop(0, first // 2 - 1, body, 0)

    @pl.when(qb > 0)
    def _():
        produce(z1_ref, 0)
        consume(z0_ref, 1, None)
        consume(z1_ref, 0, None)

    for h in range(SB_HEADS_PER_STEP):
        o_ref[0, :, h * HEAD_DIM:(h + 1) * HEAD_DIM] = acc_ref[h].T.astype(BF16)


def _sb(qc, kc, vct, batch, seq):
    tq = min(512, seq)
    hp = SB_HEADS_PER_STEP
    nkb = seq // KEY_BLOCK
    return pl.pallas_call(
        functools.partial(_sb_kernel, tq=tq),
        grid=(batch, C_HEADS // hp, seq // tq),
        in_specs=[
            pl.BlockSpec((1, tq // KEY_BLOCK, hp * HEAD_DIM, KEY_BLOCK), lambda b, h, i: (b, i, h, 0)),
            pl.BlockSpec((1, seq, hp * HEAD_DIM), lambda b, h, i: (b, 0, h)),
            pl.BlockSpec((1, nkb, hp * HEAD_DIM, KEY_BLOCK), lambda b, h, i: (b, 0, h, 0)),
        ],
        out_specs=pl.BlockSpec((1, tq, hp * HEAD_DIM), lambda b, h, i: (b, i, h)),
        out_shape=jax.ShapeDtypeStruct((batch, seq, C_HEADS * HEAD_DIM), BF16),
        scratch_shapes=[pltpu.VMEM((hp, KEY_BLOCK, tq), F32), pltpu.VMEM((hp, KEY_BLOCK, tq), F32),
                        pltpu.VMEM((hp, 1, tq), F32), pltpu.VMEM((hp, HEAD_DIM, tq), F32)],
        compiler_params=_params("arbitrary", "arbitrary", "arbitrary"),
        name="sb_attn",
    )(qc, kc, vct)


SEARCH_GROUP = 2


def _dsa_kernel(qit_ref, ki_ref, wt_ref, qat_ref, ka_ref, vt_ref, o_ref,
                keys_ref, half_ref, thr_ref, m_ref, l_ref, acc_ref, s0_ref, s1_ref,
                *, t, tq, topk, seq):
    qb = pl.program_id(1)
    sub = tq // t
    nk = (qb + 1) * sub
    row = lax.broadcasted_iota(I32, (t, tq), 0)
    col = lax.broadcasted_iota(I32, (t, tq), 1)
    wt = wt_ref[0]

    def score_block(kb):
        ks = pl.multiple_of(kb * t, t)
        kix = ki_ref[0, pl.ds(ks, t), :]
        sc = jnp.zeros((t, tq), F32)
        for h in range(IDX_HEADS):
            lg = jnp.concatenate(
                [jnp.dot(kix, qit_ref[0, c, h * IDX_DIM:(h + 1) * IDX_DIM, :], preferred_element_type=F32)
                 for c in range(sub)], axis=1)
            sc = sc + wt[h:h + 1, :] * jnp.maximum(lg, 0.0)
        bits = lax.bitcast_convert_type(sc, I32)
        key = bits ^ ((bits >> 31) & 0x7FFFFFFF)
        visible = ((ks + row) >> CHUNK_SHIFT) <= ((qb * tq + col) >> CHUNK_SHIFT)
        key = jnp.where(visible, key, INT_MIN)
        keys_ref[kb] = key
        half_ref[kb] = (key >> 16).astype(I16)

    def score_pair(j, carry):
        score_block(2 * j)
        score_block(2 * j + 1)
        return carry

    assert sub % 2 == 0
    lax.fori_loop(0, nk // 2, score_pair, 0)

    i16_min, i16_max = -2 ** 15, 2 ** 15 - 1
    for pad in range(SEARCH_GROUP - 1):
        half_ref[nk + pad] = jnp.full((t, tq), i16_min, I16)

    def search16():
        def count16(cand):
            cand16 = cand.astype(I16)

            def block_group(j, accs):
                out = []
                for c, acc in enumerate(accs):
                    parts = [acc]
                    for kb in range(SEARCH_GROUP):
                        hit = jnp.where(half_ref[SEARCH_GROUP * j + kb, :, c * t:(c + 1) * t] >= cand16[:, c * t:(c + 1) * t],
                                        jnp.ones((), I16), jnp.zeros((), I16))
                        parts += [hit[i * 16:(i + 1) * 16] for i in range(t // 16)]
                    while len(parts) > 1:
                        parts = [a + b for a, b in zip(parts[0::2], parts[1::2])] + parts[len(parts) & ~1:]
                    out.append(parts[0])
                return tuple(out)
            groups = (nk + SEARCH_GROUP - 1) // SEARCH_GROUP
            accs = lax.fori_loop(0, groups, block_group,
                                 tuple(jnp.zeros((16, t), I16) for _ in range(sub)))
            return jnp.concatenate([jnp.sum(a.astype(I32), axis=0, keepdims=True) for a in accs], axis=1)

        c0 = count16(jnp.zeros((1, tq), I32))
        v0 = jnp.where(c0 >= topk, 0, i16_min).astype(I32)
        n0 = jnp.where(c0 >= topk, c0, 0)

        def bit(b, carry):
            v, n = carry
            cand = v | lax.shift_left(jnp.int32(1), 14 - b)
            c = count16(cand)
            return jnp.where(c >= topk, cand, v), jnp.where(c >= topk, c, n)

        return lax.fori_loop(0, 15, bit, (v0, n0))

    def count(pred):
        def block(kb, acc):
            hit = jnp.where(pred(keys_ref[kb], kb * t + row), 1, 0).astype(I32)
            parts = [acc] + [hit[i * 8:(i + 1) * 8] for i in range(t // 8)]
            while len(parts) > 1:
                parts = [a + b for a, b in zip(parts[0::2], parts[1::2])] + parts[len(parts) & ~1:]
            return parts[0]
        acc = lax.fori_loop(0, nk, block, jnp.zeros((8, tq), I32))
        return jnp.sum(acc, axis=0, keepdims=True)

    thr_hi, n_hi = search16()

    def low_half_block(kb, carry):
        key = keys_ref[kb]
        hi = key >> 16
        low = (key & 0xFFFF) + i16_min
        half_ref[kb] = jnp.where(hi == thr_hi, low, jnp.where(hi > thr_hi, i16_max, i16_min)).astype(I16)
        return carry

    lax.fori_loop(0, nk, low_half_block, 0)
    thr_lo, n_lo = search16()
    thr = lax.shift_left(thr_hi, 16) + (thr_lo - i16_min)
    n_ge = jnp.where(thr_lo > i16_min, n_lo, n_hi)
    tie = (thr != INT_MIN) & (n_ge > topk)
    thr_ref[...] = jnp.maximum(thr, INT_MIN + 1)

    @pl.when(jnp.max(tie.astype(I32)) > 0)
    def _():
        nbits = seq.bit_length() - 1
        need = topk - count(lambda k, i: k > thr)

        def lim_bit(b, lim):
            cand = lim | lax.shift_left(jnp.int32(1), nbits - 1 - b)
            below = count(lambda k, i: (k == thr) & (i < cand))
            return jnp.where(below < need, cand, lim)

        lim = lax.fori_loop(0, nbits, lim_bit, jnp.zeros((1, tq), I32)) + 1
        lim = jnp.where(tie, lim, seq)

        def demote_block(kb, carry):
            key = keys_ref[kb]
            drop = (key == thr) & ((kb * t + row) >= lim)
            keys_ref[kb] = jnp.where(drop, key - 1, key)
            return carry

        lax.fori_loop(0, nk, demote_block, 0)

    m_ref[...] = jnp.full(m_ref.shape, NEG_INF, F32)
    l_ref[...] = jnp.zeros(l_ref.shape, F32)
    acc_ref[...] = jnp.zeros(acc_ref.shape, F32)

    def produce(s_ref, kb):
        ks = pl.multiple_of(kb * t, t)
        bias = jnp.where(keys_ref[kb] >= thr_ref[...], 0.0, NEG_INF)
        for h in range(A_HEADS):
            k = ka_ref[0, pl.ds(ks, t), h * HEAD_DIM:(h + 1) * HEAD_DIM]
            for c in range(sub):
                s_ref[h, :, c * t:(c + 1) * t] = jnp.dot(
                    k, qat_ref[0, c, h * HEAD_DIM:(h + 1) * HEAD_DIM, :],
                    preferred_element_type=F32) + bias[:, c * t:(c + 1) * t]

    def consume(s_ref, kb):
        for h in range(A_HEADS):
            lo = h * HEAD_DIM
            s = s_ref[h]
            m_prev = m_ref[h]
            m_new = jnp.maximum(m_prev, jnp.max(s, axis=0, keepdims=True))
            m_safe = jnp.where(m_new == NEG_INF, 0.0, m_new)
            p = jnp.exp2(s - m_safe)
            alpha = jnp.exp2(m_prev - m_safe)
            l_ref[h] = alpha * l_ref[h] + jnp.sum(p, axis=0, keepdims=True)
            acc_ref[h] = alpha * acc_ref[h] + jnp.dot(vt_ref[0, kb, lo:lo + HEAD_DIM, :], p.astype(BF16),
                                                      preferred_element_type=F32)
            m_ref[h] = m_new

    produce(s0_ref, 0)

    def attend_pair(j, carry):
        produce(s1_ref, 2 * j + 1)
        consume(s0_ref, 2 * j)
        produce(s0_ref, 2 * j + 2)
        consume(s1_ref, 2 * j + 1)
        return carry

    lax.fori_loop(0, nk // 2 - 1, attend_pair, 0)
    produce(s1_ref, nk - 1)
    consume(s0_ref, nk - 2)
    consume(s1_ref, nk - 1)

    for h in range(A_HEADS):
        o_ref[0, :, h * HEAD_DIM:(h + 1) * HEAD_DIM] = (acc_ref[h] / l_ref[h]).T.astype(BF16)


def _dsa(qi, ki, wt, qa, ka, vat, batch, seq, topk):
    t = KEY_BLOCK
    tq = min(512, seq)
    nt = seq // t
    wa = A_HEADS * HEAD_DIM
    return pl.pallas_call(
        functools.partial(_dsa_kernel, t=t, tq=tq, topk=topk, seq=seq),
        grid=(batch, seq // tq),
        in_specs=[
            pl.BlockSpec((1, tq // t, IDX_HEADS * IDX_DIM, t), lambda b, i: (b, i, 0, 0)),
            pl.BlockSpec((1, seq, IDX_DIM), lambda b, i: (b, 0, 0)),
            pl.BlockSpec((1, IDX_HEADS, tq), lambda b, i: (b, 0, i)),
            pl.BlockSpec((1, tq // t, wa, t), lambda b, i: (b, i, 0, 0)),
            pl.BlockSpec((1, seq, wa), lambda b, i: (b, 0, 0)),
            pl.BlockSpec((1, nt, wa, t), lambda b, i: (b, 0, 0, 0)),
        ],
        out_specs=pl.BlockSpec((1, tq, wa), lambda b, i: (b, i, 0)),
        out_shape=jax.ShapeDtypeStruct((batch, seq, wa), BF16),
        scratch_shapes=[
            pltpu.VMEM((nt, t, tq), I32), pltpu.VMEM((nt + SEARCH_GROUP - 1, t, tq), I16),
            pltpu.VMEM((1, tq), I32),
            pltpu.VMEM((A_HEADS, 1, tq), F32), pltpu.VMEM((A_HEADS, 1, tq), F32),
            pltpu.VMEM((A_HEADS, HEAD_DIM, tq), F32),
            pltpu.VMEM((A_HEADS, t, tq), F32), pltpu.VMEM((A_HEADS, t, tq), F32),
        ],
        compiler_params=_params("arbitrary", "arbitrary"),
        name="dsa_attn",
    )(qi, ki, wt, qa, ka, vat)


def _out_kernel(x_ref, mod_ref, oa_ref, ob_ref, oc_ref, w_ref, o_ref):
    wa = A_HEADS * HEAD_DIM
    wb = B_HEADS * HEAD_DIM
    mixed = (jnp.dot(oa_ref[...], w_ref[0:wa, :], preferred_element_type=F32)
             + jnp.dot(ob_ref[...], w_ref[wa:wa + wb, :], preferred_element_type=F32)
             + jnp.dot(oc_ref[...], w_ref[wa + wb:, :], preferred_element_type=F32))
    o_ref[...] = x_ref[...] + (1.0 + mod_ref[0, 5:6, :]) * mixed


def _out_proj(x2, mod, oa, ob, oc, w, seq):
    T, D = x2.shape
    tm = min(512, seq)
    per_seq = seq // tm
    return pl.pallas_call(
        _out_kernel,
        grid=(T // tm,),
        in_specs=[
            pl.BlockSpec((tm, D), lambda i: (i, 0)),
            pl.BlockSpec((1, N_MOD, D), lambda i: (i // per_seq, 0, 0)),
            pl.BlockSpec((tm, oa.shape[1]), lambda i: (i, 0)),
            pl.BlockSpec((tm, ob.shape[1]), lambda i: (i, 0)),
            pl.BlockSpec((tm, oc.shape[1]), lambda i: (i, 0)),
            pl.BlockSpec(w.shape, lambda i: (0, 0)),
        ],
        out_specs=pl.BlockSpec((tm, D), lambda i: (i, 0)),
        out_shape=jax.ShapeDtypeStruct((T, D), F32),
        compiler_params=_params("arbitrary"),
        name="out_proj",
    )(x2, mod, oa, ob, oc, w)


CAST_ROWS = 256


def _cast_kernel(w_ref, o_ref):
    o_ref[...] = w_ref[0].astype(BF16)


def _layer_bf16(w, layer):
    _, rows, cols = w.shape
    tr = min(CAST_ROWS, rows)
    return pl.pallas_call(
        _cast_kernel,
        grid=(rows // tr,),
        in_specs=[pl.BlockSpec((1, tr, cols), lambda i: (layer, i, 0))],
        out_specs=pl.BlockSpec((tr, cols), lambda i: (i, 0)),
        out_shape=jax.ShapeDtypeStruct((rows, cols), BF16),
        compiler_params=_params("arbitrary"),
        name="cast_bf16",
    )(w)


def _rope_tables(seq):
    def tables(dim):
        inv = 1.0 / (ROPE_THETA ** (jnp.arange(0, dim, 2, dtype=F32) / dim))
        ang = jnp.arange(seq, dtype=F32)[:, None] * inv[None, :]
        return jnp.cos(ang), jnp.sin(ang)

    def lane_tables(cos, sin, fill):
        half = cos.shape[1]
        rest = LANES - 2 * half
        zeros_h = jnp.zeros((seq, half), F32)
        zeros_r = jnp.zeros((seq, rest), F32)
        return [jnp.concatenate([cos, cos, jnp.full((seq, rest), fill, F32)], axis=1),
                jnp.concatenate([-sin, zeros_h, zeros_r], axis=1),
                jnp.concatenate([zeros_h, sin, zeros_r], axis=1)]

    cos_p, sin_p = tables(PARTIAL_ROPE_DIM)
    cos_m, sin_m = tables(MLA_ROPE)
    return jnp.stack(lane_tables(cos_p, sin_p, 1.0) + lane_tables(cos_m, sin_m, 0.0))


def _pad_cols(a, width):
    return jnp.pad(a, ((0, 0), (0, width - a.shape[1])))


def _layer_weights(w_in, w_uq, w_ukv):
    w_in_p = jnp.concatenate([
        w_in[:, 0:2560],
        w_in[:, 3280:4816],
        _pad_cols(w_in[:, 2640:3088], Q_RANK_PAD),
        w_in[:, 3088:3216],
        _pad_cols(w_in[:, 3216:3280], LANES),
        _pad_cols(w_in[:, 2560:2640], LANES),
    ], axis=1).astype(BF16)
    wuq = w_uq.reshape(MLA_Q_RANK, B_HEADS, MLA_NOPE + MLA_ROPE)
    wuq = jnp.pad(wuq, ((0, Q_RANK_PAD - MLA_Q_RANK), (0, 0), (0, 2 * LANES - MLA_NOPE - MLA_ROPE)))
    wuq = wuq.reshape(Q_RANK_PAD, B_HEADS * 2 * LANES).astype(BF16)
    wukv = w_ukv.reshape(MLA_KV_RANK, B_HEADS, MLA_NOPE + HEAD_DIM)
    wukv = jnp.concatenate([wukv[:, :, :MLA_NOPE].reshape(MLA_KV_RANK, -1),
                            wukv[:, :, MLA_NOPE:].reshape(MLA_KV_RANK, -1)], axis=1).astype(BF16)
    return w_in_p, wuq, wukv


def kernel(x, c, w_ada, b_ada, g_ffn1, w1_gate, w1_up, w1_down, g_mix, w_in, g_qa, g_ka, g_cq, g_ckv, w_uq, w_ukv, g_q_nope, g_k_nope, g_q_rope, g_k_rope, w_out, g_ffn2, w2_gate, w2_up, w2_down):
    batch, seq, d_model = x.shape
    depth = w_ada.shape[0]
    topk = min(TOPK_MAX, seq // 4)
    tabs = _rope_tables(seq)
    mods = _ada(c, w_ada, b_ada).reshape(depth, batch, N_MOD, d_model)
    x2 = x.reshape(batch * seq, d_model)

    for l in range(depth):
        mod = mods[l]
        x2 = _ffn(x2, mod, g_ffn1[l], _layer_bf16(w1_gate, l), _layer_bf16(w1_up, l),
                  _layer_bf16(w1_down, l), 0, seq)

        w_in_p, wuq, wukv = _layer_weights(w_in[l], w_uq[l], w_ukv[l])
        gains = [g_qa[l][None, :], g_ka[l][None, :], _pad_cols(g_cq[l][None, :], Q_RANK_PAD),
                 g_ckv[l][None, :], g_q_nope[l][None, :], g_k_nope[l][None, :],
                 _pad_cols(g_q_rope[l][None, :], LANES), _pad_cols(g_k_rope[l][None, :], LANES)]
        (qa, ka, vat, qi, ki, wt, qcat, kcat, vbt, qc, kc, vct) = _prep(
            x2, mod, g_mix[l], w_in_p, tabs, gains, wuq, wukv, batch, seq)

        out_a = _dsa(qi, ki, wt, qa, ka, vat, batch, seq, topk)
        out_b = _mla(qcat, kcat, vbt, batch, seq)
        out_c = _sb(qc, kc, vct, batch, seq)
        x2 = _out_proj(x2, mod, out_a.reshape(batch * seq, -1), out_b.reshape(batch * seq, -1),
                       out_c.reshape(batch * seq, -1), _layer_bf16(w_out, l), seq)

        x2 = _ffn(x2, mod, g_ffn2[l], _layer_bf16(w2_gate, l), _layer_bf16(w2_up, l),
                  _layer_bf16(w2_down, l), 6, seq)

    return x2.reshape(batch, seq, d_model)
```

```python
import functools

import jax
import jax.numpy as jnp
from jax import lax
from jax.experimental import pallas as pl
from jax.experimental.pallas import tpu as pltpu

F32 = jnp.float32
BF16 = jnp.bfloat16
I32 = jnp.int32
I16 = jnp.int16

HEAD_DIM = 128
CHUNK = 64
CHUNK_SHIFT = CHUNK.bit_length() - 1
ROPE_THETA = 500000.0
PARTIAL_ROPE_DIM = HEAD_DIM // 4
NORM_EPS = 1e-6
N_MOD = 9
A_HEADS = 4
IDX_HEADS = 16
IDX_DIM = 64
TOPK_MAX = 256
B_HEADS = 8
MLA_Q_RANK = 448
MLA_KV_RANK = 128
MLA_NOPE = 128
MLA_ROPE = 64
C_HEADS = 4

LANES = 128
KEY_BLOCK = 256
Q_RANK_PAD = 512
VMEM_LIMIT = 56 * 1024 * 1024
FFN_VMEM_LIMIT = 61 * 1024 * 1024

OFF_QA, OFF_KA, OFF_VA, OFF_QI = 0, 512, 1024, 1536
OFF_QC, OFF_KC, OFF_VC = 2560, 3072, 3584
OFF_CQ, OFF_CKV, OFF_KR, OFF_KIW = 4096, 4608, 4736, 4864
N_PROJ = 4992

LOG2E = 1.4426950408889634
A_SCALE = HEAD_DIM ** -0.5 * LOG2E
B_SCALE = (MLA_NOPE + MLA_ROPE) ** -0.5 * LOG2E
C_SCALE = HEAD_DIM ** -0.5
IDX_SCALE = (IDX_DIM ** -0.5) * (IDX_HEADS ** -0.5)

NEG_INF = float("-inf")
INT_MIN = -2 ** 31


def _params(*sem, vmem_limit=VMEM_LIMIT):
    return pltpu.CompilerParams(dimension_semantics=sem, vmem_limit_bytes=vmem_limit)


def _ada_kernel(c_ref, w_ref, b_ref, o_ref):
    c = c_ref[...]
    ca = (c * jax.nn.sigmoid(c)).astype(BF16)
    o_ref[0] = jnp.dot(ca, w_ref[0].astype(BF16), preferred_element_type=F32) + b_ref[0]


def _ada(c, w_ada, b_ada):
    L, D, N = w_ada.shape
    B = c.shape[0]
    tn = 2048
    return pl.pallas_call(
        _ada_kernel,
        grid=(L, N // tn),
        in_specs=[
            pl.BlockSpec((B, D), lambda l, j: (0, 0)),
            pl.BlockSpec((1, D, tn), lambda l, j: (l, 0, j)),
            pl.BlockSpec((1, 1, tn), lambda l, j: (l, 0, j)),
        ],
        out_specs=pl.BlockSpec((1, B, tn), lambda l, j: (l, 0, j)),
        out_shape=jax.ShapeDtypeStruct((L, B, N), F32),
        compiler_params=_params("arbitrary", "arbitrary"),
        name="ada_mod",
    )(c, w_ada, b_ada.reshape(L, 1, N))


NORM_ROWS = 16


def _norm_mod_into(h_ref, x_ref, g, shift, scale):
    gain = g * (1.0 + scale)

    def rows(r, carry):
        sl = pl.ds(pl.multiple_of(r * NORM_ROWS, NORM_ROWS), NORM_ROWS)
        x = x_ref[sl, :]
        y = x * lax.rsqrt(jnp.mean(x * x, axis=-1, keepdims=True) + NORM_EPS)
        h_ref[sl, :] = (y * gain + shift).astype(BF16)
        return carry

    lax.fori_loop(0, x_ref.shape[0] // NORM_ROWS, rows, 0, unroll=8)


def _ffn_kernel(x_ref, mod_ref, g_ref, wg_ref, wu_ref, wd_ref, o_ref, h_ref, *, row):
    j = pl.program_id(1)

    @pl.when(j == 0)
    def _():
        _norm_mod_into(h_ref, x_ref, g_ref[...], mod_ref[0, row:row + 1, :], mod_ref[0, row + 1:row + 2, :])
        o_ref[...] = jnp.zeros(o_ref.shape, F32)

    h = h_ref[...]
    g = jnp.dot(h, wg_ref[...], preferred_element_type=F32)
    u = jnp.dot(h, wu_ref[...], preferred_element_type=F32)
    a = ((g * jax.nn.sigmoid(g)) * u).astype(BF16)
    o_ref[...] += jnp.dot(a, wd_ref[...], preferred_element_type=F32)

    @pl.when(j == pl.num_programs(1) - 1)
    def _():
        gate = mod_ref[0, row + 2:row + 3, :]
        o_ref[...] = x_ref[...] + (0.5 * (1.0 + gate)) * o_ref[...]


def _ffn(x2, mod, g, wg, wu, wd, row, seq):
    T, D = x2.shape
    F = wg.shape[1]
    tm = min(1024, seq)
    tf = 512
    per_seq = seq // tm
    return pl.pallas_call(
        functools.partial(_ffn_kernel, row=row),
        grid=(T // tm, F // tf),
        in_specs=[
            pl.BlockSpec((tm, D), lambda i, j: (i, 0)),
            pl.BlockSpec((1, N_MOD, D), lambda i, j: (i // per_seq, 0, 0)),
            pl.BlockSpec((1, D), lambda i, j: (0, 0)),
            pl.BlockSpec((D, tf), lambda i, j: (0, j)),
            pl.BlockSpec((D, tf), lambda i, j: (0, j)),
            pl.BlockSpec((tf, D), lambda i, j: (j, 0)),
        ],
        out_specs=pl.BlockSpec((tm, D), lambda i, j: (i, 0)),
        out_shape=jax.ShapeDtypeStruct((T, D), F32),
        scratch_shapes=[pltpu.VMEM((tm, D), BF16)],
        compiler_params=_params("arbitrary", "arbitrary", vmem_limit=FFN_VMEM_LIMIT),
        name="ffn",
    )(x2, mod, g.reshape(1, D), wg, wu, wd)


def _rms_lanes(x, g, n):
    return x * lax.rsqrt(jnp.sum(x * x, axis=-1, keepdims=True) / n + NORM_EPS) * g


def _rope_lanes(x, cos, sin_lo, sin_hi, half):
    return (x * cos + pltpu.roll(x, LANES - half, 1) * sin_lo + pltpu.roll(x, half, 1) * sin_hi)


def _prep_kernel(x_ref, mod_ref, gmix_ref, win_ref, tab_ref,
                 gqa_ref, gka_ref, gcq_ref, gckv_ref, gqn_ref, gkn_ref, gqr_ref, gkr_ref,
                 wuq_ref, wukv_ref,
                 qa_o, ka_o, vat_o, qi_o, ki_o, wt_o, qcat_o, kcat_o, vbt_o, qc_o, kc_o, vct_o, h_ref, t_ref):
    ca, sa_lo, sa_hi = tab_ref[0], tab_ref[1], tab_ref[2]
    cm, sm_lo, sm_hi = tab_ref[3], tab_ref[4], tab_ref[5]
    half_a = PARTIAL_ROPE_DIM // 2
    half_m = MLA_ROPE // 2
    wa, wc = A_HEADS * HEAD_DIM, C_HEADS * HEAD_DIM

    _norm_mod_into(h_ref, x_ref, gmix_ref[...], mod_ref[0, 3:4, :], mod_ref[0, 4:5, :])

    def transposed(v):
        t_ref[:, 0:v.shape[1]] = v
        return t_ref[:, 0:v.shape[1]].T

    def proj(lo, width):
        return jnp.dot(h_ref[...], win_ref[:, lo:lo + width], preferred_element_type=F32)

    cq = proj(OFF_CQ, Q_RANK_PAD)
    small = proj(OFF_CKV, N_PROJ - OFF_CKV)
    qa = proj(OFF_QA, wa)
    ka = proj(OFF_KA, wa)
    cq = _rms_lanes(cq, gcq_ref[...], MLA_Q_RANK)
    ckv = _rms_lanes(small[:, 0:MLA_KV_RANK], gckv_ref[...], MLA_KV_RANK)
    qb = jnp.dot(cq.astype(BF16), wuq_ref[...], preferred_element_type=F32)
    kvb = jnp.dot(ckv.astype(BF16), wukv_ref[...], preferred_element_type=F32)
    va = proj(OFF_VA, wa)
    qi = proj(OFF_QI, IDX_HEADS * IDX_DIM)
    qc = proj(OFF_QC, wc)
    kc = proj(OFF_KC, wc)
    vc = proj(OFF_VC, wc)

    for h in range(A_HEADS):
        lo = h * HEAD_DIM
        q = _rms_lanes(qa[:, lo:lo + HEAD_DIM], gqa_ref[...], HEAD_DIM)
        t_ref[:, lo:lo + HEAD_DIM] = _rope_lanes(q, ca, sa_lo, sa_hi, half_a) * A_SCALE
    qa_o[0, 0] = t_ref[:, 0:wa].T.astype(BF16)
    for h in range(A_HEADS):
        lo = h * HEAD_DIM
        k = _rms_lanes(ka[:, lo:lo + HEAD_DIM], gka_ref[...], HEAD_DIM)
        ka_o[0, :, lo:lo + HEAD_DIM] = _rope_lanes(k, ca, sa_lo, sa_hi, half_a).astype(BF16)
    vat_o[0, 0] = transposed(va).astype(BF16)
    qi_o[0, 0] = transposed(qi).astype(BF16)

    qc_o[0, 0] = transposed(qc * C_SCALE).astype(BF16)
    kc_o[0] = kc.astype(BF16)
    vct_o[0, 0] = transposed(vc).astype(BF16)

    kr = _rms_lanes(small[:, OFF_KR - OFF_CKV:OFF_KR - OFF_CKV + LANES], gkr_ref[...], MLA_ROPE)
    kr = _rope_lanes(kr, cm, sm_lo, sm_hi, half_m).astype(BF16)
    kiw = small[:, OFF_KIW - OFF_CKV:OFF_KIW - OFF_CKV + LANES]
    ki_o[0] = kiw[:, :IDX_DIM].astype(BF16)
    wt_o[0] = transposed(kiw)[IDX_DIM:IDX_DIM + IDX_HEADS, :] * IDX_SCALE
    for h in range(B_HEADS):
        lo = h * 2 * LANES
        kn = _rms_lanes(kvb[:, h * MLA_NOPE:(h + 1) * MLA_NOPE], gkn_ref[...], MLA_NOPE)
        kcat_o[0, :, lo:lo + MLA_NOPE] = kn.astype(BF16)
        kcat_o[0, :, lo + MLA_NOPE:lo + 2 * LANES] = kr
    vbt_o[0, 0] = transposed(kvb[:, B_HEADS * MLA_NOPE:]).astype(BF16)
    for h in range(B_HEADS):
        lo = h * 2 * LANES
        qn = _rms_lanes(qb[:, lo:lo + MLA_NOPE], gqn_ref[...], MLA_NOPE)
        t_ref[:, lo:lo + MLA_NOPE] = qn * B_SCALE
        qr = _rms_lanes(qb[:, lo + MLA_NOPE:lo + 2 * LANES], gqr_ref[...], MLA_ROPE)
        t_ref[:, lo + MLA_NOPE:lo + 2 * LANES] = _rope_lanes(qr, cm, sm_lo, sm_hi, half_m) * B_SCALE
    qcat_o[0, 0] = t_ref[...].T.astype(BF16)


def _resident(shape):
    return pl.BlockSpec(shape, lambda b, i: (0,) * len(shape), pipeline_mode=pl.Buffered(1))


def _prep(x2, mod, g_mix, w_in_p, tabs, gains, wuq, wukv, batch, seq):
    tm = KEY_BLOCK
    nt = seq // tm
    d_model = x2.shape[1]
    wa, wb, wc = A_HEADS * HEAD_DIM, B_HEADS * HEAD_DIM, C_HEADS * HEAD_DIM
    wcat = B_HEADS * 2 * LANES

    def tok(width, dtype):
        return (jax.ShapeDtypeStruct((batch, seq, width), dtype),
                pl.BlockSpec((1, tm, width), lambda b, i: (b, i, 0)))

    def tok_t(width):
        return (jax.ShapeDtypeStruct((batch, nt, width, tm), BF16),
                pl.BlockSpec((1, 1, width, tm), lambda b, i: (b, i, 0, 0)))

    outs = [tok_t(wa), tok(wa, BF16), tok_t(wa), tok_t(IDX_HEADS * IDX_DIM), tok(IDX_DIM, BF16),
            (jax.ShapeDtypeStruct((batch, IDX_HEADS, seq), F32),
             pl.BlockSpec((1, IDX_HEADS, tm), lambda b, i: (b, 0, i))),
            tok_t(wcat), tok(wcat, BF16), tok_t(wb), tok_t(wc), tok(wc, BF16), tok_t(wc)]
    return pl.pallas_call(
        _prep_kernel,
        grid=(batch, nt),
        in_specs=[pl.BlockSpec((tm, d_model), lambda b, i: (b * nt + i, 0)),
                  pl.BlockSpec((1, N_MOD, d_model), lambda b, i: (b, 0, 0)),
                  _resident((1, d_model)),
                  _resident(w_in_p.shape),
                  pl.BlockSpec((6, tm, LANES), lambda b, i: (0, i, 0))]
                 + [_resident(g.shape) for g in gains]
                 + [_resident(wuq.shape), _resident(wukv.shape)],
        out_specs=[o[1] for o in outs],
        out_shape=[o[0] for o in outs],
        scratch_shapes=[pltpu.VMEM((tm, d_model), BF16), pltpu.VMEM((tm, wcat), F32)],
        compiler_params=_params("arbitrary", "arbitrary"),
        name="head_prep",
    )(x2, mod, g_mix.reshape(1, d_model), w_in_p, tabs, *gains, wuq, wukv)


MLA_HEADS_PER_STEP = 4


def _mla_kernel(qt_ref, k_ref, vt_ref, o_ref, s0_ref, s1_ref, m_ref, l_ref, acc_ref, *, t):
    qb = pl.program_id(2)
    sub = t // KEY_BLOCK
    dq = 2 * LANES
    m_ref[...] = jnp.full(m_ref.shape, NEG_INF, F32)
    l_ref[...] = jnp.zeros(l_ref.shape, F32)
    acc_ref[...] = jnp.zeros(acc_ref.shape, F32)

    def produce(s_ref, kb):
        ks = pl.multiple_of(kb * t, t)
        for h in range(MLA_HEADS_PER_STEP):
            k = k_ref[0, pl.ds(ks, t), h * dq:(h + 1) * dq]
            for c in range(sub):
                s_ref[h, :, c * KEY_BLOCK:(c + 1) * KEY_BLOCK] = jnp.dot(
                    k, qt_ref[0, c, h * dq:(h + 1) * dq, :], preferred_element_type=F32)

    def consume(s_ref, kb, diagonal):
        for h in range(MLA_HEADS_PER_STEP):
            s = s_ref[h]
            if diagonal:
                kc = lax.broadcasted_iota(I32, (t, t), 0) >> CHUNK_SHIFT
                qc = lax.broadcasted_iota(I32, (t, t), 1) >> CHUNK_SHIFT
                s = jnp.where(kc <= qc, s, NEG_INF)
            m_prev = m_ref[h]
            m_new = jnp.maximum(m_prev, jnp.max(s, axis=0, keepdims=True))
            p = jnp.exp2(s - m_new)
            alpha = jnp.exp2(m_prev - m_new)
            l_ref[h] = alpha * l_ref[h] + jnp.sum(p, axis=0, keepdims=True)
            p = p.astype(BF16)
            pv = jnp.dot(vt_ref[0, kb * sub, h * HEAD_DIM:(h + 1) * HEAD_DIM, :], p[0:KEY_BLOCK],
                         preferred_element_type=F32)
            for c in range(1, sub):
                pv += jnp.dot(vt_ref[0, kb * sub + c, h * HEAD_DIM:(h + 1) * HEAD_DIM, :],
                              p[c * KEY_BLOCK:(c + 1) * KEY_BLOCK], preferred_element_type=F32)
            acc_ref[h] = alpha * acc_ref[h] + pv
            m_ref[h] = m_new

    produce(s0_ref, 0)

    def body(j, carry):
        produce(s1_ref, 2 * j + 1)
        consume(s0_ref, 2 * j, False)
        produce(s0_ref, 2 * j + 2)
        consume(s1_ref, 2 * j + 1, False)
        return carry

    lax.fori_loop(0, qb // 2, body, 0)

    @pl.when(qb % 2 == 0)
    def _():
        consume(s0_ref, qb, True)

    @pl.when(qb % 2 == 1)
    def _():
        produce(s1_ref, qb)
        consume(s0_ref, qb - 1, False)
        consume(s1_ref, qb, True)

    for h in range(MLA_HEADS_PER_STEP):
        o_ref[0, :, h * HEAD_DIM:(h + 1) * HEAD_DIM] = (acc_ref[h] / l_ref[h]).T.astype(BF16)


def _mla(qcat, kcat, vbt, batch, seq):
    t = min(512, seq)
    hp = MLA_HEADS_PER_STEP
    nkb = seq // KEY_BLOCK
    return pl.pallas_call(
        functools.partial(_mla_kernel, t=t),
        grid=(batch, B_HEADS // hp, seq // t),
        in_specs=[
            pl.BlockSpec((1, t // KEY_BLOCK, hp * 2 * LANES, KEY_BLOCK), lambda b, h, i: (b, i, h, 0)),
            pl.BlockSpec((1, seq, hp * 2 * LANES), lambda b, h, i: (b, 0, h)),
            pl.BlockSpec((1, nkb, hp * HEAD_DIM, KEY_BLOCK), lambda b, h, i: (b, 0, h, 0)),
        ],
        out_specs=pl.BlockSpec((1, t, hp * HEAD_DIM), lambda b, h, i: (b, i, h)),
        out_shape=jax.ShapeDtypeStruct((batch, seq, B_HEADS * HEAD_DIM), BF16),
        scratch_shapes=[pltpu.VMEM((hp, t, t), F32), pltpu.VMEM((hp, t, t), F32),
                        pltpu.VMEM((hp, 1, t), F32), pltpu.VMEM((hp, 1, t), F32),
                        pltpu.VMEM((hp, HEAD_DIM, t), F32)],
        compiler_params=_params("arbitrary", "arbitrary", "arbitrary"),
        name="mla_attn",
    )(qcat, kcat, vbt)


SB_HEADS_PER_STEP = 4


def _sb_kernel(qt_ref, k_ref, vt_ref, o_ref, z0_ref, z1_ref, r_ref, acc_ref, *, tq):
    qb = pl.program_id(2)
    tk = KEY_BLOCK
    sub = tq // tk
    assert sub % 2 == 0
    tri_r = lax.broadcasted_iota(I32, (tk, tk), 0)
    tri_c = lax.broadcasted_iota(I32, (tk, tk), 1)
    later_keys = jnp.where(tri_c > tri_r, 1.0, 0.0).astype(BF16)
    r_ref[...] = jnp.zeros(r_ref.shape, F32)
    acc_ref[...] = jnp.zeros(acc_ref.shape, F32)

    heads = range(SB_HEADS_PER_STEP)

    def produce(z_ref, kb):
        ks = pl.multiple_of(kb * tk, tk)
        for h in heads:
            k = k_ref[0, pl.ds(ks, tk), h * HEAD_DIM:(h + 1) * HEAD_DIM]
            for c in range(sub):
                z_ref[h, :, c * tk:(c + 1) * tk] = jnp.dot(
                    k, qt_ref[0, c, h * HEAD_DIM:(h + 1) * HEAD_DIM, :], preferred_element_type=F32)

    def consume(z_ref, kb, key_offset):
        if key_offset is not None:
            before = (lax.broadcasted_iota(I32, (tk, tq), 0) + key_offset
                      < lax.broadcasted_iota(I32, (tk, tq), 1))
        zs = [z_ref[h] for h in heads]
        go = []
        for z in zs:
            neg_abs = lax.bitcast_convert_type(lax.bitcast_convert_type(z, I32) | INT_MIN, F32)
            g = jnp.maximum(z, 0.0) + jnp.log(1.0 + jnp.exp(neg_abs))
            if key_offset is not None:
                g = jnp.where(before, g, 0.0)
            go.append(g)
        later = [jnp.dot(later_keys, g.astype(BF16), preferred_element_type=F32) for g in go]
        for h in heads:
            w = jnp.exp(zs[h] - (go[h] + later[h] + r_ref[h]))
            if key_offset is not None:
                w = jnp.where(before, w, 0.0)
            acc_ref[h] += jnp.dot(vt_ref[0, kb, h * HEAD_DIM:(h + 1) * HEAD_DIM, :], w.astype(BF16),
                                  preferred_element_type=F32)
            r_ref[h] += jnp.sum(go[h], axis=0, keepdims=True)

    first = qb * sub
    bufs = (z0_ref, z1_ref)
    produce(bufs[0], first + sub - 1)
    for i, d in enumerate(reversed(range(sub))):
        if d > 0:
            produce(bufs[(i + 1) % 2], first + d - 1)
        else:
            @pl.when(qb > 0)
            def _():
                produce(bufs[(i + 1) % 2], first - 1)
        consume(bufs[i % 2], first + d, d * tk)

    def body(j, carry):
        top = first - 1 - 2 * j
        produce(z1_ref, top - 1)
        consume(z0_ref, top, None)
        produce(z0_ref, top - 2)
        consume(z1_ref, top - 1, None)
        return carry

    lax.fori_loop(0, first // 2 - 1, body, 0)

    @pl.when(qb > 0)
    def _():
        produce(z1_ref, 0)
        consume(z0_ref, 1, None)
        consume(z1_ref, 0, None)

    for h in range(SB_HEADS_PER_STEP):
        o_ref[0, :, h * HEAD_DIM:(h + 1) * HEAD_DIM] = acc_ref[h].T.astype(BF16)


def _sb(qc, kc, vct, batch, seq):
    tq = min(512, seq)
    hp = SB_HEADS_PER_STEP
    nkb = seq // KEY_BLOCK
    return pl.pallas_call(
        functools.partial(_sb_kernel, tq=tq),
        grid=(batch, C_HEADS // hp, seq // tq),
        in_specs=[
            pl.BlockSpec((1, tq // KEY_BLOCK, hp * HEAD_DIM, KEY_BLOCK), lambda b, h, i: (b, i, h, 0)),
            pl.BlockSpec((1, seq, hp * HEAD_DIM), lambda b, h, i: (b, 0, h)),
            pl.BlockSpec((1, nkb, hp * HEAD_DIM, KEY_BLOCK), lambda b, h, i: (b, 0, h, 0)),
        ],
        out_specs=pl.BlockSpec((1, tq, hp * HEAD_DIM), lambda b, h, i: (b, i, h)),
        out_shape=jax.ShapeDtypeStruct((batch, seq, C_HEADS * HEAD_DIM), BF16),
        scratch_shapes=[pltpu.VMEM((hp, KEY_BLOCK, tq), F32), pltpu.VMEM((hp, KEY_BLOCK, tq), F32),
                        pltpu.VMEM((hp, 1, tq), F32), pltpu.VMEM((hp, HEAD_DIM, tq), F32)],
        compiler_params=_params("arbitrary", "arbitrary", "arbitrary"),
        name="sb_attn",
    )(qc, kc, vct)


SEARCH_GROUP = 2


def _dsa_kernel(qit_ref, ki_ref, wt_ref, qat_ref, ka_ref, vt_ref, o_ref,
                keys_ref, half_ref, thr_ref, m_ref, l_ref, acc_ref, s0_ref, s1_ref,
                *, t, tq, topk, seq):
    qb = pl.program_id(1)
    sub = tq // t
    nk = (qb + 1) * sub
    row = lax.broadcasted_iota(I32, (t, tq), 0)
    col = lax.broadcasted_iota(I32, (t, tq), 1)
    wt = wt_ref[0]

    def score_block(kb):
        ks = pl.multiple_of(kb * t, t)
        kix = ki_ref[0, pl.ds(ks, t), :]
        sc = jnp.zeros((t, tq), F32)
        for h in range(IDX_HEADS):
            lg = jnp.concatenate(
                [jnp.dot(kix, qit_ref[0, c, h * IDX_DIM:(h + 1) * IDX_DIM, :], preferred_element_type=F32)
                 for c in range(sub)], axis=1)
            sc = sc + wt[h:h + 1, :] * jnp.maximum(lg, 0.0)
        bits = lax.bitcast_convert_type(sc, I32)
        key = bits ^ ((bits >> 31) & 0x7FFFFFFF)
        visible = ((ks + row) >> CHUNK_SHIFT) <= ((qb * tq + col) >> CHUNK_SHIFT)
        key = jnp.where(visible, key, INT_MIN)
        keys_ref[kb] = key
        half_ref[kb] = (key >> 16).astype(I16)

    def score_pair(j, carry):
        score_block(2 * j)
        score_block(2 * j + 1)
        return carry

    assert sub % 2 == 0
    lax.fori_loop(0, nk // 2, score_pair, 0)

    i16_min, i16_max = -2 ** 15, 2 ** 15 - 1
    for pad in range(SEARCH_GROUP - 1):
        half_ref[nk + pad] = jnp.full((t, tq), i16_min, I16)

    def search16():
        def count16(cand):
            cand16 = cand.astype(I16)

            def block_group(j, accs):
                out = []
                for c, acc in enumerate(accs):
                    parts = [acc]
                    for kb in range(SEARCH_GROUP):
                        hit = jnp.where(half_ref[SEARCH_GROUP * j + kb, :, c * t:(c + 1) * t] >= cand16[:, c * t:(c + 1) * t],
                                        jnp.ones((), I16), jnp.zeros((), I16))
                        parts += [hit[i * 16:(i + 1) * 16] for i in range(t // 16)]
                    while len(parts) > 1:
                        parts = [a + b for a, b in zip(parts[0::2], parts[1::2])] + parts[len(parts) & ~1:]
                    out.append(parts[0])
                return tuple(out)
            groups = (nk + SEARCH_GROUP - 1) // SEARCH_GROUP
            accs = lax.fori_loop(0, groups, block_group,
                                 tuple(jnp.zeros((16, t), I16) for _ in range(sub)))
            return jnp.concatenate([jnp.sum(a.astype(I32), axis=0, keepdims=True) for a in accs], axis=1)

        c0 = count16(jnp.zeros((1, tq), I32))
        v0 = jnp.where(c0 >= topk, 0, i16_min).astype(I32)
        n0 = jnp.where(c0 >= topk, c0, 0)

        def bit(b, carry):
            v, n = carry
            cand = v | lax.shift_left(jnp.int32(1), 14 - b)
            c = count16(cand)
            return jnp.where(c >= topk, cand, v), jnp.where(c >= topk, c, n)

        return lax.fori_loop(0, 15, bit, (v0, n0))

    def count(pred):
        def block(kb, acc):
            hit = jnp.where(pred(keys_ref[kb], kb * t + row), 1, 0).astype(I32)
            parts = [acc] + [hit[i * 8:(i + 1) * 8] for i in range(t // 8)]
            while len(parts) > 1:
                parts = [a + b for a, b in zip(parts[0::2], parts[1::2])] + parts[len(parts) & ~1:]
            return parts[0]
        acc = lax.fori_loop(0, nk, block, jnp.zeros((8, tq), I32))
        return jnp.sum(acc, axis=0, keepdims=True)

    thr_hi, n_hi = search16()

    def low_half_block(kb, carry):
        key = keys_ref[kb]
        hi = key >> 16
        low = (key & 0xFFFF) + i16_min
        half_ref[kb] = jnp.where(hi == thr_hi, low, jnp.where(hi > thr_hi, i16_max, i16_min)).astype(I16)
        return carry

    lax.fori_loop(0, nk, low_half_block, 0)
    thr_lo, n_lo = search16()
    thr = lax.shift_left(thr_hi, 16) + (thr_lo - i16_min)
    n_ge = jnp.where(thr_lo > i16_min, n_lo, n_hi)
    tie = (thr != INT_MIN) & (n_ge > topk)
    thr_ref[...] = jnp.maximum(thr, INT_MIN + 1)

    @pl.when(jnp.max(tie.astype(I32)) > 0)
    def _():
        nbits = seq.bit_length() - 1
        need = topk - count(lambda k, i: k > thr)

        def lim_bit(b, lim):
            cand = lim | lax.shift_left(jnp.int32(1), nbits - 1 - b)
            below = count(lambda k, i: (k == thr) & (i < cand))
            return jnp.where(below < need, cand, lim)

        lim = lax.fori_loop(0, nbits, lim_bit, jnp.zeros((1, tq), I32)) + 1
        lim = jnp.where(tie, lim, seq)

        def demote_block(kb, carry):
            key = keys_ref[kb]
            drop = (key == thr) & ((kb * t + row) >= lim)
            keys_ref[kb] = jnp.where(drop, key - 1, key)
            return carry

        lax.fori_loop(0, nk, demote_block, 0)

    m_ref[...] = jnp.full(m_ref.shape, NEG_INF, F32)
    l_ref[...] = jnp.zeros(l_ref.shape, F32)
    acc_ref[...] = jnp.zeros(acc_ref.shape, F32)

    def produce(s_ref, kb):
        ks = pl.multiple_of(kb * t, t)
        bias = jnp.where(keys_ref[kb] >= thr_ref[...], 0.0, NEG_INF)
        for h in range(A_HEADS):
            k = ka_ref[0, pl.ds(ks, t), h * HEAD_DIM:(h + 1) * HEAD_DIM]
            for c in range(sub):
                s_ref[h, :, c * t:(c + 1) * t] = jnp.dot(
                    k, qat_ref[0, c, h * HEAD_DIM:(h + 1) * HEAD_DIM, :],
                    preferred_element_type=F32) + bias[:, c * t:(c + 1) * t]

    def consume(s_ref, kb):
        for h in range(A_HEADS):
            lo = h * HEAD_DIM
            s = s_ref[h]
            m_prev = m_ref[h]
            m_new = jnp.maximum(m_prev, jnp.max(s, axis=0, keepdims=True))
            m_safe = jnp.where(m_new == NEG_INF, 0.0, m_new)
            p = jnp.exp2(s - m_safe)
            alpha = jnp.exp2(m_prev - m_safe)
            l_ref[h] = alpha * l_ref[h] + jnp.sum(p, axis=0, keepdims=True)
            acc_ref[h] = alpha * acc_ref[h] + jnp.dot(vt_ref[0, kb, lo:lo + HEAD_DIM, :], p.astype(BF16),
                                                      preferred_element_type=F32)
            m_ref[h] = m_new

    produce(s0_ref, 0)

    def attend_pair(j, carry):
        produce(s1_ref, 2 * j + 1)
        consume(s0_ref, 2 * j)
        produce(s0_ref, 2 * j + 2)
        consume(s1_ref, 2 * j + 1)
        return carry

    lax.fori_loop(0, nk // 2 - 1, attend_pair, 0)
    produce(s1_ref, nk - 1)
    consume(s0_ref, nk - 2)
    consume(s1_ref, nk - 1)

    for h in range(A_HEADS):
        o_ref[0, :, h * HEAD_DIM:(h + 1) * HEAD_DIM] = (acc_ref[h] / l_ref[h]).T.astype(BF16)


def _dsa(qi, ki, wt, qa, ka, vat, batch, seq, topk):
    t = KEY_BLOCK
    tq = min(512, seq)
    nt = seq // t
    wa = A_HEADS * HEAD_DIM
    return pl.pallas_call(
        functools.partial(_dsa_kernel, t=t, tq=tq, topk=topk, seq=seq),
        grid=(batch, seq // tq),
        in_specs=[
            pl.BlockSpec((1, tq // t, IDX_HEADS * IDX_DIM, t), lambda b, i: (b, i, 0, 0)),
            pl.BlockSpec((1, seq, IDX_DIM), lambda b, i: (b, 0, 0)),
            pl.BlockSpec((1, IDX_HEADS, tq), lambda b, i: (b, 0, i)),
            pl.BlockSpec((1, tq // t, wa, t), lambda b, i: (b, i, 0, 0)),
            pl.BlockSpec((1, seq, wa), lambda b, i: (b, 0, 0)),
            pl.BlockSpec((1, nt, wa, t), lambda b, i: (b, 0, 0, 0)),
        ],
        out_specs=pl.BlockSpec((1, tq, wa), lambda b, i: (b, i, 0)),
        out_shape=jax.ShapeDtypeStruct((batch, seq, wa), BF16),
        scratch_shapes=[
            pltpu.VMEM((nt, t, tq), I32), pltpu.VMEM((nt + SEARCH_GROUP - 1, t, tq), I16),
            pltpu.VMEM((1, tq), I32),
            pltpu.VMEM((A_HEADS, 1, tq), F32), pltpu.VMEM((A_HEADS, 1, tq), F32),
            pltpu.VMEM((A_HEADS, HEAD_DIM, tq), F32),
            pltpu.VMEM((A_HEADS, t, tq), F32), pltpu.VMEM((A_HEADS, t, tq), F32),
        ],
        compiler_params=_params("arbitrary", "arbitrary"),
        name="dsa_attn",
    )(qi, ki, wt, qa, ka, vat)


def _out_kernel(x_ref, mod_ref, oa_ref, ob_ref, oc_ref, w_ref, o_ref):
    wa = A_HEADS * HEAD_DIM
    wb = B_HEADS * HEAD_DIM
    mixed = (jnp.dot(oa_ref[...], w_ref[0:wa, :], preferred_element_type=F32)
             + jnp.dot(ob_ref[...], w_ref[wa:wa + wb, :], preferred_element_type=F32)
             + jnp.dot(oc_ref[...], w_ref[wa + wb:, :], preferred_element_type=F32))
    o_ref[...] = x_ref[...] + (1.0 + mod_ref[0, 5:6, :]) * mixed


def _out_proj(x2, mod, oa, ob, oc, w, seq):
    T, D = x2.shape
    tm = min(512, seq)
    per_seq = seq // tm
    return pl.pallas_call(
        _out_kernel,
        grid=(T // tm,),
        in_specs=[
            pl.BlockSpec((tm, D), lambda i: (i, 0)),
            pl.BlockSpec((1, N_MOD, D), lambda i: (i // per_seq, 0, 0)),
            pl.BlockSpec((tm, oa.shape[1]), lambda i: (i, 0)),
            pl.BlockSpec((tm, ob.shape[1]), lambda i: (i, 0)),
            pl.BlockSpec((tm, oc.shape[1]), lambda i: (i, 0)),
            pl.BlockSpec(w.shape, lambda i: (0, 0)),
        ],
        out_specs=pl.BlockSpec((tm, D), lambda i: (i, 0)),
        out_shape=jax.ShapeDtypeStruct((T, D), F32),
        compiler_params=_params("arbitrary"),
        name="out_proj",
    )(x2, mod, oa, ob, oc, w)


CAST_ROWS = 256


def _cast_kernel(w_ref, o_ref):
    o_ref[...] = w_ref[0].astype(BF16)


def _layer_bf16(w, layer):
    _, rows, cols = w.shape
    tr = min(CAST_ROWS, rows)
    return pl.pallas_call(
        _cast_kernel,
        grid=(rows // tr,),
        in_specs=[pl.BlockSpec((1, tr, cols), lambda i: (layer, i, 0))],
        out_specs=pl.BlockSpec((tr, cols), lambda i: (i, 0)),
        out_shape=jax.ShapeDtypeStruct((rows, cols), BF16),
        compiler_params=_params("arbitrary"),
        name="cast_bf16",
    )(w)


def _rope_tables(seq):
    def tables(dim):
        inv = 1.0 / (ROPE_THETA ** (jnp.arange(0, dim, 2, dtype=F32) / dim))
        ang = jnp.arange(seq, dtype=F32)[:, None] * inv[None, :]
        return jnp.cos(ang), jnp.sin(ang)

    def lane_tables(cos, sin, fill):
        half = cos.shape[1]
        rest = LANES - 2 * half
        zeros_h = jnp.zeros((seq, half), F32)
        zeros_r = jnp.zeros((seq, rest), F32)
        return [jnp.concatenate([cos, cos, jnp.full((seq, rest), fill, F32)], axis=1),
                jnp.concatenate([-sin, zeros_h, zeros_r], axis=1),
                jnp.concatenate([zeros_h, sin, zeros_r], axis=1)]

    cos_p, sin_p = tables(PARTIAL_ROPE_DIM)
    cos_m, sin_m = tables(MLA_ROPE)
    return jnp.stack(lane_tables(cos_p, sin_p, 1.0) + lane_tables(cos_m, sin_m, 0.0))


def _pad_cols(a, width):
    return jnp.pad(a, ((0, 0), (0, width - a.shape[1])))


def _layer_weights(w_in, w_uq, w_ukv):
    w_in_p = jnp.concatenate([
        w_in[:, 0:2560],
        w_in[:, 3280:4816],
        _pad_cols(w_in[:, 2640:3088], Q_RANK_PAD),
        w_in[:, 3088:3216],
        _pad_cols(w_in[:, 3216:3280], LANES),
        _pad_cols(w_in[:, 2560:2640], LANES),
    ], axis=1).astype(BF16)
    wuq = w_uq.reshape(MLA_Q_RANK, B_HEADS, MLA_NOPE + MLA_ROPE)
    wuq = jnp.pad(wuq, ((0, Q_RANK_PAD - MLA_Q_RANK), (0, 0), (0, 2 * LANES - MLA_NOPE - MLA_ROPE)))
    wuq = wuq.reshape(Q_RANK_PAD, B_HEADS * 2 * LANES).astype(BF16)
    wukv = w_ukv.reshape(MLA_KV_RANK, B_HEADS, MLA_NOPE + HEAD_DIM)
    wukv = jnp.concatenate([wukv[:, :, :MLA_NOPE].reshape(MLA_KV_RANK, -1),
                            wukv[:, :, MLA_NOPE:].reshape(MLA_KV_RANK, -1)], axis=1).astype(BF16)
    return w_in_p, wuq, wukv


def kernel(x, c, w_ada, b_ada, g_ffn1, w1_gate, w1_up, w1_down, g_mix, w_in, g_qa, g_ka, g_cq, g_ckv, w_uq, w_ukv, g_q_nope, g_k_nope, g_q_rope, g_k_rope, w_out, g_ffn2, w2_gate, w2_up, w2_down):
    batch, seq, d_model = x.shape
    depth = w_ada.shape[0]
    topk = min(TOPK_MAX, seq // 4)
    tabs = _rope_tables(seq)
    mods = _ada(c, w_ada, b_ada).reshape(depth, batch, N_MOD, d_model)
    x2 = x.reshape(batch * seq, d_model)

    for l in range(depth):
        mod = mods[l]
        x2 = _ffn(x2, mod, g_ffn1[l], _layer_bf16(w1_gate, l), _layer_bf16(w1_up, l),
                  _layer_bf16(w1_down, l), 0, seq)

        w_in_p, wuq, wukv = _layer_weights(w_in[l], w_uq[l], w_ukv[l])
        gains = [g_qa[l][None, :], g_ka[l][None, :], _pad_cols(g_cq[l][None, :], Q_RANK_PAD),
                 g_ckv[l][None, :], g_q_nope[l][None, :], g_k_nope[l][None, :],
                 _pad_cols(g_q_rope[l][None, :], LANES), _pad_cols(g_k_rope[l][None, :], LANES)]
        (qa, ka, vat, qi, ki, wt, qcat, kcat, vbt, qc, kc, vct) = _prep(
            x2, mod, g_mix[l], w_in_p, tabs, gains, wuq, wukv, batch, seq)

        out_a = _dsa(qi, ki, wt, qa, ka, vat, batch, seq, topk)
        out_b = _mla(qcat, kcat, vbt, batch, seq)
        out_c = _sb(qc, kc, vct, batch, seq)
        x2 = _out_proj(x2, mod, out_a.reshape(batch * seq, -1), out_b.reshape(batch * seq, -1),
                       out_c.reshape(batch * seq, -1), _layer_bf16(w_out, l), seq)

        x2 = _ffn(x2, mod, g_ffn2[l], _layer_bf16(w2_gate, l), _layer_bf16(w2_up, l),
                  _layer_bf16(w2_down, l), 6, seq)

    return x2.reshape(batch, seq, d_model)
```

```python
import functools

import jax
import jax.numpy as jnp
from jax import lax
from jax.experimental import pallas as pl
from jax.experimental.pallas import tpu as pltpu

F32 = jnp.float32
BF16 = jnp.bfloat16
I32 = jnp.int32
I16 = jnp.int16

HEAD_DIM = 128
CHUNK = 64
CHUNK_SHIFT = CHUNK.bit_length() - 1
ROPE_THETA = 500000.0
PARTIAL_ROPE_DIM = HEAD_DIM // 4
NORM_EPS = 1e-6
N_MOD = 9
A_HEADS = 4
IDX_HEADS = 16
IDX_DIM = 64
TOPK_MAX = 256
B_HEADS = 8
MLA_Q_RANK = 448
MLA_KV_RANK = 128
MLA_NOPE = 128
MLA_ROPE = 64
C_HEADS = 4

LANES = 128
KEY_BLOCK = 256
Q_RANK_PAD = 512
VMEM_LIMIT = 56 * 1024 * 1024
FFN_VMEM_LIMIT = 61 * 1024 * 1024

OFF_QA, OFF_KA, OFF_VA, OFF_QI = 0, 512, 1024, 1536
OFF_QC, OFF_KC, OFF_VC = 2560, 3072, 3584
OFF_CQ, OFF_CKV, OFF_KR, OFF_KIW = 4096, 4608, 4736, 4864
N_PROJ = 4992

LOG2E = 1.4426950408889634
A_SCALE = HEAD_DIM ** -0.5 * LOG2E
B_SCALE = (MLA_NOPE + MLA_ROPE) ** -0.5 * LOG2E
C_SCALE = HEAD_DIM ** -0.5
IDX_SCALE = (IDX_DIM ** -0.5) * (IDX_HEADS ** -0.5)

NEG_INF = float("-inf")
INT_MIN = -2 ** 31


def _params(*sem, vmem_limit=VMEM_LIMIT):
    return pltpu.CompilerParams(dimension_semantics=sem, vmem_limit_bytes=vmem_limit)


def _ada_kernel(c_ref, w_ref, b_ref, o_ref):
    c = c_ref[...]
    ca = (c * jax.nn.sigmoid(c)).astype(BF16)
    o_ref[0] = jnp.dot(ca, w_ref[0].astype(BF16), preferred_element_type=F32) + b_ref[0]


def _ada(c, w_ada, b_ada):
    L, D, N = w_ada.shape
    B = c.shape[0]
    tn = 2048
    return pl.pallas_call(
        _ada_kernel,
        grid=(L, N // tn),
        in_specs=[
            pl.BlockSpec((B, D), lambda l, j: (0, 0)),
            pl.BlockSpec((1, D, tn), lambda l, j: (l, 0, j)),
            pl.BlockSpec((1, 1, tn), lambda l, j: (l, 0, j)),
        ],
        out_specs=pl.BlockSpec((1, B, tn), lambda l, j: (l, 0, j)),
        out_shape=jax.ShapeDtypeStruct((L, B, N), F32),
        compiler_params=_params("arbitrary", "arbitrary"),
        name="ada_mod",
    )(c, w_ada, b_ada.reshape(L, 1, N))


NORM_ROWS = 16


def _norm_mod_into(h_ref, x_ref, g, shift, scale):
    gain = g * (1.0 + scale)

    def rows(r, carry):
        sl = pl.ds(pl.multiple_of(r * NORM_ROWS, NORM_ROWS), NORM_ROWS)
        x = x_ref[sl, :]
        y = x * lax.rsqrt(jnp.mean(x * x, axis=-1, keepdims=True) + NORM_EPS)
        h_ref[sl, :] = (y * gain + shift).astype(BF16)
        return carry

    lax.fori_loop(0, x_ref.shape[0] // NORM_ROWS, rows, 0, unroll=8)


def _ffn_kernel(x_ref, mod_ref, g_ref, wg_ref, wu_ref, wd_ref, o_ref, h_ref, *, row):
    j = pl.program_id(1)

    @pl.when(j == 0)
    def _():
        _norm_mod_into(h_ref, x_ref, g_ref[...], mod_ref[0, row:row + 1, :], mod_ref[0, row + 1:row + 2, :])
        o_ref[...] = jnp.zeros(o_ref.shape, F32)

    h = h_ref[...]
    g = jnp.dot(h, wg_ref[...], preferred_element_type=F32)
    u = jnp.dot(h, wu_ref[...], preferred_element_type=F32)
    a = ((g * jax.nn.sigmoid(g)) * u).astype(BF16)
    o_ref[...] += jnp.dot(a, wd_ref[...], preferred_element_type=F32)

    @pl.when(j == pl.num_programs(1) - 1)
    def _():
        gate = mod_ref[0, row + 2:row + 3, :]
        o_ref[...] = x_ref[...] + (0.5 * (1.0 + gate)) * o_ref[...]


def _ffn(x2, mod, g, wg, wu, wd, row, seq):
    T, D = x2.shape
    F = wg.shape[1]
    tm = min(1024, seq)
    tf = 512
    per_seq = seq // tm
    return pl.pallas_call(
        functools.partial(_ffn_kernel, row=row),
        grid=(T // tm, F // tf),
        in_specs=[
            pl.BlockSpec((tm, D), lambda i, j: (i, 0)),
            pl.BlockSpec((1, N_MOD, D), lambda i, j: (i // per_seq, 0, 0)),
            pl.BlockSpec((1, D), lambda i, j: (0, 0)),
            pl.BlockSpec((D, tf), lambda i, j: (0, j)),
            pl.BlockSpec((D, tf), lambda i, j: (0, j)),
            pl.BlockSpec((tf, D), lambda i, j: (j, 0)),
        ],
        out_specs=pl.BlockSpec((tm, D), lambda i, j: (i, 0)),
        out_shape=jax.ShapeDtypeStruct((T, D), F32),
        scratch_shapes=[pltpu.VMEM((tm, D), BF16)],
        compiler_params=_params("arbitrary", "arbitrary", vmem_limit=FFN_VMEM_LIMIT),
        name="ffn",
    )(x2, mod, g.reshape(1, D), wg, wu, wd)


def _rms_lanes(x, g, n):
    return x * lax.rsqrt(jnp.sum(x * x, axis=-1, keepdims=True) / n + NORM_EPS) * g


def _rope_lanes(x, cos, sin_lo, sin_hi, half):
    return (x * cos + pltpu.roll(x, LANES - half, 1) * sin_lo + pltpu.roll(x, half, 1) * sin_hi)


def _prep_kernel(x_ref, mod_ref, gmix_ref, win_ref, tab_ref,
                 gqa_ref, gka_ref, gcq_ref, gckv_ref, gqn_ref, gkn_ref, gqr_ref, gkr_ref,
                 wuq_ref, wukv_ref,
                 qa_o, ka_o, vat_o, qi_o, ki_o, wt_o, qcat_o, kcat_o, vbt_o, qc_o, kc_o, vct_o, h_ref, t_ref):
    ca, sa_lo, sa_hi = tab_ref[0], tab_ref[1], tab_ref[2]
    cm, sm_lo, sm_hi = tab_ref[3], tab_ref[4], tab_ref[5]
    half_a = PARTIAL_ROPE_DIM // 2
    half_m = MLA_ROPE // 2
    wa, wc = A_HEADS * HEAD_DIM, C_HEADS * HEAD_DIM

    _norm_mod_into(h_ref, x_ref, gmix_ref[...], mod_ref[0, 3:4, :], mod_ref[0, 4:5, :])

    def transposed(v):
        t_ref[:, 0:v.shape[1]] = v
        return t_ref[:, 0:v.shape[1]].T

    def proj(lo, width):
        return jnp.dot(h_ref[...], win_ref[:, lo:lo + width], preferred_element_type=F32)

    cq = proj(OFF_CQ, Q_RANK_PAD)
    small = proj(OFF_CKV, N_PROJ - OFF_CKV)
    qa = proj(OFF_QA, wa)
    ka = proj(OFF_KA, wa)
    cq = _rms_lanes(cq, gcq_ref[...], MLA_Q_RANK)
    ckv = _rms_lanes(small[:, 0:MLA_KV_RANK], gckv_ref[...], MLA_KV_RANK)
    qb = jnp.dot(cq.astype(BF16), wuq_ref[...], preferred_element_type=F32)
    kvb = jnp.dot(ckv.astype(BF16), wukv_ref[...], preferred_element_type=F32)
    va = proj(OFF_VA, wa)
    qi = proj(OFF_QI, IDX_HEADS * IDX_DIM)
    qc = proj(OFF_QC, wc)
    kc = proj(OFF_KC, wc)
    vc = proj(OFF_VC, wc)

    for h in range(A_HEADS):
        lo = h * HEAD_DIM
        q = _rms_lanes(qa[:, lo:lo + HEAD_DIM], gqa_ref[...], HEAD_DIM)
        t_ref[:, lo:lo + HEAD_DIM] = _rope_lanes(q, ca, sa_lo, sa_hi, half_a) * A_SCALE
    qa_o[0, 0] = t_ref[:, 0:wa].T.astype(BF16)
    for h in range(A_HEADS):
        lo = h * HEAD_DIM
        k = _rms_lanes(ka[:, lo:lo + HEAD_DIM], gka_ref[...], HEAD_DIM)
        ka_o[0, :, lo:lo + HEAD_DIM] = _rope_lanes(k, ca, sa_lo, sa_hi, half_a).astype(BF16)
    vat_o[0, 0] = transposed(va).astype(BF16)
    qi_o[0, 0] = transposed(qi).astype(BF16)

    qc_o[0, 0] = transposed(qc * C_SCALE).astype(BF16)
    kc_o[0] = kc.astype(BF16)
    vct_o[0, 0] = transposed(vc).astype(BF16)

    kr = _rms_lanes(small[:, OFF_KR - OFF_CKV:OFF_KR - OFF_CKV + LANES], gkr_ref[...], MLA_ROPE)
    kr = _rope_lanes(kr, cm, sm_lo, sm_hi, half_m).astype(BF16)
    kiw = small[:, OFF_KIW - OFF_CKV:OFF_KIW - OFF_CKV + LANES]
    ki_o[0] = kiw[:, :IDX_DIM].astype(BF16)
    wt_o[0] = transposed(kiw)[IDX_DIM:IDX_DIM + IDX_HEADS, :] * IDX_SCALE
    for h in range(B_HEADS):
        lo = h * 2 * LANES
        kn = _rms_lanes(kvb[:, h * MLA_NOPE:(h + 1) * MLA_NOPE], gkn_ref[...], MLA_NOPE)
        kcat_o[0, :, lo:lo + MLA_NOPE] = kn.astype(BF16)
        kcat_o[0, :, lo + MLA_NOPE:lo + 2 * LANES] = kr
    vbt_o[0, 0] = transposed(kvb[:, B_HEADS * MLA_NOPE:]).astype(BF16)
    for h in range(B_HEADS):
        lo = h * 2 * LANES
        qn = _rms_lanes(qb[:, lo:lo + MLA_NOPE], gqn_ref[...], MLA_NOPE)
        t_ref[:, lo:lo + MLA_NOPE] = qn * B_SCALE
        qr = _rms_lanes(qb[:, lo + MLA_NOPE:lo + 2 * LANES], gqr_ref[...], MLA_ROPE)
        t_ref[:, lo + MLA_NOPE:lo + 2 * LANES] = _rope_lanes(qr, cm, sm_lo, sm_hi, half_m) * B_SCALE
    qcat_o[0, 0] = t_ref[...].T.astype(BF16)


def _resident(shape):
    return pl.BlockSpec(shape, lambda b, i: (0,) * len(shape), pipeline_mode=pl.Buffered(1))


def _prep(x2, mod, g_mix, w_in_p, tabs, gains, wuq, wukv, batch, seq):
    tm = KEY_BLOCK
    nt = seq // tm
    d_model = x2.shape[1]
    wa, wb, wc = A_HEADS * HEAD_DIM, B_HEADS * HEAD_DIM, C_HEADS * HEAD_DIM
    wcat = B_HEADS * 2 * LANES

    def tok(width, dtype):
        return (jax.ShapeDtypeStruct((batch, seq, width), dtype),
                pl.BlockSpec((1, tm, width), lambda b, i: (b, i, 0)))

    def tok_t(width):
        return (jax.ShapeDtypeStruct((batch, nt, width, tm), BF16),
                pl.BlockSpec((1, 1, width, tm), lambda b, i: (b, i, 0, 0)))

    outs = [tok_t(wa), tok(wa, BF16), tok_t(wa), tok_t(IDX_HEADS * IDX_DIM), tok(IDX_DIM, BF16),
            (jax.ShapeDtypeStruct((batch, IDX_HEADS, seq), F32),
             pl.BlockSpec((1, IDX_HEADS, tm), lambda b, i: (b, 0, i))),
            tok_t(wcat), tok(wcat, BF16), tok_t(wb), tok_t(wc), tok(wc, BF16), tok_t(wc)]
    return pl.pallas_call(
        _prep_kernel,
        grid=(batch, nt),
        in_specs=[pl.BlockSpec((tm, d_model), lambda b, i: (b * nt + i, 0)),
                  pl.BlockSpec((1, N_MOD, d_model), lambda b, i: (b, 0, 0)),
                  _resident((1, d_model)),
                  _resident(w_in_p.shape),
                  pl.BlockSpec((6, tm, LANES), lambda b, i: (0, i, 0))]
                 + [_resident(g.shape) for g in gains]
                 + [_resident(wuq.shape), _resident(wukv.shape)],
        out_specs=[o[1] for o in outs],
        out_shape=[o[0] for o in outs],
        scratch_shapes=[pltpu.VMEM((tm, d_model), BF16), pltpu.VMEM((tm, wcat), F32)],
        compiler_params=_params("arbitrary", "arbitrary"),
        name="head_prep",
    )(x2, mod, g_mix.reshape(1, d_model), w_in_p, tabs, *gains, wuq, wukv)


MLA_HEADS_PER_STEP = 4


def _mla_kernel(qt_ref, k_ref, vt_ref, o_ref, s0_ref, s1_ref, m_ref, l_ref, acc_ref, *, t):
    qb = pl.program_id(2)
    sub = t // KEY_BLOCK
    dq = 2 * LANES
    m_ref[...] = jnp.full(m_ref.shape, NEG_INF, F32)
    l_ref[...] = jnp.zeros(l_ref.shape, F32)
    acc_ref[...] = jnp.zeros(acc_ref.shape, F32)

    def produce(s_ref, kb):
        ks = pl.multiple_of(kb * t, t)
        for h in range(MLA_HEADS_PER_STEP):
            k = k_ref[0, pl.ds(ks, t), h * dq:(h + 1) * dq]
            for c in range(sub):
                s_ref[h, :, c * KEY_BLOCK:(c + 1) * KEY_BLOCK] = jnp.dot(
                    k, qt_ref[0, c, h * dq:(h + 1) * dq, :], preferred_element_type=F32)

    def consume(s_ref, kb, diagonal):
        for h in range(MLA_HEADS_PER_STEP):
            s = s_ref[h]
            if diagonal:
                kc = lax.broadcasted_iota(I32, (t, t), 0) >> CHUNK_SHIFT
                qc = lax.broadcasted_iota(I32, (t, t), 1) >> CHUNK_SHIFT
                s = jnp.where(kc <= qc, s, NEG_INF)
            m_prev = m_ref[h]
            m_new = jnp.maximum(m_prev, jnp.max(s, axis=0, keepdims=True))
            p = jnp.exp2(s - m_new)
            alpha = jnp.exp2(m_prev - m_new)
            l_ref[h] = alpha * l_ref[h] + jnp.sum(p, axis=0, keepdims=True)
            p = p.astype(BF16)
            pv = jnp.dot(vt_ref[0, kb * sub, h * HEAD_DIM:(h + 1) * HEAD_DIM, :], p[0:KEY_BLOCK],
                         preferred_element_type=F32)
            for c in range(1, sub):
                pv += jnp.dot(vt_ref[0, kb * sub + c, h * HEAD_DIM:(h + 1) * HEAD_DIM, :],
                              p[c * KEY_BLOCK:(c + 1) * KEY_BLOCK], preferred_element_type=F32)
            acc_ref[h] = alpha * acc_ref[h] + pv
            m_ref[h] = m_new

    produce(s0_ref, 0)

    def body(j, carry):
        produce(s1_ref, 2 * j + 1)
        consume(s0_ref, 2 * j, False)
        produce(s0_ref, 2 * j + 2)
        consume(s1_ref, 2 * j + 1, False)
        return carry

    lax.fori_loop(0, qb // 2, body, 0)

    @pl.when(qb % 2 == 0)
    def _():
        consume(s0_ref, qb, True)

    @pl.when(qb % 2 == 1)
    def _():
        produce(s1_ref, qb)
        consume(s0_ref, qb - 1, False)
        consume(s1_ref, qb, True)

    for h in range(MLA_HEADS_PER_STEP):
        o_ref[0, :, h * HEAD_DIM:(h + 1) * HEAD_DIM] = (acc_ref[h] / l_ref[h]).T.astype(BF16)


def _mla(qcat, kcat, vbt, batch, seq):
    t = min(512, seq)
    hp = MLA_HEADS_PER_STEP
    nkb = seq // KEY_BLOCK
    return pl.pallas_call(
        functools.partial(_mla_kernel, t=t),
        grid=(batch, B_HEADS // hp, seq // t),
        in_specs=[
            pl.BlockSpec((1, t // KEY_BLOCK, hp * 2 * LANES, KEY_BLOCK), lambda b, h, i: (b, i, h, 0)),
            pl.BlockSpec((1, seq, hp * 2 * LANES), lambda b, h, i: (b, 0, h)),
            pl.BlockSpec((1, nkb, hp * HEAD_DIM, KEY_BLOCK), lambda b, h, i: (b, 0, h, 0)),
        ],
        out_specs=pl.BlockSpec((1, t, hp * HEAD_DIM), lambda b, h, i: (b, i, h)),
        out_shape=jax.ShapeDtypeStruct((batch, seq, B_HEADS * HEAD_DIM), BF16),
        scratch_shapes=[pltpu.VMEM((hp, t, t), F32), pltpu.VMEM((hp, t, t), F32),
                        pltpu.VMEM((hp, 1, t), F32), pltpu.VMEM((hp, 1, t), F32),
                        pltpu.VMEM((hp, HEAD_DIM, t), F32)],
        compiler_params=_params("arbitrary", "arbitrary", "arbitrary"),
        name="mla_attn",
    )(qcat, kcat, vbt)


SB_HEADS_PER_STEP = 4


def _sb_kernel(qt_ref, k_ref, vt_ref, o_ref, z0_ref, z1_ref, r_ref, acc_ref, *, tq):
    qb = pl.program_id(2)
    tk = KEY_BLOCK
    sub = tq // tk
    assert sub % 2 == 0
    tri_r = lax.broadcasted_iota(I32, (tk, tk), 0)
    tri_c = lax.broadcasted_iota(I32, (tk, tk), 1)
    later_keys = jnp.where(tri_c > tri_r, 1.0, 0.0).astype(BF16)
    r_ref[...] = jnp.zeros(r_ref.shape, F32)
    acc_ref[...] = jnp.zeros(acc_ref.shape, F32)

    heads = range(SB_HEADS_PER_STEP)

    def produce(z_ref, kb):
        ks = pl.multiple_of(kb * tk, tk)
        for h in heads:
            k = k_ref[0, pl.ds(ks, tk), h * HEAD_DIM:(h + 1) * HEAD_DIM]
            for c in range(sub):
                z_ref[h, :, c * tk:(c + 1) * tk] = jnp.dot(
                    k, qt_ref[0, c, h * HEAD_DIM:(h + 1) * HEAD_DIM, :], preferred_element_type=F32)

    def consume(z_ref, kb, key_offset):
        if key_offset is not None:
            before = (lax.broadcasted_iota(I32, (tk, tq), 0) + key_offset
                      < lax.broadcasted_iota(I32, (tk, tq), 1))
        zs = [z_ref[h] for h in heads]
        go = []
        for z in zs:
            neg_abs = lax.bitcast_convert_type(lax.bitcast_convert_type(z, I32) | INT_MIN, F32)
            g = jnp.maximum(z, 0.0) + jnp.log(1.0 + jnp.exp(neg_abs))
            if key_offset is not None:
                g = jnp.where(before, g, 0.0)
            go.append(g)
        later = [jnp.dot(later_keys, g.astype(BF16), preferred_element_type=F32) for g in go]
        for h in heads:
            w = jnp.exp(zs[h] - (go[h] + later[h] + r_ref[h]))
            if key_offset is not None:
                w = jnp.where(before, w, 0.0)
            acc_ref[h] += jnp.dot(vt_ref[0, kb, h * HEAD_DIM:(h + 1) * HEAD_DIM, :], w.astype(BF16),
                                  preferred_element_type=F32)
            r_ref[h] += jnp.sum(go[h], axis=0, keepdims=True)

    first = qb * sub
    bufs = (z0_ref, z1_ref)
    produce(bufs[0], first + sub - 1)
    for i, d in enumerate(reversed(range(sub))):
        if d > 0:
            produce(bufs[(i + 1) % 2], first + d - 1)
        else:
            @pl.when(qb > 0)
            def _():
                produce(bufs[(i + 1) % 2], first - 1)
        consume(bufs[i % 2], first + d, d * tk)

    def body(j, carry):
        top = first - 1 - 2 * j
        produce(z1_ref, top - 1)
        consume(z0_ref, top, None)
        produce(z0_ref, top - 2)
        consume(z1_ref, top - 1, None)
        return carry

    lax.fori_loop(0, first // 2 - 1, body, 0)

    @pl.when(qb > 0)
    def _():
        produce(z1_ref, 0)
        consume(z0_ref, 1, None)
        consume(z1_ref, 0, None)

    for h in range(SB_HEADS_PER_STEP):
        o_ref[0, :, h * HEAD_DIM:(h + 1) * HEAD_DIM] = acc_ref[h].T.astype(BF16)


def _sb(qc, kc, vct, batch, seq):
    tq = min(512, seq)
    hp = SB_HEADS_PER_STEP
    nkb = seq // KEY_BLOCK
    return pl.pallas_call(
        functools.partial(_sb_kernel, tq=tq),
        grid=(batch, C_HEADS // hp, seq // tq),
        in_specs=[
            pl.BlockSpec((1, tq // KEY_BLOCK, hp * HEAD_DIM, KEY_BLOCK), lambda b, h, i: (b, i, h, 0)),
            pl.BlockSpec((1, seq, hp * HEAD_DIM), lambda b, h, i: (b, 0, h)),
            pl.BlockSpec((1, nkb, hp * HEAD_DIM, KEY_BLOCK), lambda b, h, i: (b, 0, h, 0)),
        ],
        out_specs=pl.BlockSpec((1, tq, hp * HEAD_DIM), lambda b, h, i: (b, i, h)),
        out_shape=jax.ShapeDtypeStruct((batch, seq, C_HEADS * HEAD_DIM), BF16),
        scratch_shapes=[pltpu.VMEM((hp, KEY_BLOCK, tq), F32), pltpu.VMEM((hp, KEY_BLOCK, tq), F32),
                        pltpu.VMEM((hp, 1, tq), F32), pltpu.VMEM((hp, HEAD_DIM, tq), F32)],
        compiler_params=_params("arbitrary", "arbitrary", "arbitrary"),
        name="sb_attn",
    )(qc, kc, vct)


SEARCH_GROUP = 2


def _dsa_kernel(qit_ref, ki_ref, wt_ref, qat_ref, ka_ref, vt_ref, o_ref,
                keys_ref, half_ref, thr_ref, m_ref, l_ref, acc_ref, s0_ref, s1_ref,
                *, t, tq, topk, seq):
    qb = pl.program_id(1)
    sub = tq // t
    nk = (qb + 1) * sub
    row = lax.broadcasted_iota(I32, (t, tq), 0)
    col = lax.broadcasted_iota(I32, (t, tq), 1)
    wt = wt_ref[0]

    def score_block(kb):
        ks = pl.multiple_of(kb * t, t)
        kix = ki_ref[0, pl.ds(ks, t), :]
        sc = jnp.zeros((t, tq), F32)
        for h in range(IDX_HEADS):
            lg = jnp.concatenate(
                [jnp.dot(kix, qit_ref[0, c, h * IDX_DIM:(h + 1) * IDX_DIM, :], preferred_element_type=F32)
                 for c in range(sub)], axis=1)
            sc = sc + wt[h:h + 1, :] * jnp.maximum(lg, 0.0)
        bits = lax.bitcast_convert_type(sc, I32)
        key = bits ^ ((bits >> 31) & 0x7FFFFFFF)
        visible = ((ks + row) >> CHUNK_SHIFT) <= ((qb * tq + col) >> CHUNK_SHIFT)
        key = jnp.where(visible, key, INT_MIN)
        keys_ref[kb] = key
        half_ref[kb] = (key >> 16).astype(I16)

    def score_pair(j, carry):
        score_block(2 * j)
        score_block(2 * j + 1)
        return carry

    assert sub % 2 == 0
    lax.fori_loop(0, nk // 2, score_pair, 0)

    i16_min, i16_max = -2 ** 15, 2 ** 15 - 1
    for pad in range(SEARCH_GROUP - 1):
        half_ref[nk + pad] = jnp.full((t, tq), i16_min, I16)

    def search16():
        def count16(cand):
            cand16 = cand.astype(I16)

            def block_group(j, accs):
                out = []
                for c, acc in enumerate(accs):
                    parts = [acc]
                    for kb in range(SEARCH_GROUP):
                        hit = jnp.where(half_ref[SEARCH_GROUP * j + kb, :, c * t:(c + 1) * t] >= cand16[:, c * t:(c + 1) * t],
                                        jnp.ones((), I16), jnp.zeros((), I16))
                        parts += [hit[i * 16:(i + 1) * 16] for i in range(t // 16)]
                    while len(parts) > 1:
                        parts = [a + b for a, b in zip(parts[0::2], parts[1::2])] + parts[len(parts) & ~1:]
                    out.append(parts[0])
                return tuple(out)
            groups = (nk + SEARCH_GROUP - 1) // SEARCH_GROUP
            accs = lax.fori_loop(0, groups, block_group,
                                 tuple(jnp.zeros((16, t), I16) for _ in range(sub)))
            return jnp.concatenate([jnp.sum(a.astype(I32), axis=0, keepdims=True) for a in accs], axis=1)

        c0 = count16(jnp.zeros((1, tq), I32))
        v0 = jnp.where(c0 >= topk, 0, i16_min).astype(I32)
        n0 = jnp.where(c0 >= topk, c0, 0)

        def bit(b, carry):
            v, n = carry
            cand = v | lax.shift_left(jnp.int32(1), 14 - b)
            c = count16(cand)
            return jnp.where(c >= topk, cand, v), jnp.where(c >= topk, c, n)

        return lax.fori_loop(0, 15, bit, (v0, n0))

    def count(pred):
        def block(kb, acc):
            hit = jnp.where(pred(keys_ref[kb], kb * t + row), 1, 0).astype(I32)
            parts = [acc] + [hit[i * 8:(i + 1) * 8] for i in range(t // 8)]
            while len(parts) > 1:
                parts = [a + b for a, b in zip(parts[0::2], parts[1::2])] + parts[len(parts) & ~1:]
            return parts[0]
        acc = lax.fori_loop(0, nk, block, jnp.zeros((8, tq), I32))
        return jnp.sum(acc, axis=0, keepdims=True)

    thr_hi, n_hi = search16()

    thr_hi16 = thr_hi.astype(I16)

    def low_half_block(kb, carry):
        hi = half_ref[kb]
        low = ((keys_ref[kb] & 0xFFFF) + i16_min).astype(I16)
        top = jnp.full((), i16_max, I16)
        bottom = jnp.full((), i16_min, I16)
        half_ref[kb] = jnp.where(hi == thr_hi16, low, jnp.where(hi > thr_hi16, top, bottom))
        return carry

    lax.fori_loop(0, nk, low_half_block, 0)
    thr_lo, n_lo = search16()
    thr = lax.shift_left(thr_hi, 16) + (thr_lo - i16_min)
    n_ge = jnp.where(thr_lo > i16_min, n_lo, n_hi)
    tie = (thr != INT_MIN) & (n_ge > topk)
    thr_ref[...] = jnp.maximum(thr, INT_MIN + 1)

    @pl.when(jnp.max(tie.astype(I32)) > 0)
    def _():
        nbits = seq.bit_length() - 1
        need = topk - count(lambda k, i: k > thr)

        def lim_bit(b, lim):
            cand = lim | lax.shift_left(jnp.int32(1), nbits - 1 - b)
            below = count(lambda k, i: (k == thr) & (i < cand))
            return jnp.where(below < need, cand, lim)

        lim = lax.fori_loop(0, nbits, lim_bit, jnp.zeros((1, tq), I32)) + 1
        lim = jnp.where(tie, lim, seq)

        def demote_block(kb, carry):
            key = keys_ref[kb]
            drop = (key == thr) & ((kb * t + row) >= lim)
            keys_ref[kb] = jnp.where(drop, key - 1, key)
            return carry

        lax.fori_loop(0, nk, demote_block, 0)

    m_ref[...] = jnp.full(m_ref.shape, NEG_INF, F32)
    l_ref[...] = jnp.zeros(l_ref.shape, F32)
    acc_ref[...] = jnp.zeros(acc_ref.shape, F32)

    def produce(s_ref, kb):
        ks = pl.multiple_of(kb * t, t)
        bias = jnp.where(keys_ref[kb] >= thr_ref[...], 0.0, NEG_INF)
        for h in range(A_HEADS):
            k = ka_ref[0, pl.ds(ks, t), h * HEAD_DIM:(h + 1) * HEAD_DIM]
            for c in range(sub):
                s_ref[h, :, c * t:(c + 1) * t] = jnp.dot(
                    k, qat_ref[0, c, h * HEAD_DIM:(h + 1) * HEAD_DIM, :],
                    preferred_element_type=F32) + bias[:, c * t:(c + 1) * t]

    def consume(s_ref, kb):
        for h in range(A_HEADS):
            lo = h * HEAD_DIM
            s = s_ref[h]
            m_prev = m_ref[h]
            m_new = jnp.maximum(m_prev, jnp.max(s, axis=0, keepdims=True))
            m_safe = jnp.where(m_new == NEG_INF, 0.0, m_new)
            p = jnp.exp2(s - m_safe)
            alpha = jnp.exp2(m_prev - m_safe)
            l_ref[h] = alpha * l_ref[h] + jnp.sum(p, axis=0, keepdims=True)
            acc_ref[h] = alpha * acc_ref[h] + jnp.dot(vt_ref[0, kb, lo:lo + HEAD_DIM, :], p.astype(BF16),
                                                      preferred_element_type=F32)
            m_ref[h] = m_new

    produce(s0_ref, 0)

    def attend_pair(j, carry):
        produce(s1_ref, 2 * j + 1)
        consume(s0_ref, 2 * j)
        produce(s0_ref, 2 * j + 2)
        consume(s1_ref, 2 * j + 1)
        return carry

    lax.fori_loop(0, nk // 2 - 1, attend_pair, 0)
    produce(s1_ref, nk - 1)
    consume(s0_ref, nk - 2)
    consume(s1_ref, nk - 1)

    for h in range(A_HEADS):
        o_ref[0, :, h * HEAD_DIM:(h + 1) * HEAD_DIM] = (acc_ref[h] / l_ref[h]).T.astype(BF16)


def _dsa(qi, ki, wt, qa, ka, vat, batch, seq, topk):
    t = KEY_BLOCK
    tq = min(512, seq)
    nt = seq // t
    wa = A_HEADS * HEAD_DIM
    return pl.pallas_call(
        functools.partial(_dsa_kernel, t=t, tq=tq, topk=topk, seq=seq),
        grid=(batch, seq // tq),
        in_specs=[
            pl.BlockSpec((1, tq // t, IDX_HEADS * IDX_DIM, t), lambda b, i: (b, i, 0, 0)),
            pl.BlockSpec((1, seq, IDX_DIM), lambda b, i: (b, 0, 0)),
            pl.BlockSpec((1, IDX_HEADS, tq), lambda b, i: (b, 0, i)),
            pl.BlockSpec((1, tq // t, wa, t), lambda b, i: (b, i, 0, 0)),
            pl.BlockSpec((1, seq, wa), lambda b, i: (b, 0, 0)),
            pl.BlockSpec((1, nt, wa, t), lambda b, i: (b, 0, 0, 0)),
        ],
        out_specs=pl.BlockSpec((1, tq, wa), lambda b, i: (b, i, 0)),
        out_shape=jax.ShapeDtypeStruct((batch, seq, wa), BF16),
        scratch_shapes=[
            pltpu.VMEM((nt, t, tq), I32), pltpu.VMEM((nt + SEARCH_GROUP - 1, t, tq), I16),
            pltpu.VMEM((1, tq), I32),
            pltpu.VMEM((A_HEADS, 1, tq), F32), pltpu.VMEM((A_HEADS, 1, tq), F32),
            pltpu.VMEM((A_HEADS, HEAD_DIM, tq), F32),
            pltpu.VMEM((A_HEADS, t, tq), F32), pltpu.VMEM((A_HEADS, t, tq), F32),
        ],
        compiler_params=_params("arbitrary", "arbitrary"),
        name="dsa_attn",
    )(qi, ki, wt, qa, ka, vat)


def _out_kernel(x_ref, mod_ref, oa_ref, ob_ref, oc_ref, w_ref, o_ref):
    wa = A_HEADS * HEAD_DIM
    wb = B_HEADS * HEAD_DIM
    mixed = (jnp.dot(oa_ref[...], w_ref[0:wa, :], preferred_element_type=F32)
             + jnp.dot(ob_ref[...], w_ref[wa:wa + wb, :], preferred_element_type=F32)
             + jnp.dot(oc_ref[...], w_ref[wa + wb:, :], preferred_element_type=F32))
    o_ref[...] = x_ref[...] + (1.0 + mod_ref[0, 5:6, :]) * mixed


def _out_proj(x2, mod, oa, ob, oc, w, seq):
    T, D = x2.shape
    tm = min(512, seq)
    per_seq = seq // tm
    return pl.pallas_call(
        _out_kernel,
        grid=(T // tm,),
        in_specs=[
            pl.BlockSpec((tm, D), lambda i: (i, 0)),
            pl.BlockSpec((1, N_MOD, D), lambda i: (i // per_seq, 0, 0)),
            pl.BlockSpec((tm, oa.shape[1]), lambda i: (i, 0)),
            pl.BlockSpec((tm, ob.shape[1]), lambda i: (i, 0)),
            pl.BlockSpec((tm, oc.shape[1]), lambda i: (i, 0)),
            pl.BlockSpec(w.shape, lambda i: (0, 0)),
        ],
        out_specs=pl.BlockSpec((tm, D), lambda i: (i, 0)),
        out_shape=jax.ShapeDtypeStruct((T, D), F32),
        compiler_params=_params("arbitrary"),
        name="out_proj",
    )(x2, mod, oa, ob, oc, w)


CAST_ROWS = 256


def _cast_kernel(w_ref, o_ref):
    o_ref[...] = w_ref[0].astype(BF16)


def _layer_bf16(w, layer):
    _, rows, cols = w.shape
    tr = min(CAST_ROWS, rows)
    return pl.pallas_call(
        _cast_kernel,
        grid=(rows // tr,),
        in_specs=[pl.BlockSpec((1, tr, cols), lambda i: (layer, i, 0))],
        out_specs=pl.BlockSpec((tr, cols), lambda i: (i, 0)),
        out_shape=jax.ShapeDtypeStruct((rows, cols), BF16),
        compiler_params=_params("arbitrary"),
        name="cast_bf16",
    )(w)


def _rope_tables(seq):
    def tables(dim):
        inv = 1.0 / (ROPE_THETA ** (jnp.arange(0, dim, 2, dtype=F32) / dim))
        ang = jnp.arange(seq, dtype=F32)[:, None] * inv[None, :]
        return jnp.cos(ang), jnp.sin(ang)

    def lane_tables(cos, sin, fill):
        half = cos.shape[1]
        rest = LANES - 2 * half
        zeros_h = jnp.zeros((seq, half), F32)
        zeros_r = jnp.zeros((seq, rest), F32)
        return [jnp.concatenate([cos, cos, jnp.full((seq, rest), fill, F32)], axis=1),
                jnp.concatenate([-sin, zeros_h, zeros_r], axis=1),
                jnp.concatenate([zeros_h, sin, zeros_r], axis=1)]

    cos_p, sin_p = tables(PARTIAL_ROPE_DIM)
    cos_m, sin_m = tables(MLA_ROPE)
    return jnp.stack(lane_tables(cos_p, sin_p, 1.0) + lane_tables(cos_m, sin_m, 0.0))


def _pad_cols(a, width):
    return jnp.pad(a, ((0, 0), (0, width - a.shape[1])))


def _layer_weights(w_in, w_uq, w_ukv):
    w_in_p = jnp.concatenate([
        w_in[:, 0:2560],
        w_in[:, 3280:4816],
        _pad_cols(w_in[:, 2640:3088], Q_RANK_PAD),
        w_in[:, 3088:3216],
        _pad_cols(w_in[:, 3216:3280], LANES),
        _pad_cols(w_in[:, 2560:2640], LANES),
    ], axis=1).astype(BF16)
    wuq = w_uq.reshape(MLA_Q_RANK, B_HEADS, MLA_NOPE + MLA_ROPE)
    wuq = jnp.pad(wuq, ((0, Q_RANK_PAD - MLA_Q_RANK), (0, 0), (0, 2 * LANES - MLA_NOPE - MLA_ROPE)))
    wuq = wuq.reshape(Q_RANK_PAD, B_HEADS * 2 * LANES).astype(BF16)
    wukv = w_ukv.reshape(MLA_KV_RANK, B_HEADS, MLA_NOPE + HEAD_DIM)
    wukv = jnp.concatenate([wukv[:, :, :MLA_NOPE].reshape(MLA_KV_RANK, -1),
                            wukv[:, :, MLA_NOPE:].reshape(MLA_KV_RANK, -1)], axis=1).astype(BF16)
    return w_in_p, wuq, wukv


def kernel(x, c, w_ada, b_ada, g_ffn1, w1_gate, w1_up, w1_down, g_mix, w_in, g_qa, g_ka, g_cq, g_ckv, w_uq, w_ukv, g_q_nope, g_k_nope, g_q_rope, g_k_rope, w_out, g_ffn2, w2_gate, w2_up, w2_down):
    batch, seq, d_model = x.shape
    depth = w_ada.shape[0]
    topk = min(TOPK_MAX, seq // 4)
    tabs = _rope_tables(seq)
    mods = _ada(c, w_ada, b_ada).reshape(depth, batch, N_MOD, d_model)
    x2 = x.reshape(batch * seq, d_model)

    for l in range(depth):
        mod = mods[l]
        x2 = _ffn(x2, mod, g_ffn1[l], _layer_bf16(w1_gate, l), _layer_bf16(w1_up, l),
                  _layer_bf16(w1_down, l), 0, seq)

        w_in_p, wuq, wukv = _layer_weights(w_in[l], w_uq[l], w_ukv[l])
        gains = [g_qa[l][None, :], g_ka[l][None, :], _pad_cols(g_cq[l][None, :], Q_RANK_PAD),
                 g_ckv[l][None, :], g_q_nope[l][None, :], g_k_nope[l][None, :],
                 _pad_cols(g_q_rope[l][None, :], LANES), _pad_cols(g_k_rope[l][None, :], LANES)]
        (qa, ka, vat, qi, ki, wt, qcat, kcat, vbt, qc, kc, vct) = _prep(
            x2, mod, g_mix[l], w_in_p, tabs, gains, wuq, wukv, batch, seq)

        out_a = _dsa(qi, ki, wt, qa, ka, vat, batch, seq, topk)
        out_b = _mla(qcat, kcat, vbt, batch, seq)
        out_c = _sb(qc, kc, vct, batch, seq)
        x2 = _out_proj(x2, mod, out_a.reshape(batch * seq, -1), out_b.reshape(batch * seq, -1),
                       out_c.reshape(batch * seq, -1), _layer_bf16(w_out, l), seq)

        x2 = _ffn(x2, mod, g_ffn2[l], _layer_bf16(w2_gate, l), _layer_bf16(w2_up, l),
                  _layer_bf16(w2_down, l), 6, seq)

    return x2.reshape(batch, seq, d_model)
```

```python
import functools

import jax
import jax.numpy as jnp
from jax import lax
from jax.experimental import pallas as pl
from jax.experimental.pallas import tpu as pltpu

F32 = jnp.float32
BF16 = jnp.bfloat16
I32 = jnp.int32
I16 = jnp.int16

HEAD_DIM = 128
CHUNK = 64
CHUNK_SHIFT = CHUNK.bit_length() - 1
ROPE_THETA = 500000.0
PARTIAL_ROPE_DIM = HEAD_DIM // 4
NORM_EPS = 1e-6
N_MOD = 9
A_HEADS = 4
IDX_HEADS = 16
IDX_DIM = 64
TOPK_MAX = 256
B_HEADS = 8
MLA_Q_RANK = 448
MLA_KV_RANK = 128
MLA_NOPE = 128
MLA_ROPE = 64
C_HEADS = 4

LANES = 128
KEY_BLOCK = 256
Q_RANK_PAD = 512
VMEM_LIMIT = 56 * 1024 * 1024
FFN_VMEM_LIMIT = 61 * 1024 * 1024

OFF_QA, OFF_KA, OFF_VA, OFF_QI = 0, 512, 1024, 1536
OFF_QC, OFF_KC, OFF_VC = 2560, 3072, 3584
OFF_CQ, OFF_CKV, OFF_KR, OFF_KIW = 4096, 4608, 4736, 4864
N_PROJ = 4992

LOG2E = 1.4426950408889634
A_SCALE = HEAD_DIM ** -0.5 * LOG2E
B_SCALE = (MLA_NOPE + MLA_ROPE) ** -0.5 * LOG2E
C_SCALE = HEAD_DIM ** -0.5
IDX_SCALE = (IDX_DIM ** -0.5) * (IDX_HEADS ** -0.5)

NEG_INF = float("-inf")
INT_MIN = -2 ** 31


def _params(*sem, vmem_limit=VMEM_LIMIT):
    return pltpu.CompilerParams(dimension_semantics=sem, vmem_limit_bytes=vmem_limit)


def _ada_kernel(c_ref, w_ref, b_ref, o_ref):
    c = c_ref[...]
    ca = (c * jax.nn.sigmoid(c)).astype(BF16)
    o_ref[0] = jnp.dot(ca, w_ref[0].astype(BF16), preferred_element_type=F32) + b_ref[0]


def _ada(c, w_ada, b_ada):
    L, D, N = w_ada.shape
    B = c.shape[0]
    tn = 2048
    return pl.pallas_call(
        _ada_kernel,
        grid=(L, N // tn),
        in_specs=[
            pl.BlockSpec((B, D), lambda l, j: (0, 0)),
            pl.BlockSpec((1, D, tn), lambda l, j: (l, 0, j)),
            pl.BlockSpec((1, 1, tn), lambda l, j: (l, 0, j)),
        ],
        out_specs=pl.BlockSpec((1, B, tn), lambda l, j: (l, 0, j)),
        out_shape=jax.ShapeDtypeStruct((L, B, N), F32),
        compiler_params=_params("arbitrary", "arbitrary"),
        name="ada_mod",
    )(c, w_ada, b_ada.reshape(L, 1, N))


NORM_ROWS = 16


def _norm_mod_into(h_ref, x_ref, g, shift, scale):
    gain = g * (1.0 + scale)

    def rows(r, carry):
        sl = pl.ds(pl.multiple_of(r * NORM_ROWS, NORM_ROWS), NORM_ROWS)
        x = x_ref[sl, :]
        y = x * lax.rsqrt(jnp.mean(x * x, axis=-1, keepdims=True) + NORM_EPS)
        h_ref[sl, :] = (y * gain + shift).astype(BF16)
        return carry

    lax.fori_loop(0, x_ref.shape[0] // NORM_ROWS, rows, 0, unroll=8)


def _ffn_kernel(x_ref, mod_ref, g_ref, wg_ref, wu_ref, wd_ref, o_ref, h_ref, *, row):
    j = pl.program_id(1)

    @pl.when(j == 0)
    def _():
        _norm_mod_into(h_ref, x_ref, g_ref[...], mod_ref[0, row:row + 1, :], mod_ref[0, row + 1:row + 2, :])
        o_ref[...] = jnp.zeros(o_ref.shape, F32)

    h = h_ref[...]
    g = jnp.dot(h, wg_ref[...], preferred_element_type=F32)
    u = jnp.dot(h, wu_ref[...], preferred_element_type=F32)
    a = ((g * jax.nn.sigmoid(g)) * u).astype(BF16)
    o_ref[...] += jnp.dot(a, wd_ref[...], preferred_element_type=F32)

    @pl.when(j == pl.num_programs(1) - 1)
    def _():
        gate = mod_ref[0, row + 2:row + 3, :]
        o_ref[...] = x_ref[...] + (0.5 * (1.0 + gate)) * o_ref[...]


def _ffn(x2, mod, g, wg, wu, wd, row, seq):
    T, D = x2.shape
    F = wg.shape[1]
    tm = min(1024, seq)
    tf = 512
    per_seq = seq // tm
    return pl.pallas_call(
        functools.partial(_ffn_kernel, row=row),
        grid=(T // tm, F // tf),
        in_specs=[
            pl.BlockSpec((tm, D), lambda i, j: (i, 0)),
            pl.BlockSpec((1, N_MOD, D), lambda i, j: (i // per_seq, 0, 0)),
            pl.BlockSpec((1, D), lambda i, j: (0, 0)),
            pl.BlockSpec((D, tf), lambda i, j: (0, j)),
            pl.BlockSpec((D, tf), lambda i, j: (0, j)),
            pl.BlockSpec((tf, D), lambda i, j: (j, 0)),
        ],
        out_specs=pl.BlockSpec((tm, D), lambda i, j: (i, 0)),
        out_shape=jax.ShapeDtypeStruct((T, D), F32),
        scratch_shapes=[pltpu.VMEM((tm, D), BF16)],
        compiler_params=_params("arbitrary", "arbitrary", vmem_limit=FFN_VMEM_LIMIT),
        name="ffn",
    )(x2, mod, g.reshape(1, D), wg, wu, wd)


def _rms_lanes(x, g, n):
    return x * lax.rsqrt(jnp.sum(x * x, axis=-1, keepdims=True) / n + NORM_EPS) * g


def _rope_lanes(x, cos, sin_lo, sin_hi, half):
    return (x * cos + pltpu.roll(x, LANES - half, 1) * sin_lo + pltpu.roll(x, half, 1) * sin_hi)


def _prep_kernel(x_ref, mod_ref, gmix_ref, win_ref, tab_ref,
                 gqa_ref, gka_ref, gcq_ref, gckv_ref, gqn_ref, gkn_ref, gqr_ref, gkr_ref,
                 wuq_ref, wukv_ref,
                 qa_o, ka_o, vat_o, qi_o, ki_o, wt_o, qcat_o, kcat_o, vbt_o, qc_o, kc_o, vct_o, h_ref, t_ref):
    ca, sa_lo, sa_hi = tab_ref[0], tab_ref[1], tab_ref[2]
    cm, sm_lo, sm_hi = tab_ref[3], tab_ref[4], tab_ref[5]
    half_a = PARTIAL_ROPE_DIM // 2
    half_m = MLA_ROPE // 2
    wa, wc = A_HEADS * HEAD_DIM, C_HEADS * HEAD_DIM

    _norm_mod_into(h_ref, x_ref, gmix_ref[...], mod_ref[0, 3:4, :], mod_ref[0, 4:5, :])

    def transposed(v):
        t_ref[:, 0:v.shape[1]] = v
        return t_ref[:, 0:v.shape[1]].T

    def proj(lo, width):
        return jnp.dot(h_ref[...], win_ref[:, lo:lo + width], preferred_element_type=F32)

    cq = proj(OFF_CQ, Q_RANK_PAD)
    small = proj(OFF_CKV, N_PROJ - OFF_CKV)
    qa = proj(OFF_QA, wa)
    ka = proj(OFF_KA, wa)
    cq = _rms_lanes(cq, gcq_ref[...], MLA_Q_RANK)
    ckv = _rms_lanes(small[:, 0:MLA_KV_RANK], gckv_ref[...], MLA_KV_RANK)
    qb = jnp.dot(cq.astype(BF16), wuq_ref[...], preferred_element_type=F32)
    kvb = jnp.dot(ckv.astype(BF16), wukv_ref[...], preferred_element_type=F32)
    va = proj(OFF_VA, wa)
    qi = proj(OFF_QI, IDX_HEADS * IDX_DIM)
    qc = proj(OFF_QC, wc)
    kc = proj(OFF_KC, wc)
    vc = proj(OFF_VC, wc)

    for h in range(A_HEADS):
        lo = h * HEAD_DIM
        q = _rms_lanes(qa[:, lo:lo + HEAD_DIM], gqa_ref[...], HEAD_DIM)
        t_ref[:, lo:lo + HEAD_DIM] = _rope_lanes(q, ca, sa_lo, sa_hi, half_a) * A_SCALE
    qa_o[0, 0] = t_ref[:, 0:wa].T.astype(BF16)
    for h in range(A_HEADS):
        lo = h * HEAD_DIM
        k = _rms_lanes(ka[:, lo:lo + HEAD_DIM], gka_ref[...], HEAD_DIM)
        ka_o[0, :, lo:lo + HEAD_DIM] = _rope_lanes(k, ca, sa_lo, sa_hi, half_a).astype(BF16)
    vat_o[0, 0] = transposed(va).astype(BF16)
    qi_o[0, 0] = transposed(qi).astype(BF16)

    qc_o[0, 0] = transposed(qc * C_SCALE).astype(BF16)
    kc_o[0] = kc.astype(BF16)
    vct_o[0, 0] = transposed(vc).astype(BF16)

    kr = _rms_lanes(small[:, OFF_KR - OFF_CKV:OFF_KR - OFF_CKV + LANES], gkr_ref[...], MLA_ROPE)
    kr = _rope_lanes(kr, cm, sm_lo, sm_hi, half_m).astype(BF16)
    kiw = small[:, OFF_KIW - OFF_CKV:OFF_KIW - OFF_CKV + LANES]
    ki_o[0] = kiw[:, :IDX_DIM].astype(BF16)
    wt_o[0] = transposed(kiw)[IDX_DIM:IDX_DIM + IDX_HEADS, :] * IDX_SCALE
    for h in range(B_HEADS):
        lo = h * 2 * LANES
        kn = _rms_lanes(kvb[:, h * MLA_NOPE:(h + 1) * MLA_NOPE], gkn_ref[...], MLA_NOPE)
        kcat_o[0, :, lo:lo + MLA_NOPE] = kn.astype(BF16)
        kcat_o[0, :, lo + MLA_NOPE:lo + 2 * LANES] = kr
    vbt_o[0, 0] = transposed(kvb[:, B_HEADS * MLA_NOPE:]).astype(BF16)
    for h in range(B_HEADS):
        lo = h * 2 * LANES
        qn = _rms_lanes(qb[:, lo:lo + MLA_NOPE], gqn_ref[...], MLA_NOPE)
        t_ref[:, lo:lo + MLA_NOPE] = qn * B_SCALE
        qr = _rms_lanes(qb[:, lo + MLA_NOPE:lo + 2 * LANES], gqr_ref[...], MLA_ROPE)
        t_ref[:, lo + MLA_NOPE:lo + 2 * LANES] = _rope_lanes(qr, cm, sm_lo, sm_hi, half_m) * B_SCALE
    qcat_o[0, 0] = t_ref[...].T.astype(BF16)


def _resident(shape):
    return pl.BlockSpec(shape, lambda b, i: (0,) * len(shape), pipeline_mode=pl.Buffered(1))


def _prep(x2, mod, g_mix, w_in_p, tabs, gains, wuq, wukv, batch, seq):
    tm = KEY_BLOCK
    nt = seq // tm
    d_model = x2.shape[1]
    wa, wb, wc = A_HEADS * HEAD_DIM, B_HEADS * HEAD_DIM, C_HEADS * HEAD_DIM
    wcat = B_HEADS * 2 * LANES

    def tok(width, dtype):
        return (jax.ShapeDtypeStruct((batch, seq, width), dtype),
                pl.BlockSpec((1, tm, width), lambda b, i: (b, i, 0)))

    def tok_t(width):
        return (jax.ShapeDtypeStruct((batch, nt, width, tm), BF16),
                pl.BlockSpec((1, 1, width, tm), lambda b, i: (b, i, 0, 0)))

    outs = [tok_t(wa), tok(wa, BF16), tok_t(wa), tok_t(IDX_HEADS * IDX_DIM), tok(IDX_DIM, BF16),
            (jax.ShapeDtypeStruct((batch, IDX_HEADS, seq), F32),
             pl.BlockSpec((1, IDX_HEADS, tm), lambda b, i: (b, 0, i))),
            tok_t(wcat), tok(wcat, BF16), tok_t(wb), tok_t(wc), tok(wc, BF16), tok_t(wc)]
    return pl.pallas_call(
        _prep_kernel,
        grid=(batch, nt),
        in_specs=[pl.BlockSpec((tm, d_model), lambda b, i: (b * nt + i, 0)),
                  pl.BlockSpec((1, N_MOD, d_model), lambda b, i: (b, 0, 0)),
                  _resident((1, d_model)),
                  _resident(w_in_p.shape),
                  pl.BlockSpec((6, tm, LANES), lambda b, i: (0, i, 0))]
                 + [_resident(g.shape) for g in gains]
                 + [_resident(wuq.shape), _resident(wukv.shape)],
        out_specs=[o[1] for o in outs],
        out_shape=[o[0] for o in outs],
        scratch_shapes=[pltpu.VMEM((tm, d_model), BF16), pltpu.VMEM((tm, wcat), F32)],
        compiler_params=_params("arbitrary", "arbitrary"),
        name="head_prep",
    )(x2, mod, g_mix.reshape(1, d_model), w_in_p, tabs, *gains, wuq, wukv)


MLA_HEADS_PER_STEP = 4


def _mla_kernel(qt_ref, k_ref, vt_ref, o_ref, s0_ref, s1_ref, m_ref, l_ref, acc_ref, *, t):
    qb = pl.program_id(2)
    sub = t // KEY_BLOCK
    dq = 2 * LANES
    m_ref[...] = jnp.full(m_ref.shape, NEG_INF, F32)
    l_ref[...] = jnp.zeros(l_ref.shape, F32)
    acc_ref[...] = jnp.zeros(acc_ref.shape, F32)

    def produce(s_ref, kb):
        ks = pl.multiple_of(kb * t, t)
        for h in range(MLA_HEADS_PER_STEP):
            k = k_ref[0, pl.ds(ks, t), h * dq:(h + 1) * dq]
            for c in range(sub):
                s_ref[h, :, c * KEY_BLOCK:(c + 1) * KEY_BLOCK] = jnp.dot(
                    k, qt_ref[0, c, h * dq:(h + 1) * dq, :], preferred_element_type=F32)

    def consume(s_ref, kb, diagonal):
        for h in range(MLA_HEADS_PER_STEP):
            s = s_ref[h]
            if diagonal:
                kc = lax.broadcasted_iota(I32, (t, t), 0) >> CHUNK_SHIFT
                qc = lax.broadcasted_iota(I32, (t, t), 1) >> CHUNK_SHIFT
                s = jnp.where(kc <= qc, s, NEG_INF)
            m_prev = m_ref[h]
            m_new = jnp.maximum(m_prev, jnp.max(s, axis=0, keepdims=True))
            p = jnp.exp2(s - m_new)
            alpha = jnp.exp2(m_prev - m_new)
            l_ref[h] = alpha * l_ref[h] + jnp.sum(p, axis=0, keepdims=True)
            p = p.astype(BF16)
            pv = jnp.dot(vt_ref[0, kb * sub, h * HEAD_DIM:(h + 1) * HEAD_DIM, :], p[0:KEY_BLOCK],
                         preferred_element_type=F32)
            for c in range(1, sub):
                pv += jnp.dot(vt_ref[0, kb * sub + c, h * HEAD_DIM:(h + 1) * HEAD_DIM, :],
                              p[c * KEY_BLOCK:(c + 1) * KEY_BLOCK], preferred_element_type=F32)
            acc_ref[h] = alpha * acc_ref[h] + pv
            m_ref[h] = m_new

    produce(s0_ref, 0)

    def body(j, carry):
        produce(s1_ref, 2 * j + 1)
        consume(s0_ref, 2 * j, False)
        produce(s0_ref, 2 * j + 2)
        consume(s1_ref, 2 * j + 1, False)
        return carry

    lax.fori_loop(0, qb // 2, body, 0)

    @pl.when(qb % 2 == 0)
    def _():
        consume(s0_ref, qb, True)

    @pl.when(qb % 2 == 1)
    def _():
        produce(s1_ref, qb)
        consume(s0_ref, qb - 1, False)
        consume(s1_ref, qb, True)

    for h in range(MLA_HEADS_PER_STEP):
        o_ref[0, :, h * HEAD_DIM:(h + 1) * HEAD_DIM] = (acc_ref[h] / l_ref[h]).T.astype(BF16)


def _mla(qcat, kcat, vbt, batch, seq):
    t = min(512, seq)
    hp = MLA_HEADS_PER_STEP
    nkb = seq // KEY_BLOCK
    return pl.pallas_call(
        functools.partial(_mla_kernel, t=t),
        grid=(batch, B_HEADS // hp, seq // t),
        in_specs=[
            pl.BlockSpec((1, t // KEY_BLOCK, hp * 2 * LANES, KEY_BLOCK), lambda b, h, i: (b, i, h, 0)),
            pl.BlockSpec((1, seq, hp * 2 * LANES), lambda b, h, i: (b, 0, h)),
            pl.BlockSpec((1, nkb, hp * HEAD_DIM, KEY_BLOCK), lambda b, h, i: (b, 0, h, 0)),
        ],
        out_specs=pl.BlockSpec((1, t, hp * HEAD_DIM), lambda b, h, i: (b, i, h)),
        out_shape=jax.ShapeDtypeStruct((batch, seq, B_HEADS * HEAD_DIM), BF16),
        scratch_shapes=[pltpu.VMEM((hp, t, t), F32), pltpu.VMEM((hp, t, t), F32),
                        pltpu.VMEM((hp, 1, t), F32), pltpu.VMEM((hp, 1, t), F32),
                        pltpu.VMEM((hp, HEAD_DIM, t), F32)],
        compiler_params=_params("arbitrary", "arbitrary", "arbitrary"),
        name="mla_attn",
    )(qcat, kcat, vbt)


SB_HEADS_PER_STEP = 4


def _sb_kernel(qt_ref, k_ref, vt_ref, o_ref, z0_ref, z1_ref, r_ref, acc_ref, *, tq):
    qb = pl.program_id(2)
    tk = KEY_BLOCK
    sub = tq // tk
    assert sub % 2 == 0
    tri_r = lax.broadcasted_iota(I32, (tk, tk), 0)
    tri_c = lax.broadcasted_iota(I32, (tk, tk), 1)
    later_keys = jnp.where(tri_c > tri_r, 1.0, 0.0).astype(BF16)
    r_ref[...] = jnp.zeros(r_ref.shape, F32)
    acc_ref[...] = jnp.zeros(acc_ref.shape, F32)

    heads = range(SB_HEADS_PER_STEP)

    def produce(z_ref, kb):
        ks = pl.multiple_of(kb * tk, tk)
        for h in heads:
            k = k_ref[0, pl.ds(ks, tk), h * HEAD_DIM:(h + 1) * HEAD_DIM]
            for c in range(sub):
                z_ref[h, :, c * tk:(c + 1) * tk] = jnp.dot(
                    k, qt_ref[0, c, h * HEAD_DIM:(h + 1) * HEAD_DIM, :], preferred_element_type=F32)

    def consume(z_ref, kb, key_offset):
        if key_offset is not None:
            before = (lax.broadcasted_iota(I32, (tk, tq), 0) + key_offset
                      < lax.broadcasted_iota(I32, (tk, tq), 1))
        zs = [z_ref[h] for h in heads]
        go = []
        for z in zs:
            neg_abs = lax.bitcast_convert_type(lax.bitcast_convert_type(z, I32) | INT_MIN, F32)
            g = jnp.maximum(z, 0.0) + jnp.log(1.0 + jnp.exp(neg_abs))
            if key_offset is not None:
                g = jnp.where(before, g, 0.0)
            go.append(g)
        later = [jnp.dot(later_keys, g.astype(BF16), preferred_element_type=F32) for g in go]
        for h in heads:
            w = jnp.exp(zs[h] - (go[h] + later[h] + r_ref[h]))
            if key_offset is not None:
                w = jnp.where(before, w, 0.0)
            acc_ref[h] += jnp.dot(vt_ref[0, kb, h * HEAD_DIM:(h + 1) * HEAD_DIM, :], w.astype(BF16),
                                  preferred_element_type=F32)
            r_ref[h] += jnp.sum(go[h], axis=0, keepdims=True)

    first = qb * sub
    bufs = (z0_ref, z1_ref)
    produce(bufs[0], first + sub - 1)
    for i, d in enumerate(reversed(range(sub))):
        if d > 0:
            produce(bufs[(i + 1) % 2], first + d - 1)
        else:
            @pl.when(qb > 0)
            def _():
                produce(bufs[(i + 1) % 2], first - 1)
        consume(bufs[i % 2], first + d, d * tk)

    def body(j, carry):
        top = first - 1 - 2 * j
        produce(z1_ref, top - 1)
        consume(z0_ref, top, None)
        produce(z0_ref, top - 2)
        consume(z1_ref, top - 1, None)
        return carry

    lax.fori_loop(0, first // 2 - 1, body, 0)

    @pl.when(qb > 0)
    def _():
        produce(z1_ref, 0)
        consume(z0_ref, 1, None)
        consume(z1_ref, 0, None)

    for h in range(SB_HEADS_PER_STEP):
        o_ref[0, :, h * HEAD_DIM:(h + 1) * HEAD_DIM] = acc_ref[h].T.astype(BF16)


def _sb(qc, kc, vct, batch, seq):
    tq = min(512, seq)
    hp = SB_HEADS_PER_STEP
    nkb = seq // KEY_BLOCK
    return pl.pallas_call(
        functools.partial(_sb_kernel, tq=tq),
        grid=(batch, C_HEADS // hp, seq // tq),
        in_specs=[
            pl.BlockSpec((1, tq // KEY_BLOCK, hp * HEAD_DIM, KEY_BLOCK), lambda b, h, i: (b, i, h, 0)),
            pl.BlockSpec((1, seq, hp * HEAD_DIM), lambda b, h, i: (b, 0, h)),
            pl.BlockSpec((1, nkb, hp * HEAD_DIM, KEY_BLOCK), lambda b, h, i: (b, 0, h, 0)),
        ],
        out_specs=pl.BlockSpec((1, tq, hp * HEAD_DIM), lambda b, h, i: (b, i, h)),
        out_shape=jax.ShapeDtypeStruct((batch, seq, C_HEADS * HEAD_DIM), BF16),
        scratch_shapes=[pltpu.VMEM((hp, KEY_BLOCK, tq), F32), pltpu.VMEM((hp, KEY_BLOCK, tq), F32),
                        pltpu.VMEM((hp, 1, tq), F32), pltpu.VMEM((hp, HEAD_DIM, tq), F32)],
        compiler_params=_params("arbitrary", "arbitrary", "arbitrary"),
        name="sb_attn",
    )(qc, kc, vct)


SEARCH_GROUP = 2


def _dsa_kernel(qit_ref, ki_ref, wt_ref, qat_ref, ka_ref, vt_ref, o_ref,
                keys_ref, half_ref, thr_ref, m_ref, l_ref, acc_ref, s0_ref, s1_ref,
                *, t, tq, topk, seq):
    qb = pl.program_id(1)
    sub = tq // t
    nk = (qb + 1) * sub
    row = lax.broadcasted_iota(I32, (t, tq), 0)
    col = lax.broadcasted_iota(I32, (t, tq), 1)
    wt = wt_ref[0]

    def score_block(kb):
        ks = pl.multiple_of(kb * t, t)
        kix = ki_ref[0, pl.ds(ks, t), :]
        sc = jnp.zeros((t, tq), F32)
        for h in range(IDX_HEADS):
            lg = jnp.concatenate(
                [jnp.dot(kix, qit_ref[0, c, h * IDX_DIM:(h + 1) * IDX_DIM, :], preferred_element_type=F32)
                 for c in range(sub)], axis=1)
            sc = sc + wt[h:h + 1, :] * jnp.maximum(lg, 0.0)
        bits = lax.bitcast_convert_type(sc, I32)
        key = bits ^ ((bits >> 31) & 0x7FFFFFFF)
        visible = ((ks + row) >> CHUNK_SHIFT) <= ((qb * tq + col) >> CHUNK_SHIFT)
        key = jnp.where(visible, key, INT_MIN)
        keys_ref[kb] = key
        half_ref[kb] = (key >> 16).astype(I16)

    def score_pair(j, carry):
        score_block(2 * j)
        score_block(2 * j + 1)
        return carry

    assert sub % 2 == 0
    lax.fori_loop(0, nk // 2, score_pair, 0)

    i16_min, i16_max = -2 ** 15, 2 ** 15 - 1
    for pad in range(SEARCH_GROUP - 1):
        half_ref[nk + pad] = jnp.full((t, tq), i16_min, I16)

    def search16():
        def count16(cand):
            cand16 = cand.astype(I16)

            def block_group(j, accs):
                out = []
                for c, acc in enumerate(accs):
                    parts = [acc]
                    for kb in range(SEARCH_GROUP):
                        hit = jnp.where(half_ref[SEARCH_GROUP * j + kb, :, c * t:(c + 1) * t] >= cand16[:, c * t:(c + 1) * t],
                                        jnp.ones((), I16), jnp.zeros((), I16))
                        parts += [hit[i * 16:(i + 1) * 16] for i in range(t // 16)]
                    while len(parts) > 1:
                        parts = [a + b for a, b in zip(parts[0::2], parts[1::2])] + parts[len(parts) & ~1:]
                    out.append(parts[0])
                return tuple(out)
            groups = (nk + SEARCH_GROUP - 1) // SEARCH_GROUP
            accs = lax.fori_loop(0, groups, block_group,
                                 tuple(jnp.zeros((16, t), I16) for _ in range(sub)))
            return jnp.concatenate([jnp.sum(a.astype(I32), axis=0, keepdims=True) for a in accs], axis=1)

        c0 = count16(jnp.zeros((1, tq), I32))
        v0 = jnp.where(c0 >= topk, 0, i16_min).astype(I32)
        n0 = jnp.where(c0 >= topk, c0, 0)

        def bit(b, carry):
            v, n = carry
            cand = v | lax.shift_left(jnp.int32(1), 14 - b)
            c = count16(cand)
            return jnp.where(c >= topk, cand, v), jnp.where(c >= topk, c, n)

        return lax.fori_loop(0, 15, bit, (v0, n0))

    def count(pred):
        def block(kb, acc):
            hit = jnp.where(pred(keys_ref[kb], kb * t + row), 1, 0).astype(I32)
            parts = [acc] + [hit[i * 8:(i + 1) * 8] for i in range(t // 8)]
            while len(parts) > 1:
                parts = [a + b for a, b in zip(parts[0::2], parts[1::2])] + parts[len(parts) & ~1:]
            return parts[0]
        acc = lax.fori_loop(0, nk, block, jnp.zeros((8, tq), I32))
        return jnp.sum(acc, axis=0, keepdims=True)

    thr_hi, n_hi = search16()

    thr_hi16 = thr_hi.astype(I16)

    def low_half_block(kb, carry):
        hi = half_ref[kb]
        low = ((keys_ref[kb] & 0xFFFF) + i16_min).astype(I16)
        top = jnp.full((), i16_max, I16)
        bottom = jnp.full((), i16_min, I16)
        half_ref[kb] = jnp.where(hi == thr_hi16, low, jnp.where(hi > thr_hi16, top, bottom))
        return carry

    lax.fori_loop(0, nk, low_half_block, 0)
    thr_lo, n_lo = search16()
    thr = lax.shift_left(thr_hi, 16) + (thr_lo - i16_min)
    n_ge = jnp.where(thr_lo > i16_min, n_lo, n_hi)
    tie = (thr != INT_MIN) & (n_ge > topk)
    thr_ref[...] = jnp.maximum(thr, INT_MIN + 1)

    @pl.when(jnp.max(tie.astype(I32)) > 0)
    def _():
        nbits = seq.bit_length() - 1
        need = topk - count(lambda k, i: k > thr)

        def lim_bit(b, lim):
            cand = lim | lax.shift_left(jnp.int32(1), nbits - 1 - b)
            below = count(lambda k, i: (k == thr) & (i < cand))
            return jnp.where(below < need, cand, lim)

        lim = lax.fori_loop(0, nbits, lim_bit, jnp.zeros((1, tq), I32)) + 1
        lim = jnp.where(tie, lim, seq)

        def demote_block(kb, carry):
            key = keys_ref[kb]
            drop = (key == thr) & ((kb * t + row) >= lim)
            keys_ref[kb] = jnp.where(drop, key - 1, key)
            return carry

        lax.fori_loop(0, nk, demote_block, 0)

    m_ref[...] = jnp.full(m_ref.shape, NEG_INF, F32)
    l_ref[...] = jnp.zeros(l_ref.shape, F32)
    acc_ref[...] = jnp.zeros(acc_ref.shape, F32)

    def produce(s_ref, kb):
        ks = pl.multiple_of(kb * t, t)
        bias = jnp.where(keys_ref[kb] >= thr_ref[...], 0.0, NEG_INF)
        for h in range(A_HEADS):
            k = ka_ref[0, pl.ds(ks, t), h * HEAD_DIM:(h + 1) * HEAD_DIM]
            for c in range(sub):
                s_ref[h, :, c * t:(c + 1) * t] = jnp.dot(
                    k, qat_ref[0, c, h * HEAD_DIM:(h + 1) * HEAD_DIM, :],
                    preferred_element_type=F32) + bias[:, c * t:(c + 1) * t]

    def consume(s_ref, kb):
        for h in range(A_HEADS):
            lo = h * HEAD_DIM
            s = s_ref[h]
            m_prev = m_ref[h]
            m_new = jnp.maximum(m_prev, jnp.max(s, axis=0, keepdims=True))
            m_safe = jnp.where(m_new == NEG_INF, 0.0, m_new)
            p = jnp.exp2(s - m_safe)
            alpha = jnp.exp2(m_prev - m_safe)
            l_ref[h] = alpha * l_ref[h] + jnp.sum(p, axis=0, keepdims=True)
            acc_ref[h] = alpha * acc_ref[h] + jnp.dot(vt_ref[0, kb, lo:lo + HEAD_DIM, :], p.astype(BF16),
                                                      preferred_element_type=F32)
            m_ref[h] = m_new

    produce(s0_ref, 0)

    def attend_pair(j, carry):
        produce(s1_ref, 2 * j + 1)
        consume(s0_ref, 2 * j)
        produce(s0_ref, 2 * j + 2)
        consume(s1_ref, 2 * j + 1)
        return carry

    lax.fori_loop(0, nk // 2 - 1, attend_pair, 0)
    produce(s1_ref, nk - 1)
    consume(s0_ref, nk - 2)
    consume(s1_ref, nk - 1)

    for h in range(A_HEADS):
        o_ref[0, :, h * HEAD_DIM:(h + 1) * HEAD_DIM] = (acc_ref[h] / l_ref[h]).T.astype(BF16)


def _dsa(qi, ki, wt, qa, ka, vat, batch, seq, topk):
    t = KEY_BLOCK
    tq = min(512, seq)
    nt = seq // t
    wa = A_HEADS * HEAD_DIM
    return pl.pallas_call(
        functools.partial(_dsa_kernel, t=t, tq=tq, topk=topk, seq=seq),
        grid=(batch, seq // tq),
        in_specs=[
            pl.BlockSpec((1, tq // t, IDX_HEADS * IDX_DIM, t), lambda b, i: (b, i, 0, 0)),
            pl.BlockSpec((1, seq, IDX_DIM), lambda b, i: (b, 0, 0)),
            pl.BlockSpec((1, IDX_HEADS, tq), lambda b, i: (b, 0, i)),
            pl.BlockSpec((1, tq // t, wa, t), lambda b, i: (b, i, 0, 0)),
            pl.BlockSpec((1, seq, wa), lambda b, i: (b, 0, 0)),
            pl.BlockSpec((1, nt, wa, t), lambda b, i: (b, 0, 0, 0)),
        ],
        out_specs=pl.BlockSpec((1, tq, wa), lambda b, i: (b, i, 0)),
        out_shape=jax.ShapeDtypeStruct((batch, seq, wa), BF16),
        scratch_shapes=[
            pltpu.VMEM((nt, t, tq), I32), pltpu.VMEM((nt + SEARCH_GROUP - 1, t, tq), I16),
            pltpu.VMEM((1, tq), I32),
            pltpu.VMEM((A_HEADS, 1, tq), F32), pltpu.VMEM((A_HEADS, 1, tq), F32),
            pltpu.VMEM((A_HEADS, HEAD_DIM, tq), F32),
            pltpu.VMEM((A_HEADS, t, tq), F32), pltpu.VMEM((A_HEADS, t, tq), F32),
        ],
        compiler_params=_params("arbitrary", "arbitrary"),
        name="dsa_attn",
    )(qi, ki, wt, qa, ka, vat)


def _out_kernel(x_ref, mod_ref, oa_ref, ob_ref, oc_ref, w_ref, o_ref):
    wa = A_HEADS * HEAD_DIM
    wb = B_HEADS * HEAD_DIM
    mixed = (jnp.dot(oa_ref[...], w_ref[0:wa, :], preferred_element_type=F32)
             + jnp.dot(ob_ref[...], w_ref[wa:wa + wb, :], preferred_element_type=F32)
             + jnp.dot(oc_ref[...], w_ref[wa + wb:, :], preferred_element_type=F32))
    o_ref[...] = x_ref[...] + (1.0 + mod_ref[0, 5:6, :]) * mixed


def _out_proj(x2, mod, oa, ob, oc, w, seq):
    T, D = x2.shape
    tm = min(1024, seq)
    per_seq = seq // tm
    return pl.pallas_call(
        _out_kernel,
        grid=(T // tm,),
        in_specs=[
            pl.BlockSpec((tm, D), lambda i: (i, 0)),
            pl.BlockSpec((1, N_MOD, D), lambda i: (i // per_seq, 0, 0)),
            pl.BlockSpec((tm, oa.shape[1]), lambda i: (i, 0)),
            pl.BlockSpec((tm, ob.shape[1]), lambda i: (i, 0)),
            pl.BlockSpec((tm, oc.shape[1]), lambda i: (i, 0)),
            pl.BlockSpec(w.shape, lambda i: (0, 0), pipeline_mode=pl.Buffered(1)),
        ],
        out_specs=pl.BlockSpec((tm, D), lambda i: (i, 0)),
        out_shape=jax.ShapeDtypeStruct((T, D), F32),
        compiler_params=_params("arbitrary", vmem_limit=FFN_VMEM_LIMIT),
        name="out_proj",
    )(x2, mod, oa, ob, oc, w)


CAST_ROWS = 256


def _cast_kernel(w_ref, o_ref):
    o_ref[...] = w_ref[0].astype(BF16)


def _layer_bf16(w, layer):
    _, rows, cols = w.shape
    tr = min(CAST_ROWS, rows)
    return pl.pallas_call(
        _cast_kernel,
        grid=(rows // tr,),
        in_specs=[pl.BlockSpec((1, tr, cols), lambda i: (layer, i, 0))],
        out_specs=pl.BlockSpec((tr, cols), lambda i: (i, 0)),
        out_shape=jax.ShapeDtypeStruct((rows, cols), BF16),
        compiler_params=_params("arbitrary"),
        name="cast_bf16",
    )(w)


def _rope_tables(seq):
    def tables(dim):
        inv = 1.0 / (ROPE_THETA ** (jnp.arange(0, dim, 2, dtype=F32) / dim))
        ang = jnp.arange(seq, dtype=F32)[:, None] * inv[None, :]
        return jnp.cos(ang), jnp.sin(ang)

    def lane_tables(cos, sin, fill):
        half = cos.shape[1]
        rest = LANES - 2 * half
        zeros_h = jnp.zeros((seq, half), F32)
        zeros_r = jnp.zeros((seq, rest), F32)
        return [jnp.concatenate([cos, cos, jnp.full((seq, rest), fill, F32)], axis=1),
                jnp.concatenate([-sin, zeros_h, zeros_r], axis=1),
                jnp.concatenate([zeros_h, sin, zeros_r], axis=1)]

    cos_p, sin_p = tables(PARTIAL_ROPE_DIM)
    cos_m, sin_m = tables(MLA_ROPE)
    return jnp.stack(lane_tables(cos_p, sin_p, 1.0) + lane_tables(cos_m, sin_m, 0.0))


def _pad_cols(a, width):
    return jnp.pad(a, ((0, 0), (0, width - a.shape[1])))


def _layer_weights(w_in, w_uq, w_ukv):
    w_in_p = jnp.concatenate([
        w_in[:, 0:2560],
        w_in[:, 3280:4816],
        _pad_cols(w_in[:, 2640:3088], Q_RANK_PAD),
        w_in[:, 3088:3216],
        _pad_cols(w_in[:, 3216:3280], LANES),
        _pad_cols(w_in[:, 2560:2640], LANES),
    ], axis=1).astype(BF16)
    wuq = w_uq.reshape(MLA_Q_RANK, B_HEADS, MLA_NOPE + MLA_ROPE)
    wuq = jnp.pad(wuq, ((0, Q_RANK_PAD - MLA_Q_RANK), (0, 0), (0, 2 * LANES - MLA_NOPE - MLA_ROPE)))
    wuq = wuq.reshape(Q_RANK_PAD, B_HEADS * 2 * LANES).astype(BF16)
    wukv = w_ukv.reshape(MLA_KV_RANK, B_HEADS, MLA_NOPE + HEAD_DIM)
    wukv = jnp.concatenate([wukv[:, :, :MLA_NOPE].reshape(MLA_KV_RANK, -1),
                            wukv[:, :, MLA_NOPE:].reshape(MLA_KV_RANK, -1)], axis=1).astype(BF16)
    return w_in_p, wuq, wukv


def kernel(x, c, w_ada, b_ada, g_ffn1, w1_gate, w1_up, w1_down, g_mix, w_in, g_qa, g_ka, g_cq, g_ckv, w_uq, w_ukv, g_q_nope, g_k_nope, g_q_rope, g_k_rope, w_out, g_ffn2, w2_gate, w2_up, w2_down):
    batch, seq, d_model = x.shape
    depth = w_ada.shape[0]
    topk = min(TOPK_MAX, seq // 4)
    tabs = _rope_tables(seq)
    mods = _ada(c, w_ada, b_ada).reshape(depth, batch, N_MOD, d_model)
    x2 = x.reshape(batch * seq, d_model)

    for l in range(depth):
        mod = mods[l]
        x2 = _ffn(x2, mod, g_ffn1[l], _layer_bf16(w1_gate, l), _layer_bf16(w1_up, l),
                  _layer_bf16(w1_down, l), 0, seq)

        w_in_p, wuq, wukv = _layer_weights(w_in[l], w_uq[l], w_ukv[l])
        gains = [g_qa[l][None, :], g_ka[l][None, :], _pad_cols(g_cq[l][None, :], Q_RANK_PAD),
                 g_ckv[l][None, :], g_q_nope[l][None, :], g_k_nope[l][None, :],
                 _pad_cols(g_q_rope[l][None, :], LANES), _pad_cols(g_k_rope[l][None, :], LANES)]
        (qa, ka, vat, qi, ki, wt, qcat, kcat, vbt, qc, kc, vct) = _prep(
            x2, mod, g_mix[l], w_in_p, tabs, gains, wuq, wukv, batch, seq)

        out_a = _dsa(qi, ki, wt, qa, ka, vat, batch, seq, topk)
        out_b = _mla(qcat, kcat, vbt, batch, seq)
        out_c = _sb(qc, kc, vct, batch, seq)
        x2 = _out_proj(x2, mod, out_a.reshape(batch * seq, -1), out_b.reshape(batch * seq, -1),
                       out_c.reshape(batch * seq, -1), _layer_bf16(w_out, l), seq)

        x2 = _ffn(x2, mod, g_ffn2[l], _layer_bf16(w2_gate, l), _layer_bf16(w2_up, l),
                  _layer_bf16(w2_down, l), 6, seq)

    return x2.reshape(batch, seq, d_model)
```
